```python
import jax, jax.numpy as jnp
from jax import lax
import numpy as np

D_MODEL = 1024
BATCH = 16
SEQ = 2048
DEPTH = 1

CHUNK = 128
SGU_WIDTH = 1024
SGU_GROUPS = 8
SGU_GROUP_DIM = SGU_WIDTH // SGU_GROUPS
ATT_HEADS = 8
ATT_HEAD_DIM = 128
ATT_WIDTH = ATT_HEADS * ATT_HEAD_DIM
DILATED_PATTERNS = ((128, 1), (512, 4), (2048, 16))
IN_WIDTH = 2 * SGU_WIDTH + 3 * ATT_WIDTH + 2 * D_MODEL
N_GROUPS = 4
EXPERTS_PER_GROUP = 4
N_EXPERTS = N_GROUPS * EXPERTS_PER_GROUP
TOP_K_EXPERTS = 2
D_EXPERT = 512
EPS = 1e-6
NEG_INF = -1e30

kernel_name = "hybrid_gmlp_dilated_attn_hmoe_block"


def _rmsnorm(x, g):
    xf = x.astype(jnp.float32)
    y = xf * lax.rsqrt(jnp.mean(xf * xf, axis=-1, keepdims=True) + EPS)
    return (y * g.astype(jnp.float32)).astype(x.dtype)


def _modulate(h, shift, scale):
    return h * (1 + scale[:, None, :]) + shift[:, None, :]


def _spatial_gating(u, v, norm_g, w_s, b_s):
    B, S, _ = v.shape
    v = _rmsnorm(v, norm_g)
    vc = v.reshape(B, S // CHUNK, CHUNK, SGU_GROUPS, SGU_GROUP_DIM)
    causal = jnp.tril(jnp.ones((CHUNK, CHUNK), dtype=bool))
    w = jnp.where(causal[None], w_s, 0)
    mixed = jnp.einsum('gts,bcsgk->bctgk', w, vc) + b_s.T[None, None, :, :, None]
    return u * mixed.reshape(B, S, SGU_WIDTH)


def _dilated_window_attention(q, k, v, window, dilation):
    B, S, H, Dh = q.shape
    L = window // dilation
    span = dilation * L
    Sp = -(-S // span) * span
    n = Sp // dilation
    nb = n // L

    def to_blocks(t):
        t = jnp.pad(t, ((0, 0), (0, Sp - S), (0, 0), (0, 0)))
        t = t.reshape(B, n, dilation, H, Dh).transpose(0, 2, 3, 1, 4)
        return t.reshape(B, dilation, H, nb, L, Dh)

    qb, kb, vb = to_blocks(q), to_blocks(k), to_blocks(v)

    def with_prev(t):
        prev = jnp.pad(t[:, :, :, :-1], ((0, 0), (0, 0), (0, 0), (1, 0), (0, 0), (0, 0)))
        return jnp.concatenate([prev, t], axis=4)

    k2, v2 = with_prev(kb), with_prev(vb)
    a = jnp.arange(L)[:, None]
    b = jnp.arange(2 * L)[None, :]
    band = (b >= a) & (b <= a + L)
    not_first = (jnp.arange(nb) > 0)[:, None, None]
    mask = band[None] & (not_first | (b >= L)[None])

    s = jnp.einsum('bdhnqe,bdhnke->bdhnqk', qb, k2) * (Dh ** -0.5)
    s = jnp.where(mask, s, NEG_INF)
    m = jnp.max(s, axis=-1, keepdims=True)
    p = jnp.exp(s - m)
    den = jnp.sum(p, axis=-1, keepdims=True)
    o = jnp.einsum('bdhnqk,bdhnke->bdhnqe', p, v2) / den
    lse = m + jnp.log(den)

    def from_blocks(t):
        last = t.shape[-1]
        t = t.reshape(B, dilation, H, n, last).transpose(0, 3, 1, 2, 4)
        return t.reshape(B, Sp, H, last)[:, :S]

    return from_blocks(o), from_blocks(lse)[..., 0]


def _mixer(h, w_in, sgu_norm_g, sgu_w, sgu_b, q_norm_g, k_norm_g, w_proj_a, w_proj_b, w_out):
    B, S, _ = h.shape
    proj = h @ w_in
    splits = np.cumsum([SGU_WIDTH, SGU_WIDTH, ATT_WIDTH, ATT_WIDTH, ATT_WIDTH, D_MODEL]).tolist()
    u, v_sgu, q, k, v_att, g_a, g_b = jnp.split(proj, splits, axis=-1)

    y_a = _spatial_gating(jax.nn.gelu(u), jax.nn.gelu(v_sgu), sgu_norm_g, sgu_w, sgu_b) @ w_proj_a

    q = _rmsnorm(q.reshape(B, S, ATT_HEADS, ATT_HEAD_DIM), q_norm_g).astype(jnp.float32)
    k = _rmsnorm(k.reshape(B, S, ATT_HEADS, ATT_HEAD_DIM), k_norm_g).astype(jnp.float32)
    v_att = v_att.reshape(B, S, ATT_HEADS, ATT_HEAD_DIM).astype(jnp.float32)
    outs, lses = [], []
    for window, dilation in DILATED_PATTERNS:
        o_p, lse_p = _dilated_window_attention(q, k, v_att, window, dilation)
        outs.append(o_p)
        lses.append(lse_p)
    alpha = jax.nn.softmax(jnp.stack(lses, axis=0), axis=0)
    o = jnp.einsum('pbsh,pbshe->bshe', alpha, jnp.stack(outs, axis=0))
    y_b = o.reshape(B, S, ATT_WIDTH).astype(h.dtype) @ w_proj_b

    merged = jax.nn.sigmoid(g_a) * y_a + jax.nn.sigmoid(g_b) * y_b
    return merged @ w_out


def _hierarchical_moe(h, w_rg, b_rg, w_re, b_re, w_gate, w_up, w_down):
    B, S, D = h.shape
    t = h.reshape(B * S, D)
    g_logits = (t @ w_rg + b_rg).astype(jnp.float32)
    g_prob = jax.nn.softmax(g_logits, axis=-1)
    g_top = jnp.argmax(g_logits, axis=-1)
    p_group = jnp.take_along_axis(g_prob, g_top[:, None], axis=-1)
    e_logits = (jnp.einsum('td,gde->tge', t, w_re) + b_re).astype(jnp.float32)
    e_logits = jnp.take_along_axis(e_logits, g_top[:, None, None], axis=1)[:, 0]
    top_val, top_idx = lax.top_k(e_logits, TOP_K_EXPERTS)
    p_exp = jax.nn.softmax(top_val, axis=-1) * p_group
    expert_ids = g_top[:, None] * EXPERTS_PER_GROUP + top_idx
    combine = jnp.sum(jax.nn.one_hot(expert_ids, N_EXPERTS, dtype=jnp.float32) * p_exp[..., None], axis=1)
    combine = combine.astype(t.dtype)
    y = jnp.zeros_like(t)
    for e in range(N_EXPERTS):
        he = jax.nn.silu(t @ w_gate[e]) * (t @ w_up[e])
        y = y + combine[:, e:e + 1] * (he @ w_down[e])
    return y.reshape(B, S, D)


def setup_inputs(seed: int = 0) -> dict:
    key = jax.random.key(seed)
    ks = jax.random.split(key, 24)
    f32 = jnp.float32

    def nrm(k, shape, scale):
        return jax.random.normal(k, shape, f32) * scale

    Dp = DEPTH
    return {
        "x": nrm(ks[0], (BATCH, SEQ, D_MODEL), 1.0),
        "c": nrm(ks[1], (BATCH, D_MODEL), 1.0),
        "w_ada": nrm(ks[2], (Dp, D_MODEL, 6 * D_MODEL), 0.5 * D_MODEL ** -0.5),
        "b_ada": nrm(ks[3], (Dp, 6 * D_MODEL), 0.02),
        "norm1_g": 1.0 + nrm(ks[4], (Dp, D_MODEL), 0.02),
        "w_in": nrm(ks[5], (Dp, D_MODEL, IN_WIDTH), D_MODEL ** -0.5),
        "sgu_norm_g": 1.0 + nrm(ks[6], (Dp, SGU_WIDTH), 0.02),
        "sgu_w": nrm(ks[7], (Dp, SGU_GROUPS, CHUNK, CHUNK), CHUNK ** -0.5),
        "sgu_b": 1.0 + nrm(ks[8], (Dp, SGU_GROUPS, CHUNK), 0.02),
        "q_norm_g": 1.0 + nrm(ks[9], (Dp, ATT_HEAD_DIM), 0.02),
        "k_norm_g": 1.0 + nrm(ks[10], (Dp, ATT_HEAD_DIM), 0.02),
        "w_proj_a": nrm(ks[11], (Dp, SGU_WIDTH, D_MODEL), SGU_WIDTH ** -0.5),
        "w_proj_b": nrm(ks[12], (Dp, ATT_WIDTH, D_MODEL), ATT_WIDTH ** -0.5),
        "w_out": nrm(ks[13], (Dp, D_MODEL, D_MODEL), D_MODEL ** -0.5),
        "norm2_g": 1.0 + nrm(ks[14], (Dp, D_MODEL), 0.02),
        "w_router_group": nrm(ks[15], (Dp, D_MODEL, N_GROUPS), D_MODEL ** -0.5),
        "b_router_group": nrm(ks[16], (Dp, N_GROUPS), 0.01),
        "w_router_expert": nrm(ks[17], (Dp, N_GROUPS, D_MODEL, EXPERTS_PER_GROUP), D_MODEL ** -0.5),
        "b_router_expert": nrm(ks[18], (Dp, N_GROUPS, EXPERTS_PER_GROUP), 0.01),
        "w_gate": nrm(ks[19], (Dp, N_EXPERTS, D_MODEL, D_EXPERT), D_MODEL ** -0.5),
        "w_up": nrm(ks[20], (Dp, N_EXPERTS, D_MODEL, D_EXPERT), D_MODEL ** -0.5),
        "w_down": nrm(ks[21], (Dp, N_EXPERTS, D_EXPERT, D_MODEL), D_EXPERT ** -0.5),
    }


def reference(x, c, w_ada, b_ada, norm1_g, w_in, sgu_norm_g, sgu_w, sgu_b, q_norm_g, k_norm_g,
              w_proj_a, w_proj_b, w_out, norm2_g, w_router_group, b_router_group,
              w_router_expert, b_router_expert, w_gate, w_up, w_down):
    cond = jax.nn.silu(c)
    for l in range(DEPTH):
        mod = cond @ w_ada[l] + b_ada[l]
        shift1, scale1, gate1, shift2, scale2, gate2 = jnp.split(mod, 6, axis=-1)
        h = _modulate(_rmsnorm(x, norm1_g[l]), shift1, scale1)
        y = _mixer(h, w_in[l], sgu_norm_g[l], sgu_w[l], sgu_b[l], q_norm_g[l], k_norm_g[l],
                   w_proj_a[l], w_proj_b[l], w_out[l])
        x = x + gate1[:, None, :] * y
        h2 = _modulate(_rmsnorm(x, norm2_g[l]), shift2, scale2)
        y2 = _hierarchical_moe(h2, w_router_group[l], b_router_group[l], w_router_expert[l],
                               b_router_expert[l], w_gate[l], w_up[l], w_down[l])
        x = x + gate2[:, None, :] * y2
    return x
```

```python
import functools
import math

import jax
import jax.numpy as jnp
import numpy as np
from jax import lax
from jax.experimental import pallas as pl
from jax.experimental.pallas import tpu as pltpu

D_MODEL = 1024
CHUNK = 128
SGU_GROUPS = 8
ATT_HEADS = 8
ATT_HEAD_DIM = 128
DILATED_PATTERNS = ((128, 1), (512, 4), (2048, 16))
N_GROUPS = 4
EXPERTS_PER_GROUP = 4
N_EXPERTS = N_GROUPS * EXPERTS_PER_GROUP
D_EXPERT = 512
EPS = 1e-6
NEG_INF = -1e30

N_IN_SPLITS = 7
ROUTER_LANES = 128
EXPERT_LANE0 = N_GROUPS

VMEM_LIMIT = 56 * 1024 * 1024
BF16 = jnp.bfloat16
F32 = jnp.float32


def _rms(x, g):
    return x * lax.rsqrt(jnp.mean(x * x, axis=-1, keepdims=True) + EPS) * g


def _ada_kernel(c_ref, w_ref, b_ref, o_ref):
    c = c_ref[...]
    cond = c * jax.nn.sigmoid(c)
    o_ref[...] = jnp.dot(cond, w_ref[...], preferred_element_type=F32,
                         precision=lax.Precision.HIGHEST) + b_ref[...]


def _ada_call(c, w_ada, b_ada):
    B, D = c.shape
    N = w_ada.shape[1]
    tn = 1024
    return pl.pallas_call(
        _ada_kernel,
        grid=(N // tn,),
        in_specs=[pl.BlockSpec((B, D), lambda j: (0, 0)),
                  pl.BlockSpec((D, tn), lambda j: (0, j)),
                  pl.BlockSpec((1, tn), lambda j: (0, j))],
        out_specs=pl.BlockSpec((B, tn), lambda j: (0, j)),
        out_shape=jax.ShapeDtypeStruct((B, N), F32),
        compiler_params=pltpu.CompilerParams(dimension_semantics=("arbitrary",),
                                             vmem_limit_bytes=VMEM_LIMIT),
        name="adaln",
    )(c, w_ada, b_ada.reshape(1, N))


def _inproj_kernel(x_ref, shift_ref, scale_ref, n1g_ref, w_ref, sgug_ref, qg_ref, kg_ref,
                   u_ref, v_ref, q_ref, k_ref, va_ref, ga_ref, gb_ref):
    x = x_ref[...]
    h = _rms(x, n1g_ref[...]) * (1.0 + scale_ref[0]) + shift_ref[0]
    h = h.astype(BF16)

    def proj(i):
        return jnp.dot(h, w_ref[:, i * D_MODEL:(i + 1) * D_MODEL], preferred_element_type=F32)

    u_ref[...] = jax.nn.gelu(proj(0)).astype(BF16)
    v_ref[...] = _rms(jax.nn.gelu(proj(1)), sgug_ref[...]).astype(BF16)

    q = proj(2)
    qscale = ATT_HEAD_DIM ** -0.5
    for hd in range(ATT_HEADS):
        sl = slice(hd * ATT_HEAD_DIM, (hd + 1) * ATT_HEAD_DIM)
        q_ref[0, hd] = (_rms(q[:, sl], qg_ref[...]) * qscale).astype(BF16)
    k = proj(3)
    for hd in range(ATT_HEADS):
        sl = slice(hd * ATT_HEAD_DIM, (hd + 1) * ATT_HEAD_DIM)
        k_ref[0, hd] = _rms(k[:, sl], kg_ref[...]).astype(BF16)
    va = proj(4)
    for hd in range(ATT_HEADS):
        sl = slice(hd * ATT_HEAD_DIM, (hd + 1) * ATT_HEAD_DIM)
        va_ref[0, hd] = va[:, sl].astype(BF16)
    ga_ref[...] = jax.nn.sigmoid(proj(5)).astype(BF16)
    gb_ref[...] = jax.nn.sigmoid(proj(6)).astype(BF16)


def _inproj_call(x2, shift1, scale1, n1g, w_in, sgug, qg, kg, B, S, tm):
    T, D = x2.shape
    tpb = S // tm
    tok = pl.BlockSpec((tm, D), lambda i: (i, 0))
    per_b = pl.BlockSpec((1, 1, D), lambda i: (i // tpb, 0, 0))
    row = lambda n: pl.BlockSpec((1, n), lambda i: (0, 0))
    head = pl.BlockSpec((1, ATT_HEADS, tm, ATT_HEAD_DIM), lambda i: (i // tpb, 0, i % tpb, 0))
    tok_sds = jax.ShapeDtypeStruct((T, D), BF16)
    head_sds = jax.ShapeDtypeStruct((B, ATT_HEADS, S, ATT_HEAD_DIM), BF16)
    return pl.pallas_call(
        _inproj_kernel,
        grid=(T // tm,),
        in_specs=[tok, per_b, per_b, row(D),
                  pl.BlockSpec((D, N_IN_SPLITS * D), lambda i: (0, 0), pipeline_mode=pl.Buffered(1)),
                  row(D), row(ATT_HEAD_DIM), row(ATT_HEAD_DIM)],
        out_specs=[tok, tok, head, head, head, tok, tok],
        out_shape=[tok_sds, tok_sds, head_sds, head_sds, head_sds, tok_sds, tok_sds],
        compiler_params=pltpu.CompilerParams(dimension_semantics=("parallel",),
                                             vmem_limit_bytes=VMEM_LIMIT),
        name="inproj",
    )(x2, shift1, scale1, n1g, w_in, sgug, qg, kg)


ATT_BLK = 256


def _attention_bias(S):
    nblk = S // ATT_BLK
    a = np.arange(ATT_BLK)[:, None]
    col = np.arange(S)[None, :]
    delta = ATT_BLK * (nblk - 1 - col // ATT_BLK) + a - (col % ATT_BLK)
    count = np.zeros(delta.shape, np.int64)
    for window, dilation in DILATED_PATTERNS:
        count += (delta >= 0) & (delta <= window) & (delta % dilation == 0)
    bias = np.where(count > 0, np.log(np.maximum(count, 1)), NEG_INF)
    return jnp.asarray(bias, F32)


def _attention_kernel(q_ref, k_ref, v_ref, bias_ref, o_ref):
    S = q_ref.shape[2]
    nblk = S // ATT_BLK
    for i in range(nblk):
        nk = (i + 1) * ATT_BLK
        q = q_ref[0, 0, i * ATT_BLK:(i + 1) * ATT_BLK, :]
        s = lax.dot_general(q, k_ref[0, 0, :nk, :], (((1,), (1,)), ((), ())),
                            preferred_element_type=F32)
        s = s + bias_ref[:, (nblk - 1 - i) * ATT_BLK:]
        m = jnp.max(s, axis=-1, keepdims=True)
        p = jnp.exp(s - m)
        den = jnp.sum(p, axis=-1, keepdims=True)
        o = jnp.dot(p.astype(BF16), v_ref[0, 0, :nk, :], preferred_element_type=F32)
        o_ref[0, 0, i * ATT_BLK:(i + 1) * ATT_BLK, :] = (o / den).astype(BF16)


def _attention_call(q, k, v):
    B, H, S, Dh = q.shape
    blk = pl.BlockSpec((1, 1, S, Dh), lambda b, h: (b, h, 0, 0))
    return pl.pallas_call(
        _attention_kernel,
        grid=(B, H),
        in_specs=[blk, blk, blk, pl.BlockSpec((ATT_BLK, S), lambda b, h: (0, 0))],
        out_specs=blk,
        out_shape=jax.ShapeDtypeStruct((B, H, S, Dh), BF16),
        compiler_params=pltpu.CompilerParams(dimension_semantics=("parallel", "parallel"),
                                             vmem_limit_bytes=VMEM_LIMIT),
        name="attention",
    )(q, k, v, _attention_bias(S))


def _merge_kernel(x_ref, u_ref, v_ref, o_ref, ga_ref, gb_ref, gate1_ref, shift2_ref, scale2_ref,
                  n2g_ref, sw_ref, sb_ref, wa_ref, wb_ref, wo_ref, wr_ref, br_ref,
                  x1_ref, h2_ref, comb_ref, s_scr):
    tm = x_ref.shape[0]
    row = lax.broadcasted_iota(jnp.int32, (CHUNK, CHUNK), 0)
    colm = lax.broadcasted_iota(jnp.int32, (CHUNK, CHUNK), 1)
    causal = colm <= row
    for g in range(SGU_GROUPS):
        w = jnp.where(causal, sw_ref[g], 0.0).astype(BF16)
        gs = slice(g * CHUNK, (g + 1) * CHUNK)
        for c in range(tm // CHUNK):
            cs = slice(c * CHUNK, (c + 1) * CHUNK)
            mixed = jnp.dot(w, v_ref[cs, gs], preferred_element_type=F32) + sb_ref[:, gs]
            s_scr[cs, gs] = (u_ref[cs, gs].astype(F32) * mixed).astype(BF16)

    y_a = jnp.dot(s_scr[...], wa_ref[...], preferred_element_type=F32)
    o = jnp.concatenate([o_ref[0, hd] for hd in range(ATT_HEADS)], axis=-1)
    y_b = jnp.dot(o, wb_ref[...], preferred_element_type=F32)
    merged = ga_ref[...].astype(F32) * y_a + gb_ref[...].astype(F32) * y_b
    y = jnp.dot(merged.astype(BF16), wo_ref[...], preferred_element_type=F32)
    x1 = x_ref[...] + gate1_ref[0] * y
    x1_ref[...] = x1

    h2 = _rms(x1, n2g_ref[...]) * (1.0 + scale2_ref[0]) + shift2_ref[0]
    h2_ref[...] = h2.astype(BF16)

    logits = jnp.dot(h2, wr_ref[...], preferred_element_type=F32,
                     precision=lax.Precision.HIGHEST) + br_ref[...]
    lane = lax.broadcasted_iota(jnp.int32, logits.shape, 1)
    big = jnp.int32(ROUTER_LANES)

    def masked_top(vals, mask):
        top = jnp.max(jnp.where(mask, vals, -jnp.inf), axis=-1, keepdims=True)
        idx = jnp.min(jnp.where(mask & (vals == top), lane, big), axis=-1, keepdims=True)
        return top, idx

    gmask = lane < N_GROUPS
    gmax, gidx = masked_top(logits, gmask)
    p_group = 1.0 / jnp.sum(jnp.where(gmask, jnp.exp(logits - gmax), 0.0), axis=-1, keepdims=True)
    e_lo = EXPERT_LANE0 + gidx * EXPERTS_PER_GROUP
    emask = (lane >= e_lo) & (lane < e_lo + EXPERTS_PER_GROUP)
    v1, i1 = masked_top(logits, emask)
    v2, i2 = masked_top(logits, emask & (lane != i1))
    e21 = jnp.exp(v2 - v1)
    p1 = p_group / (1.0 + e21)
    p2 = p_group * e21 / (1.0 + e21)
    comb_ref[...] = jnp.where(lane == i1, p1, 0.0) + jnp.where(lane == i2, p2, 0.0)


def _merge_call(x2, u, v, o, ga, gb, gate1, shift2, scale2, n2g, sgu_w, sgu_bias, wa, wb, wo,
                w_router, b_router, B, S, tm):
    T, D = x2.shape
    tpb = S // tm
    tok = pl.BlockSpec((tm, D), lambda i: (i, 0))
    per_b = pl.BlockSpec((1, 1, D), lambda i: (i // tpb, 0, 0))
    head = pl.BlockSpec((1, ATT_HEADS, tm, ATT_HEAD_DIM), lambda i: (i // tpb, 0, i % tpb, 0))
    full = lambda *shape: pl.BlockSpec(shape, lambda i: (0,) * len(shape))
    return pl.pallas_call(
        _merge_kernel,
        grid=(T // tm,),
        in_specs=[tok, tok, tok, head, tok, tok, per_b, per_b, per_b, full(1, D),
                  full(SGU_GROUPS, CHUNK, CHUNK), full(CHUNK, D),
                  full(D, D), full(D, D), full(D, D), full(D, ROUTER_LANES), full(1, ROUTER_LANES)],
        out_specs=[tok, tok, pl.BlockSpec((tm, ROUTER_LANES), lambda i: (i, 0))],
        out_shape=[jax.ShapeDtypeStruct((T, D), F32), jax.ShapeDtypeStruct((T, D), BF16),
                   jax.ShapeDtypeStruct((T, ROUTER_LANES), F32)],
        scratch_shapes=[pltpu.VMEM((tm, D), BF16)],
        compiler_params=pltpu.CompilerParams(dimension_semantics=("parallel",),
                                             vmem_limit_bytes=VMEM_LIMIT),
        name="merge",
    )(x2, u, v, o, ga, gb, gate1, shift2, scale2, n2g, sgu_w, sgu_bias, wa, wb, wo,
      w_router, b_router)


def _moe_kernel(x1_ref, h2_ref, comb_ref, gate2_ref, wg_ref, wu_ref, wd_ref, o_ref, acc_ref):
    e = pl.program_id(1)

    @pl.when(e == 0)
    def _():
        acc_ref[...] = jnp.zeros_like(acc_ref)

    h2 = h2_ref[...]
    a = jnp.dot(h2, wg_ref[0], preferred_element_type=F32)
    b = jnp.dot(h2, wu_ref[0], preferred_element_type=F32)
    he = (a * jax.nn.sigmoid(a) * b).astype(BF16)
    ye = jnp.dot(he, wd_ref[0], preferred_element_type=F32)
    lane = lax.broadcasted_iota(jnp.int32, comb_ref.shape, 1)
    ce = jnp.sum(jnp.where(lane == e + EXPERT_LANE0, comb_ref[...], 0.0), axis=-1, keepdims=True)
    acc_ref[...] += ce * ye

    @pl.when(e == N_EXPERTS - 1)
    def _():
        o_ref[...] = x1_ref[...] + gate2_ref[0] * acc_ref[...]


def _moe_call(x1, h2, comb, gate2, wg, wu, wd, B, S, tm):
    T, D = x1.shape
    tpb = S // tm
    tok = pl.BlockSpec((tm, D), lambda i, e: (i, 0))
    return pl.pallas_call(
        _moe_kernel,
        grid=(T // tm, N_EXPERTS),
        in_specs=[tok, tok, pl.BlockSpec((tm, ROUTER_LANES), lambda i, e: (i, 0)),
                  pl.BlockSpec((1, 1, D), lambda i, e: (i // tpb, 0, 0)),
                  pl.BlockSpec((1, D, D_EXPERT), lambda i, e: (e, 0, 0)),
                  pl.BlockSpec((1, D, D_EXPERT), lambda i, e: (e, 0, 0)),
                  pl.BlockSpec((1, D_EXPERT, D), lambda i, e: (e, 0, 0))],
        out_specs=tok,
        out_shape=jax.ShapeDtypeStruct((T, D), F32),
        scratch_shapes=[pltpu.VMEM((tm, D), F32)],
        compiler_params=pltpu.CompilerParams(dimension_semantics=("parallel", "arbitrary"),
                                             vmem_limit_bytes=VMEM_LIMIT),
        name="moe",
    )(x1, h2, comb, gate2, wg, wu, wd)


def kernel(x, c, w_ada, b_ada, norm1_g, w_in, sgu_norm_g, sgu_w, sgu_b, q_norm_g, k_norm_g,
           w_proj_a, w_proj_b, w_out, norm2_g, w_router_group, b_router_group,
           w_router_expert, b_router_expert, w_gate, w_up, w_down):
    B, S, D = x.shape
    depth = w_ada.shape[0]
    x2 = x.reshape(B * S, D)
    for l in range(depth):
        mod = _ada_call(c, w_ada[l], b_ada[l])
        shift1, scale1, gate1, shift2, scale2, gate2 = [
            m.reshape(B, 1, D) for m in jnp.split(mod, 6, axis=-1)]

        u, v, q, k, va, ga, gb = _inproj_call(
            x2, shift1, scale1, norm1_g[l].reshape(1, D), w_in[l].astype(BF16),
            sgu_norm_g[l].reshape(1, D), q_norm_g[l].reshape(1, ATT_HEAD_DIM),
            k_norm_g[l].reshape(1, ATT_HEAD_DIM), B, S, tm=512)

        o = _attention_call(q, k, va)

        w_router = jnp.concatenate(
            [w_router_group[l],
             jnp.transpose(w_router_expert[l], (1, 0, 2)).reshape(D, N_EXPERTS)], axis=1)
        w_router = jnp.pad(w_router, ((0, 0), (0, ROUTER_LANES - w_router.shape[1])))
        b_router = jnp.concatenate([b_router_group[l], b_router_expert[l].reshape(N_EXPERTS)])
        b_router = jnp.pad(b_router, (0, ROUTER_LANES - b_router.shape[0])).reshape(1, ROUTER_LANES)
        sgu_bias = jnp.repeat(sgu_b[l].T, D // SGU_GROUPS, axis=1)

        x1, h2, comb = _merge_call(
            x2, u, v, o, ga, gb, gate1, shift2, scale2, norm2_g[l].reshape(1, D), sgu_w[l],
            sgu_bias, w_proj_a[l].astype(BF16), w_proj_b[l].astype(BF16), w_out[l].astype(BF16),
            w_router, b_router, B, S, tm=512)

        x2 = _moe_call(x1, h2, comb, gate2, w_gate[l].astype(BF16), w_up[l].astype(BF16),
                       w_down[l].astype(BF16), B, S, tm=1024)
    return x2.reshape(B, S, D)
```

```python
import jax
import jax.numpy as jnp
import numpy as np
from jax import lax
from jax.experimental import pallas as pl
from jax.experimental.pallas import tpu as pltpu

D_MODEL = 1024
CHUNK = 128
SGU_GROUPS = 8
ATT_HEADS = 8
ATT_HEAD_DIM = 128
DILATED_PATTERNS = ((128, 1), (512, 4), (2048, 16))
N_GROUPS = 4
EXPERTS_PER_GROUP = 4
N_EXPERTS = N_GROUPS * EXPERTS_PER_GROUP
D_EXPERT = 512
EPS = 1e-6
NEG_INF = -1e30

N_IN_SPLITS = 7
LANES = 128
SUBLANES = 8
EXPERT_LANE0 = N_GROUPS

PAIRS = ((0, 1), (0, 2), (0, 3), (1, 2), (1, 3), (2, 3))
N_CLASSES = N_GROUPS * len(PAIRS)
TOKEN_TILE = 512
SORT_ROWS = TOKEN_TILE + 3 * 64
ROW_WIDTH = D_MODEL + LANES
FFN_TILE = 256
INFO_DEST, INFO_PLO_H, INFO_PHI_H, INFO_PLO_L, INFO_PHI_L = 0, 1, 2, 3, 4

VMEM_LIMIT = 56 * 1024 * 1024
BF16 = jnp.bfloat16
F32 = jnp.float32

assert SORT_ROWS >= TOKEN_TILE + N_CLASSES * (SUBLANES - 1)


def _rms(x, g):
    return x * lax.rsqrt(jnp.mean(x * x, axis=-1, keepdims=True) + EPS) * g


def _ada_kernel(c_ref, w_ref, b_ref, o_ref):
    c = c_ref[...]
    cond = c * jax.nn.sigmoid(c)
    o_ref[...] = jnp.dot(cond, w_ref[...], preferred_element_type=F32,
                         precision=lax.Precision.HIGHEST) + b_ref[...]


def _ada_call(c, w_ada, b_ada):
    B, D = c.shape
    N = w_ada.shape[1]
    tn = 1024
    return pl.pallas_call(
        _ada_kernel,
        grid=(N // tn,),
        in_specs=[pl.BlockSpec((B, D), lambda j: (0, 0)),
                  pl.BlockSpec((D, tn), lambda j: (0, j)),
                  pl.BlockSpec((1, tn), lambda j: (0, j))],
        out_specs=pl.BlockSpec((B, tn), lambda j: (0, j)),
        out_shape=jax.ShapeDtypeStruct((B, N), F32),
        compiler_params=pltpu.CompilerParams(dimension_semantics=("arbitrary",),
                                             vmem_limit_bytes=VMEM_LIMIT),
        name="adaln",
    )(c, w_ada, b_ada.reshape(1, N))


def _inproj_kernel(x_ref, shift_ref, scale_ref, n1g_ref, w_ref, sgug_ref, qg_ref, kg_ref,
                   u_ref, v_ref, q_ref, k_ref, va_ref, ga_ref, gb_ref):
    x = x_ref[...]
    h = _rms(x, n1g_ref[...]) * (1.0 + scale_ref[0]) + shift_ref[0]
    h = h.astype(BF16)

    def proj(i):
        return jnp.dot(h, w_ref[:, i * D_MODEL:(i + 1) * D_MODEL], preferred_element_type=F32)

    u_ref[...] = jax.nn.gelu(proj(0)).astype(BF16)
    v_ref[...] = _rms(jax.nn.gelu(proj(1)), sgug_ref[...]).astype(BF16)

    q = proj(2)
    qscale = ATT_HEAD_DIM ** -0.5
    for hd in range(ATT_HEADS):
        sl = slice(hd * ATT_HEAD_DIM, (hd + 1) * ATT_HEAD_DIM)
        q_ref[0, hd] = (_rms(q[:, sl], qg_ref[...]) * qscale).astype(BF16)
    k = proj(3)
    for hd in range(ATT_HEADS):
        sl = slice(hd * ATT_HEAD_DIM, (hd + 1) * ATT_HEAD_DIM)
        k_ref[0, hd] = _rms(k[:, sl], kg_ref[...]).astype(BF16)
    va = proj(4)
    for hd in range(ATT_HEADS):
        sl = slice(hd * ATT_HEAD_DIM, (hd + 1) * ATT_HEAD_DIM)
        va_ref[0, hd] = va[:, sl].astype(BF16)
    ga_ref[...] = jax.nn.sigmoid(proj(5)).astype(BF16)
    gb_ref[...] = jax.nn.sigmoid(proj(6)).astype(BF16)


def _inproj_call(x2, shift1, scale1, n1g, w_in, sgug, qg, kg, B, S, tm):
    T, D = x2.shape
    tpb = S // tm
    tok = pl.BlockSpec((tm, D), lambda i: (i, 0))
    per_b = pl.BlockSpec((1, 1, D), lambda i: (i // tpb, 0, 0))
    row = lambda n: pl.BlockSpec((1, n), lambda i: (0, 0))
    head = pl.BlockSpec((1, ATT_HEADS, tm, ATT_HEAD_DIM), lambda i: (i // tpb, 0, i % tpb, 0))
    tok_sds = jax.ShapeDtypeStruct((T, D), BF16)
    head_sds = jax.ShapeDtypeStruct((B, ATT_HEADS, S, ATT_HEAD_DIM), BF16)
    return pl.pallas_call(
        _inproj_kernel,
        grid=(T // tm,),
        in_specs=[tok, per_b, per_b, row(D),
                  pl.BlockSpec((D, N_IN_SPLITS * D), lambda i: (0, 0), pipeline_mode=pl.Buffered(1)),
                  row(D), row(ATT_HEAD_DIM), row(ATT_HEAD_DIM)],
        out_specs=[tok, tok, head, head, head, tok, tok],
        out_shape=[tok_sds, tok_sds, head_sds, head_sds, head_sds, tok_sds, tok_sds],
        compiler_params=pltpu.CompilerParams(dimension_semantics=("parallel",),
                                             vmem_limit_bytes=VMEM_LIMIT),
        name="inproj",
    )(x2, shift1, scale1, n1g, w_in, sgug, qg, kg)


ATT_BLK = 256


def _attention_bias(S):
    nblk = S // ATT_BLK
    a = np.arange(ATT_BLK)[:, None]
    col = np.arange(S)[None, :]
    delta = ATT_BLK * (nblk - 1 - col // ATT_BLK) + a - (col % ATT_BLK)
    count = np.zeros(delta.shape, np.int64)
    for window, dilation in DILATED_PATTERNS:
        count += (delta >= 0) & (delta <= window) & (delta % dilation == 0)
    bias = np.where(count > 0, np.log(np.maximum(count, 1)), NEG_INF)
    return jnp.asarray(bias, F32)


def _attention_kernel(q_ref, k_ref, v_ref, bias_ref, o_ref):
    S = q_ref.shape[2]
    nblk = S // ATT_BLK
    for i in range(nblk):
        nk = (i + 1) * ATT_BLK
        q = q_ref[0, 0, i * ATT_BLK:(i + 1) * ATT_BLK, :]
        s = lax.dot_general(q, k_ref[0, 0, :nk, :], (((1,), (1,)), ((), ())),
                            preferred_element_type=F32)
        s = s + bias_ref[:, (nblk - 1 - i) * ATT_BLK:]
        m = jnp.max(s, axis=-1, keepdims=True)
        p = jnp.exp(s - m)
        den = jnp.sum(p, axis=-1, keepdims=True)
        o = jnp.dot(p.astype(BF16), v_ref[0, 0, :nk, :], preferred_element_type=F32)
        o_ref[0, 0, i * ATT_BLK:(i + 1) * ATT_BLK, :] = (o / den).astype(BF16)


def _attention_call(q, k, v):
    B, H, S, Dh = q.shape
    blk = pl.BlockSpec((1, 1, S, Dh), lambda b, h: (b, h, 0, 0))
    return pl.pallas_call(
        _attention_kernel,
        grid=(B, H),
        in_specs=[blk, blk, blk, pl.BlockSpec((ATT_BLK, S), lambda b, h: (0, 0))],
        out_specs=blk,
        out_shape=jax.ShapeDtypeStruct((B, H, S, Dh), BF16),
        compiler_params=pltpu.CompilerParams(dimension_semantics=("parallel", "parallel"),
                                             vmem_limit_bytes=VMEM_LIMIT),
        name="attention",
    )(q, k, v, _attention_bias(S))


def _route(logits):
    tm = logits.shape[0]
    lane = lax.broadcasted_iota(jnp.int32, logits.shape, 1)
    lanef = lane.astype(F32)

    def masked_top(mask):
        top = jnp.max(jnp.where(mask, logits, -jnp.inf), axis=-1, keepdims=True)
        idx = jnp.min(jnp.where(mask & (logits == top), lanef, float(LANES)), axis=-1, keepdims=True)
        return top, idx

    gmask = lane < N_GROUPS
    gmax, gidx = masked_top(gmask)
    p_group = 1.0 / jnp.sum(jnp.where(gmask, jnp.exp(logits - gmax), 0.0), axis=-1, keepdims=True)
    e_lo = EXPERT_LANE0 + gidx * EXPERTS_PER_GROUP
    emask = (lanef >= e_lo) & (lanef < e_lo + EXPERTS_PER_GROUP)
    v1, i1 = masked_top(emask)
    v2, i2 = masked_top(emask & (lanef != i1))
    e21 = jnp.exp(v2 - v1)
    p1 = p_group / (1.0 + e21)
    p2 = p_group * e21 / (1.0 + e21)
    first_is_lo = i1 < i2
    lo = jnp.minimum(i1, i2) - e_lo
    hi = jnp.maximum(i1, i2) - e_lo
    pair = lo * (7.0 - lo) * 0.5 + (hi - lo - 1.0)
    cls = gidx * float(len(PAIRS)) + pair
    p_lo = jnp.where(first_is_lo, p1, p2)
    p_hi = jnp.where(first_is_lo, p2, p1)

    onehot = lanef == cls
    onehot_b = jnp.where(onehot, 1.0, 0.0).astype(BF16)
    r = lax.broadcasted_iota(jnp.int32, (tm, tm), 0)
    c = lax.broadcasted_iota(jnp.int32, (tm, tm), 1)
    before = jnp.where(c < r, 1.0, 0.0).astype(BF16)
    rank = jnp.dot(before, onehot_b, preferred_element_type=F32)
    counts = jnp.sum(jnp.where(onehot, 1.0, 0.0), axis=0, keepdims=True)
    units = jnp.ceil(counts * (1.0 / SUBLANES))
    ur = lax.broadcasted_iota(jnp.int32, (LANES, LANES), 0)
    uc = lax.broadcasted_iota(jnp.int32, (LANES, LANES), 1)
    upper = jnp.where(ur < uc, 1.0, 0.0).astype(BF16)
    start = jnp.dot(jnp.broadcast_to(units, (SUBLANES, LANES)).astype(BF16), upper,
                    preferred_element_type=F32)[0:1] * float(SUBLANES)
    dest = jnp.sum(jnp.where(onehot, start + rank, 0.0), axis=-1, keepdims=True)

    def hi_part(p):
        return p.astype(BF16).astype(F32)

    info = jnp.where(lane == INFO_DEST, dest, 0.0)
    info = jnp.where(lane == INFO_PLO_H, hi_part(p_lo), info)
    info = jnp.where(lane == INFO_PHI_H, hi_part(p_hi), info)
    info = jnp.where(lane == INFO_PLO_L, p_lo - hi_part(p_lo), info)
    info = jnp.where(lane == INFO_PHI_L, p_hi - hi_part(p_hi), info)
    return info, counts


def _merge_kernel(x_ref, u_ref, v_ref, o_ref, ga_ref, gb_ref, gate1_ref, shift2_ref, scale2_ref,
                  n2g_ref, sw_ref, sb_ref, wa_ref, wb_ref, wo_ref, wr_ref, br_ref,
                  x1_ref, h2_ref, info_ref, counts_ref, s_scr):
    tm = x_ref.shape[0]
    row = lax.broadcasted_iota(jnp.int32, (CHUNK, CHUNK), 0)
    colm = lax.broadcasted_iota(jnp.int32, (CHUNK, CHUNK), 1)
    causal = colm <= row
    for g in range(SGU_GROUPS):
        w = jnp.where(causal, sw_ref[g], 0.0).astype(BF16)
        gs = slice(g * CHUNK, (g + 1) * CHUNK)
        for c in range(tm // CHUNK):
            cs = slice(c * CHUNK, (c + 1) * CHUNK)
            mixed = jnp.dot(w, v_ref[cs, gs], preferred_element_type=F32) + sb_ref[:, gs]
            s_scr[cs, gs] = (u_ref[cs, gs].astype(F32) * mixed).astype(BF16)

    y_a = jnp.dot(s_scr[...], wa_ref[...], preferred_element_type=F32)
    o = jnp.concatenate([o_ref[0, hd] for hd in range(ATT_HEADS)], axis=-1)
    y_b = jnp.dot(o, wb_ref[...], preferred_element_type=F32)
    merged = ga_ref[...].astype(F32) * y_a + gb_ref[...].astype(F32) * y_b
    y = jnp.dot(merged.astype(BF16), wo_ref[...], preferred_element_type=F32)
    x1 = x_ref[...] + gate1_ref[0] * y
    x1_ref[...] = x1

    h2 = _rms(x1, n2g_ref[...]) * (1.0 + scale2_ref[0]) + shift2_ref[0]
    h2_ref[...] = h2.astype(BF16)
    logits = jnp.dot(h2, wr_ref[...], preferred_element_type=F32,
                     precision=lax.Precision.HIGHEST) + br_ref[...]
    info, counts = _route(logits)
    info_ref[...] = info
    counts_ref[0] = counts


def _merge_call(x2, u, v, o, ga, gb, gate1, shift2, scale2, n2g, sgu_w, sgu_bias, wa, wb, wo,
                w_router, b_router, B, S):
    T, D = x2.shape
    tm = TOKEN_TILE
    tpb = S // tm
    tok = pl.BlockSpec((tm, D), lambda i: (i, 0))
    per_b = pl.BlockSpec((1, 1, D), lambda i: (i // tpb, 0, 0))
    head = pl.BlockSpec((1, ATT_HEADS, tm, ATT_HEAD_DIM), lambda i: (i // tpb, 0, i % tpb, 0))
    full = lambda *shape: pl.BlockSpec(shape, lambda i: (0,) * len(shape))
    return pl.pallas_call(
        _merge_kernel,
        grid=(T // tm,),
        in_specs=[tok, tok, tok, head, tok, tok, per_b, per_b, per_b, full(1, D),
                  full(SGU_GROUPS, CHUNK, CHUNK), full(CHUNK, D),
                  full(D, D), full(D, D), full(D, D), full(D, LANES), full(1, LANES)],
        out_specs=[tok, tok, pl.BlockSpec((tm, LANES), lambda i: (i, 0)),
                   pl.BlockSpec((1, 1, LANES), lambda i: (i, 0, 0))],
        out_shape=[jax.ShapeDtypeStruct((T, D), F32), jax.ShapeDtypeStruct((T, D), BF16),
                   jax.ShapeDtypeStruct((T, LANES), F32),
                   jax.ShapeDtypeStruct((T // tm, 1, LANES), F32)],
        scratch_shapes=[pltpu.VMEM((tm, D), BF16)],
        compiler_params=pltpu.CompilerParams(dimension_semantics=("parallel",),
                                             vmem_limit_bytes=VMEM_LIMIT),
        name="merge",
    )(x2, u, v, o, ga, gb, gate1, shift2, scale2, n2g, sgu_w, sgu_bias, wa, wb, wo,
      w_router, b_router)


def _dispatch_plan(counts, n_ffn_tiles):
    units = (counts + SUBLANES - 1) // SUBLANES
    rows = units * SUBLANES
    local_start = jnp.cumsum(rows, axis=1) - rows
    class_rows = jnp.sum(rows, axis=0)
    class_tiles = (class_rows + FFN_TILE - 1) // FFN_TILE
    cum_tiles = jnp.cumsum(class_tiles)
    seg_start = (cum_tiles - class_tiles) * FFN_TILE
    run_start = seg_start[None, :] + jnp.cumsum(rows, axis=0) - rows
    n_used = cum_tiles[-1]
    tile_idx = jnp.minimum(jnp.arange(n_ffn_tiles, dtype=jnp.int32), n_used - 1)
    tile_cls = jnp.sum((tile_idx[:, None] >= cum_tiles[None, :]).astype(jnp.int32), axis=1)
    pair = tile_cls % len(PAIRS)
    group = tile_cls // len(PAIRS)
    pairs = jnp.asarray(PAIRS, jnp.int32)
    i32 = lambda a: a.astype(jnp.int32).reshape(-1)
    return dict(
        run_start=i32(run_start), local_start=i32(local_start), units=i32(units),
        tile_units=i32(jnp.sum(units, axis=1)),
        tail_start=i32(seg_start + class_rows),
        tail_units=i32((class_tiles * FFN_TILE - class_rows) // SUBLANES),
        tile_idx=i32(tile_idx), n_used=i32(n_used),
        tile_ea=i32(group * EXPERTS_PER_GROUP + pairs[pair, 0]),
        tile_eb=i32(group * EXPERTS_PER_GROUP + pairs[pair, 1]))


def _run_pieces(tile, run_start, local_start, units, make_copy):
    def per_class(c, carry):
        k = tile * N_CLASSES + c
        g0 = run_start[k]
        l0 = local_start[k]

        def per_piece(u, carry2):
            off = u * SUBLANES
            make_copy(pl.multiple_of(l0 + off, SUBLANES), pl.multiple_of(g0 + off, SUBLANES)).start()
            return carry2

        return lax.fori_loop(0, units[k], per_piece, carry)

    lax.fori_loop(0, N_CLASSES, per_class, 0)


def _wait_pieces(n, make_copy):
    def body(_, carry):
        make_copy(0, 0).wait()
        return carry

    lax.fori_loop(0, n, body, 0)


def _dispatch_kernel(run_start, local_start, units, tile_units, tail_start, tail_units, n_used,
                     h2_ref, info_ref, hs_ref, sorted_scr, zero_scr, sems, tail_sem, unused_sem):
    i = pl.program_id(0)
    nt = pl.num_programs(0)
    slot = i % 2
    n_ffn_tiles = hs_ref.shape[0] // FFN_TILE

    def piece_copy(s):
        def make(local_row, global_row):
            return pltpu.make_async_copy(sorted_scr.at[s, pl.ds(local_row, SUBLANES), :],
                                         hs_ref.at[pl.ds(global_row, SUBLANES), :], sems.at[s])
        return make

    def tail_copy(global_row):
        return pltpu.make_async_copy(zero_scr.at[pl.ds(0, SUBLANES), :],
                                     hs_ref.at[pl.ds(global_row, SUBLANES), :], tail_sem)

    def unused_copy(t):
        return pltpu.make_async_copy(
            zero_scr, hs_ref.at[pl.ds(pl.multiple_of(t * FFN_TILE, FFN_TILE), FFN_TILE), :],
            unused_sem)

    def for_slot(fn):
        for s in range(2):
            pl.when(slot == s)(lambda s=s: fn(s))

    @pl.when(i >= 2)
    def _():
        for_slot(lambda s: _wait_pieces(tile_units[i - 2], piece_copy(s)))

    @pl.when(i == 0)
    def _():
        zero_scr[...] = jnp.zeros_like(zero_scr)

        def per_class(c, carry):
            def per_piece(u, carry2):
                tail_copy(pl.multiple_of(tail_start[c] + u * SUBLANES, SUBLANES)).start()
                return carry2
            return lax.fori_loop(0, tail_units[c], per_piece, carry)

        lax.fori_loop(0, N_CLASSES, per_class, 0)

        def per_unused(t, carry):
            unused_copy(t).start()
            return carry

        lax.fori_loop(n_used[0], n_ffn_tiles, per_unused, 0)

    info = info_ref[...]
    dest_row = info.T[INFO_DEST:INFO_DEST + 1, :]
    rows = lax.broadcasted_iota(jnp.int32, (SORT_ROWS, TOKEN_TILE), 0).astype(F32)
    perm = jnp.where(rows == dest_row, 1.0, 0.0).astype(BF16)
    lane = lax.broadcasted_iota(jnp.int32, info.shape, 1)
    weights = jnp.where(lane == INFO_DEST, 0.0, info).astype(BF16)

    def sort_and_send(s):
        sorted_scr[s, :, :D_MODEL] = jnp.dot(perm, h2_ref[...], preferred_element_type=F32)
        sorted_scr[s, :, D_MODEL:] = jnp.dot(perm, weights, preferred_element_type=F32)
        _run_pieces(i, run_start, local_start, units, piece_copy(s))

    for_slot(sort_and_send)

    @pl.when(i == nt - 1)
    def _():
        for_slot(lambda s: _wait_pieces(tile_units[i], piece_copy(s)))

        @pl.when(nt >= 2)
        def _():
            for_slot(lambda s: _wait_pieces(tile_units[i - 1], piece_copy(1 - s)))

        def per_class(c, carry):
            def per_piece(u, carry2):
                tail_copy(0).wait()
                return carry2
            return lax.fori_loop(0, tail_units[c], per_piece, carry)

        lax.fori_loop(0, N_CLASSES, per_class, 0)

        def per_unused(t, carry):
            unused_copy(0).wait()
            return carry

        lax.fori_loop(n_used[0], n_ffn_tiles, per_unused, 0)


def _dispatch_call(plan, h2, info, n_ffn_tiles):
    T, D = h2.shape
    tm = TOKEN_TILE
    grid_spec = pltpu.PrefetchScalarGridSpec(
        num_scalar_prefetch=7,
        grid=(T // tm,),
        in_specs=[pl.BlockSpec((tm, D), lambda i, *_: (i, 0)),
                  pl.BlockSpec((tm, LANES), lambda i, *_: (i, 0))],
        out_specs=pl.BlockSpec(memory_space=pl.ANY),
        scratch_shapes=[pltpu.VMEM((2, SORT_ROWS, ROW_WIDTH), F32),
                        pltpu.VMEM((FFN_TILE, ROW_WIDTH), F32),
                        pltpu.SemaphoreType.DMA((2,)),
                        pltpu.SemaphoreType.DMA(()),
                        pltpu.SemaphoreType.DMA(())])
    return pl.pallas_call(
        _dispatch_kernel,
        grid_spec=grid_spec,
        out_shape=jax.ShapeDtypeStruct((n_ffn_tiles * FFN_TILE, ROW_WIDTH), F32),
        compiler_params=pltpu.CompilerParams(dimension_semantics=("arbitrary",),
                                             vmem_limit_bytes=VMEM_LIMIT),
        name="dispatch",
    )(plan["run_start"], plan["local_start"], plan["units"], plan["tile_units"],
      plan["tail_start"], plan["tail_units"], plan["n_used"], h2, info)


def _ffn_kernel(tile_idx, tile_ea, tile_eb, n_used, hs_ref, wga_ref, wua_ref, wda_ref,
                wgb_ref, wub_ref, wdb_ref, ys_ref):
    @pl.when(pl.program_id(0) >= n_used[0])
    def _():
        ys_ref[...] = jnp.zeros_like(ys_ref)

    @pl.when(pl.program_id(0) < n_used[0])
    def _():
        h = hs_ref[:, :D_MODEL].astype(BF16)
        pv = hs_ref[:, D_MODEL:]
        p_lo = pv[:, INFO_PLO_H:INFO_PLO_H + 1] + pv[:, INFO_PLO_L:INFO_PLO_L + 1]
        p_hi = pv[:, INFO_PHI_H:INFO_PHI_H + 1] + pv[:, INFO_PHI_L:INFO_PHI_L + 1]

        def hidden(wg_ref, wu_ref, p):
            a = jnp.dot(h, wg_ref[0], preferred_element_type=F32)
            b = jnp.dot(h, wu_ref[0], preferred_element_type=F32)
            return (a * jax.nn.sigmoid(a) * b * p).astype(BF16)

        ys_ref[...] = (jnp.dot(hidden(wga_ref, wua_ref, p_lo), wda_ref[0], preferred_element_type=F32)
                       + jnp.dot(hidden(wgb_ref, wub_ref, p_hi), wdb_ref[0], preferred_element_type=F32))


def _ffn_call(plan, hs, wg, wu, wd, n_ffn_tiles):
    D = D_MODEL
    rows = pl.BlockSpec((FFN_TILE, ROW_WIDTH), lambda t, idx, ea, eb, n: (idx[t], 0))
    w_in_a = pl.BlockSpec((1, D, D_EXPERT), lambda t, idx, ea, eb, n: (ea[t], 0, 0))
    w_out_a = pl.BlockSpec((1, D_EXPERT, D), lambda t, idx, ea, eb, n: (ea[t], 0, 0))
    w_in_b = pl.BlockSpec((1, D, D_EXPERT), lambda t, idx, ea, eb, n: (eb[t], 0, 0))
    w_out_b = pl.BlockSpec((1, D_EXPERT, D), lambda t, idx, ea, eb, n: (eb[t], 0, 0))
    grid_spec = pltpu.PrefetchScalarGridSpec(
        num_scalar_prefetch=4,
        grid=(n_ffn_tiles,),
        in_specs=[rows, w_in_a, w_in_a, w_out_a, w_in_b, w_in_b, w_out_b],
        out_specs=pl.BlockSpec((FFN_TILE, D), lambda t, idx, ea, eb, n: (t, 0)))
    return pl.pallas_call(
        _ffn_kernel,
        grid_spec=grid_spec,
        out_shape=jax.ShapeDtypeStruct((n_ffn_tiles * FFN_TILE, D), F32),
        compiler_params=pltpu.CompilerParams(dimension_semantics=("arbitrary",),
                                             vmem_limit_bytes=VMEM_LIMIT),
        name="ffn",
    )(plan["tile_idx"], plan["tile_ea"], plan["tile_eb"], plan["n_used"],
      hs, wg, wu, wd, wg, wu, wd)


def _combine_kernel(run_start, local_start, units, tile_units,
                    x1_ref, info_ref, gate2_ref, ys_ref, o_ref, ybuf, sems):
    i = pl.program_id(0)
    nt = pl.num_programs(0)
    slot = i % 2

    def piece_copy(s):
        def make(local_row, global_row):
            return pltpu.make_async_copy(ys_ref.at[pl.ds(global_row, SUBLANES), :],
                                         ybuf.at[s, pl.ds(local_row, SUBLANES), :], sems.at[s])
        return make

    def for_slot(fn):
        for s in range(2):
            pl.when(slot == s)(lambda s=s: fn(s))

    def fetch(tile, s):
        _run_pieces(tile, run_start, local_start, units, piece_copy(s))

    @pl.when(i == 0)
    def _():
        ybuf[...] = jnp.zeros_like(ybuf)
        fetch(0, 0)

    @pl.when(i + 1 < nt)
    def _():
        for_slot(lambda s: fetch(i + 1, 1 - s))

    dest_col = info_ref[:, INFO_DEST:INFO_DEST + 1]
    cols = lax.broadcasted_iota(jnp.int32, (TOKEN_TILE, SORT_ROWS), 1).astype(F32)
    unperm = jnp.where(cols == dest_col, 1.0, 0.0).astype(BF16)

    def finish(s):
        _wait_pieces(tile_units[i], piece_copy(s))
        y = jnp.dot(unperm, ybuf[s].astype(BF16), preferred_element_type=F32)
        o_ref[...] = x1_ref[...] + gate2_ref[0] * y

    for_slot(finish)


def _combine_call(plan, x1, info, gate2, ys, B, S):
    T, D = x1.shape
    tm = TOKEN_TILE
    tpb = S // tm
    grid_spec = pltpu.PrefetchScalarGridSpec(
        num_scalar_prefetch=4,
        grid=(T // tm,),
        in_specs=[pl.BlockSpec((tm, D), lambda i, *_: (i, 0)),
                  pl.BlockSpec((tm, LANES), lambda i, *_: (i, 0)),
                  pl.BlockSpec((1, 1, D), lambda i, *_: (i // tpb, 0, 0)),
                  pl.BlockSpec(memory_space=pl.ANY)],
        out_specs=pl.BlockSpec((tm, D), lambda i, *_: (i, 0)),
        scratch_shapes=[pltpu.VMEM((2, SORT_ROWS, D), F32),
                        pltpu.SemaphoreType.DMA((2,))])
    return pl.pallas_call(
        _combine_kernel,
        grid_spec=grid_spec,
        out_shape=jax.ShapeDtypeStruct((T, D), F32),
        compiler_params=pltpu.CompilerParams(dimension_semantics=("arbitrary",),
                                             vmem_limit_bytes=VMEM_LIMIT),
        name="combine",
    )(plan["run_start"], plan["local_start"], plan["units"], plan["tile_units"],
      x1, info, gate2, ys)


def kernel(x, c, w_ada, b_ada, norm1_g, w_in, sgu_norm_g, sgu_w, sgu_b, q_norm_g, k_norm_g,
           w_proj_a, w_proj_b, w_out, norm2_g, w_router_group, b_router_group,
           w_router_expert, b_router_expert, w_gate, w_up, w_down):
    B, S, D = x.shape
    T = B * S
    depth = w_ada.shape[0]
    n_token_tiles = T // TOKEN_TILE
    max_rows = T + n_token_tiles * N_CLASSES * (SUBLANES - 1) + N_CLASSES * (FFN_TILE - SUBLANES)
    n_ffn_tiles = -(-max_rows // FFN_TILE)
    x2 = x.reshape(T, D)
    for l in range(depth):
        mod = _ada_call(c, w_ada[l], b_ada[l])
        shift1, scale1, gate1, shift2, scale2, gate2 = [
            m.reshape(B, 1, D) for m in jnp.split(mod, 6, axis=-1)]

        u, v, q, k, va, ga, gb = _inproj_call(
            x2, shift1, scale1, norm1_g[l].reshape(1, D), w_in[l].astype(BF16),
            sgu_norm_g[l].reshape(1, D), q_norm_g[l].reshape(1, ATT_HEAD_DIM),
            k_norm_g[l].reshape(1, ATT_HEAD_DIM), B, S, tm=512)

        o = _attention_call(q, k, va)

        w_router = jnp.concatenate(
            [w_router_group[l],
             jnp.transpose(w_router_expert[l], (1, 0, 2)).reshape(D, N_EXPERTS)], axis=1)
        w_router = jnp.pad(w_router, ((0, 0), (0, LANES - w_router.shape[1])))
        b_router = jnp.concatenate([b_router_group[l], b_router_expert[l].reshape(N_EXPERTS)])
        b_router = jnp.pad(b_router, (0, LANES - b_router.shape[0])).reshape(1, LANES)
        sgu_bias = jnp.repeat(sgu_b[l].T, D // SGU_GROUPS, axis=1)

        x1, h2, info, counts = _merge_call(
            x2, u, v, o, ga, gb, gate1, shift2, scale2, norm2_g[l].reshape(1, D), sgu_w[l],
            sgu_bias, w_proj_a[l].astype(BF16), w_proj_b[l].astype(BF16), w_out[l].astype(BF16),
            w_router, b_router, B, S)

        plan = _dispatch_plan(counts[:, 0, :N_CLASSES].astype(jnp.int32), n_ffn_tiles)
        hs = _dispatch_call(plan, h2, info, n_ffn_tiles)
        ys = _ffn_call(plan, hs, w_gate[l].astype(BF16), w_up[l].astype(BF16),
                       w_down[l].astype(BF16), n_ffn_tiles)
        x2 = _combine_call(plan, x1, info, gate2, ys, B, S)
    return x2.reshape(B, S, D)
```

```python
import jax
import jax.numpy as jnp
import numpy as np
from jax import lax
from jax.experimental import pallas as pl
from jax.experimental.pallas import tpu as pltpu

D_MODEL = 1024
CHUNK = 128
SGU_GROUPS = 8
ATT_HEADS = 8
ATT_HEAD_DIM = 128
DILATED_PATTERNS = ((128, 1), (512, 4), (2048, 16))
N_GROUPS = 4
EXPERTS_PER_GROUP = 4
N_EXPERTS = N_GROUPS * EXPERTS_PER_GROUP
D_EXPERT = 512
EPS = 1e-6
NEG_INF = -1e30

N_IN_SPLITS = 7
LANES = 128
SUBLANES = 8
EXPERT_LANE0 = N_GROUPS

PAIRS = ((0, 1), (0, 2), (0, 3), (1, 2), (1, 3), (2, 3))
N_CLASSES = N_GROUPS * len(PAIRS)
TOKEN_TILE = 512
SORT_ROWS = TOKEN_TILE + 3 * 64
ROW_WIDTH = D_MODEL + LANES
FFN_TILE = 256
INFO_DEST, INFO_PLO_H, INFO_PHI_H, INFO_PLO_L, INFO_PHI_L = 0, 1, 2, 3, 4

VMEM_LIMIT = 56 * 1024 * 1024
BF16 = jnp.bfloat16
F32 = jnp.float32

assert SORT_ROWS >= TOKEN_TILE + N_CLASSES * (SUBLANES - 1)


def _rms(x, g):
    return x * lax.rsqrt(jnp.mean(x * x, axis=-1, keepdims=True) + EPS) * g


def _ada_kernel(c_ref, w_ref, b_ref, o_ref):
    c = c_ref[...]
    cond = c * jax.nn.sigmoid(c)
    o_ref[...] = jnp.dot(cond, w_ref[...], preferred_element_type=F32,
                         precision=lax.Precision.HIGHEST) + b_ref[...]


def _ada_call(c, w_ada, b_ada):
    B, D = c.shape
    N = w_ada.shape[1]
    tn = 1024
    return pl.pallas_call(
        _ada_kernel,
        grid=(N // tn,),
        in_specs=[pl.BlockSpec((B, D), lambda j: (0, 0)),
                  pl.BlockSpec((D, tn), lambda j: (0, j)),
                  pl.BlockSpec((1, tn), lambda j: (0, j))],
        out_specs=pl.BlockSpec((B, tn), lambda j: (0, j)),
        out_shape=jax.ShapeDtypeStruct((B, N), F32),
        compiler_params=pltpu.CompilerParams(dimension_semantics=("arbitrary",),
                                             vmem_limit_bytes=VMEM_LIMIT),
        name="adaln",
    )(c, w_ada, b_ada.reshape(1, N))


def _inproj_kernel(x_ref, shift_ref, scale_ref, n1g_ref, w_ref, sgug_ref, qg_ref, kg_ref,
                   u_ref, v_ref, q_ref, k_ref, va_ref, ga_ref, gb_ref):
    x = x_ref[...]
    h = _rms(x, n1g_ref[...]) * (1.0 + scale_ref[0]) + shift_ref[0]
    h = h.astype(BF16)

    def proj(i):
        return jnp.dot(h, w_ref[:, i * D_MODEL:(i + 1) * D_MODEL], preferred_element_type=F32)

    u_ref[...] = jax.nn.gelu(proj(0)).astype(BF16)
    v_ref[...] = _rms(jax.nn.gelu(proj(1)), sgug_ref[...]).astype(BF16)

    q = proj(2)
    qscale = ATT_HEAD_DIM ** -0.5
    for hd in range(ATT_HEADS):
        sl = slice(hd * ATT_HEAD_DIM, (hd + 1) * ATT_HEAD_DIM)
        q_ref[0, hd] = (_rms(q[:, sl], qg_ref[...]) * qscale).astype(BF16)
    k = proj(3)
    for hd in range(ATT_HEADS):
        sl = slice(hd * ATT_HEAD_DIM, (hd + 1) * ATT_HEAD_DIM)
        k_ref[0, hd] = _rms(k[:, sl], kg_ref[...]).astype(BF16)
    va = proj(4)
    for hd in range(ATT_HEADS):
        sl = slice(hd * ATT_HEAD_DIM, (hd + 1) * ATT_HEAD_DIM)
        va_ref[0, hd] = va[:, sl].astype(BF16)
    ga_ref[...] = jax.nn.sigmoid(proj(5)).astype(BF16)
    gb_ref[...] = jax.nn.sigmoid(proj(6)).astype(BF16)


def _inproj_call(x2, shift1, scale1, n1g, w_in, sgug, qg, kg, B, S, tm):
    T, D = x2.shape
    tpb = S // tm
    tok = pl.BlockSpec((tm, D), lambda i: (i, 0))
    per_b = pl.BlockSpec((1, 1, D), lambda i: (i // tpb, 0, 0))
    row = lambda n: pl.BlockSpec((1, n), lambda i: (0, 0))
    head = pl.BlockSpec((1, ATT_HEADS, tm, ATT_HEAD_DIM), lambda i: (i // tpb, 0, i % tpb, 0))
    tok_sds = jax.ShapeDtypeStruct((T, D), BF16)
    head_sds = jax.ShapeDtypeStruct((B, ATT_HEADS, S, ATT_HEAD_DIM), BF16)
    return pl.pallas_call(
        _inproj_kernel,
        grid=(T // tm,),
        in_specs=[tok, per_b, per_b, row(D),
                  pl.BlockSpec((D, N_IN_SPLITS * D), lambda i: (0, 0), pipeline_mode=pl.Buffered(1)),
                  row(D), row(ATT_HEAD_DIM), row(ATT_HEAD_DIM)],
        out_specs=[tok, tok, head, head, head, tok, tok],
        out_shape=[tok_sds, tok_sds, head_sds, head_sds, head_sds, tok_sds, tok_sds],
        compiler_params=pltpu.CompilerParams(dimension_semantics=("parallel",),
                                             vmem_limit_bytes=VMEM_LIMIT),
        name="inproj",
    )(x2, shift1, scale1, n1g, w_in, sgug, qg, kg)


ATT_BLK = 256


def _attention_bias(S):
    nblk = S // ATT_BLK
    a = np.arange(ATT_BLK)[:, None]
    col = np.arange(S)[None, :]
    delta = ATT_BLK * (nblk - 1 - col // ATT_BLK) + a - (col % ATT_BLK)
    count = np.zeros(delta.shape, np.int64)
    for window, dilation in DILATED_PATTERNS:
        count += (delta >= 0) & (delta <= window) & (delta % dilation == 0)
    bias = np.where(count > 0, np.log(np.maximum(count, 1)), NEG_INF)
    return jnp.asarray(bias, F32)


def _attention_kernel(q_ref, k_ref, v_ref, bias_ref, o_ref):
    S = q_ref.shape[2]
    nblk = S // ATT_BLK
    for i in range(nblk):
        nk = (i + 1) * ATT_BLK
        q = q_ref[0, 0, i * ATT_BLK:(i + 1) * ATT_BLK, :]
        s = lax.dot_general(q, k_ref[0, 0, :nk, :], (((1,), (1,)), ((), ())),
                            preferred_element_type=F32)
        s = s + bias_ref[:, (nblk - 1 - i) * ATT_BLK:]
        m = jnp.max(s, axis=-1, keepdims=True)
        p = jnp.exp(s - m)
        den = jnp.sum(p, axis=-1, keepdims=True)
        o = jnp.dot(p.astype(BF16), v_ref[0, 0, :nk, :], preferred_element_type=F32)
        o_ref[0, 0, i * ATT_BLK:(i + 1) * ATT_BLK, :] = (o / den).astype(BF16)


def _attention_call(q, k, v):
    B, H, S, Dh = q.shape
    blk = pl.BlockSpec((1, 1, S, Dh), lambda b, h: (b, h, 0, 0))
    return pl.pallas_call(
        _attention_kernel,
        grid=(B, H),
        in_specs=[blk, blk, blk, pl.BlockSpec((ATT_BLK, S), lambda b, h: (0, 0))],
        out_specs=blk,
        out_shape=jax.ShapeDtypeStruct((B, H, S, Dh), BF16),
        compiler_params=pltpu.CompilerParams(dimension_semantics=("parallel", "parallel"),
                                             vmem_limit_bytes=VMEM_LIMIT),
        name="attention",
    )(q, k, v, _attention_bias(S))


def _route(logits):
    tm = logits.shape[0]
    lane = lax.broadcasted_iota(jnp.int32, logits.shape, 1)
    lanef = lane.astype(F32)

    def masked_top(mask):
        top = jnp.max(jnp.where(mask, logits, -jnp.inf), axis=-1, keepdims=True)
        idx = jnp.min(jnp.where(mask & (logits == top), lanef, float(LANES)), axis=-1, keepdims=True)
        return top, idx

    gmask = lane < N_GROUPS
    gmax, gidx = masked_top(gmask)
    p_group = 1.0 / jnp.sum(jnp.where(gmask, jnp.exp(logits - gmax), 0.0), axis=-1, keepdims=True)
    e_lo = EXPERT_LANE0 + gidx * EXPERTS_PER_GROUP
    emask = (lanef >= e_lo) & (lanef < e_lo + EXPERTS_PER_GROUP)
    v1, i1 = masked_top(emask)
    v2, i2 = masked_top(emask & (lanef != i1))
    e21 = jnp.exp(v2 - v1)
    p1 = p_group / (1.0 + e21)
    p2 = p_group * e21 / (1.0 + e21)
    first_is_lo = i1 < i2
    lo = jnp.minimum(i1, i2) - e_lo
    hi = jnp.maximum(i1, i2) - e_lo
    pair = lo * (7.0 - lo) * 0.5 + (hi - lo - 1.0)
    cls = gidx * float(len(PAIRS)) + pair
    p_lo = jnp.where(first_is_lo, p1, p2)
    p_hi = jnp.where(first_is_lo, p2, p1)

    onehot = lanef == cls
    onehot_b = jnp.where(onehot, 1.0, 0.0).astype(BF16)
    r = lax.broadcasted_iota(jnp.int32, (tm, tm), 0)
    c = lax.broadcasted_iota(jnp.int32, (tm, tm), 1)
    before = jnp.where(c < r, 1.0, 0.0).astype(BF16)
    rank = jnp.dot(before, onehot_b, preferred_element_type=F32)
    counts = jnp.sum(jnp.where(onehot, 1.0, 0.0), axis=0, keepdims=True)
    units = jnp.ceil(counts * (1.0 / SUBLANES))
    ur = lax.broadcasted_iota(jnp.int32, (LANES, LANES), 0)
    uc = lax.broadcasted_iota(jnp.int32, (LANES, LANES), 1)
    upper = jnp.where(ur < uc, 1.0, 0.0).astype(BF16)
    start = jnp.dot(jnp.broadcast_to(units, (SUBLANES, LANES)).astype(BF16), upper,
                    preferred_element_type=F32)[0:1] * float(SUBLANES)
    dest = jnp.sum(jnp.where(onehot, start + rank, 0.0), axis=-1, keepdims=True)

    def hi_part(p):
        return p.astype(BF16).astype(F32)

    info = jnp.where(lane == INFO_DEST, dest, 0.0)
    info = jnp.where(lane == INFO_PLO_H, hi_part(p_lo), info)
    info = jnp.where(lane == INFO_PHI_H, hi_part(p_hi), info)
    info = jnp.where(lane == INFO_PLO_L, p_lo - hi_part(p_lo), info)
    info = jnp.where(lane == INFO_PHI_L, p_hi - hi_part(p_hi), info)
    return info, counts


def _merge_kernel(x_ref, u_ref, v_ref, o_ref, ga_ref, gb_ref, gate1_ref, shift2_ref, scale2_ref,
                  n2g_ref, sw_ref, sb_ref, wa_ref, wb_ref, wo_ref, wr_ref, br_ref,
                  x1_ref, h2_ref, info_ref, counts_ref, s_scr):
    tm = x_ref.shape[0]
    row = lax.broadcasted_iota(jnp.int32, (CHUNK, CHUNK), 0)
    colm = lax.broadcasted_iota(jnp.int32, (CHUNK, CHUNK), 1)
    causal = colm <= row
    for g in range(SGU_GROUPS):
        w = jnp.where(causal, sw_ref[g], 0.0).astype(BF16)
        gs = slice(g * CHUNK, (g + 1) * CHUNK)
        for c in range(tm // CHUNK):
            cs = slice(c * CHUNK, (c + 1) * CHUNK)
            mixed = jnp.dot(w, v_ref[cs, gs], preferred_element_type=F32) + sb_ref[:, gs]
            s_scr[cs, gs] = (u_ref[cs, gs].astype(F32) * mixed).astype(BF16)

    y_a = jnp.dot(s_scr[...], wa_ref[...], preferred_element_type=F32)
    o = jnp.concatenate([o_ref[0, hd] for hd in range(ATT_HEADS)], axis=-1)
    y_b = jnp.dot(o, wb_ref[...], preferred_element_type=F32)
    merged = ga_ref[...].astype(F32) * y_a + gb_ref[...].astype(F32) * y_b
    y = jnp.dot(merged.astype(BF16), wo_ref[...], preferred_element_type=F32)
    x1 = x_ref[...] + gate1_ref[0] * y
    x1_ref[...] = x1

    h2 = _rms(x1, n2g_ref[...]) * (1.0 + scale2_ref[0]) + shift2_ref[0]
    h2_ref[...] = h2.astype(BF16)
    h_hi = h2.astype(BF16)
    h_lo = (h2 - h_hi.astype(F32)).astype(BF16)
    r_hi = jnp.dot(h_hi, wr_ref[...], preferred_element_type=F32)
    r_lo = jnp.dot(h_lo, wr_ref[:, :LANES], preferred_element_type=F32)
    logits = r_hi[:, :LANES] + r_hi[:, LANES:] + r_lo + br_ref[...]
    info, counts = _route(logits)
    info_ref[...] = info
    counts_ref[0] = counts


def _merge_call(x2, u, v, o, ga, gb, gate1, shift2, scale2, n2g, sgu_w, sgu_bias, wa, wb, wo,
                w_router, b_router, B, S):
    T, D = x2.shape
    tm = TOKEN_TILE
    tpb = S // tm
    tok = pl.BlockSpec((tm, D), lambda i: (i, 0))
    per_b = pl.BlockSpec((1, 1, D), lambda i: (i // tpb, 0, 0))
    head = pl.BlockSpec((1, ATT_HEADS, tm, ATT_HEAD_DIM), lambda i: (i // tpb, 0, i % tpb, 0))
    full = lambda *shape: pl.BlockSpec(shape, lambda i: (0,) * len(shape))
    return pl.pallas_call(
        _merge_kernel,
        grid=(T // tm,),
        in_specs=[tok, tok, tok, head, tok, tok, per_b, per_b, per_b, full(1, D),
                  full(SGU_GROUPS, CHUNK, CHUNK), full(CHUNK, D),
                  full(D, D), full(D, D), full(D, D), full(D, 2 * LANES), full(1, LANES)],
        out_specs=[tok, tok, pl.BlockSpec((tm, LANES), lambda i: (i, 0)),
                   pl.BlockSpec((1, 1, LANES), lambda i: (i, 0, 0))],
        out_shape=[jax.ShapeDtypeStruct((T, D), F32), jax.ShapeDtypeStruct((T, D), BF16),
                   jax.ShapeDtypeStruct((T, LANES), F32),
                   jax.ShapeDtypeStruct((T // tm, 1, LANES), F32)],
        scratch_shapes=[pltpu.VMEM((tm, D), BF16)],
        compiler_params=pltpu.CompilerParams(dimension_semantics=("parallel",),
                                             vmem_limit_bytes=VMEM_LIMIT),
        name="merge",
    )(x2, u, v, o, ga, gb, gate1, shift2, scale2, n2g, sgu_w, sgu_bias, wa, wb, wo,
      w_router, b_router)


def _dispatch_plan(counts, n_ffn_tiles):
    units = (counts + SUBLANES - 1) // SUBLANES
    rows = units * SUBLANES
    local_start = jnp.cumsum(rows, axis=1) - rows
    class_rows = jnp.sum(rows, axis=0)
    class_tiles = (class_rows + FFN_TILE - 1) // FFN_TILE
    cum_tiles = jnp.cumsum(class_tiles)
    seg_start = (cum_tiles - class_tiles) * FFN_TILE
    run_start = seg_start[None, :] + jnp.cumsum(rows, axis=0) - rows
    n_used = cum_tiles[-1]
    tile_idx = jnp.maximum(jnp.minimum(jnp.arange(n_ffn_tiles, dtype=jnp.int32), n_used - 1), 0)
    tile_cls = jnp.sum((tile_idx[:, None] >= cum_tiles[None, :]).astype(jnp.int32), axis=1)
    pair = tile_cls % len(PAIRS)
    group = tile_cls // len(PAIRS)
    pairs = jnp.asarray(PAIRS, jnp.int32)
    i32 = lambda a: a.astype(jnp.int32).reshape(-1)
    return dict(
        run_start=i32(run_start), local_start=i32(local_start), units=i32(units),
        tile_units=i32(jnp.sum(units, axis=1)),
        tail_start=i32(seg_start + class_rows),
        tail_units=i32((class_tiles * FFN_TILE - class_rows) // SUBLANES),
        tile_idx=i32(tile_idx), n_used=i32(n_used),
        tile_ea=i32(group * EXPERTS_PER_GROUP + pairs[pair, 0]),
        tile_eb=i32(group * EXPERTS_PER_GROUP + pairs[pair, 1]))


def _run_pieces(tile, run_start, local_start, units, make_copy):
    def per_class(c, carry):
        k = tile * N_CLASSES + c
        g0 = run_start[k]
        l0 = local_start[k]

        def per_piece(u, carry2):
            off = u * SUBLANES
            make_copy(pl.multiple_of(l0 + off, SUBLANES), pl.multiple_of(g0 + off, SUBLANES)).start()
            return carry2

        return lax.fori_loop(0, units[k], per_piece, carry)

    lax.fori_loop(0, N_CLASSES, per_class, 0)


def _wait_pieces(n, make_copy):
    def body(_, carry):
        make_copy(0, 0).wait()
        return carry

    lax.fori_loop(0, n, body, 0)


def _dispatch_kernel(run_start, local_start, units, tile_units, tail_start, tail_units, n_used,
                     h2_ref, info_ref, hs_ref, sorted_scr, zero_scr, sems, tail_sem, unused_sem):
    i = pl.program_id(0)
    nt = pl.num_programs(0)
    slot = i % 2
    n_ffn_tiles = hs_ref.shape[0] // FFN_TILE

    def piece_copy(s):
        def make(local_row, global_row):
            return pltpu.make_async_copy(sorted_scr.at[s, pl.ds(local_row, SUBLANES), :],
                                         hs_ref.at[pl.ds(global_row, SUBLANES), :], sems.at[s])
        return make

    def tail_copy(global_row):
        return pltpu.make_async_copy(zero_scr.at[pl.ds(0, SUBLANES), :],
                                     hs_ref.at[pl.ds(global_row, SUBLANES), :], tail_sem)

    def unused_copy(t):
        return pltpu.make_async_copy(
            zero_scr, hs_ref.at[pl.ds(pl.multiple_of(t * FFN_TILE, FFN_TILE), FFN_TILE), :],
            unused_sem)

    def for_slot(fn):
        for s in range(2):
            pl.when(slot == s)(lambda s=s: fn(s))

    @pl.when(i >= 2)
    def _():
        for_slot(lambda s: _wait_pieces(tile_units[i - 2], piece_copy(s)))

    @pl.when(i == 0)
    def _():
        zero_scr[...] = jnp.zeros_like(zero_scr)

        def per_class(c, carry):
            def per_piece(u, carry2):
                tail_copy(pl.multiple_of(tail_start[c] + u * SUBLANES, SUBLANES)).start()
                return carry2
            return lax.fori_loop(0, tail_units[c], per_piece, carry)

        lax.fori_loop(0, N_CLASSES, per_class, 0)

        def per_unused(t, carry):
            unused_copy(t).start()
            return carry

        lax.fori_loop(n_used[0], n_ffn_tiles, per_unused, 0)

    info = info_ref[...]
    dest_row = info.T[INFO_DEST:INFO_DEST + 1, :]
    rows = lax.broadcasted_iota(jnp.int32, (SORT_ROWS, TOKEN_TILE), 0).astype(F32)
    perm = jnp.where(rows == dest_row, 1.0, 0.0).astype(BF16)
    lane = lax.broadcasted_iota(jnp.int32, info.shape, 1)
    weights = jnp.where(lane == INFO_DEST, 0.0, info).astype(BF16)

    def sort_and_send(s):
        sorted_scr[s, :, :D_MODEL] = jnp.dot(perm, h2_ref[...], preferred_element_type=F32)
        sorted_scr[s, :, D_MODEL:] = jnp.dot(perm, weights, preferred_element_type=F32)
        _run_pieces(i, run_start, local_start, units, piece_copy(s))

    for_slot(sort_and_send)

    @pl.when(i == nt - 1)
    def _():
        for_slot(lambda s: _wait_pieces(tile_units[i], piece_copy(s)))

        @pl.when(nt >= 2)
        def _():
            for_slot(lambda s: _wait_pieces(tile_units[i - 1], piece_copy(1 - s)))

        def per_class(c, carry):
            def per_piece(u, carry2):
                tail_copy(0).wait()
                return carry2
            return lax.fori_loop(0, tail_units[c], per_piece, carry)

        lax.fori_loop(0, N_CLASSES, per_class, 0)

        def per_unused(t, carry):
            unused_copy(0).wait()
            return carry

        lax.fori_loop(n_used[0], n_ffn_tiles, per_unused, 0)


def _dispatch_call(plan, h2, info, n_ffn_tiles):
    T, D = h2.shape
    tm = TOKEN_TILE
    grid_spec = pltpu.PrefetchScalarGridSpec(
        num_scalar_prefetch=7,
        grid=(T // tm,),
        in_specs=[pl.BlockSpec((tm, D), lambda i, *_: (i, 0)),
                  pl.BlockSpec((tm, LANES), lambda i, *_: (i, 0))],
        out_specs=pl.BlockSpec(memory_space=pl.ANY),
        scratch_shapes=[pltpu.VMEM((2, SORT_ROWS, ROW_WIDTH), F32),
                        pltpu.VMEM((FFN_TILE, ROW_WIDTH), F32),
                        pltpu.SemaphoreType.DMA((2,)),
                        pltpu.SemaphoreType.DMA(()),
                        pltpu.SemaphoreType.DMA(())])
    return pl.pallas_call(
        _dispatch_kernel,
        grid_spec=grid_spec,
        out_shape=jax.ShapeDtypeStruct((n_ffn_tiles * FFN_TILE, ROW_WIDTH), F32),
        compiler_params=pltpu.CompilerParams(dimension_semantics=("arbitrary",),
                                             vmem_limit_bytes=VMEM_LIMIT),
        name="dispatch",
    )(plan["run_start"], plan["local_start"], plan["units"], plan["tile_units"],
      plan["tail_start"], plan["tail_units"], plan["n_used"], h2, info)


def _ffn_kernel(tile_idx, tile_ea, tile_eb, n_used, hs_ref, wga_ref, wua_ref, wda_ref,
                wgb_ref, wub_ref, wdb_ref, ys_ref):
    @pl.when(pl.program_id(0) >= n_used[0])
    def _():
        ys_ref[...] = jnp.zeros_like(ys_ref)

    @pl.when(pl.program_id(0) < n_used[0])
    def _():
        h = hs_ref[:, :D_MODEL].astype(BF16)
        pv = hs_ref[:, D_MODEL:]
        p_lo = pv[:, INFO_PLO_H:INFO_PLO_H + 1] + pv[:, INFO_PLO_L:INFO_PLO_L + 1]
        p_hi = pv[:, INFO_PHI_H:INFO_PHI_H + 1] + pv[:, INFO_PHI_L:INFO_PHI_L + 1]

        def hidden(wg_ref, wu_ref, p):
            a = jnp.dot(h, wg_ref[0], preferred_element_type=F32)
            b = jnp.dot(h, wu_ref[0], preferred_element_type=F32)
            return (a * jax.nn.sigmoid(a) * b * p).astype(BF16)

        ys_ref[...] = (jnp.dot(hidden(wga_ref, wua_ref, p_lo), wda_ref[0], preferred_element_type=F32)
                       + jnp.dot(hidden(wgb_ref, wub_ref, p_hi), wdb_ref[0], preferred_element_type=F32))


def _ffn_call(plan, hs, wg, wu, wd, n_ffn_tiles):
    D = D_MODEL
    rows = pl.BlockSpec((FFN_TILE, ROW_WIDTH), lambda t, idx, ea, eb, n: (idx[t], 0))
    w_in_a = pl.BlockSpec((1, D, D_EXPERT), lambda t, idx, ea, eb, n: (ea[t], 0, 0))
    w_out_a = pl.BlockSpec((1, D_EXPERT, D), lambda t, idx, ea, eb, n: (ea[t], 0, 0))
    w_in_b = pl.BlockSpec((1, D, D_EXPERT), lambda t, idx, ea, eb, n: (eb[t], 0, 0))
    w_out_b = pl.BlockSpec((1, D_EXPERT, D), lambda t, idx, ea, eb, n: (eb[t], 0, 0))
    grid_spec = pltpu.PrefetchScalarGridSpec(
        num_scalar_prefetch=4,
        grid=(n_ffn_tiles,),
        in_specs=[rows, w_in_a, w_in_a, w_out_a, w_in_b, w_in_b, w_out_b],
        out_specs=pl.BlockSpec((FFN_TILE, D), lambda t, idx, ea, eb, n: (t, 0)))
    return pl.pallas_call(
        _ffn_kernel,
        grid_spec=grid_spec,
        out_shape=jax.ShapeDtypeStruct((n_ffn_tiles * FFN_TILE, D), F32),
        compiler_params=pltpu.CompilerParams(dimension_semantics=("arbitrary",),
                                             vmem_limit_bytes=VMEM_LIMIT),
        name="ffn",
    )(plan["tile_idx"], plan["tile_ea"], plan["tile_eb"], plan["n_used"],
      hs, wg, wu, wd, wg, wu, wd)


def _combine_kernel(run_start, local_start, units, tile_units,
                    x1_ref, info_ref, gate2_ref, ys_ref, o_ref, ybuf, sems):
    i = pl.program_id(0)
    nt = pl.num_programs(0)
    slot = i % 2

    def piece_copy(s):
        def make(local_row, global_row):
            return pltpu.make_async_copy(ys_ref.at[pl.ds(global_row, SUBLANES), :],
                                         ybuf.at[s, pl.ds(local_row, SUBLANES), :], sems.at[s])
        return make

    def for_slot(fn):
        for s in range(2):
            pl.when(slot == s)(lambda s=s: fn(s))

    def fetch(tile, s):
        _run_pieces(tile, run_start, local_start, units, piece_copy(s))

    @pl.when(i == 0)
    def _():
        ybuf[...] = jnp.zeros_like(ybuf)
        fetch(0, 0)

    @pl.when(i + 1 < nt)
    def _():
        for_slot(lambda s: fetch(i + 1, 1 - s))

    dest_col = info_ref[:, INFO_DEST:INFO_DEST + 1]
    cols = lax.broadcasted_iota(jnp.int32, (TOKEN_TILE, SORT_ROWS), 1).astype(F32)
    unperm = jnp.where(cols == dest_col, 1.0, 0.0).astype(BF16)

    def finish(s):
        _wait_pieces(tile_units[i], piece_copy(s))
        y = jnp.dot(unperm, ybuf[s].astype(BF16), preferred_element_type=F32)
        o_ref[...] = x1_ref[...] + gate2_ref[0] * y

    for_slot(finish)


def _combine_call(plan, x1, info, gate2, ys, B, S):
    T, D = x1.shape
    tm = TOKEN_TILE
    tpb = S // tm
    grid_spec = pltpu.PrefetchScalarGridSpec(
        num_scalar_prefetch=4,
        grid=(T // tm,),
        in_specs=[pl.BlockSpec((tm, D), lambda i, *_: (i, 0)),
                  pl.BlockSpec((tm, LANES), lambda i, *_: (i, 0)),
                  pl.BlockSpec((1, 1, D), lambda i, *_: (i // tpb, 0, 0)),
                  pl.BlockSpec(memory_space=pl.ANY)],
        out_specs=pl.BlockSpec((tm, D), lambda i, *_: (i, 0)),
        scratch_shapes=[pltpu.VMEM((2, SORT_ROWS, D), F32),
                        pltpu.SemaphoreType.DMA((2,))])
    return pl.pallas_call(
        _combine_kernel,
        grid_spec=grid_spec,
        out_shape=jax.ShapeDtypeStruct((T, D), F32),
        compiler_params=pltpu.CompilerParams(dimension_semantics=("arbitrary",),
                                             vmem_limit_bytes=VMEM_LIMIT),
        name="combine",
    )(plan["run_start"], plan["local_start"], plan["units"], plan["tile_units"],
      x1, info, gate2, ys)


def kernel(x, c, w_ada, b_ada, norm1_g, w_in, sgu_norm_g, sgu_w, sgu_b, q_norm_g, k_norm_g,
           w_proj_a, w_proj_b, w_out, norm2_g, w_router_group, b_router_group,
           w_router_expert, b_router_expert, w_gate, w_up, w_down):
    B, S, D = x.shape
    T = B * S
    depth = w_ada.shape[0]
    n_token_tiles = T // TOKEN_TILE
    max_rows = T + n_token_tiles * N_CLASSES * (SUBLANES - 1) + N_CLASSES * (FFN_TILE - SUBLANES)
    n_ffn_tiles = -(-max_rows // FFN_TILE)
    x2 = x.reshape(T, D)
    for l in range(depth):
        mod = _ada_call(c, w_ada[l], b_ada[l])
        shift1, scale1, gate1, shift2, scale2, gate2 = [
            m.reshape(B, 1, D) for m in jnp.split(mod, 6, axis=-1)]

        u, v, q, k, va, ga, gb = _inproj_call(
            x2, shift1, scale1, norm1_g[l].reshape(1, D), w_in[l].astype(BF16),
            sgu_norm_g[l].reshape(1, D), q_norm_g[l].reshape(1, ATT_HEAD_DIM),
            k_norm_g[l].reshape(1, ATT_HEAD_DIM), B, S, tm=512)

        o = _attention_call(q, k, va)

        w_router = jnp.concatenate(
            [w_router_group[l],
             jnp.transpose(w_router_expert[l], (1, 0, 2)).reshape(D, N_EXPERTS)], axis=1)
        w_router = jnp.pad(w_router, ((0, 0), (0, LANES - w_router.shape[1])))
        w_router_hi = w_router.astype(BF16)
        w_router_lo = (w_router - w_router_hi.astype(F32)).astype(BF16)
        w_router = jnp.concatenate([w_router_hi, w_router_lo], axis=1)
        b_router = jnp.concatenate([b_router_group[l], b_router_expert[l].reshape(N_EXPERTS)])
        b_router = jnp.pad(b_router, (0, LANES - b_router.shape[0])).reshape(1, LANES)
        sgu_bias = jnp.repeat(sgu_b[l].T, D // SGU_GROUPS, axis=1)

        x1, h2, info, counts = _merge_call(
            x2, u, v, o, ga, gb, gate1, shift2, scale2, norm2_g[l].reshape(1, D), sgu_w[l],
            sgu_bias, w_proj_a[l].astype(BF16), w_proj_b[l].astype(BF16), w_out[l].astype(BF16),
            w_router, b_router, B, S)

        plan = _dispatch_plan(counts[:, 0, :N_CLASSES].astype(jnp.int32), n_ffn_tiles)
        hs = _dispatch_call(plan, h2, info, n_ffn_tiles)
        ys = _ffn_call(plan, hs, w_gate[l].astype(BF16), w_up[l].astype(BF16),
                       w_down[l].astype(BF16), n_ffn_tiles)
        x2 = _combine_call(plan, x1, info, gate2, ys, B, S)
    return x2.reshape(B, S, D)
```

```python
import jax
import jax.numpy as jnp
import numpy as np
from jax import lax
from jax.experimental import pallas as pl
from jax.experimental.pallas import tpu as pltpu

D_MODEL = 1024
CHUNK = 128
SGU_GROUPS = 8
ATT_HEADS = 8
ATT_HEAD_DIM = 128
DILATED_PATTERNS = ((128, 1), (512, 4), (2048, 16))
N_GROUPS = 4
EXPERTS_PER_GROUP = 4
N_EXPERTS = N_GROUPS * EXPERTS_PER_GROUP
D_EXPERT = 512
EPS = 1e-6
NEG_INF = -1e30

N_IN_SPLITS = 7
LANES = 128
SUBLANES = 8
EXPERT_LANE0 = N_GROUPS

PAIRS = ((0, 1), (0, 2), (0, 3), (1, 2), (1, 3), (2, 3))
N_CLASSES = N_GROUPS * len(PAIRS)
TOKEN_TILE = 512
SORT_ROWS = TOKEN_TILE + 3 * 64
ROW_WIDTH = D_MODEL + LANES
FFN_TILE = 256
INFO_DEST, INFO_PLO_H, INFO_PHI_H, INFO_PLO_L, INFO_PHI_L = 0, 1, 2, 3, 4

VMEM_LIMIT = 56 * 1024 * 1024
BF16 = jnp.bfloat16
F32 = jnp.float32

assert SORT_ROWS >= TOKEN_TILE + N_CLASSES * (SUBLANES - 1)


def _rms(x, g):
    return x * lax.rsqrt(jnp.mean(x * x, axis=-1, keepdims=True) + EPS) * g


def _ada_kernel(c_ref, w_ref, b_ref, o_ref):
    c = c_ref[...]
    cond = c * jax.nn.sigmoid(c)
    o_ref[...] = jnp.dot(cond, w_ref[...], preferred_element_type=F32,
                         precision=lax.Precision.HIGHEST) + b_ref[...]


def _ada_call(c, w_ada, b_ada):
    B, D = c.shape
    N = w_ada.shape[1]
    tn = 1024
    return pl.pallas_call(
        _ada_kernel,
        grid=(N // tn,),
        in_specs=[pl.BlockSpec((B, D), lambda j: (0, 0)),
                  pl.BlockSpec((D, tn), lambda j: (0, j)),
                  pl.BlockSpec((1, tn), lambda j: (0, j))],
        out_specs=pl.BlockSpec((B, tn), lambda j: (0, j)),
        out_shape=jax.ShapeDtypeStruct((B, N), F32),
        compiler_params=pltpu.CompilerParams(dimension_semantics=("arbitrary",),
                                             vmem_limit_bytes=VMEM_LIMIT),
        name="adaln",
    )(c, w_ada, b_ada.reshape(1, N))


def _inproj_kernel(x_ref, shift_ref, scale_ref, n1g_ref, w_ref, sgug_ref, qg_ref, kg_ref,
                   u_ref, v_ref, q_ref, k_ref, va_ref, ga_ref, gb_ref):
    x = x_ref[...]
    h = _rms(x, n1g_ref[...]) * (1.0 + scale_ref[0]) + shift_ref[0]
    h = h.astype(BF16)

    def proj(i):
        return jnp.dot(h, w_ref[:, i * D_MODEL:(i + 1) * D_MODEL], preferred_element_type=F32)

    u_ref[...] = jax.nn.gelu(proj(0)).astype(BF16)
    v_ref[...] = _rms(jax.nn.gelu(proj(1)), sgug_ref[...]).astype(BF16)

    q = proj(2)
    qscale = ATT_HEAD_DIM ** -0.5
    for hd in range(ATT_HEADS):
        sl = slice(hd * ATT_HEAD_DIM, (hd + 1) * ATT_HEAD_DIM)
        q_ref[0, hd] = (_rms(q[:, sl], qg_ref[...]) * qscale).astype(BF16)
    k = proj(3)
    for hd in range(ATT_HEADS):
        sl = slice(hd * ATT_HEAD_DIM, (hd + 1) * ATT_HEAD_DIM)
        k_ref[0, hd] = _rms(k[:, sl], kg_ref[...]).astype(BF16)
    va = proj(4)
    for hd in range(ATT_HEADS):
        sl = slice(hd * ATT_HEAD_DIM, (hd + 1) * ATT_HEAD_DIM)
        va_ref[0, hd] = va[:, sl].astype(BF16)
    ga_ref[...] = jax.nn.sigmoid(proj(5)).astype(BF16)
    gb_ref[...] = jax.nn.sigmoid(proj(6)).astype(BF16)


def _inproj_call(x2, shift1, scale1, n1g, w_in, sgug, qg, kg, B, S, tm):
    T, D = x2.shape
    tpb = S // tm
    tok = pl.BlockSpec((tm, D), lambda i: (i, 0))
    per_b = pl.BlockSpec((1, 1, D), lambda i: (i // tpb, 0, 0))
    row = lambda n: pl.BlockSpec((1, n), lambda i: (0, 0))
    head = pl.BlockSpec((1, ATT_HEADS, tm, ATT_HEAD_DIM), lambda i: (i // tpb, 0, i % tpb, 0))
    tok_sds = jax.ShapeDtypeStruct((T, D), BF16)
    head_sds = jax.ShapeDtypeStruct((B, ATT_HEADS, S, ATT_HEAD_DIM), BF16)
    return pl.pallas_call(
        _inproj_kernel,
        grid=(T // tm,),
        in_specs=[tok, per_b, per_b, row(D),
                  pl.BlockSpec((D, N_IN_SPLITS * D), lambda i: (0, 0), pipeline_mode=pl.Buffered(1)),
                  row(D), row(ATT_HEAD_DIM), row(ATT_HEAD_DIM)],
        out_specs=[tok, tok, head, head, head, tok, tok],
        out_shape=[tok_sds, tok_sds, head_sds, head_sds, head_sds, tok_sds, tok_sds],
        compiler_params=pltpu.CompilerParams(dimension_semantics=("parallel",),
                                             vmem_limit_bytes=VMEM_LIMIT),
        name="inproj",
    )(x2, shift1, scale1, n1g, w_in, sgug, qg, kg)


ATT_BLK = 128
STRIDED_BLK = 256
RESIDUES = 4
LOCAL_PATTERNS = tuple(p for p in DILATED_PATTERNS if p[1] % RESIDUES != 0)
STRIDED_PATTERNS = tuple(p for p in DILATED_PATTERNS if p[1] % RESIDUES == 0)
assert all(d == 1 and w <= ATT_BLK for w, d in LOCAL_PATTERNS)


def _log_count_bias(count):
    return jnp.asarray(np.where(count > 0, np.log(np.maximum(count, 1)), NEG_INF), F32)


def _local_bias():
    a = np.arange(ATT_BLK)[:, None]
    col = np.arange(2 * ATT_BLK)[None, :]
    delta = ATT_BLK + a - col
    count = np.zeros(delta.shape, np.int64)
    for window, _ in LOCAL_PATTERNS:
        count += (delta >= 0) & (delta <= window)
    return _log_count_bias(count)


def _strided_bias(n):
    nblk = n // STRIDED_BLK
    a = np.arange(STRIDED_BLK)[:, None]
    col = np.arange(n)[None, :]
    delta = (STRIDED_BLK * (nblk - 1 - col // STRIDED_BLK) + a - col % STRIDED_BLK) * RESIDUES
    count = np.zeros(delta.shape, np.int64)
    for window, dilation in STRIDED_PATTERNS:
        count += (delta >= 0) & (delta <= window) & (delta % dilation == 0)
    return _log_count_bias(count)


def _qk(q, k):
    return lax.dot_general(q, k, (((1,), (1,)), ((), ())), preferred_element_type=F32)


def _attention_kernel(q_ref, k_ref, v_ref, lbias_ref, sbias_ref, o_ref,
                      stage, q4, k4, v4, ve, acc_scr, m_scr, l_scr):
    S = q_ref.shape[2]
    n = S // RESIDUES
    nblk_s = n // STRIDED_BLK
    nblk = S // ATT_BLK
    Dh = ATT_HEAD_DIM

    ones = jnp.ones((S, LANES), BF16)
    ve[:, Dh:] = ones
    ve[:, :Dh] = v_ref[0, 0]
    for r in range(RESIDUES):
        v4[r, :, Dh:] = ones[:n]
    for src, dst in ((q_ref, q4), (k_ref, k4), (v_ref, v4)):
        stage[...] = src[0, 0].astype(F32)
        for r in range(RESIDUES):
            dst[r, :, :Dh] = stage[pl.ds(r, n, stride=RESIDUES), :].astype(BF16)

    for r in range(RESIDUES):
        for jb in range(nblk_s):
            nk = (jb + 1) * STRIDED_BLK
            s = _qk(q4[r, jb * STRIDED_BLK:(jb + 1) * STRIDED_BLK, :], k4[r, :nk, :])
            s = s + sbias_ref[:, (nblk_s - 1 - jb) * STRIDED_BLK:]
            m = jnp.max(s, axis=-1, keepdims=True)
            p = jnp.exp(s - m)
            acc = jnp.dot(p.astype(BF16), v4[r, :nk, :], preferred_element_type=F32)
            rows = pl.ds(RESIDUES * jb * STRIDED_BLK + r, STRIDED_BLK, stride=RESIDUES)
            acc_scr[rows, :] = acc[:, :Dh]
            m_scr[rows, :] = jnp.broadcast_to(m, (STRIDED_BLK, LANES))
            l_scr[rows, :] = acc[:, Dh:]

    for c in range(nblk):
        k0 = max(c - 1, 0) * ATT_BLK
        nk = (c + 1) * ATT_BLK - k0
        blk = slice(c * ATT_BLK, (c + 1) * ATT_BLK)
        s = _qk(q_ref[0, 0, blk, :], k_ref[0, 0, k0:k0 + nk, :]) + lbias_ref[:, 2 * ATT_BLK - nk:]
        m_s = m_scr[blk, :]
        m = jnp.maximum(jnp.max(s, axis=-1, keepdims=True), m_s)
        w = jnp.exp(m_s - m)
        p = jnp.exp(s - jnp.concatenate([m] * (nk // LANES), axis=-1))
        acc = jnp.dot(p.astype(BF16), ve[k0:k0 + nk, :], preferred_element_type=F32)
        l = acc[:, Dh:] + l_scr[blk, :] * w
        o_ref[0, 0, blk, :] = ((acc[:, :Dh] + acc_scr[blk, :] * w) / l).astype(BF16)


def _attention_call(q, k, v):
    B, H, S, Dh = q.shape
    n = S // RESIDUES
    blk = pl.BlockSpec((1, 1, S, Dh), lambda b, h: (b, h, 0, 0))
    return pl.pallas_call(
        _attention_kernel,
        grid=(B, H),
        in_specs=[blk, blk, blk,
                  pl.BlockSpec((ATT_BLK, 2 * ATT_BLK), lambda b, h: (0, 0)),
                  pl.BlockSpec((STRIDED_BLK, n), lambda b, h: (0, 0))],
        out_specs=blk,
        out_shape=jax.ShapeDtypeStruct((B, H, S, Dh), BF16),
        scratch_shapes=[pltpu.VMEM((S, Dh), F32),
                        pltpu.VMEM((RESIDUES, n, Dh), BF16),
                        pltpu.VMEM((RESIDUES, n, Dh), BF16),
                        pltpu.VMEM((RESIDUES, n, Dh + LANES), BF16),
                        pltpu.VMEM((S, Dh + LANES), BF16),
                        pltpu.VMEM((S, Dh), F32),
                        pltpu.VMEM((S, LANES), F32),
                        pltpu.VMEM((S, LANES), F32)],
        compiler_params=pltpu.CompilerParams(dimension_semantics=("parallel", "parallel"),
                                             vmem_limit_bytes=VMEM_LIMIT),
        name="attention",
    )(q, k, v, _local_bias(), _strided_bias(n))


def _route(logits):
    tm = logits.shape[0]
    lane = lax.broadcasted_iota(jnp.int32, logits.shape, 1)
    lanef = lane.astype(F32)

    def masked_top(mask):
        top = jnp.max(jnp.where(mask, logits, -jnp.inf), axis=-1, keepdims=True)
        idx = jnp.min(jnp.where(mask & (logits == top), lanef, float(LANES)), axis=-1, keepdims=True)
        return top, idx

    gmask = lane < N_GROUPS
    gmax, gidx = masked_top(gmask)
    p_group = 1.0 / jnp.sum(jnp.where(gmask, jnp.exp(logits - gmax), 0.0), axis=-1, keepdims=True)
    e_lo = EXPERT_LANE0 + gidx * EXPERTS_PER_GROUP
    emask = (lanef >= e_lo) & (lanef < e_lo + EXPERTS_PER_GROUP)
    v1, i1 = masked_top(emask)
    v2, i2 = masked_top(emask & (lanef != i1))
    e21 = jnp.exp(v2 - v1)
    p1 = p_group / (1.0 + e21)
    p2 = p_group * e21 / (1.0 + e21)
    first_is_lo = i1 < i2
    lo = jnp.minimum(i1, i2) - e_lo
    hi = jnp.maximum(i1, i2) - e_lo
    pair = lo * (7.0 - lo) * 0.5 + (hi - lo - 1.0)
    cls = gidx * float(len(PAIRS)) + pair
    p_lo = jnp.where(first_is_lo, p1, p2)
    p_hi = jnp.where(first_is_lo, p2, p1)

    onehot = lanef == cls
    onehot_b = jnp.where(onehot, 1.0, 0.0).astype(BF16)
    r = lax.broadcasted_iota(jnp.int32, (tm, tm), 0)
    c = lax.broadcasted_iota(jnp.int32, (tm, tm), 1)
    before = jnp.where(c < r, 1.0, 0.0).astype(BF16)
    rank = jnp.dot(before, onehot_b, preferred_element_type=F32)
    counts = jnp.sum(jnp.where(onehot, 1.0, 0.0), axis=0, keepdims=True)
    units = jnp.ceil(counts * (1.0 / SUBLANES))
    ur = lax.broadcasted_iota(jnp.int32, (LANES, LANES), 0)
    uc = lax.broadcasted_iota(jnp.int32, (LANES, LANES), 1)
    upper = jnp.where(ur < uc, 1.0, 0.0).astype(BF16)
    start = jnp.dot(jnp.broadcast_to(units, (SUBLANES, LANES)).astype(BF16), upper,
                    preferred_element_type=F32)[0:1] * float(SUBLANES)
    dest = jnp.sum(jnp.where(onehot, start + rank, 0.0), axis=-1, keepdims=True)

    def hi_part(p):
        return p.astype(BF16).astype(F32)

    info = jnp.where(lane == INFO_DEST, dest, 0.0)
    info = jnp.where(lane == INFO_PLO_H, hi_part(p_lo), info)
    info = jnp.where(lane == INFO_PHI_H, hi_part(p_hi), info)
    info = jnp.where(lane == INFO_PLO_L, p_lo - hi_part(p_lo), info)
    info = jnp.where(lane == INFO_PHI_L, p_hi - hi_part(p_hi), info)
    return info, counts


def _merge_kernel(x_ref, u_ref, v_ref, o_ref, ga_ref, gb_ref, gate1_ref, shift2_ref, scale2_ref,
                  n2g_ref, sw_ref, sb_ref, wa_ref, wb_ref, wo_ref, wr_ref, br_ref,
                  x1_ref, h2_ref, info_ref, counts_ref, s_scr):
    tm = x_ref.shape[0]
    row = lax.broadcasted_iota(jnp.int32, (CHUNK, CHUNK), 0)
    colm = lax.broadcasted_iota(jnp.int32, (CHUNK, CHUNK), 1)
    causal = colm <= row
    for g in range(SGU_GROUPS):
        w = jnp.where(causal, sw_ref[g], 0.0).astype(BF16)
        gs = slice(g * CHUNK, (g + 1) * CHUNK)
        for c in range(tm // CHUNK):
            cs = slice(c * CHUNK, (c + 1) * CHUNK)
            mixed = jnp.dot(w, v_ref[cs, gs], preferred_element_type=F32) + sb_ref[:, gs]
            s_scr[cs, gs] = (u_ref[cs, gs].astype(F32) * mixed).astype(BF16)

    y_a = jnp.dot(s_scr[...], wa_ref[...], preferred_element_type=F32)
    o = jnp.concatenate([o_ref[0, hd] for hd in range(ATT_HEADS)], axis=-1)
    y_b = jnp.dot(o, wb_ref[...], preferred_element_type=F32)
    merged = ga_ref[...].astype(F32) * y_a + gb_ref[...].astype(F32) * y_b
    y = jnp.dot(merged.astype(BF16), wo_ref[...], preferred_element_type=F32)
    x1 = x_ref[...] + gate1_ref[0] * y
    x1_ref[...] = x1

    h2 = _rms(x1, n2g_ref[...]) * (1.0 + scale2_ref[0]) + shift2_ref[0]
    h2_ref[...] = h2.astype(BF16)
    h_hi = h2.astype(BF16)
    h_lo = (h2 - h_hi.astype(F32)).astype(BF16)
    r_hi = jnp.dot(h_hi, wr_ref[...], preferred_element_type=F32)
    r_lo = jnp.dot(h_lo, wr_ref[:, :LANES], preferred_element_type=F32)
    logits = r_hi[:, :LANES] + r_hi[:, LANES:] + r_lo + br_ref[...]
    info, counts = _route(logits)
    info_ref[...] = info
    counts_ref[0] = counts


def _merge_call(x2, u, v, o, ga, gb, gate1, shift2, scale2, n2g, sgu_w, sgu_bias, wa, wb, wo,
                w_router, b_router, B, S):
    T, D = x2.shape
    tm = TOKEN_TILE
    tpb = S // tm
    tok = pl.BlockSpec((tm, D), lambda i: (i, 0))
    per_b = pl.BlockSpec((1, 1, D), lambda i: (i // tpb, 0, 0))
    head = pl.BlockSpec((1, ATT_HEADS, tm, ATT_HEAD_DIM), lambda i: (i // tpb, 0, i % tpb, 0))
    full = lambda *shape: pl.BlockSpec(shape, lambda i: (0,) * len(shape))
    return pl.pallas_call(
        _merge_kernel,
        grid=(T // tm,),
        in_specs=[tok, tok, tok, head, tok, tok, per_b, per_b, per_b, full(1, D),
                  full(SGU_GROUPS, CHUNK, CHUNK), full(CHUNK, D),
                  full(D, D), full(D, D), full(D, D), full(D, 2 * LANES), full(1, LANES)],
        out_specs=[tok, tok, pl.BlockSpec((tm, LANES), lambda i: (i, 0)),
                   pl.BlockSpec((1, 1, LANES), lambda i: (i, 0, 0))],
        out_shape=[jax.ShapeDtypeStruct((T, D), F32), jax.ShapeDtypeStruct((T, D), BF16),
                   jax.ShapeDtypeStruct((T, LANES), F32),
                   jax.ShapeDtypeStruct((T // tm, 1, LANES), F32)],
        scratch_shapes=[pltpu.VMEM((tm, D), BF16)],
        compiler_params=pltpu.CompilerParams(dimension_semantics=("parallel",),
                                             vmem_limit_bytes=VMEM_LIMIT),
        name="merge",
    )(x2, u, v, o, ga, gb, gate1, shift2, scale2, n2g, sgu_w, sgu_bias, wa, wb, wo,
      w_router, b_router)


def _dispatch_plan(counts, n_ffn_tiles):
    units = (counts + SUBLANES - 1) // SUBLANES
    rows = units * SUBLANES
    local_start = jnp.cumsum(rows, axis=1) - rows
    class_rows = jnp.sum(rows, axis=0)
    class_tiles = (class_rows + FFN_TILE - 1) // FFN_TILE
    cum_tiles = jnp.cumsum(class_tiles)
    seg_start = (cum_tiles - class_tiles) * FFN_TILE
    run_start = seg_start[None, :] + jnp.cumsum(rows, axis=0) - rows
    n_used = cum_tiles[-1]
    tile_idx = jnp.maximum(jnp.minimum(jnp.arange(n_ffn_tiles, dtype=jnp.int32), n_used - 1), 0)
    tile_cls = jnp.sum((tile_idx[:, None] >= cum_tiles[None, :]).astype(jnp.int32), axis=1)
    pair = tile_cls % len(PAIRS)
    group = tile_cls // len(PAIRS)
    pairs = jnp.asarray(PAIRS, jnp.int32)
    i32 = lambda a: a.astype(jnp.int32).reshape(-1)
    return dict(
        run_start=i32(run_start), local_start=i32(local_start), units=i32(units),
        tile_units=i32(jnp.sum(units, axis=1)),
        tail_start=i32(seg_start + class_rows),
        tail_units=i32((class_tiles * FFN_TILE - class_rows) // SUBLANES),
        tile_idx=i32(tile_idx), n_used=i32(n_used),
        tile_ea=i32(group * EXPERTS_PER_GROUP + pairs[pair, 0]),
        tile_eb=i32(group * EXPERTS_PER_GROUP + pairs[pair, 1]))


def _run_pieces(tile, run_start, local_start, units, make_copy):
    def per_class(c, carry):
        k = tile * N_CLASSES + c
        g0 = run_start[k]
        l0 = local_start[k]

        def per_piece(u, carry2):
            off = u * SUBLANES
            make_copy(pl.multiple_of(l0 + off, SUBLANES), pl.multiple_of(g0 + off, SUBLANES)).start()
            return carry2

        return lax.fori_loop(0, units[k], per_piece, carry)

    lax.fori_loop(0, N_CLASSES, per_class, 0)


def _wait_pieces(n, make_copy):
    def body(_, carry):
        make_copy(0, 0).wait()
        return carry

    lax.fori_loop(0, n, body, 0)


def _dispatch_kernel(run_start, local_start, units, tile_units, tail_start, tail_units, n_used,
                     h2_ref, info_ref, hs_ref, sorted_scr, zero_scr, sems, tail_sem, unused_sem):
    i = pl.program_id(0)
    nt = pl.num_programs(0)
    slot = i % 2
    n_ffn_tiles = hs_ref.shape[0] // FFN_TILE

    def piece_copy(s):
        def make(local_row, global_row):
            return pltpu.make_async_copy(sorted_scr.at[s, pl.ds(local_row, SUBLANES), :],
                                         hs_ref.at[pl.ds(global_row, SUBLANES), :], sems.at[s])
        return make

    def tail_copy(global_row):
        return pltpu.make_async_copy(zero_scr.at[pl.ds(0, SUBLANES), :],
                                     hs_ref.at[pl.ds(global_row, SUBLANES), :], tail_sem)

    def unused_copy(t):
        return pltpu.make_async_copy(
            zero_scr, hs_ref.at[pl.ds(pl.multiple_of(t * FFN_TILE, FFN_TILE), FFN_TILE), :],
            unused_sem)

    def for_slot(fn):
        for s in range(2):
            pl.when(slot == s)(lambda s=s: fn(s))

    @pl.when(i >= 2)
    def _():
        for_slot(lambda s: _wait_pieces(tile_units[i - 2], piece_copy(s)))

    @pl.when(i == 0)
    def _():
        zero_scr[...] = jnp.zeros_like(zero_scr)

        def per_class(c, carry):
            def per_piece(u, carry2):
                tail_copy(pl.multiple_of(tail_start[c] + u * SUBLANES, SUBLANES)).start()
                return carry2
            return lax.fori_loop(0, tail_units[c], per_piece, carry)

        lax.fori_loop(0, N_CLASSES, per_class, 0)

        def per_unused(t, carry):
            unused_copy(t).start()
            return carry

        lax.fori_loop(n_used[0], n_ffn_tiles, per_unused, 0)

    info = info_ref[...]
    dest_row = info.T[INFO_DEST:INFO_DEST + 1, :]
    rows = lax.broadcasted_iota(jnp.int32, (SORT_ROWS, TOKEN_TILE), 0).astype(F32)
    perm = jnp.where(rows == dest_row, 1.0, 0.0).astype(BF16)
    lane = lax.broadcasted_iota(jnp.int32, info.shape, 1)
    weights = jnp.where(lane == INFO_DEST, 0.0, info).astype(BF16)

    def sort_and_send(s):
        sorted_scr[s, :, :D_MODEL] = jnp.dot(perm, h2_ref[...], preferred_element_type=F32)
        sorted_scr[s, :, D_MODEL:] = jnp.dot(perm, weights, preferred_element_type=F32)
        _run_pieces(i, run_start, local_start, units, piece_copy(s))

    for_slot(sort_and_send)

    @pl.when(i == nt - 1)
    def _():
        for_slot(lambda s: _wait_pieces(tile_units[i], piece_copy(s)))

        @pl.when(nt >= 2)
        def _():
            for_slot(lambda s: _wait_pieces(tile_units[i - 1], piece_copy(1 - s)))

        def per_class(c, carry):
            def per_piece(u, carry2):
                tail_copy(0).wait()
                return carry2
            return lax.fori_loop(0, tail_units[c], per_piece, carry)

        lax.fori_loop(0, N_CLASSES, per_class, 0)

        def per_unused(t, carry):
            unused_copy(0).wait()
            return carry

        lax.fori_loop(n_used[0], n_ffn_tiles, per_unused, 0)


def _dispatch_call(plan, h2, info, n_ffn_tiles):
    T, D = h2.shape
    tm = TOKEN_TILE
    grid_spec = pltpu.PrefetchScalarGridSpec(
        num_scalar_prefetch=7,
        grid=(T // tm,),
        in_specs=[pl.BlockSpec((tm, D), lambda i, *_: (i, 0)),
                  pl.BlockSpec((tm, LANES), lambda i, *_: (i, 0))],
        out_specs=pl.BlockSpec(memory_space=pl.ANY),
        scratch_shapes=[pltpu.VMEM((2, SORT_ROWS, ROW_WIDTH), F32),
                        pltpu.VMEM((FFN_TILE, ROW_WIDTH), F32),
                        pltpu.SemaphoreType.DMA((2,)),
                        pltpu.SemaphoreType.DMA(()),
                        pltpu.SemaphoreType.DMA(())])
    return pl.pallas_call(
        _dispatch_kernel,
        grid_spec=grid_spec,
        out_shape=jax.ShapeDtypeStruct((n_ffn_tiles * FFN_TILE, ROW_WIDTH), F32),
        compiler_params=pltpu.CompilerParams(dimension_semantics=("arbitrary",),
                                             vmem_limit_bytes=VMEM_LIMIT),
        name="dispatch",
    )(plan["run_start"], plan["local_start"], plan["units"], plan["tile_units"],
      plan["tail_start"], plan["tail_units"], plan["n_used"], h2, info)


def _ffn_kernel(tile_idx, tile_ea, tile_eb, n_used, hs_ref, wga_ref, wua_ref, wda_ref,
                wgb_ref, wub_ref, wdb_ref, ys_ref):
    @pl.when(pl.program_id(0) >= n_used[0])
    def _():
        ys_ref[...] = jnp.zeros_like(ys_ref)

    @pl.when(pl.program_id(0) < n_used[0])
    def _():
        h = hs_ref[:, :D_MODEL].astype(BF16)
        pv = hs_ref[:, D_MODEL:]
        p_lo = pv[:, INFO_PLO_H:INFO_PLO_H + 1] + pv[:, INFO_PLO_L:INFO_PLO_L + 1]
        p_hi = pv[:, INFO_PHI_H:INFO_PHI_H + 1] + pv[:, INFO_PHI_L:INFO_PHI_L + 1]

        def hidden(wg_ref, wu_ref, p):
            a = jnp.dot(h, wg_ref[0], preferred_element_type=F32)
            b = jnp.dot(h, wu_ref[0], preferred_element_type=F32)
            return (a * jax.nn.sigmoid(a) * b * p).astype(BF16)

        ys_ref[...] = (jnp.dot(hidden(wga_ref, wua_ref, p_lo), wda_ref[0], preferred_element_type=F32)
                       + jnp.dot(hidden(wgb_ref, wub_ref, p_hi), wdb_ref[0], preferred_element_type=F32))


def _ffn_call(plan, hs, wg, wu, wd, n_ffn_tiles):
    D = D_MODEL
    rows = pl.BlockSpec((FFN_TILE, ROW_WIDTH), lambda t, idx, ea, eb, n: (idx[t], 0))
    w_in_a = pl.BlockSpec((1, D, D_EXPERT), lambda t, idx, ea, eb, n: (ea[t], 0, 0))
    w_out_a = pl.BlockSpec((1, D_EXPERT, D), lambda t, idx, ea, eb, n: (ea[t], 0, 0))
    w_in_b = pl.BlockSpec((1, D, D_EXPERT), lambda t, idx, ea, eb, n: (eb[t], 0, 0))
    w_out_b = pl.BlockSpec((1, D_EXPERT, D), lambda t, idx, ea, eb, n: (eb[t], 0, 0))
    grid_spec = pltpu.PrefetchScalarGridSpec(
        num_scalar_prefetch=4,
        grid=(n_ffn_tiles,),
        in_specs=[rows, w_in_a, w_in_a, w_out_a, w_in_b, w_in_b, w_out_b],
        out_specs=pl.BlockSpec((FFN_TILE, D), lambda t, idx, ea, eb, n: (t, 0)))
    return pl.pallas_call(
        _ffn_kernel,
        grid_spec=grid_spec,
        out_shape=jax.ShapeDtypeStruct((n_ffn_tiles * FFN_TILE, D), F32),
        compiler_params=pltpu.CompilerParams(dimension_semantics=("arbitrary",),
                                             vmem_limit_bytes=VMEM_LIMIT),
        name="ffn",
    )(plan["tile_idx"], plan["tile_ea"], plan["tile_eb"], plan["n_used"],
      hs, wg, wu, wd, wg, wu, wd)


def _combine_kernel(run_start, local_start, units, tile_units,
                    x1_ref, info_ref, gate2_ref, ys_ref, o_ref, ybuf, sems):
    i = pl.program_id(0)
    nt = pl.num_programs(0)
    slot = i % 2

    def piece_copy(s):
        def make(local_row, global_row):
            return pltpu.make_async_copy(ys_ref.at[pl.ds(global_row, SUBLANES), :],
                                         ybuf.at[s, pl.ds(local_row, SUBLANES), :], sems.at[s])
        return make

    def for_slot(fn):
        for s in range(2):
            pl.when(slot == s)(lambda s=s: fn(s))

    def fetch(tile, s):
        _run_pieces(tile, run_start, local_start, units, piece_copy(s))

    @pl.when(i == 0)
    def _():
        ybuf[...] = jnp.zeros_like(ybuf)
        fetch(0, 0)

    @pl.when(i + 1 < nt)
    def _():
        for_slot(lambda s: fetch(i + 1, 1 - s))

    dest_col = info_ref[:, INFO_DEST:INFO_DEST + 1]
    cols = lax.broadcasted_iota(jnp.int32, (TOKEN_TILE, SORT_ROWS), 1).astype(F32)
    unperm = jnp.where(cols == dest_col, 1.0, 0.0).astype(BF16)

    def finish(s):
        _wait_pieces(tile_units[i], piece_copy(s))
        y = jnp.dot(unperm, ybuf[s].astype(BF16), preferred_element_type=F32)
        o_ref[...] = x1_ref[...] + gate2_ref[0] * y

    for_slot(finish)


def _combine_call(plan, x1, info, gate2, ys, B, S):
    T, D = x1.shape
    tm = TOKEN_TILE
    tpb = S // tm
    grid_spec = pltpu.PrefetchScalarGridSpec(
        num_scalar_prefetch=4,
        grid=(T // tm,),
        in_specs=[pl.BlockSpec((tm, D), lambda i, *_: (i, 0)),
                  pl.BlockSpec((tm, LANES), lambda i, *_: (i, 0)),
                  pl.BlockSpec((1, 1, D), lambda i, *_: (i // tpb, 0, 0)),
                  pl.BlockSpec(memory_space=pl.ANY)],
        out_specs=pl.BlockSpec((tm, D), lambda i, *_: (i, 0)),
        scratch_shapes=[pltpu.VMEM((2, SORT_ROWS, D), F32),
                        pltpu.SemaphoreType.DMA((2,))])
    return pl.pallas_call(
        _combine_kernel,
        grid_spec=grid_spec,
        out_shape=jax.ShapeDtypeStruct((T, D), F32),
        compiler_params=pltpu.CompilerParams(dimension_semantics=("arbitrary",),
                                             vmem_limit_bytes=VMEM_LIMIT),
        name="combine",
    )(plan["run_start"], plan["local_start"], plan["units"], plan["tile_units"],
      x1, info, gate2, ys)


def kernel(x, c, w_ada, b_ada, norm1_g, w_in, sgu_norm_g, sgu_w, sgu_b, q_norm_g, k_norm_g,
           w_proj_a, w_proj_b, w_out, norm2_g, w_router_group, b_router_group,
           w_router_expert, b_router_expert, w_gate, w_up, w_down):
    B, S, D = x.shape
    T = B * S
    depth = w_ada.shape[0]
    n_token_tiles = T // TOKEN_TILE
    max_rows = T + n_token_tiles * N_CLASSES * (SUBLANES - 1) + N_CLASSES * (FFN_TILE - SUBLANES)
    n_ffn_tiles = -(-max_rows // FFN_TILE)
    x2 = x.reshape(T, D)
    for l in range(depth):
        mod = _ada_call(c, w_ada[l], b_ada[l])
        shift1, scale1, gate1, shift2, scale2, gate2 = [
            m.reshape(B, 1, D) for m in jnp.split(mod, 6, axis=-1)]

        u, v, q, k, va, ga, gb = _inproj_call(
            x2, shift1, scale1, norm1_g[l].reshape(1, D), w_in[l].astype(BF16),
            sgu_norm_g[l].reshape(1, D), q_norm_g[l].reshape(1, ATT_HEAD_DIM),
            k_norm_g[l].reshape(1, ATT_HEAD_DIM), B, S, tm=512)

        o = _attention_call(q, k, va)

        w_router = jnp.concatenate(
            [w_router_group[l],
             jnp.transpose(w_router_expert[l], (1, 0, 2)).reshape(D, N_EXPERTS)], axis=1)
        w_router = jnp.pad(w_router, ((0, 0), (0, LANES - w_router.shape[1])))
        w_router_hi = w_router.astype(BF16)
        w_router_lo = (w_router - w_router_hi.astype(F32)).astype(BF16)
        w_router = jnp.concatenate([w_router_hi, w_router_lo], axis=1)
        b_router = jnp.concatenate([b_router_group[l], b_router_expert[l].reshape(N_EXPERTS)])
        b_router = jnp.pad(b_router, (0, LANES - b_router.shape[0])).reshape(1, LANES)
        sgu_bias = jnp.repeat(sgu_b[l].T, D // SGU_GROUPS, axis=1)

        x1, h2, info, counts = _merge_call(
            x2, u, v, o, ga, gb, gate1, shift2, scale2, norm2_g[l].reshape(1, D), sgu_w[l],
            sgu_bias, w_proj_a[l].astype(BF16), w_proj_b[l].astype(BF16), w_out[l].astype(BF16),
            w_router, b_router, B, S)

        plan = _dispatch_plan(counts[:, 0, :N_CLASSES].astype(jnp.int32), n_ffn_tiles)
        hs = _dispatch_call(plan, h2, info, n_ffn_tiles)
        ys = _ffn_call(plan, hs, w_gate[l].astype(BF16), w_up[l].astype(BF16),
                       w_down[l].astype(BF16), n_ffn_tiles)
        x2 = _combine_call(plan, x1, info, gate2, ys, B, S)
    return x2.reshape(B, S, D)
```

```python
import jax
import jax.numpy as jnp
import numpy as np
from jax import lax
from jax.experimental import pallas as pl
from jax.experimental.pallas import tpu as pltpu

D_MODEL = 1024
CHUNK = 128
SGU_GROUPS = 8
ATT_HEADS = 8
ATT_HEAD_DIM = 128
DILATED_PATTERNS = ((128, 1), (512, 4), (2048, 16))
N_GROUPS = 4
EXPERTS_PER_GROUP = 4
N_EXPERTS = N_GROUPS * EXPERTS_PER_GROUP
D_EXPERT = 512
EPS = 1e-6
NEG_INF = -1e30

N_IN_SPLITS = 7
LANES = 128
SUBLANES = 8
EXPERT_LANE0 = N_GROUPS

PAIRS = ((0, 1), (0, 2), (0, 3), (1, 2), (1, 3), (2, 3))
N_CLASSES = N_GROUPS * len(PAIRS)
TOKEN_TILE = 512
SORT_ROWS = TOKEN_TILE + 3 * 64
ROW_WIDTH = D_MODEL + LANES
FFN_TILE = 256
INFO_DEST, INFO_PLO_H, INFO_PHI_H, INFO_PLO_L, INFO_PHI_L = 0, 1, 2, 3, 4

VMEM_LIMIT = 56 * 1024 * 1024
BF16 = jnp.bfloat16
F32 = jnp.float32

assert SORT_ROWS >= TOKEN_TILE + N_CLASSES * (SUBLANES - 1)


def _rms(x, g):
    return x * lax.rsqrt(jnp.mean(x * x, axis=-1, keepdims=True) + EPS) * g


def _ada_kernel(c_ref, w_ref, b_ref, o_ref):
    c = c_ref[...]
    cond = c * jax.nn.sigmoid(c)
    o_ref[...] = jnp.dot(cond, w_ref[...], preferred_element_type=F32,
                         precision=lax.Precision.HIGHEST) + b_ref[...]


def _ada_call(c, w_ada, b_ada):
    B, D = c.shape
    N = w_ada.shape[1]
    tn = 1024
    return pl.pallas_call(
        _ada_kernel,
        grid=(N // tn,),
        in_specs=[pl.BlockSpec((B, D), lambda j: (0, 0)),
                  pl.BlockSpec((D, tn), lambda j: (0, j)),
                  pl.BlockSpec((1, tn), lambda j: (0, j))],
        out_specs=pl.BlockSpec((B, tn), lambda j: (0, j)),
        out_shape=jax.ShapeDtypeStruct((B, N), F32),
        compiler_params=pltpu.CompilerParams(dimension_semantics=("arbitrary",),
                                             vmem_limit_bytes=VMEM_LIMIT),
        name="adaln",
    )(c, w_ada, b_ada.reshape(1, N))


def _inproj_kernel(x_ref, shift_ref, scale_ref, n1g_ref, w_ref, sgug_ref, qg_ref, kg_ref,
                   u_ref, v_ref, q_ref, k_ref, va_ref, ga_ref, gb_ref):
    x = x_ref[...]
    h = _rms(x, n1g_ref[...]) * (1.0 + scale_ref[0]) + shift_ref[0]
    h = h.astype(BF16)

    def proj(i):
        return jnp.dot(h, w_ref[:, i * D_MODEL:(i + 1) * D_MODEL], preferred_element_type=F32)

    u_ref[...] = jax.nn.gelu(proj(0)).astype(BF16)
    v_ref[...] = _rms(jax.nn.gelu(proj(1)), sgug_ref[...]).astype(BF16)

    q = proj(2)
    qscale = ATT_HEAD_DIM ** -0.5
    for hd in range(ATT_HEADS):
        sl = slice(hd * ATT_HEAD_DIM, (hd + 1) * ATT_HEAD_DIM)
        q_ref[0, hd] = (_rms(q[:, sl], qg_ref[...]) * qscale).astype(BF16)
    k = proj(3)
    for hd in range(ATT_HEADS):
        sl = slice(hd * ATT_HEAD_DIM, (hd + 1) * ATT_HEAD_DIM)
        k_ref[0, hd] = _rms(k[:, sl], kg_ref[...]).astype(BF16)
    va = proj(4)
    for hd in range(ATT_HEADS):
        sl = slice(hd * ATT_HEAD_DIM, (hd + 1) * ATT_HEAD_DIM)
        va_ref[0, hd] = va[:, sl].astype(BF16)
    ga_ref[...] = jax.nn.sigmoid(proj(5)).astype(BF16)
    gb_ref[...] = jax.nn.sigmoid(proj(6)).astype(BF16)


def _inproj_call(x2, shift1, scale1, n1g, w_in, sgug, qg, kg, B, S, tm):
    T, D = x2.shape
    tpb = S // tm
    tok = pl.BlockSpec((tm, D), lambda i: (i, 0))
    per_b = pl.BlockSpec((1, 1, D), lambda i: (i // tpb, 0, 0))
    row = lambda n: pl.BlockSpec((1, n), lambda i: (0, 0))
    head = pl.BlockSpec((1, ATT_HEADS, tm, ATT_HEAD_DIM), lambda i: (i // tpb, 0, i % tpb, 0))
    tok_sds = jax.ShapeDtypeStruct((T, D), BF16)
    head_sds = jax.ShapeDtypeStruct((B, ATT_HEADS, S, ATT_HEAD_DIM), BF16)
    return pl.pallas_call(
        _inproj_kernel,
        grid=(T // tm,),
        in_specs=[tok, per_b, per_b, row(D),
                  pl.BlockSpec((D, N_IN_SPLITS * D), lambda i: (0, 0), pipeline_mode=pl.Buffered(1)),
                  row(D), row(ATT_HEAD_DIM), row(ATT_HEAD_DIM)],
        out_specs=[tok, tok, head, head, head, tok, tok],
        out_shape=[tok_sds, tok_sds, head_sds, head_sds, head_sds, tok_sds, tok_sds],
        compiler_params=pltpu.CompilerParams(dimension_semantics=("parallel",),
                                             vmem_limit_bytes=VMEM_LIMIT),
        name="inproj",
    )(x2, shift1, scale1, n1g, w_in, sgug, qg, kg)


ATT_BLK = 128
STRIDED_BLK = 256
HEADS_PER_STEP = 2
RESIDUES = 4
LOCAL_PATTERNS = tuple(p for p in DILATED_PATTERNS if p[1] % RESIDUES != 0)
STRIDED_PATTERNS = tuple(p for p in DILATED_PATTERNS if p[1] % RESIDUES == 0)
assert all(d == 1 and w <= ATT_BLK for w, d in LOCAL_PATTERNS)


def _log_count_bias(count):
    return jnp.asarray(np.where(count > 0, np.log(np.maximum(count, 1)), NEG_INF), F32)


def _local_bias():
    a = np.arange(ATT_BLK)[:, None]
    col = np.arange(2 * ATT_BLK)[None, :]
    delta = ATT_BLK + a - col
    count = np.zeros(delta.shape, np.int64)
    for window, _ in LOCAL_PATTERNS:
        count += (delta >= 0) & (delta <= window)
    return _log_count_bias(count)


def _strided_bias(n):
    nblk = n // STRIDED_BLK
    a = np.arange(STRIDED_BLK)[:, None]
    col = np.arange(n)[None, :]
    delta = (STRIDED_BLK * (nblk - 1 - col // STRIDED_BLK) + a - col % STRIDED_BLK) * RESIDUES
    count = np.zeros(delta.shape, np.int64)
    for window, dilation in STRIDED_PATTERNS:
        count += (delta >= 0) & (delta <= window) & (delta % dilation == 0)
    return _log_count_bias(count)


def _qk(q, k):
    return lax.dot_general(q, k, (((1,), (1,)), ((), ())), preferred_element_type=F32)


def _attention_kernel(q_ref, k_ref, v_ref, lbias_ref, sbias_ref, o_ref,
                      stage, q4, k4, v4, ve, acc_scr, m_scr, l_scr):
    S = q_ref.shape[2]
    n = S // RESIDUES
    nblk_s = n // STRIDED_BLK
    nblk = S // ATT_BLK
    Dh = ATT_HEAD_DIM
    ones = jnp.ones((S, LANES), BF16)

    for hd in range(HEADS_PER_STEP):
        ve[hd, :, Dh:] = ones
        ve[hd, :, :Dh] = v_ref[0, hd]
        for r in range(RESIDUES):
            v4[hd, r, :, Dh:] = ones[:n]
        for src, dst in ((q_ref, q4), (k_ref, k4), (v_ref, v4)):
            stage[hd] = src[0, hd].astype(F32)
            for r in range(RESIDUES):
                dst[hd, r, :, :Dh] = stage[hd, pl.ds(r, n, stride=RESIDUES), :].astype(BF16)

        for r in range(RESIDUES):
            for jb in range(nblk_s):
                nk = (jb + 1) * STRIDED_BLK
                s = _qk(q4[hd, r, jb * STRIDED_BLK:(jb + 1) * STRIDED_BLK, :], k4[hd, r, :nk, :])
                s = s + sbias_ref[:, (nblk_s - 1 - jb) * STRIDED_BLK:]
                m = jnp.max(s, axis=-1, keepdims=True)
                p = jnp.exp(s - m)
                acc = jnp.dot(p.astype(BF16), v4[hd, r, :nk, :], preferred_element_type=F32)
                rows = pl.ds(RESIDUES * jb * STRIDED_BLK + r, STRIDED_BLK, stride=RESIDUES)
                acc_scr[hd, rows, :] = acc[:, :Dh]
                m_scr[hd, rows, :] = jnp.broadcast_to(m, (STRIDED_BLK, LANES))
                l_scr[hd, rows, :] = acc[:, Dh:]

        for c in range(nblk):
            k0 = max(c - 1, 0) * ATT_BLK
            nk = (c + 1) * ATT_BLK - k0
            blk = slice(c * ATT_BLK, (c + 1) * ATT_BLK)
            s = _qk(q_ref[0, hd, blk, :], k_ref[0, hd, k0:k0 + nk, :]) + lbias_ref[:, 2 * ATT_BLK - nk:]
            m_s = m_scr[hd, blk, :]
            m = jnp.maximum(jnp.max(s, axis=-1, keepdims=True), m_s)
            w = jnp.exp(m_s - m)
            p = jnp.exp(s - jnp.concatenate([m] * (nk // LANES), axis=-1))
            acc = jnp.dot(p.astype(BF16), ve[hd, k0:k0 + nk, :], preferred_element_type=F32)
            l = acc[:, Dh:] + l_scr[hd, blk, :] * w
            o_ref[0, hd, blk, :] = ((acc[:, :Dh] + acc_scr[hd, blk, :] * w) / l).astype(BF16)


def _attention_call(q, k, v):
    B, H, S, Dh = q.shape
    n = S // RESIDUES
    P = HEADS_PER_STEP
    blk = pl.BlockSpec((1, P, S, Dh), lambda b, h: (b, h, 0, 0))
    return pl.pallas_call(
        _attention_kernel,
        grid=(B, H // P),
        in_specs=[blk, blk, blk,
                  pl.BlockSpec((ATT_BLK, 2 * ATT_BLK), lambda b, h: (0, 0)),
                  pl.BlockSpec((STRIDED_BLK, n), lambda b, h: (0, 0))],
        out_specs=blk,
        out_shape=jax.ShapeDtypeStruct((B, H, S, Dh), BF16),
        scratch_shapes=[pltpu.VMEM((P, S, Dh), F32),
                        pltpu.VMEM((P, RESIDUES, n, Dh), BF16),
                        pltpu.VMEM((P, RESIDUES, n, Dh), BF16),
                        pltpu.VMEM((P, RESIDUES, n, Dh + LANES), BF16),
                        pltpu.VMEM((P, S, Dh + LANES), BF16),
                        pltpu.VMEM((P, S, Dh), F32),
                        pltpu.VMEM((P, S, LANES), F32),
                        pltpu.VMEM((P, S, LANES), F32)],
        compiler_params=pltpu.CompilerParams(dimension_semantics=("parallel", "parallel"),
                                             vmem_limit_bytes=VMEM_LIMIT),
        name="attention",
    )(q, k, v, _local_bias(), _strided_bias(n))


def _route(logits):
    tm = logits.shape[0]
    lane = lax.broadcasted_iota(jnp.int32, logits.shape, 1)
    lanef = lane.astype(F32)

    def masked_top(mask):
        top = jnp.max(jnp.where(mask, logits, -jnp.inf), axis=-1, keepdims=True)
        idx = jnp.min(jnp.where(mask & (logits == top), lanef, float(LANES)), axis=-1, keepdims=True)
        return top, idx

    gmask = lane < N_GROUPS
    gmax, gidx = masked_top(gmask)
    p_group = 1.0 / jnp.sum(jnp.where(gmask, jnp.exp(logits - gmax), 0.0), axis=-1, keepdims=True)
    e_lo = EXPERT_LANE0 + gidx * EXPERTS_PER_GROUP
    emask = (lanef >= e_lo) & (lanef < e_lo + EXPERTS_PER_GROUP)
    v1, i1 = masked_top(emask)
    v2, i2 = masked_top(emask & (lanef != i1))
    e21 = jnp.exp(v2 - v1)
    p1 = p_group / (1.0 + e21)
    p2 = p_group * e21 / (1.0 + e21)
    first_is_lo = i1 < i2
    lo = jnp.minimum(i1, i2) - e_lo
    hi = jnp.maximum(i1, i2) - e_lo
    pair = lo * (7.0 - lo) * 0.5 + (hi - lo - 1.0)
    cls = gidx * float(len(PAIRS)) + pair
    p_lo = jnp.where(first_is_lo, p1, p2)
    p_hi = jnp.where(first_is_lo, p2, p1)

    onehot = lanef == cls
    onehot_b = jnp.where(onehot, 1.0, 0.0).astype(BF16)
    r = lax.broadcasted_iota(jnp.int32, (tm, tm), 0)
    c = lax.broadcasted_iota(jnp.int32, (tm, tm), 1)
    before = jnp.where(c < r, 1.0, 0.0).astype(BF16)
    rank = jnp.dot(before, onehot_b, preferred_element_type=F32)
    counts = jnp.sum(jnp.where(onehot, 1.0, 0.0), axis=0, keepdims=True)
    units = jnp.ceil(counts * (1.0 / SUBLANES))
    ur = lax.broadcasted_iota(jnp.int32, (LANES, LANES), 0)
    uc = lax.broadcasted_iota(jnp.int32, (LANES, LANES), 1)
    upper = jnp.where(ur < uc, 1.0, 0.0).astype(BF16)
    start = jnp.dot(jnp.broadcast_to(units, (SUBLANES, LANES)).astype(BF16), upper,
                    preferred_element_type=F32)[0:1] * float(SUBLANES)
    dest = jnp.sum(jnp.where(onehot, start + rank, 0.0), axis=-1, keepdims=True)

    def hi_part(p):
        return p.astype(BF16).astype(F32)

    info = jnp.where(lane == INFO_DEST, dest, 0.0)
    info = jnp.where(lane == INFO_PLO_H, hi_part(p_lo), info)
    info = jnp.where(lane == INFO_PHI_H, hi_part(p_hi), info)
    info = jnp.where(lane == INFO_PLO_L, p_lo - hi_part(p_lo), info)
    info = jnp.where(lane == INFO_PHI_L, p_hi - hi_part(p_hi), info)
    return info, counts


def _merge_kernel(x_ref, u_ref, v_ref, o_ref, ga_ref, gb_ref, gate1_ref, shift2_ref, scale2_ref,
                  n2g_ref, sw_ref, sb_ref, wa_ref, wb_ref, wo_ref, wr_ref, br_ref,
                  x1_ref, h2_ref, info_ref, counts_ref, s_scr):
    tm = x_ref.shape[0]
    row = lax.broadcasted_iota(jnp.int32, (CHUNK, CHUNK), 0)
    colm = lax.broadcasted_iota(jnp.int32, (CHUNK, CHUNK), 1)
    causal = colm <= row
    for g in range(SGU_GROUPS):
        w = jnp.where(causal, sw_ref[g], 0.0).astype(BF16)
        gs = slice(g * CHUNK, (g + 1) * CHUNK)
        chunks = [slice(c * CHUNK, (c + 1) * CHUNK) for c in range(tm // CHUNK)]
        mixed = jnp.dot(w, jnp.concatenate([v_ref[cs, gs] for cs in chunks], axis=1),
                        preferred_element_type=F32)
        for cs in chunks:
            s_scr[cs, gs] = (u_ref[cs, gs].astype(F32) * (mixed[:, cs] + sb_ref[:, gs])).astype(BF16)

    y_a = jnp.dot(s_scr[...], wa_ref[...], preferred_element_type=F32)
    o = jnp.concatenate([o_ref[0, hd] for hd in range(ATT_HEADS)], axis=-1)
    y_b = jnp.dot(o, wb_ref[...], preferred_element_type=F32)
    merged = ga_ref[...].astype(F32) * y_a + gb_ref[...].astype(F32) * y_b
    y = jnp.dot(merged.astype(BF16), wo_ref[...], preferred_element_type=F32)
    x1 = x_ref[...] + gate1_ref[0] * y
    x1_ref[...] = x1

    h2 = _rms(x1, n2g_ref[...]) * (1.0 + scale2_ref[0]) + shift2_ref[0]
    h2_ref[...] = h2.astype(BF16)
    h_hi = h2.astype(BF16)
    h_lo = (h2 - h_hi.astype(F32)).astype(BF16)
    r_hi = jnp.dot(h_hi, wr_ref[...], preferred_element_type=F32)
    r_lo = jnp.dot(h_lo, wr_ref[:, :LANES], preferred_element_type=F32)
    logits = r_hi[:, :LANES] + r_hi[:, LANES:] + r_lo + br_ref[...]
    info, counts = _route(logits)
    info_ref[...] = info
    counts_ref[0] = counts


def _merge_call(x2, u, v, o, ga, gb, gate1, shift2, scale2, n2g, sgu_w, sgu_bias, wa, wb, wo,
                w_router, b_router, B, S):
    T, D = x2.shape
    tm = TOKEN_TILE
    tpb = S // tm
    tok = pl.BlockSpec((tm, D), lambda i: (i, 0))
    per_b = pl.BlockSpec((1, 1, D), lambda i: (i // tpb, 0, 0))
    head = pl.BlockSpec((1, ATT_HEADS, tm, ATT_HEAD_DIM), lambda i: (i // tpb, 0, i % tpb, 0))
    full = lambda *shape: pl.BlockSpec(shape, lambda i: (0,) * len(shape))
    return pl.pallas_call(
        _merge_kernel,
        grid=(T // tm,),
        in_specs=[tok, tok, tok, head, tok, tok, per_b, per_b, per_b, full(1, D),
                  full(SGU_GROUPS, CHUNK, CHUNK), full(CHUNK, D),
                  full(D, D), full(D, D), full(D, D), full(D, 2 * LANES), full(1, LANES)],
        out_specs=[tok, tok, pl.BlockSpec((tm, LANES), lambda i: (i, 0)),
                   pl.BlockSpec((1, 1, LANES), lambda i: (i, 0, 0))],
        out_shape=[jax.ShapeDtypeStruct((T, D), F32), jax.ShapeDtypeStruct((T, D), BF16),
                   jax.ShapeDtypeStruct((T, LANES), F32),
                   jax.ShapeDtypeStruct((T // tm, 1, LANES), F32)],
        scratch_shapes=[pltpu.VMEM((tm, D), BF16)],
        compiler_params=pltpu.CompilerParams(dimension_semantics=("parallel",),
                                             vmem_limit_bytes=VMEM_LIMIT),
        name="merge",
    )(x2, u, v, o, ga, gb, gate1, shift2, scale2, n2g, sgu_w, sgu_bias, wa, wb, wo,
      w_router, b_router)


def _dispatch_plan(counts, n_ffn_tiles):
    units = (counts + SUBLANES - 1) // SUBLANES
    rows = units * SUBLANES
    local_start = jnp.cumsum(rows, axis=1) - rows
    class_rows = jnp.sum(rows, axis=0)
    class_tiles = (class_rows + FFN_TILE - 1) // FFN_TILE
    cum_tiles = jnp.cumsum(class_tiles)
    seg_start = (cum_tiles - class_tiles) * FFN_TILE
    run_start = seg_start[None, :] + jnp.cumsum(rows, axis=0) - rows
    n_used = cum_tiles[-1]
    tile_idx = jnp.maximum(jnp.minimum(jnp.arange(n_ffn_tiles, dtype=jnp.int32), n_used - 1), 0)
    tile_cls = jnp.sum((tile_idx[:, None] >= cum_tiles[None, :]).astype(jnp.int32), axis=1)
    pair = tile_cls % len(PAIRS)
    group = tile_cls // len(PAIRS)
    pairs = jnp.asarray(PAIRS, jnp.int32)
    i32 = lambda a: a.astype(jnp.int32).reshape(-1)
    return dict(
        run_start=i32(run_start), local_start=i32(local_start), units=i32(units),
        tile_units=i32(jnp.sum(units, axis=1)),
        tail_start=i32(seg_start + class_rows),
        tail_units=i32((class_tiles * FFN_TILE - class_rows) // SUBLANES),
        tile_idx=i32(tile_idx), n_used=i32(n_used),
        tile_ea=i32(group * EXPERTS_PER_GROUP + pairs[pair, 0]),
        tile_eb=i32(group * EXPERTS_PER_GROUP + pairs[pair, 1]))


def _run_pieces(tile, run_start, local_start, units, make_copy):
    def per_class(c, carry):
        k = tile * N_CLASSES + c
        g0 = run_start[k]
        l0 = local_start[k]

        def per_piece(u, carry2):
            off = u * SUBLANES
            make_copy(pl.multiple_of(l0 + off, SUBLANES), pl.multiple_of(g0 + off, SUBLANES)).start()
            return carry2

        return lax.fori_loop(0, units[k], per_piece, carry)

    lax.fori_loop(0, N_CLASSES, per_class, 0)


def _wait_pieces(n, make_copy):
    def body(_, carry):
        make_copy(0, 0).wait()
        return carry

    lax.fori_loop(0, n, body, 0)


def _dispatch_kernel(run_start, local_start, units, tile_units, tail_start, tail_units, n_used,
                     h2_ref, info_ref, hs_ref, sorted_scr, zero_scr, sems, tail_sem, unused_sem):
    i = pl.program_id(0)
    nt = pl.num_programs(0)
    slot = i % 2
    n_ffn_tiles = hs_ref.shape[0] // FFN_TILE

    def piece_copy(s):
        def make(local_row, global_row):
            return pltpu.make_async_copy(sorted_scr.at[s, pl.ds(local_row, SUBLANES), :],
                                         hs_ref.at[pl.ds(global_row, SUBLANES), :], sems.at[s])
        return make

    def tail_copy(global_row):
        return pltpu.make_async_copy(zero_scr.at[pl.ds(0, SUBLANES), :],
                                     hs_ref.at[pl.ds(global_row, SUBLANES), :], tail_sem)

    def unused_copy(t):
        return pltpu.make_async_copy(
            zero_scr, hs_ref.at[pl.ds(pl.multiple_of(t * FFN_TILE, FFN_TILE), FFN_TILE), :],
            unused_sem)

    def for_slot(fn):
        for s in range(2):
            pl.when(slot == s)(lambda s=s: fn(s))

    @pl.when(i >= 2)
    def _():
        for_slot(lambda s: _wait_pieces(tile_units[i - 2], piece_copy(s)))

    @pl.when(i == 0)
    def _():
        zero_scr[...] = jnp.zeros_like(zero_scr)

        def per_class(c, carry):
            def per_piece(u, carry2):
                tail_copy(pl.multiple_of(tail_start[c] + u * SUBLANES, SUBLANES)).start()
                return carry2
            return lax.fori_loop(0, tail_units[c], per_piece, carry)

        lax.fori_loop(0, N_CLASSES, per_class, 0)

        def per_unused(t, carry):
            unused_copy(t).start()
            return carry

        lax.fori_loop(n_used[0], n_ffn_tiles, per_unused, 0)

    info = info_ref[...]
    dest_row = info.T[INFO_DEST:INFO_DEST + 1, :]
    rows = lax.broadcasted_iota(jnp.int32, (SORT_ROWS, TOKEN_TILE), 0).astype(F32)
    perm = jnp.where(rows == dest_row, 1.0, 0.0).astype(BF16)
    lane = lax.broadcasted_iota(jnp.int32, info.shape, 1)
    weights = jnp.where(lane == INFO_DEST, 0.0, info).astype(BF16)

    def sort_and_send(s):
        sorted_scr[s, :, :D_MODEL] = jnp.dot(perm, h2_ref[...], preferred_element_type=F32)
        sorted_scr[s, :, D_MODEL:] = jnp.dot(perm, weights, preferred_element_type=F32)
        _run_pieces(i, run_start, local_start, units, piece_copy(s))

    for_slot(sort_and_send)

    @pl.when(i == nt - 1)
    def _():
        for_slot(lambda s: _wait_pieces(tile_units[i], piece_copy(s)))

        @pl.when(nt >= 2)
        def _():
            for_slot(lambda s: _wait_pieces(tile_units[i - 1], piece_copy(1 - s)))

        def per_class(c, carry):
            def per_piece(u, carry2):
                tail_copy(0).wait()
                return carry2
            return lax.fori_loop(0, tail_units[c], per_piece, carry)

        lax.fori_loop(0, N_CLASSES, per_class, 0)

        def per_unused(t, carry):
            unused_copy(0).wait()
            return carry

        lax.fori_loop(n_used[0], n_ffn_tiles, per_unused, 0)


def _dispatch_call(plan, h2, info, n_ffn_tiles):
    T, D = h2.shape
    tm = TOKEN_TILE
    grid_spec = pltpu.PrefetchScalarGridSpec(
        num_scalar_prefetch=7,
        grid=(T // tm,),
        in_specs=[pl.BlockSpec((tm, D), lambda i, *_: (i, 0)),
                  pl.BlockSpec((tm, LANES), lambda i, *_: (i, 0))],
        out_specs=pl.BlockSpec(memory_space=pl.ANY),
        scratch_shapes=[pltpu.VMEM((2, SORT_ROWS, ROW_WIDTH), F32),
                        pltpu.VMEM((FFN_TILE, ROW_WIDTH), F32),
                        pltpu.SemaphoreType.DMA((2,)),
                        pltpu.SemaphoreType.DMA(()),
                        pltpu.SemaphoreType.DMA(())])
    return pl.pallas_call(
        _dispatch_kernel,
        grid_spec=grid_spec,
        out_shape=jax.ShapeDtypeStruct((n_ffn_tiles * FFN_TILE, ROW_WIDTH), F32),
        compiler_params=pltpu.CompilerParams(dimension_semantics=("arbitrary",),
                                             vmem_limit_bytes=VMEM_LIMIT),
        name="dispatch",
    )(plan["run_start"], plan["local_start"], plan["units"], plan["tile_units"],
      plan["tail_start"], plan["tail_units"], plan["n_used"], h2, info)


def _ffn_kernel(tile_idx, tile_ea, tile_eb, n_used, hs_ref, wga_ref, wua_ref, wda_ref,
                wgb_ref, wub_ref, wdb_ref, ys_ref):
    @pl.when(pl.program_id(0) >= n_used[0])
    def _():
        ys_ref[...] = jnp.zeros_like(ys_ref)

    @pl.when(pl.program_id(0) < n_used[0])
    def _():
        h = hs_ref[:, :D_MODEL].astype(BF16)
        pv = hs_ref[:, D_MODEL:]
        p_lo = pv[:, INFO_PLO_H:INFO_PLO_H + 1] + pv[:, INFO_PLO_L:INFO_PLO_L + 1]
        p_hi = pv[:, INFO_PHI_H:INFO_PHI_H + 1] + pv[:, INFO_PHI_L:INFO_PHI_L + 1]

        def hidden(wg_ref, wu_ref, p):
            a = jnp.dot(h, wg_ref[0], preferred_element_type=F32)
            b = jnp.dot(h, wu_ref[0], preferred_element_type=F32)
            return (a * jax.nn.sigmoid(a) * b * p).astype(BF16)

        ys_ref[...] = (jnp.dot(hidden(wga_ref, wua_ref, p_lo), wda_ref[0], preferred_element_type=F32)
                       + jnp.dot(hidden(wgb_ref, wub_ref, p_hi), wdb_ref[0], preferred_element_type=F32))


def _ffn_call(plan, hs, wg, wu, wd, n_ffn_tiles):
    D = D_MODEL
    rows = pl.BlockSpec((FFN_TILE, ROW_WIDTH), lambda t, idx, ea, eb, n: (idx[t], 0))
    w_in_a = pl.BlockSpec((1, D, D_EXPERT), lambda t, idx, ea, eb, n: (ea[t], 0, 0))
    w_out_a = pl.BlockSpec((1, D_EXPERT, D), lambda t, idx, ea, eb, n: (ea[t], 0, 0))
    w_in_b = pl.BlockSpec((1, D, D_EXPERT), lambda t, idx, ea, eb, n: (eb[t], 0, 0))
    w_out_b = pl.BlockSpec((1, D_EXPERT, D), lambda t, idx, ea, eb, n: (eb[t], 0, 0))
    grid_spec = pltpu.PrefetchScalarGridSpec(
        num_scalar_prefetch=4,
        grid=(n_ffn_tiles,),
        in_specs=[rows, w_in_a, w_in_a, w_out_a, w_in_b, w_in_b, w_out_b],
        out_specs=pl.BlockSpec((FFN_TILE, D), lambda t, idx, ea, eb, n: (t, 0)))
    return pl.pallas_call(
        _ffn_kernel,
        grid_spec=grid_spec,
        out_shape=jax.ShapeDtypeStruct((n_ffn_tiles * FFN_TILE, D), F32),
        compiler_params=pltpu.CompilerParams(dimension_semantics=("arbitrary",),
                                             vmem_limit_bytes=VMEM_LIMIT),
        name="ffn",
    )(plan["tile_idx"], plan["tile_ea"], plan["tile_eb"], plan["n_used"],
      hs, wg, wu, wd, wg, wu, wd)


def _combine_kernel(run_start, local_start, units, tile_units,
                    x1_ref, info_ref, gate2_ref, ys_ref, o_ref, ybuf, sems):
    i = pl.program_id(0)
    nt = pl.num_programs(0)
    slot = i % 2

    def piece_copy(s):
        def make(local_row, global_row):
            return pltpu.make_async_copy(ys_ref.at[pl.ds(global_row, SUBLANES), :],
                                         ybuf.at[s, pl.ds(local_row, SUBLANES), :], sems.at[s])
        return make

    def for_slot(fn):
        for s in range(2):
            pl.when(slot == s)(lambda s=s: fn(s))

    def fetch(tile, s):
        _run_pieces(tile, run_start, local_start, units, piece_copy(s))

    @pl.when(i == 0)
    def _():
        ybuf[...] = jnp.zeros_like(ybuf)
        fetch(0, 0)

    @pl.when(i + 1 < nt)
    def _():
        for_slot(lambda s: fetch(i + 1, 1 - s))

    dest_col = info_ref[:, INFO_DEST:INFO_DEST + 1]
    cols = lax.broadcasted_iota(jnp.int32, (TOKEN_TILE, SORT_ROWS), 1).astype(F32)
    unperm = jnp.where(cols == dest_col, 1.0, 0.0).astype(BF16)

    def finish(s):
        _wait_pieces(tile_units[i], piece_copy(s))
        y = jnp.dot(unperm, ybuf[s].astype(BF16), preferred_element_type=F32)
        o_ref[...] = x1_ref[...] + gate2_ref[0] * y

    for_slot(finish)


def _combine_call(plan, x1, info, gate2, ys, B, S):
    T, D = x1.shape
    tm = TOKEN_TILE
    tpb = S // tm
    grid_spec = pltpu.PrefetchScalarGridSpec(
        num_scalar_prefetch=4,
        grid=(T // tm,),
        in_specs=[pl.BlockSpec((tm, D), lambda i, *_: (i, 0)),
                  pl.BlockSpec((tm, LANES), lambda i, *_: (i, 0)),
                  pl.BlockSpec((1, 1, D), lambda i, *_: (i // tpb, 0, 0)),
                  pl.BlockSpec(memory_space=pl.ANY)],
        out_specs=pl.BlockSpec((tm, D), lambda i, *_: (i, 0)),
        scratch_shapes=[pltpu.VMEM((2, SORT_ROWS, D), F32),
                        pltpu.SemaphoreType.DMA((2,))])
    return pl.pallas_call(
        _combine_kernel,
        grid_spec=grid_spec,
        out_shape=jax.ShapeDtypeStruct((T, D), F32),
        compiler_params=pltpu.CompilerParams(dimension_semantics=("arbitrary",),
                                             vmem_limit_bytes=VMEM_LIMIT),
        name="combine",
    )(plan["run_start"], plan["local_start"], plan["units"], plan["tile_units"],
      x1, info, gate2, ys)


def kernel(x, c, w_ada, b_ada, norm1_g, w_in, sgu_norm_g, sgu_w, sgu_b, q_norm_g, k_norm_g,
           w_proj_a, w_proj_b, w_out, norm2_g, w_router_group, b_router_group,
           w_router_expert, b_router_expert, w_gate, w_up, w_down):
    B, S, D = x.shape
    T = B * S
    depth = w_ada.shape[0]
    n_token_tiles = T // TOKEN_TILE
    max_rows = T + n_token_tiles * N_CLASSES * (SUBLANES - 1) + N_CLASSES * (FFN_TILE - SUBLANES)
    n_ffn_tiles = -(-max_rows // FFN_TILE)
    x2 = x.reshape(T, D)
    for l in range(depth):
        mod = _ada_call(c, w_ada[l], b_ada[l])
        shift1, scale1, gate1, shift2, scale2, gate2 = [
            m.reshape(B, 1, D) for m in jnp.split(mod, 6, axis=-1)]

        u, v, q, k, va, ga, gb = _inproj_call(
            x2, shift1, scale1, norm1_g[l].reshape(1, D), w_in[l].astype(BF16),
            sgu_norm_g[l].reshape(1, D), q_norm_g[l].reshape(1, ATT_HEAD_DIM),
            k_norm_g[l].reshape(1, ATT_HEAD_DIM), B, S, tm=512)

        o = _attention_call(q, k, va)

        w_router = jnp.concatenate(
            [w_router_group[l],
             jnp.transpose(w_router_expert[l], (1, 0, 2)).reshape(D, N_EXPERTS)], axis=1)
        w_router = jnp.pad(w_router, ((0, 0), (0, LANES - w_router.shape[1])))
        w_router_hi = w_router.astype(BF16)
        w_router_lo = (w_router - w_router_hi.astype(F32)).astype(BF16)
        w_router = jnp.concatenate([w_router_hi, w_router_lo], axis=1)
        b_router = jnp.concatenate([b_router_group[l], b_router_expert[l].reshape(N_EXPERTS)])
        b_router = jnp.pad(b_router, (0, LANES - b_router.shape[0])).reshape(1, LANES)
        sgu_bias = jnp.repeat(sgu_b[l].T, D // SGU_GROUPS, axis=1)

        x1, h2, info, counts = _merge_call(
            x2, u, v, o, ga, gb, gate1, shift2, scale2, norm2_g[l].reshape(1, D), sgu_w[l],
            sgu_bias, w_proj_a[l].astype(BF16), w_proj_b[l].astype(BF16), w_out[l].astype(BF16),
            w_router, b_router, B, S)

        plan = _dispatch_plan(counts[:, 0, :N_CLASSES].astype(jnp.int32), n_ffn_tiles)
        hs = _dispatch_call(plan, h2, info, n_ffn_tiles)
        ys = _ffn_call(plan, hs, w_gate[l].astype(BF16), w_up[l].astype(BF16),
                       w_down[l].astype(BF16), n_ffn_tiles)
        x2 = _combine_call(plan, x1, info, gate2, ys, B, S)
    return x2.reshape(B, S, D)
```

```python
import jax
import jax.numpy as jnp
import numpy as np
from jax import lax
from jax.experimental import pallas as pl
from jax.experimental.pallas import tpu as pltpu

D_MODEL = 1024
CHUNK = 128
SGU_GROUPS = 8
ATT_HEADS = 8
ATT_HEAD_DIM = 128
DILATED_PATTERNS = ((128, 1), (512, 4), (2048, 16))
N_GROUPS = 4
EXPERTS_PER_GROUP = 4
N_EXPERTS = N_GROUPS * EXPERTS_PER_GROUP
D_EXPERT = 512
EPS = 1e-6
NEG_INF = -1e30

N_IN_SPLITS = 7
LANES = 128
SUBLANES = 8
EXPERT_LANE0 = N_GROUPS

PAIRS = ((0, 1), (0, 2), (0, 3), (1, 2), (1, 3), (2, 3))
N_CLASSES = N_GROUPS * len(PAIRS)
TOKEN_TILE = 512
SORT_ROWS = TOKEN_TILE + 3 * 64
ROW_WIDTH = D_MODEL + LANES
FFN_TILE = 512
INFO_DEST, INFO_PLO_H, INFO_PHI_H, INFO_PLO_L, INFO_PHI_L = 0, 1, 2, 3, 4

VMEM_LIMIT = 56 * 1024 * 1024
BF16 = jnp.bfloat16
F32 = jnp.float32

assert SORT_ROWS >= TOKEN_TILE + N_CLASSES * (SUBLANES - 1)


def _rms(x, g):
    return x * lax.rsqrt(jnp.mean(x * x, axis=-1, keepdims=True) + EPS) * g


def _ada_kernel(c_ref, w_ref, b_ref, o_ref):
    c = c_ref[...]
    cond = c * jax.nn.sigmoid(c)
    o_ref[...] = jnp.dot(cond, w_ref[...], preferred_element_type=F32,
                         precision=lax.Precision.HIGHEST) + b_ref[...]


def _ada_call(c, w_ada, b_ada):
    B, D = c.shape
    N = w_ada.shape[1]
    tn = 1024
    return pl.pallas_call(
        _ada_kernel,
        grid=(N // tn,),
        in_specs=[pl.BlockSpec((B, D), lambda j: (0, 0)),
                  pl.BlockSpec((D, tn), lambda j: (0, j)),
                  pl.BlockSpec((1, tn), lambda j: (0, j))],
        out_specs=pl.BlockSpec((B, tn), lambda j: (0, j)),
        out_shape=jax.ShapeDtypeStruct((B, N), F32),
        compiler_params=pltpu.CompilerParams(dimension_semantics=("arbitrary",),
                                             vmem_limit_bytes=VMEM_LIMIT),
        name="adaln",
    )(c, w_ada, b_ada.reshape(1, N))


def _inproj_kernel(x_ref, shift_ref, scale_ref, n1g_ref, w_ref, sgug_ref, qg_ref, kg_ref,
                   u_ref, v_ref, q_ref, k_ref, va_ref, ga_ref, gb_ref):
    x = x_ref[...]
    h = _rms(x, n1g_ref[...]) * (1.0 + scale_ref[0]) + shift_ref[0]
    h = h.astype(BF16)

    def proj(i):
        return jnp.dot(h, w_ref[:, i * D_MODEL:(i + 1) * D_MODEL], preferred_element_type=F32)

    u_ref[...] = jax.nn.gelu(proj(0)).astype(BF16)
    v_ref[...] = _rms(jax.nn.gelu(proj(1)), sgug_ref[...]).astype(BF16)

    q = proj(2)
    qscale = ATT_HEAD_DIM ** -0.5
    for hd in range(ATT_HEADS):
        sl = slice(hd * ATT_HEAD_DIM, (hd + 1) * ATT_HEAD_DIM)
        q_ref[0, hd] = (_rms(q[:, sl], qg_ref[...]) * qscale).astype(BF16)
    k = proj(3)
    for hd in range(ATT_HEADS):
        sl = slice(hd * ATT_HEAD_DIM, (hd + 1) * ATT_HEAD_DIM)
        k_ref[0, hd] = _rms(k[:, sl], kg_ref[...]).astype(BF16)
    va = proj(4)
    for hd in range(ATT_HEADS):
        sl = slice(hd * ATT_HEAD_DIM, (hd + 1) * ATT_HEAD_DIM)
        va_ref[0, hd] = va[:, sl].astype(BF16)
    ga_ref[...] = jax.nn.sigmoid(proj(5)).astype(BF16)
    gb_ref[...] = jax.nn.sigmoid(proj(6)).astype(BF16)


def _inproj_call(x2, shift1, scale1, n1g, w_in, sgug, qg, kg, B, S, tm):
    T, D = x2.shape
    tpb = S // tm
    tok = pl.BlockSpec((tm, D), lambda i: (i, 0))
    per_b = pl.BlockSpec((1, 1, D), lambda i: (i // tpb, 0, 0))
    row = lambda n: pl.BlockSpec((1, n), lambda i: (0, 0))
    head = pl.BlockSpec((1, ATT_HEADS, tm, ATT_HEAD_DIM), lambda i: (i // tpb, 0, i % tpb, 0))
    tok_sds = jax.ShapeDtypeStruct((T, D), BF16)
    head_sds = jax.ShapeDtypeStruct((B, ATT_HEADS, S, ATT_HEAD_DIM), BF16)
    return pl.pallas_call(
        _inproj_kernel,
        grid=(T // tm,),
        in_specs=[tok, per_b, per_b, row(D),
                  pl.BlockSpec((D, N_IN_SPLITS * D), lambda i: (0, 0), pipeline_mode=pl.Buffered(1)),
                  row(D), row(ATT_HEAD_DIM), row(ATT_HEAD_DIM)],
        out_specs=[tok, tok, head, head, head, tok, tok],
        out_shape=[tok_sds, tok_sds, head_sds, head_sds, head_sds, tok_sds, tok_sds],
        compiler_params=pltpu.CompilerParams(dimension_semantics=("parallel",),
                                             vmem_limit_bytes=VMEM_LIMIT),
        name="inproj",
    )(x2, shift1, scale1, n1g, w_in, sgug, qg, kg)


ATT_BLK = 128
STRIDED_BLK = 256
HEADS_PER_STEP = 2
RESIDUES = 4
LOCAL_PATTERNS = tuple(p for p in DILATED_PATTERNS if p[1] % RESIDUES != 0)
STRIDED_PATTERNS = tuple(p for p in DILATED_PATTERNS if p[1] % RESIDUES == 0)
assert all(d == 1 and w <= ATT_BLK for w, d in LOCAL_PATTERNS)


def _log_count_bias(count):
    return jnp.asarray(np.where(count > 0, np.log(np.maximum(count, 1)), NEG_INF), F32)


def _local_bias():
    a = np.arange(ATT_BLK)[:, None]
    col = np.arange(2 * ATT_BLK)[None, :]
    delta = ATT_BLK + a - col
    count = np.zeros(delta.shape, np.int64)
    for window, _ in LOCAL_PATTERNS:
        count += (delta >= 0) & (delta <= window)
    return _log_count_bias(count)


def _strided_bias(n):
    nblk = n // STRIDED_BLK
    a = np.arange(STRIDED_BLK)[:, None]
    col = np.arange(n)[None, :]
    delta = (STRIDED_BLK * (nblk - 1 - col // STRIDED_BLK) + a - col % STRIDED_BLK) * RESIDUES
    count = np.zeros(delta.shape, np.int64)
    for window, dilation in STRIDED_PATTERNS:
        count += (delta >= 0) & (delta <= window) & (delta % dilation == 0)
    return _log_count_bias(count)


def _qk(q, k):
    return lax.dot_general(q, k, (((1,), (1,)), ((), ())), preferred_element_type=F32)


def _attention_kernel(q_ref, k_ref, v_ref, lbias_ref, sbias_ref, o_ref,
                      stage, q4, k4, v4, ve, acc_scr, m_scr, l_scr):
    S = q_ref.shape[2]
    n = S // RESIDUES
    nblk_s = n // STRIDED_BLK
    nblk = S // ATT_BLK
    Dh = ATT_HEAD_DIM
    ones = jnp.ones((S, LANES), BF16)

    for hd in range(HEADS_PER_STEP):
        ve[hd, :, Dh:] = ones
        ve[hd, :, :Dh] = v_ref[0, hd]
        for r in range(RESIDUES):
            v4[hd, r, :, Dh:] = ones[:n]
        for src, dst in ((q_ref, q4), (k_ref, k4), (v_ref, v4)):
            stage[hd] = src[0, hd].astype(F32)
            for r in range(RESIDUES):
                dst[hd, r, :, :Dh] = stage[hd, pl.ds(r, n, stride=RESIDUES), :].astype(BF16)

        for r in range(RESIDUES):
            for jb in range(nblk_s):
                nk = (jb + 1) * STRIDED_BLK
                s = _qk(q4[hd, r, jb * STRIDED_BLK:(jb + 1) * STRIDED_BLK, :], k4[hd, r, :nk, :])
                s = s + sbias_ref[:, (nblk_s - 1 - jb) * STRIDED_BLK:]
                m = jnp.max(s, axis=-1, keepdims=True)
                p = jnp.exp(s - m)
                acc = jnp.dot(p.astype(BF16), v4[hd, r, :nk, :], preferred_element_type=F32)
                rows = pl.ds(RESIDUES * jb * STRIDED_BLK + r, STRIDED_BLK, stride=RESIDUES)
                acc_scr[hd, rows, :] = acc[:, :Dh]
                m_scr[hd, rows, :] = jnp.broadcast_to(m, (STRIDED_BLK, LANES))
                l_scr[hd, rows, :] = acc[:, Dh:]

        for c in range(nblk):
            k0 = max(c - 1, 0) * ATT_BLK
            nk = (c + 1) * ATT_BLK - k0
            blk = slice(c * ATT_BLK, (c + 1) * ATT_BLK)
            s = _qk(q_ref[0, hd, blk, :], k_ref[0, hd, k0:k0 + nk, :]) + lbias_ref[:, 2 * ATT_BLK - nk:]
            m_s = m_scr[hd, blk, :]
            m = jnp.maximum(jnp.max(s, axis=-1, keepdims=True), m_s)
            w = jnp.exp(m_s - m)
            p = jnp.exp(s - jnp.concatenate([m] * (nk // LANES), axis=-1))
            acc = jnp.dot(p.astype(BF16), ve[hd, k0:k0 + nk, :], preferred_element_type=F32)
            l = acc[:, Dh:] + l_scr[hd, blk, :] * w
            o_ref[0, hd, blk, :] = ((acc[:, :Dh] + acc_scr[hd, blk, :] * w) / l).astype(BF16)


def _attention_call(q, k, v):
    B, H, S, Dh = q.shape
    n = S // RESIDUES
    P = HEADS_PER_STEP
    blk = pl.BlockSpec((1, P, S, Dh), lambda b, h: (b, h, 0, 0))
    return pl.pallas_call(
        _attention_kernel,
        grid=(B, H // P),
        in_specs=[blk, blk, blk,
                  pl.BlockSpec((ATT_BLK, 2 * ATT_BLK), lambda b, h: (0, 0)),
                  pl.BlockSpec((STRIDED_BLK, n), lambda b, h: (0, 0))],
        out_specs=blk,
        out_shape=jax.ShapeDtypeStruct((B, H, S, Dh), BF16),
        scratch_shapes=[pltpu.VMEM((P, S, Dh), F32),
                        pltpu.VMEM((P, RESIDUES, n, Dh), BF16),
                        pltpu.VMEM((P, RESIDUES, n, Dh), BF16),
                        pltpu.VMEM((P, RESIDUES, n, Dh + LANES), BF16),
                        pltpu.VMEM((P, S, Dh + LANES), BF16),
                        pltpu.VMEM((P, S, Dh), F32),
                        pltpu.VMEM((P, S, LANES), F32),
                        pltpu.VMEM((P, S, LANES), F32)],
        compiler_params=pltpu.CompilerParams(dimension_semantics=("parallel", "parallel"),
                                             vmem_limit_bytes=VMEM_LIMIT),
        name="attention",
    )(q, k, v, _local_bias(), _strided_bias(n))


def _route(logits):
    tm = logits.shape[0]
    lane = lax.broadcasted_iota(jnp.int32, logits.shape, 1)
    lanef = lane.astype(F32)

    def masked_top(mask):
        top = jnp.max(jnp.where(mask, logits, -jnp.inf), axis=-1, keepdims=True)
        idx = jnp.min(jnp.where(mask & (logits == top), lanef, float(LANES)), axis=-1, keepdims=True)
        return top, idx

    gmask = lane < N_GROUPS
    gmax, gidx = masked_top(gmask)
    p_group = 1.0 / jnp.sum(jnp.where(gmask, jnp.exp(logits - gmax), 0.0), axis=-1, keepdims=True)
    e_lo = EXPERT_LANE0 + gidx * EXPERTS_PER_GROUP
    emask = (lanef >= e_lo) & (lanef < e_lo + EXPERTS_PER_GROUP)
    v1, i1 = masked_top(emask)
    v2, i2 = masked_top(emask & (lanef != i1))
    e21 = jnp.exp(v2 - v1)
    p1 = p_group / (1.0 + e21)
    p2 = p_group * e21 / (1.0 + e21)
    first_is_lo = i1 < i2
    lo = jnp.minimum(i1, i2) - e_lo
    hi = jnp.maximum(i1, i2) - e_lo
    pair = lo * (7.0 - lo) * 0.5 + (hi - lo - 1.0)
    cls = gidx * float(len(PAIRS)) + pair
    p_lo = jnp.where(first_is_lo, p1, p2)
    p_hi = jnp.where(first_is_lo, p2, p1)

    onehot = lanef == cls
    onehot_b = jnp.where(onehot, 1.0, 0.0).astype(BF16)
    r = lax.broadcasted_iota(jnp.int32, (tm, tm), 0)
    c = lax.broadcasted_iota(jnp.int32, (tm, tm), 1)
    before = jnp.where(c < r, 1.0, 0.0).astype(BF16)
    rank = jnp.dot(before, onehot_b, preferred_element_type=F32)
    counts = jnp.sum(jnp.where(onehot, 1.0, 0.0), axis=0, keepdims=True)
    units = jnp.ceil(counts * (1.0 / SUBLANES))
    ur = lax.broadcasted_iota(jnp.int32, (LANES, LANES), 0)
    uc = lax.broadcasted_iota(jnp.int32, (LANES, LANES), 1)
    upper = jnp.where(ur < uc, 1.0, 0.0).astype(BF16)
    start = jnp.dot(jnp.broadcast_to(units, (SUBLANES, LANES)).astype(BF16), upper,
                    preferred_element_type=F32)[0:1] * float(SUBLANES)
    dest = jnp.sum(jnp.where(onehot, start + rank, 0.0), axis=-1, keepdims=True)

    def hi_part(p):
        return p.astype(BF16).astype(F32)

    info = jnp.where(lane == INFO_DEST, dest, 0.0)
    info = jnp.where(lane == INFO_PLO_H, hi_part(p_lo), info)
    info = jnp.where(lane == INFO_PHI_H, hi_part(p_hi), info)
    info = jnp.where(lane == INFO_PLO_L, p_lo - hi_part(p_lo), info)
    info = jnp.where(lane == INFO_PHI_L, p_hi - hi_part(p_hi), info)
    return info, counts


def _merge_kernel(x_ref, u_ref, v_ref, o_ref, ga_ref, gb_ref, gate1_ref, shift2_ref, scale2_ref,
                  n2g_ref, sw_ref, sb_ref, wa_ref, wb_ref, wo_ref, wr_ref, br_ref,
                  x1_ref, h2_ref, info_ref, counts_ref, s_scr):
    tm = x_ref.shape[0]
    row = lax.broadcasted_iota(jnp.int32, (CHUNK, CHUNK), 0)
    colm = lax.broadcasted_iota(jnp.int32, (CHUNK, CHUNK), 1)
    causal = colm <= row
    for g in range(SGU_GROUPS):
        w = jnp.where(causal, sw_ref[g], 0.0).astype(BF16)
        gs = slice(g * CHUNK, (g + 1) * CHUNK)
        chunks = [slice(c * CHUNK, (c + 1) * CHUNK) for c in range(tm // CHUNK)]
        mixed = jnp.dot(w, jnp.concatenate([v_ref[cs, gs] for cs in chunks], axis=1),
                        preferred_element_type=F32)
        for cs in chunks:
            s_scr[cs, gs] = (u_ref[cs, gs].astype(F32) * (mixed[:, cs] + sb_ref[:, gs])).astype(BF16)

    y_a = jnp.dot(s_scr[...], wa_ref[...], preferred_element_type=F32)
    o = jnp.concatenate([o_ref[0, hd] for hd in range(ATT_HEADS)], axis=-1)
    y_b = jnp.dot(o, wb_ref[...], preferred_element_type=F32)
    merged = ga_ref[...].astype(F32) * y_a + gb_ref[...].astype(F32) * y_b
    y = jnp.dot(merged.astype(BF16), wo_ref[...], preferred_element_type=F32)
    x1 = x_ref[...] + gate1_ref[0] * y
    x1_ref[...] = x1

    h2 = _rms(x1, n2g_ref[...]) * (1.0 + scale2_ref[0]) + shift2_ref[0]
    h2_ref[...] = h2.astype(BF16)
    h_hi = h2.astype(BF16)
    h_lo = (h2 - h_hi.astype(F32)).astype(BF16)
    r_hi = jnp.dot(h_hi, wr_ref[...], preferred_element_type=F32)
    r_lo = jnp.dot(h_lo, wr_ref[:, :LANES], preferred_element_type=F32)
    logits = r_hi[:, :LANES] + r_hi[:, LANES:] + r_lo + br_ref[...]
    info, counts = _route(logits)
    info_ref[...] = info
    counts_ref[0] = counts


def _merge_call(x2, u, v, o, ga, gb, gate1, shift2, scale2, n2g, sgu_w, sgu_bias, wa, wb, wo,
                w_router, b_router, B, S):
    T, D = x2.shape
    tm = TOKEN_TILE
    tpb = S // tm
    tok = pl.BlockSpec((tm, D), lambda i: (i, 0))
    per_b = pl.BlockSpec((1, 1, D), lambda i: (i // tpb, 0, 0))
    head = pl.BlockSpec((1, ATT_HEADS, tm, ATT_HEAD_DIM), lambda i: (i // tpb, 0, i % tpb, 0))
    full = lambda *shape: pl.BlockSpec(shape, lambda i: (0,) * len(shape))
    return pl.pallas_call(
        _merge_kernel,
        grid=(T // tm,),
        in_specs=[tok, tok, tok, head, tok, tok, per_b, per_b, per_b, full(1, D),
                  full(SGU_GROUPS, CHUNK, CHUNK), full(CHUNK, D),
                  full(D, D), full(D, D), full(D, D), full(D, 2 * LANES), full(1, LANES)],
        out_specs=[tok, tok, pl.BlockSpec((tm, LANES), lambda i: (i, 0)),
                   pl.BlockSpec((1, 1, LANES), lambda i: (i, 0, 0))],
        out_shape=[jax.ShapeDtypeStruct((T, D), F32), jax.ShapeDtypeStruct((T, D), BF16),
                   jax.ShapeDtypeStruct((T, LANES), F32),
                   jax.ShapeDtypeStruct((T // tm, 1, LANES), F32)],
        scratch_shapes=[pltpu.VMEM((tm, D), BF16)],
        compiler_params=pltpu.CompilerParams(dimension_semantics=("parallel",),
                                             vmem_limit_bytes=VMEM_LIMIT),
        name="merge",
    )(x2, u, v, o, ga, gb, gate1, shift2, scale2, n2g, sgu_w, sgu_bias, wa, wb, wo,
      w_router, b_router)


def _dispatch_plan(counts, n_ffn_tiles):
    units = (counts + SUBLANES - 1) // SUBLANES
    rows = units * SUBLANES
    local_start = jnp.cumsum(rows, axis=1) - rows
    class_rows = jnp.sum(rows, axis=0)
    class_tiles = (class_rows + FFN_TILE - 1) // FFN_TILE
    cum_tiles = jnp.cumsum(class_tiles)
    seg_start = (cum_tiles - class_tiles) * FFN_TILE
    run_start = seg_start[None, :] + jnp.cumsum(rows, axis=0) - rows
    n_used = cum_tiles[-1]
    tile_idx = jnp.maximum(jnp.minimum(jnp.arange(n_ffn_tiles, dtype=jnp.int32), n_used - 1), 0)
    tile_cls = jnp.sum((tile_idx[:, None] >= cum_tiles[None, :]).astype(jnp.int32), axis=1)
    pair = tile_cls % len(PAIRS)
    group = tile_cls // len(PAIRS)
    pairs = jnp.asarray(PAIRS, jnp.int32)
    i32 = lambda a: a.astype(jnp.int32).reshape(-1)
    return dict(
        run_start=i32(run_start), local_start=i32(local_start), units=i32(units),
        tile_units=i32(jnp.sum(units, axis=1)),
        tail_start=i32(seg_start + class_rows),
        tail_units=i32((class_tiles * FFN_TILE - class_rows) // SUBLANES),
        tile_idx=i32(tile_idx), n_used=i32(n_used),
        tile_ea=i32(group * EXPERTS_PER_GROUP + pairs[pair, 0]),
        tile_eb=i32(group * EXPERTS_PER_GROUP + pairs[pair, 1]))


def _run_pieces(tile, run_start, local_start, units, make_copy):
    def per_class(c, carry):
        k = tile * N_CLASSES + c
        g0 = run_start[k]
        l0 = local_start[k]

        def per_piece(u, carry2):
            off = u * SUBLANES
            make_copy(pl.multiple_of(l0 + off, SUBLANES), pl.multiple_of(g0 + off, SUBLANES)).start()
            return carry2

        return lax.fori_loop(0, units[k], per_piece, carry)

    lax.fori_loop(0, N_CLASSES, per_class, 0)


def _wait_pieces(n, make_copy):
    def body(_, carry):
        make_copy(0, 0).wait()
        return carry

    lax.fori_loop(0, n, body, 0)


def _dispatch_kernel(run_start, local_start, units, tile_units, tail_start, tail_units, n_used,
                     h2_ref, info_ref, hs_ref, sorted_scr, zero_scr, sems, tail_sem, unused_sem):
    i = pl.program_id(0)
    nt = pl.num_programs(0)
    slot = i % 2
    n_ffn_tiles = hs_ref.shape[0] // FFN_TILE

    def piece_copy(s):
        def make(local_row, global_row):
            return pltpu.make_async_copy(sorted_scr.at[s, pl.ds(local_row, SUBLANES), :],
                                         hs_ref.at[pl.ds(global_row, SUBLANES), :], sems.at[s])
        return make

    def tail_copy(global_row):
        return pltpu.make_async_copy(zero_scr.at[pl.ds(0, SUBLANES), :],
                                     hs_ref.at[pl.ds(global_row, SUBLANES), :], tail_sem)

    def unused_copy(t):
        return pltpu.make_async_copy(
            zero_scr, hs_ref.at[pl.ds(pl.multiple_of(t * FFN_TILE, FFN_TILE), FFN_TILE), :],
            unused_sem)

    def for_slot(fn):
        for s in range(2):
            pl.when(slot == s)(lambda s=s: fn(s))

    @pl.when(i >= 2)
    def _():
        for_slot(lambda s: _wait_pieces(tile_units[i - 2], piece_copy(s)))

    @pl.when(i == 0)
    def _():
        zero_scr[...] = jnp.zeros_like(zero_scr)

        def per_class(c, carry):
            def per_piece(u, carry2):
                tail_copy(pl.multiple_of(tail_start[c] + u * SUBLANES, SUBLANES)).start()
                return carry2
            return lax.fori_loop(0, tail_units[c], per_piece, carry)

        lax.fori_loop(0, N_CLASSES, per_class, 0)

        def per_unused(t, carry):
            unused_copy(t).start()
            return carry

        lax.fori_loop(n_used[0], n_ffn_tiles, per_unused, 0)

    info = info_ref[...]
    dest_row = info.T[INFO_DEST:INFO_DEST + 1, :]
    rows = lax.broadcasted_iota(jnp.int32, (SORT_ROWS, TOKEN_TILE), 0).astype(F32)
    perm = jnp.where(rows == dest_row, 1.0, 0.0).astype(BF16)
    lane = lax.broadcasted_iota(jnp.int32, info.shape, 1)
    weights = jnp.where(lane == INFO_DEST, 0.0, info).astype(BF16)

    def sort_and_send(s):
        sorted_scr[s, :, :D_MODEL] = jnp.dot(perm, h2_ref[...], preferred_element_type=F32)
        sorted_scr[s, :, D_MODEL:] = jnp.dot(perm, weights, preferred_element_type=F32)
        _run_pieces(i, run_start, local_start, units, piece_copy(s))

    for_slot(sort_and_send)

    @pl.when(i == nt - 1)
    def _():
        for_slot(lambda s: _wait_pieces(tile_units[i], piece_copy(s)))

        @pl.when(nt >= 2)
        def _():
            for_slot(lambda s: _wait_pieces(tile_units[i - 1], piece_copy(1 - s)))

        def per_class(c, carry):
            def per_piece(u, carry2):
                tail_copy(0).wait()
                return carry2
            return lax.fori_loop(0, tail_units[c], per_piece, carry)

        lax.fori_loop(0, N_CLASSES, per_class, 0)

        def per_unused(t, carry):
            unused_copy(0).wait()
            return carry

        lax.fori_loop(n_used[0], n_ffn_tiles, per_unused, 0)


def _dispatch_call(plan, h2, info, n_ffn_tiles):
    T, D = h2.shape
    tm = TOKEN_TILE
    grid_spec = pltpu.PrefetchScalarGridSpec(
        num_scalar_prefetch=7,
        grid=(T // tm,),
        in_specs=[pl.BlockSpec((tm, D), lambda i, *_: (i, 0)),
                  pl.BlockSpec((tm, LANES), lambda i, *_: (i, 0))],
        out_specs=pl.BlockSpec(memory_space=pl.ANY),
        scratch_shapes=[pltpu.VMEM((2, SORT_ROWS, ROW_WIDTH), F32),
                        pltpu.VMEM((FFN_TILE, ROW_WIDTH), F32),
                        pltpu.SemaphoreType.DMA((2,)),
                        pltpu.SemaphoreType.DMA(()),
                        pltpu.SemaphoreType.DMA(())])
    return pl.pallas_call(
        _dispatch_kernel,
        grid_spec=grid_spec,
        out_shape=jax.ShapeDtypeStruct((n_ffn_tiles * FFN_TILE, ROW_WIDTH), F32),
        compiler_params=pltpu.CompilerParams(dimension_semantics=("arbitrary",),
                                             vmem_limit_bytes=VMEM_LIMIT),
        name="dispatch",
    )(plan["run_start"], plan["local_start"], plan["units"], plan["tile_units"],
      plan["tail_start"], plan["tail_units"], plan["n_used"], h2, info)


def _ffn_kernel(tile_idx, tile_ea, tile_eb, n_used, hs_ref, wga_ref, wua_ref, wda_ref,
                wgb_ref, wub_ref, wdb_ref, ys_ref):
    @pl.when(pl.program_id(0) >= n_used[0])
    def _():
        ys_ref[...] = jnp.zeros_like(ys_ref)

    @pl.when(pl.program_id(0) < n_used[0])
    def _():
        h = hs_ref[:, :D_MODEL].astype(BF16)
        pv = hs_ref[:, D_MODEL:]
        p_lo = pv[:, INFO_PLO_H:INFO_PLO_H + 1] + pv[:, INFO_PLO_L:INFO_PLO_L + 1]
        p_hi = pv[:, INFO_PHI_H:INFO_PHI_H + 1] + pv[:, INFO_PHI_L:INFO_PHI_L + 1]

        def hidden(wg_ref, wu_ref, p):
            a = jnp.dot(h, wg_ref[0], preferred_element_type=F32)
            b = jnp.dot(h, wu_ref[0], preferred_element_type=F32)
            return (a * jax.nn.sigmoid(a) * b * p).astype(BF16)

        ys_ref[...] = (jnp.dot(hidden(wga_ref, wua_ref, p_lo), wda_ref[0], preferred_element_type=F32)
                       + jnp.dot(hidden(wgb_ref, wub_ref, p_hi), wdb_ref[0], preferred_element_type=F32))


def _ffn_call(plan, hs, wg, wu, wd, n_ffn_tiles):
    D = D_MODEL
    rows = pl.BlockSpec((FFN_TILE, ROW_WIDTH), lambda t, idx, ea, eb, n: (idx[t], 0))
    w_in_a = pl.BlockSpec((1, D, D_EXPERT), lambda t, idx, ea, eb, n: (ea[t], 0, 0))
    w_out_a = pl.BlockSpec((1, D_EXPERT, D), lambda t, idx, ea, eb, n: (ea[t], 0, 0))
    w_in_b = pl.BlockSpec((1, D, D_EXPERT), lambda t, idx, ea, eb, n: (eb[t], 0, 0))
    w_out_b = pl.BlockSpec((1, D_EXPERT, D), lambda t, idx, ea, eb, n: (eb[t], 0, 0))
    grid_spec = pltpu.PrefetchScalarGridSpec(
        num_scalar_prefetch=4,
        grid=(n_ffn_tiles,),
        in_specs=[rows, w_in_a, w_in_a, w_out_a, w_in_b, w_in_b, w_out_b],
        out_specs=pl.BlockSpec((FFN_TILE, D), lambda t, idx, ea, eb, n: (t, 0)))
    return pl.pallas_call(
        _ffn_kernel,
        grid_spec=grid_spec,
        out_shape=jax.ShapeDtypeStruct((n_ffn_tiles * FFN_TILE, D), F32),
        compiler_params=pltpu.CompilerParams(dimension_semantics=("arbitrary",),
                                             vmem_limit_bytes=VMEM_LIMIT),
        name="ffn",
    )(plan["tile_idx"], plan["tile_ea"], plan["tile_eb"], plan["n_used"],
      hs, wg, wu, wd, wg, wu, wd)


def _combine_kernel(run_start, local_start, units, tile_units,
                    x1_ref, info_ref, gate2_ref, ys_ref, o_ref, ybuf, sems):
    i = pl.program_id(0)
    nt = pl.num_programs(0)
    slot = i % 2

    def piece_copy(s):
        def make(local_row, global_row):
            return pltpu.make_async_copy(ys_ref.at[pl.ds(global_row, SUBLANES), :],
                                         ybuf.at[s, pl.ds(local_row, SUBLANES), :], sems.at[s])
        return make

    def for_slot(fn):
        for s in range(2):
            pl.when(slot == s)(lambda s=s: fn(s))

    def fetch(tile, s):
        _run_pieces(tile, run_start, local_start, units, piece_copy(s))

    @pl.when(i == 0)
    def _():
        ybuf[...] = jnp.zeros_like(ybuf)
        fetch(0, 0)

    @pl.when(i + 1 < nt)
    def _():
        for_slot(lambda s: fetch(i + 1, 1 - s))

    dest_col = info_ref[:, INFO_DEST:INFO_DEST + 1]
    cols = lax.broadcasted_iota(jnp.int32, (TOKEN_TILE, SORT_ROWS), 1).astype(F32)
    unperm = jnp.where(cols == dest_col, 1.0, 0.0).astype(BF16)

    def finish(s):
        _wait_pieces(tile_units[i], piece_copy(s))
        y = jnp.dot(unperm, ybuf[s].astype(BF16), preferred_element_type=F32)
        o_ref[...] = x1_ref[...] + gate2_ref[0] * y

    for_slot(finish)


def _combine_call(plan, x1, info, gate2, ys, B, S):
    T, D = x1.shape
    tm = TOKEN_TILE
    tpb = S // tm
    grid_spec = pltpu.PrefetchScalarGridSpec(
        num_scalar_prefetch=4,
        grid=(T // tm,),
        in_specs=[pl.BlockSpec((tm, D), lambda i, *_: (i, 0)),
                  pl.BlockSpec((tm, LANES), lambda i, *_: (i, 0)),
                  pl.BlockSpec((1, 1, D), lambda i, *_: (i // tpb, 0, 0)),
                  pl.BlockSpec(memory_space=pl.ANY)],
        out_specs=pl.BlockSpec((tm, D), lambda i, *_: (i, 0)),
        scratch_shapes=[pltpu.VMEM((2, SORT_ROWS, D), F32),
                        pltpu.SemaphoreType.DMA((2,))])
    return pl.pallas_call(
        _combine_kernel,
        grid_spec=grid_spec,
        out_shape=jax.ShapeDtypeStruct((T, D), F32),
        compiler_params=pltpu.CompilerParams(dimension_semantics=("arbitrary",),
                                             vmem_limit_bytes=VMEM_LIMIT),
        name="combine",
    )(plan["run_start"], plan["local_start"], plan["units"], plan["tile_units"],
      x1, info, gate2, ys)


def kernel(x, c, w_ada, b_ada, norm1_g, w_in, sgu_norm_g, sgu_w, sgu_b, q_norm_g, k_norm_g,
           w_proj_a, w_proj_b, w_out, norm2_g, w_router_group, b_router_group,
           w_router_expert, b_router_expert, w_gate, w_up, w_down):
    B, S, D = x.shape
    T = B * S
    depth = w_ada.shape[0]
    n_token_tiles = T // TOKEN_TILE
    max_rows = T + n_token_tiles * N_CLASSES * (SUBLANES - 1) + N_CLASSES * (FFN_TILE - SUBLANES)
    n_ffn_tiles = -(-max_rows // FFN_TILE)
    x2 = x.reshape(T, D)
    for l in range(depth):
        mod = _ada_call(c, w_ada[l], b_ada[l])
        shift1, scale1, gate1, shift2, scale2, gate2 = [
            m.reshape(B, 1, D) for m in jnp.split(mod, 6, axis=-1)]

        u, v, q, k, va, ga, gb = _inproj_call(
            x2, shift1, scale1, norm1_g[l].reshape(1, D), w_in[l].astype(BF16),
            sgu_norm_g[l].reshape(1, D), q_norm_g[l].reshape(1, ATT_HEAD_DIM),
            k_norm_g[l].reshape(1, ATT_HEAD_DIM), B, S, tm=512)

        o = _attention_call(q, k, va)

        w_router = jnp.concatenate(
            [w_router_group[l],
             jnp.transpose(w_router_expert[l], (1, 0, 2)).reshape(D, N_EXPERTS)], axis=1)
        w_router = jnp.pad(w_router, ((0, 0), (0, LANES - w_router.shape[1])))
        w_router_hi = w_router.astype(BF16)
        w_router_lo = (w_router - w_router_hi.astype(F32)).astype(BF16)
        w_router = jnp.concatenate([w_router_hi, w_router_lo], axis=1)
        b_router = jnp.concatenate([b_router_group[l], b_router_expert[l].reshape(N_EXPERTS)])
        b_router = jnp.pad(b_router, (0, LANES - b_router.shape[0])).reshape(1, LANES)
        sgu_bias = jnp.repeat(sgu_b[l].T, D // SGU_GROUPS, axis=1)

        x1, h2, info, counts = _merge_call(
            x2, u, v, o, ga, gb, gate1, shift2, scale2, norm2_g[l].reshape(1, D), sgu_w[l],
            sgu_bias, w_proj_a[l].astype(BF16), w_proj_b[l].astype(BF16), w_out[l].astype(BF16),
            w_router, b_router, B, S)

        plan = _dispatch_plan(counts[:, 0, :N_CLASSES].astype(jnp.int32), n_ffn_tiles)
        hs = _dispatch_call(plan, h2, info, n_ffn_tiles)
        ys = _ffn_call(plan, hs, w_gate[l].astype(BF16), w_up[l].astype(BF16),
                       w_down[l].astype(BF16), n_ffn_tiles)
        x2 = _combine_call(plan, x1, info, gate2, ys, B, S)
    return x2.reshape(B, S, D)
```

```python
import jax
import jax.numpy as jnp
import numpy as np
from jax import lax
from jax.experimental import pallas as pl
from jax.experimental.pallas import tpu as pltpu

D_MODEL = 1024
CHUNK = 128
SGU_GROUPS = 8
ATT_HEADS = 8
ATT_HEAD_DIM = 128
DILATED_PATTERNS = ((128, 1), (512, 4), (2048, 16))
N_GROUPS = 4
EXPERTS_PER_GROUP = 4
N_EXPERTS = N_GROUPS * EXPERTS_PER_GROUP
D_EXPERT = 512
EPS = 1e-6
NEG_INF = -1e30

N_IN_SPLITS = 7
LANES = 128
SUBLANES = 8
EXPERT_LANE0 = N_GROUPS

PAIRS = ((0, 1), (0, 2), (0, 3), (1, 2), (1, 3), (2, 3))
N_CLASSES = N_GROUPS * len(PAIRS)
TOKEN_TILE = 512
SORT_ROWS = TOKEN_TILE + 3 * 64
PIECES = SORT_ROWS // SUBLANES
SPARE_ROWS = SORT_ROWS - TOKEN_TILE
ROW_WIDTH = D_MODEL + LANES
FFN_TILE = 512
INFO_DEST, INFO_PLO_H, INFO_PHI_H, INFO_PLO_L, INFO_PHI_L = 0, 1, 2, 3, 4

VMEM_LIMIT = 56 * 1024 * 1024
BF16 = jnp.bfloat16
F32 = jnp.float32

assert SORT_ROWS >= TOKEN_TILE + N_CLASSES * (SUBLANES - 1)


def _rms(x, g):
    return x * lax.rsqrt(jnp.mean(x * x, axis=-1, keepdims=True) + EPS) * g


def _ada_kernel(c_ref, w_ref, b_ref, o_ref):
    c = c_ref[...]
    cond = c * jax.nn.sigmoid(c)
    o_ref[...] = jnp.dot(cond, w_ref[...], preferred_element_type=F32,
                         precision=lax.Precision.HIGHEST) + b_ref[...]


def _ada_call(c, w_ada, b_ada):
    B, D = c.shape
    N = w_ada.shape[1]
    tn = 1024
    return pl.pallas_call(
        _ada_kernel,
        grid=(N // tn,),
        in_specs=[pl.BlockSpec((B, D), lambda j: (0, 0)),
                  pl.BlockSpec((D, tn), lambda j: (0, j)),
                  pl.BlockSpec((1, tn), lambda j: (0, j))],
        out_specs=pl.BlockSpec((B, tn), lambda j: (0, j)),
        out_shape=jax.ShapeDtypeStruct((B, N), F32),
        compiler_params=pltpu.CompilerParams(dimension_semantics=("arbitrary",),
                                             vmem_limit_bytes=VMEM_LIMIT),
        name="adaln",
    )(c, w_ada, b_ada.reshape(1, N))


def _inproj_kernel(x_ref, shift_ref, scale_ref, n1g_ref, w_ref, sgug_ref, qg_ref, kg_ref,
                   u_ref, v_ref, q_ref, k_ref, va_ref, ga_ref, gb_ref):
    x = x_ref[...]
    h = _rms(x, n1g_ref[...]) * (1.0 + scale_ref[0]) + shift_ref[0]
    h = h.astype(BF16)

    def proj(i):
        return jnp.dot(h, w_ref[:, i * D_MODEL:(i + 1) * D_MODEL], preferred_element_type=F32)

    u_ref[...] = jax.nn.gelu(proj(0)).astype(BF16)
    v_ref[...] = _rms(jax.nn.gelu(proj(1)), sgug_ref[...]).astype(BF16)

    q = proj(2)
    qscale = ATT_HEAD_DIM ** -0.5
    for hd in range(ATT_HEADS):
        sl = slice(hd * ATT_HEAD_DIM, (hd + 1) * ATT_HEAD_DIM)
        q_ref[0, hd] = (_rms(q[:, sl], qg_ref[...]) * qscale).astype(BF16)
    k = proj(3)
    for hd in range(ATT_HEADS):
        sl = slice(hd * ATT_HEAD_DIM, (hd + 1) * ATT_HEAD_DIM)
        k_ref[0, hd] = _rms(k[:, sl], kg_ref[...]).astype(BF16)
    va = proj(4)
    for hd in range(ATT_HEADS):
        sl = slice(hd * ATT_HEAD_DIM, (hd + 1) * ATT_HEAD_DIM)
        va_ref[0, hd] = va[:, sl].astype(BF16)
    ga_ref[...] = jax.nn.sigmoid(proj(5)).astype(BF16)
    gb_ref[...] = jax.nn.sigmoid(proj(6)).astype(BF16)


def _inproj_call(x2, shift1, scale1, n1g, w_in, sgug, qg, kg, B, S, tm):
    T, D = x2.shape
    tpb = S // tm
    tok = pl.BlockSpec((tm, D), lambda i: (i, 0))
    per_b = pl.BlockSpec((1, 1, D), lambda i: (i // tpb, 0, 0))
    row = lambda n: pl.BlockSpec((1, n), lambda i: (0, 0))
    head = pl.BlockSpec((1, ATT_HEADS, tm, ATT_HEAD_DIM), lambda i: (i // tpb, 0, i % tpb, 0))
    tok_sds = jax.ShapeDtypeStruct((T, D), BF16)
    head_sds = jax.ShapeDtypeStruct((B, ATT_HEADS, S, ATT_HEAD_DIM), BF16)
    return pl.pallas_call(
        _inproj_kernel,
        grid=(T // tm,),
        in_specs=[tok, per_b, per_b, row(D),
                  pl.BlockSpec((D, N_IN_SPLITS * D), lambda i: (0, 0), pipeline_mode=pl.Buffered(1)),
                  row(D), row(ATT_HEAD_DIM), row(ATT_HEAD_DIM)],
        out_specs=[tok, tok, head, head, head, tok, tok],
        out_shape=[tok_sds, tok_sds, head_sds, head_sds, head_sds, tok_sds, tok_sds],
        compiler_params=pltpu.CompilerParams(dimension_semantics=("parallel",),
                                             vmem_limit_bytes=VMEM_LIMIT),
        name="inproj",
    )(x2, shift1, scale1, n1g, w_in, sgug, qg, kg)


ATT_BLK = 128
STRIDED_BLK = 256
HEADS_PER_STEP = 2
RESIDUES = 4
LOCAL_PATTERNS = tuple(p for p in DILATED_PATTERNS if p[1] % RESIDUES != 0)
STRIDED_PATTERNS = tuple(p for p in DILATED_PATTERNS if p[1] % RESIDUES == 0)
assert all(d == 1 and w <= ATT_BLK for w, d in LOCAL_PATTERNS)


def _log_count_bias(count):
    return jnp.asarray(np.where(count > 0, np.log(np.maximum(count, 1)), NEG_INF), F32)


def _local_bias():
    a = np.arange(ATT_BLK)[:, None]
    col = np.arange(2 * ATT_BLK)[None, :]
    delta = ATT_BLK + a - col
    count = np.zeros(delta.shape, np.int64)
    for window, _ in LOCAL_PATTERNS:
        count += (delta >= 0) & (delta <= window)
    return _log_count_bias(count)


def _strided_bias(n):
    nblk = n // STRIDED_BLK
    a = np.arange(STRIDED_BLK)[:, None]
    col = np.arange(n)[None, :]
    delta = (STRIDED_BLK * (nblk - 1 - col // STRIDED_BLK) + a - col % STRIDED_BLK) * RESIDUES
    count = np.zeros(delta.shape, np.int64)
    for window, dilation in STRIDED_PATTERNS:
        count += (delta >= 0) & (delta <= window) & (delta % dilation == 0)
    return _log_count_bias(count)


def _qk(q, k):
    return lax.dot_general(q, k, (((1,), (1,)), ((), ())), preferred_element_type=F32)


def _attention_kernel(q_ref, k_ref, v_ref, lbias_ref, sbias_ref, o_ref,
                      stage, q4, k4, v4, ve, acc_scr, m_scr, l_scr):
    S = q_ref.shape[2]
    n = S // RESIDUES
    nblk_s = n // STRIDED_BLK
    nblk = S // ATT_BLK
    Dh = ATT_HEAD_DIM
    ones = jnp.ones((S, LANES), BF16)

    for hd in range(HEADS_PER_STEP):
        ve[hd, :, Dh:] = ones
        ve[hd, :, :Dh] = v_ref[0, hd]
        for r in range(RESIDUES):
            v4[hd, r, :, Dh:] = ones[:n]
        for src, dst in ((q_ref, q4), (k_ref, k4), (v_ref, v4)):
            stage[hd] = src[0, hd].astype(F32)
            for r in range(RESIDUES):
                dst[hd, r, :, :Dh] = stage[hd, pl.ds(r, n, stride=RESIDUES), :].astype(BF16)

        for r in range(RESIDUES):
            for jb in range(nblk_s):
                nk = (jb + 1) * STRIDED_BLK
                s = _qk(q4[hd, r, jb * STRIDED_BLK:(jb + 1) * STRIDED_BLK, :], k4[hd, r, :nk, :])
                s = s + sbias_ref[:, (nblk_s - 1 - jb) * STRIDED_BLK:]
                m = jnp.max(s, axis=-1, keepdims=True)
                p = jnp.exp(s - m)
                acc = jnp.dot(p.astype(BF16), v4[hd, r, :nk, :], preferred_element_type=F32)
                rows = pl.ds(RESIDUES * jb * STRIDED_BLK + r, STRIDED_BLK, stride=RESIDUES)
                acc_scr[hd, rows, :] = acc[:, :Dh]
                m_scr[hd, rows, :] = jnp.broadcast_to(m, (STRIDED_BLK, LANES))
                l_scr[hd, rows, :] = acc[:, Dh:]

        for c in range(nblk):
            k0 = max(c - 1, 0) * ATT_BLK
            nk = (c + 1) * ATT_BLK - k0
            blk = slice(c * ATT_BLK, (c + 1) * ATT_BLK)
            s = _qk(q_ref[0, hd, blk, :], k_ref[0, hd, k0:k0 + nk, :]) + lbias_ref[:, 2 * ATT_BLK - nk:]
            m_s = m_scr[hd, blk, :]
            m = jnp.maximum(jnp.max(s, axis=-1, keepdims=True), m_s)
            w = jnp.exp(m_s - m)
            p = jnp.exp(s - jnp.concatenate([m] * (nk // LANES), axis=-1))
            acc = jnp.dot(p.astype(BF16), ve[hd, k0:k0 + nk, :], preferred_element_type=F32)
            l = acc[:, Dh:] + l_scr[hd, blk, :] * w
            o_ref[0, hd, blk, :] = ((acc[:, :Dh] + acc_scr[hd, blk, :] * w) / l).astype(BF16)


def _attention_call(q, k, v):
    B, H, S, Dh = q.shape
    n = S // RESIDUES
    P = HEADS_PER_STEP
    blk = pl.BlockSpec((1, P, S, Dh), lambda b, h: (b, h, 0, 0))
    return pl.pallas_call(
        _attention_kernel,
        grid=(B, H // P),
        in_specs=[blk, blk, blk,
                  pl.BlockSpec((ATT_BLK, 2 * ATT_BLK), lambda b, h: (0, 0)),
                  pl.BlockSpec((STRIDED_BLK, n), lambda b, h: (0, 0))],
        out_specs=blk,
        out_shape=jax.ShapeDtypeStruct((B, H, S, Dh), BF16),
        scratch_shapes=[pltpu.VMEM((P, S, Dh), F32),
                        pltpu.VMEM((P, RESIDUES, n, Dh), BF16),
                        pltpu.VMEM((P, RESIDUES, n, Dh), BF16),
                        pltpu.VMEM((P, RESIDUES, n, Dh + LANES), BF16),
                        pltpu.VMEM((P, S, Dh + LANES), BF16),
                        pltpu.VMEM((P, S, Dh), F32),
                        pltpu.VMEM((P, S, LANES), F32),
                        pltpu.VMEM((P, S, LANES), F32)],
        compiler_params=pltpu.CompilerParams(dimension_semantics=("parallel", "parallel"),
                                             vmem_limit_bytes=VMEM_LIMIT),
        name="attention",
    )(q, k, v, _local_bias(), _strided_bias(n))


def _route(logits):
    tm = logits.shape[0]
    lane = lax.broadcasted_iota(jnp.int32, logits.shape, 1)
    lanef = lane.astype(F32)

    def masked_top(mask):
        top = jnp.max(jnp.where(mask, logits, -jnp.inf), axis=-1, keepdims=True)
        idx = jnp.min(jnp.where(mask & (logits == top), lanef, float(LANES)), axis=-1, keepdims=True)
        return top, idx

    gmask = lane < N_GROUPS
    gmax, gidx = masked_top(gmask)
    p_group = 1.0 / jnp.sum(jnp.where(gmask, jnp.exp(logits - gmax), 0.0), axis=-1, keepdims=True)
    e_lo = EXPERT_LANE0 + gidx * EXPERTS_PER_GROUP
    emask = (lanef >= e_lo) & (lanef < e_lo + EXPERTS_PER_GROUP)
    v1, i1 = masked_top(emask)
    v2, i2 = masked_top(emask & (lanef != i1))
    e21 = jnp.exp(v2 - v1)
    p1 = p_group / (1.0 + e21)
    p2 = p_group * e21 / (1.0 + e21)
    first_is_lo = i1 < i2
    lo = jnp.minimum(i1, i2) - e_lo
    hi = jnp.maximum(i1, i2) - e_lo
    pair = lo * (7.0 - lo) * 0.5 + (hi - lo - 1.0)
    cls = gidx * float(len(PAIRS)) + pair
    p_lo = jnp.where(first_is_lo, p1, p2)
    p_hi = jnp.where(first_is_lo, p2, p1)

    onehot = lanef == cls
    onehot_b = jnp.where(onehot, 1.0, 0.0).astype(BF16)
    r = lax.broadcasted_iota(jnp.int32, (tm, tm), 0)
    c = lax.broadcasted_iota(jnp.int32, (tm, tm), 1)
    before = jnp.where(c < r, 1.0, 0.0).astype(BF16)
    rank = jnp.dot(before, onehot_b, preferred_element_type=F32)
    counts = jnp.sum(jnp.where(onehot, 1.0, 0.0), axis=0, keepdims=True)
    units = jnp.ceil(counts * (1.0 / SUBLANES))
    ur = lax.broadcasted_iota(jnp.int32, (LANES, LANES), 0)
    uc = lax.broadcasted_iota(jnp.int32, (LANES, LANES), 1)
    upper = jnp.where(ur < uc, 1.0, 0.0).astype(BF16)
    start = jnp.dot(jnp.broadcast_to(units, (SUBLANES, LANES)).astype(BF16), upper,
                    preferred_element_type=F32)[0:1] * float(SUBLANES)
    dest = jnp.sum(jnp.where(onehot, start + rank, 0.0), axis=-1, keepdims=True)

    def hi_part(p):
        return p.astype(BF16).astype(F32)

    info = jnp.where(lane == INFO_DEST, dest, 0.0)
    info = jnp.where(lane == INFO_PLO_H, hi_part(p_lo), info)
    info = jnp.where(lane == INFO_PHI_H, hi_part(p_hi), info)
    info = jnp.where(lane == INFO_PLO_L, p_lo - hi_part(p_lo), info)
    info = jnp.where(lane == INFO_PHI_L, p_hi - hi_part(p_hi), info)
    return info, counts


def _merge_kernel(x_ref, u_ref, v_ref, o_ref, ga_ref, gb_ref, gate1_ref, shift2_ref, scale2_ref,
                  n2g_ref, sw_ref, sb_ref, wa_ref, wb_ref, wo_ref, wr_ref, br_ref,
                  x1_ref, h2_ref, info_ref, counts_ref, s_scr):
    tm = x_ref.shape[0]
    row = lax.broadcasted_iota(jnp.int32, (CHUNK, CHUNK), 0)
    colm = lax.broadcasted_iota(jnp.int32, (CHUNK, CHUNK), 1)
    causal = colm <= row
    for g in range(SGU_GROUPS):
        w = jnp.where(causal, sw_ref[g], 0.0).astype(BF16)
        gs = slice(g * CHUNK, (g + 1) * CHUNK)
        chunks = [slice(c * CHUNK, (c + 1) * CHUNK) for c in range(tm // CHUNK)]
        mixed = jnp.dot(w, jnp.concatenate([v_ref[cs, gs] for cs in chunks], axis=1),
                        preferred_element_type=F32)
        for cs in chunks:
            s_scr[cs, gs] = (u_ref[cs, gs].astype(F32) * (mixed[:, cs] + sb_ref[:, gs])).astype(BF16)

    y_a = jnp.dot(s_scr[...], wa_ref[...], preferred_element_type=F32)
    o = jnp.concatenate([o_ref[0, hd] for hd in range(ATT_HEADS)], axis=-1)
    y_b = jnp.dot(o, wb_ref[...], preferred_element_type=F32)
    merged = ga_ref[...].astype(F32) * y_a + gb_ref[...].astype(F32) * y_b
    y = jnp.dot(merged.astype(BF16), wo_ref[...], preferred_element_type=F32)
    x1 = x_ref[...] + gate1_ref[0] * y
    x1_ref[...] = x1

    h2 = _rms(x1, n2g_ref[...]) * (1.0 + scale2_ref[0]) + shift2_ref[0]
    h2_ref[...] = h2.astype(BF16)
    h_hi = h2.astype(BF16)
    h_lo = (h2 - h_hi.astype(F32)).astype(BF16)
    r_hi = jnp.dot(h_hi, wr_ref[...], preferred_element_type=F32)
    r_lo = jnp.dot(h_lo, wr_ref[:, :LANES], preferred_element_type=F32)
    logits = r_hi[:, :LANES] + r_hi[:, LANES:] + r_lo + br_ref[...]
    info, counts = _route(logits)
    info_ref[...] = info
    counts_ref[0] = counts


def _merge_call(x2, u, v, o, ga, gb, gate1, shift2, scale2, n2g, sgu_w, sgu_bias, wa, wb, wo,
                w_router, b_router, B, S):
    T, D = x2.shape
    tm = TOKEN_TILE
    tpb = S // tm
    tok = pl.BlockSpec((tm, D), lambda i: (i, 0))
    per_b = pl.BlockSpec((1, 1, D), lambda i: (i // tpb, 0, 0))
    head = pl.BlockSpec((1, ATT_HEADS, tm, ATT_HEAD_DIM), lambda i: (i // tpb, 0, i % tpb, 0))
    full = lambda *shape: pl.BlockSpec(shape, lambda i: (0,) * len(shape))
    return pl.pallas_call(
        _merge_kernel,
        grid=(T // tm,),
        in_specs=[tok, tok, tok, head, tok, tok, per_b, per_b, per_b, full(1, D),
                  full(SGU_GROUPS, CHUNK, CHUNK), full(CHUNK, D),
                  full(D, D), full(D, D), full(D, D), full(D, 2 * LANES), full(1, LANES)],
        out_specs=[tok, tok, pl.BlockSpec((tm, LANES), lambda i: (i, 0)),
                   pl.BlockSpec((1, 1, LANES), lambda i: (i, 0, 0))],
        out_shape=[jax.ShapeDtypeStruct((T, D), F32), jax.ShapeDtypeStruct((T, D), BF16),
                   jax.ShapeDtypeStruct((T, LANES), F32),
                   jax.ShapeDtypeStruct((T // tm, 1, LANES), F32)],
        scratch_shapes=[pltpu.VMEM((tm, D), BF16)],
        compiler_params=pltpu.CompilerParams(dimension_semantics=("parallel",),
                                             vmem_limit_bytes=VMEM_LIMIT),
        name="merge",
    )(x2, u, v, o, ga, gb, gate1, shift2, scale2, n2g, sgu_w, sgu_bias, wa, wb, wo,
      w_router, b_router)


def _dispatch_plan(counts, n_ffn_tiles):
    units = (counts + SUBLANES - 1) // SUBLANES
    rows = units * SUBLANES
    local_start = jnp.cumsum(rows, axis=1) - rows
    class_rows = jnp.sum(rows, axis=0)
    class_tiles = (class_rows + FFN_TILE - 1) // FFN_TILE
    cum_tiles = jnp.cumsum(class_tiles)
    seg_start = (cum_tiles - class_tiles) * FFN_TILE
    run_start = seg_start[None, :] + jnp.cumsum(rows, axis=0) - rows
    n_used = cum_tiles[-1]
    tile_idx = jnp.maximum(jnp.minimum(jnp.arange(n_ffn_tiles, dtype=jnp.int32), n_used - 1), 0)
    tile_cls = jnp.sum((tile_idx[:, None] >= cum_tiles[None, :]).astype(jnp.int32), axis=1)
    pair = tile_cls % len(PAIRS)
    group = tile_cls // len(PAIRS)
    pairs = jnp.asarray(PAIRS, jnp.int32)
    i32 = lambda a: a.astype(jnp.int32).reshape(-1)

    n_token_tiles = counts.shape[0]
    piece_row = jnp.arange(PIECES, dtype=jnp.int32)[None, :] * SUBLANES
    local_end = jnp.cumsum(rows, axis=1)
    piece_cls = jnp.sum((piece_row[:, :, None] >= local_end[:, None, :]).astype(jnp.int32), axis=2)
    covered = piece_cls < N_CLASSES
    onehot = (jnp.minimum(piece_cls, N_CLASSES - 1)[:, :, None]
              == jnp.arange(N_CLASSES, dtype=jnp.int32)[None, None, :]).astype(jnp.int32)
    global_row = jnp.sum(onehot * (run_start - local_start)[:, None, :], axis=2) + piece_row
    slot = (jnp.arange(n_token_tiles, dtype=jnp.int32) % 2)[:, None]
    spare_row = n_ffn_tiles * FFN_TILE + slot * SPARE_ROWS + piece_row - TOKEN_TILE
    return dict(
        piece_dst=i32(jnp.where(covered, global_row, spare_row)),
        piece_src=i32(jnp.where(covered, global_row, 0)),
        tail_start=i32(seg_start + class_rows),
        tail_units=i32((class_tiles * FFN_TILE - class_rows) // SUBLANES),
        tile_idx=i32(tile_idx), n_used=i32(n_used),
        tile_ea=i32(group * EXPERTS_PER_GROUP + pairs[pair, 0]),
        tile_eb=i32(group * EXPERTS_PER_GROUP + pairs[pair, 1]))


def _start_pieces(tile, piece_rows, make_copy):
    for k in range(PIECES):
        make_copy(k, pl.multiple_of(piece_rows[tile * PIECES + k], SUBLANES)).start()


def _wait_pieces(make_copy):
    for k in range(PIECES):
        make_copy(k, 0).wait()


def _dispatch_kernel(piece_dst, tail_start, tail_units, n_used,
                     h2_ref, info_ref, hs_ref, sorted_scr, zero_scr, sems, tail_sem, unused_sem,
                     spare_sem):
    i = pl.program_id(0)
    nt = pl.num_programs(0)
    slot = i % 2
    n_ffn_tiles = (hs_ref.shape[0] - 2 * SPARE_ROWS) // FFN_TILE

    def piece_copy(s):
        def make(k, global_row):
            return pltpu.make_async_copy(sorted_scr.at[s, pl.ds(k * SUBLANES, SUBLANES), :],
                                         hs_ref.at[pl.ds(global_row, SUBLANES), :], sems.at[s])
        return make

    def tail_copy(global_row):
        return pltpu.make_async_copy(zero_scr.at[pl.ds(0, SUBLANES), :],
                                     hs_ref.at[pl.ds(global_row, SUBLANES), :], tail_sem)

    def unused_copy(t):
        return pltpu.make_async_copy(
            zero_scr, hs_ref.at[pl.ds(pl.multiple_of(t * FFN_TILE, FFN_TILE), FFN_TILE), :],
            unused_sem)

    def spare_copy():
        return pltpu.make_async_copy(
            zero_scr.at[pl.ds(0, 2 * SPARE_ROWS), :],
            hs_ref.at[pl.ds(n_ffn_tiles * FFN_TILE, 2 * SPARE_ROWS), :], spare_sem)

    def for_slot(fn):
        for s in range(2):
            pl.when(slot == s)(lambda s=s: fn(s))

    @pl.when(i >= 2)
    def _():
        for_slot(lambda s: _wait_pieces(piece_copy(s)))

    @pl.when(i == 0)
    def _():
        zero_scr[...] = jnp.zeros_like(zero_scr)
        spare_copy().start()
        spare_copy().wait()

        def per_class(c, carry):
            def per_piece(u, carry2):
                tail_copy(pl.multiple_of(tail_start[c] + u * SUBLANES, SUBLANES)).start()
                return carry2
            return lax.fori_loop(0, tail_units[c], per_piece, carry)

        lax.fori_loop(0, N_CLASSES, per_class, 0)

        def per_unused(t, carry):
            unused_copy(t).start()
            return carry

        lax.fori_loop(n_used[0], n_ffn_tiles, per_unused, 0)

    info = info_ref[...]
    dest_row = info.T[INFO_DEST:INFO_DEST + 1, :]
    rows = lax.broadcasted_iota(jnp.int32, (SORT_ROWS, TOKEN_TILE), 0).astype(F32)
    perm = jnp.where(rows == dest_row, 1.0, 0.0).astype(BF16)
    lane = lax.broadcasted_iota(jnp.int32, info.shape, 1)
    weights = jnp.where(lane == INFO_DEST, 0.0, info).astype(BF16)

    def sort_and_send(s):
        sorted_scr[s, :, :D_MODEL] = jnp.dot(perm, h2_ref[...], preferred_element_type=F32)
        sorted_scr[s, :, D_MODEL:] = jnp.dot(perm, weights, preferred_element_type=F32)
        _start_pieces(i, piece_dst, piece_copy(s))

    for_slot(sort_and_send)

    @pl.when(i == nt - 1)
    def _():
        for_slot(lambda s: _wait_pieces(piece_copy(s)))

        @pl.when(nt >= 2)
        def _():
            for_slot(lambda s: _wait_pieces(piece_copy(1 - s)))

        def per_class(c, carry):
            def per_piece(u, carry2):
                tail_copy(0).wait()
                return carry2
            return lax.fori_loop(0, tail_units[c], per_piece, carry)

        lax.fori_loop(0, N_CLASSES, per_class, 0)

        def per_unused(t, carry):
            unused_copy(0).wait()
            return carry

        lax.fori_loop(n_used[0], n_ffn_tiles, per_unused, 0)


def _dispatch_call(plan, h2, info, n_ffn_tiles):
    T, D = h2.shape
    tm = TOKEN_TILE
    grid_spec = pltpu.PrefetchScalarGridSpec(
        num_scalar_prefetch=4,
        grid=(T // tm,),
        in_specs=[pl.BlockSpec((tm, D), lambda i, *_: (i, 0)),
                  pl.BlockSpec((tm, LANES), lambda i, *_: (i, 0))],
        out_specs=pl.BlockSpec(memory_space=pl.ANY),
        scratch_shapes=[pltpu.VMEM((2, SORT_ROWS, ROW_WIDTH), F32),
                        pltpu.VMEM((FFN_TILE, ROW_WIDTH), F32),
                        pltpu.SemaphoreType.DMA((2,)),
                        pltpu.SemaphoreType.DMA(()),
                        pltpu.SemaphoreType.DMA(()),
                        pltpu.SemaphoreType.DMA(())])
    return pl.pallas_call(
        _dispatch_kernel,
        grid_spec=grid_spec,
        out_shape=jax.ShapeDtypeStruct((n_ffn_tiles * FFN_TILE + 2 * SPARE_ROWS, ROW_WIDTH), F32),
        compiler_params=pltpu.CompilerParams(dimension_semantics=("arbitrary",),
                                             vmem_limit_bytes=VMEM_LIMIT),
        name="dispatch",
    )(plan["piece_dst"], plan["tail_start"], plan["tail_units"], plan["n_used"], h2, info)


def _ffn_kernel(tile_idx, tile_ea, tile_eb, n_used, hs_ref, wga_ref, wua_ref, wda_ref,
                wgb_ref, wub_ref, wdb_ref, ys_ref):
    @pl.when(pl.program_id(0) >= n_used[0])
    def _():
        ys_ref[...] = jnp.zeros_like(ys_ref)

    @pl.when(pl.program_id(0) < n_used[0])
    def _():
        h = hs_ref[:, :D_MODEL].astype(BF16)
        pv = hs_ref[:, D_MODEL:]
        p_lo = pv[:, INFO_PLO_H:INFO_PLO_H + 1] + pv[:, INFO_PLO_L:INFO_PLO_L + 1]
        p_hi = pv[:, INFO_PHI_H:INFO_PHI_H + 1] + pv[:, INFO_PHI_L:INFO_PHI_L + 1]

        def hidden(wg_ref, wu_ref, p):
            a = jnp.dot(h, wg_ref[0], preferred_element_type=F32)
            b = jnp.dot(h, wu_ref[0], preferred_element_type=F32)
            return (a * jax.nn.sigmoid(a) * b * p).astype(BF16)

        ys_ref[...] = (jnp.dot(hidden(wga_ref, wua_ref, p_lo), wda_ref[0], preferred_element_type=F32)
                       + jnp.dot(hidden(wgb_ref, wub_ref, p_hi), wdb_ref[0], preferred_element_type=F32))


def _ffn_call(plan, hs, wg, wu, wd, n_ffn_tiles):
    D = D_MODEL
    rows = pl.BlockSpec((FFN_TILE, ROW_WIDTH), lambda t, idx, ea, eb, n: (idx[t], 0))
    w_in_a = pl.BlockSpec((1, D, D_EXPERT), lambda t, idx, ea, eb, n: (ea[t], 0, 0))
    w_out_a = pl.BlockSpec((1, D_EXPERT, D), lambda t, idx, ea, eb, n: (ea[t], 0, 0))
    w_in_b = pl.BlockSpec((1, D, D_EXPERT), lambda t, idx, ea, eb, n: (eb[t], 0, 0))
    w_out_b = pl.BlockSpec((1, D_EXPERT, D), lambda t, idx, ea, eb, n: (eb[t], 0, 0))
    grid_spec = pltpu.PrefetchScalarGridSpec(
        num_scalar_prefetch=4,
        grid=(n_ffn_tiles,),
        in_specs=[rows, w_in_a, w_in_a, w_out_a, w_in_b, w_in_b, w_out_b],
        out_specs=pl.BlockSpec((FFN_TILE, D), lambda t, idx, ea, eb, n: (t, 0)))
    return pl.pallas_call(
        _ffn_kernel,
        grid_spec=grid_spec,
        out_shape=jax.ShapeDtypeStruct((n_ffn_tiles * FFN_TILE, D), F32),
        compiler_params=pltpu.CompilerParams(dimension_semantics=("arbitrary",),
                                             vmem_limit_bytes=VMEM_LIMIT),
        name="ffn",
    )(plan["tile_idx"], plan["tile_ea"], plan["tile_eb"], plan["n_used"],
      hs, wg, wu, wd, wg, wu, wd)


def _combine_kernel(piece_src, x1_ref, info_ref, gate2_ref, ys_ref, o_ref, ybuf, sems):
    i = pl.program_id(0)
    nt = pl.num_programs(0)
    slot = i % 2

    def piece_copy(s):
        def make(k, global_row):
            return pltpu.make_async_copy(ys_ref.at[pl.ds(global_row, SUBLANES), :],
                                         ybuf.at[s, pl.ds(k * SUBLANES, SUBLANES), :], sems.at[s])
        return make

    def for_slot(fn):
        for s in range(2):
            pl.when(slot == s)(lambda s=s: fn(s))

    def fetch(tile, s):
        _start_pieces(tile, piece_src, piece_copy(s))

    @pl.when(i == 0)
    def _():
        fetch(0, 0)

    @pl.when(i + 1 < nt)
    def _():
        for_slot(lambda s: fetch(i + 1, 1 - s))

    dest_col = info_ref[:, INFO_DEST:INFO_DEST + 1]
    cols = lax.broadcasted_iota(jnp.int32, (TOKEN_TILE, SORT_ROWS), 1).astype(F32)
    unperm = jnp.where(cols == dest_col, 1.0, 0.0).astype(BF16)

    def finish(s):
        _wait_pieces(piece_copy(s))
        y = jnp.dot(unperm, ybuf[s].astype(BF16), preferred_element_type=F32)
        o_ref[...] = x1_ref[...] + gate2_ref[0] * y

    for_slot(finish)


def _combine_call(plan, x1, info, gate2, ys, B, S):
    T, D = x1.shape
    tm = TOKEN_TILE
    tpb = S // tm
    grid_spec = pltpu.PrefetchScalarGridSpec(
        num_scalar_prefetch=1,
        grid=(T // tm,),
        in_specs=[pl.BlockSpec((tm, D), lambda i, *_: (i, 0)),
                  pl.BlockSpec((tm, LANES), lambda i, *_: (i, 0)),
                  pl.BlockSpec((1, 1, D), lambda i, *_: (i // tpb, 0, 0)),
                  pl.BlockSpec(memory_space=pl.ANY)],
        out_specs=pl.BlockSpec((tm, D), lambda i, *_: (i, 0)),
        scratch_shapes=[pltpu.VMEM((2, SORT_ROWS, D), F32),
                        pltpu.SemaphoreType.DMA((2,))])
    return pl.pallas_call(
        _combine_kernel,
        grid_spec=grid_spec,
        out_shape=jax.ShapeDtypeStruct((T, D), F32),
        compiler_params=pltpu.CompilerParams(dimension_semantics=("arbitrary",),
                                             vmem_limit_bytes=VMEM_LIMIT),
        name="combine",
    )(plan["piece_src"], x1, info, gate2, ys)


def kernel(x, c, w_ada, b_ada, norm1_g, w_in, sgu_norm_g, sgu_w, sgu_b, q_norm_g, k_norm_g,
           w_proj_a, w_proj_b, w_out, norm2_g, w_router_group, b_router_group,
           w_router_expert, b_router_expert, w_gate, w_up, w_down):
    B, S, D = x.shape
    T = B * S
    depth = w_ada.shape[0]
    n_token_tiles = T // TOKEN_TILE
    max_rows = T + n_token_tiles * N_CLASSES * (SUBLANES - 1) + N_CLASSES * (FFN_TILE - SUBLANES)
    n_ffn_tiles = -(-max_rows // FFN_TILE)
    x2 = x.reshape(T, D)
    for l in range(depth):
        mod = _ada_call(c, w_ada[l], b_ada[l])
        shift1, scale1, gate1, shift2, scale2, gate2 = [
            m.reshape(B, 1, D) for m in jnp.split(mod, 6, axis=-1)]

        u, v, q, k, va, ga, gb = _inproj_call(
            x2, shift1, scale1, norm1_g[l].reshape(1, D), w_in[l].astype(BF16),
            sgu_norm_g[l].reshape(1, D), q_norm_g[l].reshape(1, ATT_HEAD_DIM),
            k_norm_g[l].reshape(1, ATT_HEAD_DIM), B, S, tm=512)

        o = _attention_call(q, k, va)

        w_router = jnp.concatenate(
            [w_router_group[l],
             jnp.transpose(w_router_expert[l], (1, 0, 2)).reshape(D, N_EXPERTS)], axis=1)
        w_router = jnp.pad(w_router, ((0, 0), (0, LANES - w_router.shape[1])))
        w_router_hi = w_router.astype(BF16)
        w_router_lo = (w_router - w_router_hi.astype(F32)).astype(BF16)
        w_router = jnp.concatenate([w_router_hi, w_router_lo], axis=1)
        b_router = jnp.concatenate([b_router_group[l], b_router_expert[l].reshape(N_EXPERTS)])
        b_router = jnp.pad(b_router, (0, LANES - b_router.shape[0])).reshape(1, LANES)
        sgu_bias = jnp.repeat(sgu_b[l].T, D // SGU_GROUPS, axis=1)

        x1, h2, info, counts = _merge_call(
            x2, u, v, o, ga, gb, gate1, shift2, scale2, norm2_g[l].reshape(1, D), sgu_w[l],
            sgu_bias, w_proj_a[l].astype(BF16), w_proj_b[l].astype(BF16), w_out[l].astype(BF16),
            w_router, b_router, B, S)

        plan = _dispatch_plan(counts[:, 0, :N_CLASSES].astype(jnp.int32), n_ffn_tiles)
        hs = _dispatch_call(plan, h2, info, n_ffn_tiles)
        ys = _ffn_call(plan, hs, w_gate[l].astype(BF16), w_up[l].astype(BF16),
                       w_down[l].astype(BF16), n_ffn_tiles)
        x2 = _combine_call(plan, x1, info, gate2, ys, B, S)
    return x2.reshape(B, S, D)
```

```python
import jax
import jax.numpy as jnp
import numpy as np
from jax import lax
from jax.experimental import pallas as pl
from jax.experimental.pallas import tpu as pltpu

D_MODEL = 1024
CHUNK = 128
SGU_GROUPS = 8
ATT_HEADS = 8
ATT_HEAD_DIM = 128
DILATED_PATTERNS = ((128, 1), (512, 4), (2048, 16))
N_GROUPS = 4
EXPERTS_PER_GROUP = 4
N_EXPERTS = N_GROUPS * EXPERTS_PER_GROUP
D_EXPERT = 512
EPS = 1e-6
NEG_INF = -1e30

N_IN_SPLITS = 7
LANES = 128
SUBLANES = 8
EXPERT_LANE0 = N_GROUPS

PAIRS = ((0, 1), (0, 2), (0, 3), (1, 2), (1, 3), (2, 3))
N_CLASSES = N_GROUPS * len(PAIRS)
TOKEN_TILE = 512
SORT_ROWS = TOKEN_TILE + 3 * 64
PIECES = SORT_ROWS // SUBLANES
SPARE_ROWS = SORT_ROWS - TOKEN_TILE
ROW_WIDTH = D_MODEL + LANES
FFN_TILE = 512
INFO_DEST, INFO_PLO_H, INFO_PHI_H, INFO_PLO_L, INFO_PHI_L = 0, 1, 2, 3, 4

VMEM_LIMIT = 56 * 1024 * 1024
BF16 = jnp.bfloat16
F32 = jnp.float32

assert SORT_ROWS >= TOKEN_TILE + N_CLASSES * (SUBLANES - 1)


def _rms(x, g):
    return x * lax.rsqrt(jnp.mean(x * x, axis=-1, keepdims=True) + EPS) * g


def _ada_kernel(c_ref, w_ref, b_ref, o_ref):
    c = c_ref[...]
    cond = c * jax.nn.sigmoid(c)
    o_ref[...] = jnp.dot(cond, w_ref[...], preferred_element_type=F32,
                         precision=lax.Precision.HIGHEST) + b_ref[...]


def _ada_call(c, w_ada, b_ada):
    B, D = c.shape
    N = w_ada.shape[1]
    tn = 1024
    return pl.pallas_call(
        _ada_kernel,
        grid=(N // tn,),
        in_specs=[pl.BlockSpec((B, D), lambda j: (0, 0)),
                  pl.BlockSpec((D, tn), lambda j: (0, j)),
                  pl.BlockSpec((1, tn), lambda j: (0, j))],
        out_specs=pl.BlockSpec((B, tn), lambda j: (0, j)),
        out_shape=jax.ShapeDtypeStruct((B, N), F32),
        compiler_params=pltpu.CompilerParams(dimension_semantics=("arbitrary",),
                                             vmem_limit_bytes=VMEM_LIMIT),
        name="adaln",
    )(c, w_ada, b_ada.reshape(1, N))


def _inproj_kernel(x_ref, shift_ref, scale_ref, n1g_ref, w_ref, sgug_ref, qg_ref, kg_ref,
                   u_ref, v_ref, q_ref, k_ref, va_ref, ga_ref, gb_ref):
    x = x_ref[...]
    h = _rms(x, n1g_ref[...]) * (1.0 + scale_ref[0]) + shift_ref[0]
    h = h.astype(BF16)

    def proj(i):
        return jnp.dot(h, w_ref[:, i * D_MODEL:(i + 1) * D_MODEL], preferred_element_type=F32)

    u_ref[...] = jax.nn.gelu(proj(0)).astype(BF16)
    v_ref[...] = _rms(jax.nn.gelu(proj(1)), sgug_ref[...]).astype(BF16)

    q = proj(2)
    qscale = ATT_HEAD_DIM ** -0.5
    for hd in range(ATT_HEADS):
        sl = slice(hd * ATT_HEAD_DIM, (hd + 1) * ATT_HEAD_DIM)
        q_ref[0, hd] = (_rms(q[:, sl], qg_ref[...]) * qscale).astype(BF16)
    k = proj(3)
    for hd in range(ATT_HEADS):
        sl = slice(hd * ATT_HEAD_DIM, (hd + 1) * ATT_HEAD_DIM)
        k_ref[0, hd] = _rms(k[:, sl], kg_ref[...]).astype(BF16)
    va = proj(4)
    for hd in range(ATT_HEADS):
        sl = slice(hd * ATT_HEAD_DIM, (hd + 1) * ATT_HEAD_DIM)
        va_ref[0, hd] = va[:, sl].astype(BF16)
    ga_ref[...] = jax.nn.sigmoid(proj(5)).astype(BF16)
    gb_ref[...] = jax.nn.sigmoid(proj(6)).astype(BF16)


def _inproj_call(x2, shift1, scale1, n1g, w_in, sgug, qg, kg, B, S, tm):
    T, D = x2.shape
    tpb = S // tm
    tok = pl.BlockSpec((tm, D), lambda i: (i, 0))
    per_b = pl.BlockSpec((1, 1, D), lambda i: (i // tpb, 0, 0))
    row = lambda n: pl.BlockSpec((1, n), lambda i: (0, 0))
    head = pl.BlockSpec((1, ATT_HEADS, tm, ATT_HEAD_DIM), lambda i: (i // tpb, 0, i % tpb, 0))
    tok_sds = jax.ShapeDtypeStruct((T, D), BF16)
    head_sds = jax.ShapeDtypeStruct((B, ATT_HEADS, S, ATT_HEAD_DIM), BF16)
    return pl.pallas_call(
        _inproj_kernel,
        grid=(T // tm,),
        in_specs=[tok, per_b, per_b, row(D),
                  pl.BlockSpec((D, N_IN_SPLITS * D), lambda i: (0, 0), pipeline_mode=pl.Buffered(1)),
                  row(D), row(ATT_HEAD_DIM), row(ATT_HEAD_DIM)],
        out_specs=[tok, tok, head, head, head, tok, tok],
        out_shape=[tok_sds, tok_sds, head_sds, head_sds, head_sds, tok_sds, tok_sds],
        compiler_params=pltpu.CompilerParams(dimension_semantics=("parallel",),
                                             vmem_limit_bytes=VMEM_LIMIT),
        name="inproj",
    )(x2, shift1, scale1, n1g, w_in, sgug, qg, kg)


ATT_BLK = 128
STRIDED_BLK = 256
HEADS_PER_STEP = 2
RESIDUES = 4
LOCAL_PATTERNS = tuple(p for p in DILATED_PATTERNS if p[1] % RESIDUES != 0)
STRIDED_PATTERNS = tuple(p for p in DILATED_PATTERNS if p[1] % RESIDUES == 0)
assert all(d == 1 and w <= ATT_BLK for w, d in LOCAL_PATTERNS)


def _log_count_bias(count):
    return jnp.asarray(np.where(count > 0, np.log(np.maximum(count, 1)), NEG_INF), F32)


def _local_bias():
    a = np.arange(ATT_BLK)[:, None]
    col = np.arange(2 * ATT_BLK)[None, :]
    delta = ATT_BLK + a - col
    count = np.zeros(delta.shape, np.int64)
    for window, _ in LOCAL_PATTERNS:
        count += (delta >= 0) & (delta <= window)
    return _log_count_bias(count)


def _strided_bias(n):
    nblk = n // STRIDED_BLK
    a = np.arange(STRIDED_BLK)[:, None]
    col = np.arange(n)[None, :]
    delta = (STRIDED_BLK * (nblk - 1 - col // STRIDED_BLK) + a - col % STRIDED_BLK) * RESIDUES
    count = np.zeros(delta.shape, np.int64)
    for window, dilation in STRIDED_PATTERNS:
        count += (delta >= 0) & (delta <= window) & (delta % dilation == 0)
    return _log_count_bias(count)


def _qk(q, k):
    return lax.dot_general(q, k, (((1,), (1,)), ((), ())), preferred_element_type=F32)


def _attention_kernel(q_ref, k_ref, v_ref, lbias_ref, sbias_ref, o_ref,
                      stage, q4, k4, v4, ve, acc_scr, m_scr, l_scr):
    S = q_ref.shape[2]
    n = S // RESIDUES
    nblk_s = n // STRIDED_BLK
    nblk = S // ATT_BLK
    Dh = ATT_HEAD_DIM
    ones = jnp.ones((S, LANES), BF16)

    for hd in range(HEADS_PER_STEP):
        ve[hd, :, Dh:] = ones
        ve[hd, :, :Dh] = v_ref[0, hd]
        for r in range(RESIDUES):
            v4[hd, r, :, Dh:] = ones[:n]
        for src, dst in ((q_ref, q4), (k_ref, k4), (v_ref, v4)):
            stage[hd] = src[0, hd].astype(F32)
            for r in range(RESIDUES):
                dst[hd, r, :, :Dh] = stage[hd, pl.ds(r, n, stride=RESIDUES), :].astype(BF16)

        for r in range(RESIDUES):
            for jb in range(nblk_s):
                nk = (jb + 1) * STRIDED_BLK
                s = _qk(q4[hd, r, jb * STRIDED_BLK:(jb + 1) * STRIDED_BLK, :], k4[hd, r, :nk, :])
                s = s + sbias_ref[:, (nblk_s - 1 - jb) * STRIDED_BLK:]
                m = jnp.max(s, axis=-1, keepdims=True)
                p = jnp.exp(s - m)
                acc = jnp.dot(p.astype(BF16), v4[hd, r, :nk, :], preferred_element_type=F32)
                rows = pl.ds(RESIDUES * jb * STRIDED_BLK + r, STRIDED_BLK, stride=RESIDUES)
                acc_scr[hd, rows, :] = acc[:, :Dh]
                m_scr[hd, rows, :] = jnp.broadcast_to(m, (STRIDED_BLK, LANES))
                l_scr[hd, rows, :] = acc[:, Dh:]

        for c in range(nblk):
            k0 = max(c - 1, 0) * ATT_BLK
            nk = (c + 1) * ATT_BLK - k0
            blk = slice(c * ATT_BLK, (c + 1) * ATT_BLK)
            s = _qk(q_ref[0, hd, blk, :], k_ref[0, hd, k0:k0 + nk, :]) + lbias_ref[:, 2 * ATT_BLK - nk:]
            m_s = m_scr[hd, blk, :]
            m = jnp.maximum(jnp.max(s, axis=-1, keepdims=True), m_s)
            w = jnp.exp(m_s - m)
            p = jnp.exp(s - jnp.concatenate([m] * (nk // LANES), axis=-1))
            acc = jnp.dot(p.astype(BF16), ve[hd, k0:k0 + nk, :], preferred_element_type=F32)
            l = acc[:, Dh:] + l_scr[hd, blk, :] * w
            o_ref[0, hd, blk, :] = ((acc[:, :Dh] + acc_scr[hd, blk, :] * w) / l).astype(BF16)


def _attention_call(q, k, v):
    B, H, S, Dh = q.shape
    n = S // RESIDUES
    P = HEADS_PER_STEP
    blk = pl.BlockSpec((1, P, S, Dh), lambda b, h: (b, h, 0, 0))
    return pl.pallas_call(
        _attention_kernel,
        grid=(B, H // P),
        in_specs=[blk, blk, blk,
                  pl.BlockSpec((ATT_BLK, 2 * ATT_BLK), lambda b, h: (0, 0)),
                  pl.BlockSpec((STRIDED_BLK, n), lambda b, h: (0, 0))],
        out_specs=blk,
        out_shape=jax.ShapeDtypeStruct((B, H, S, Dh), BF16),
        scratch_shapes=[pltpu.VMEM((P, S, Dh), F32),
                        pltpu.VMEM((P, RESIDUES, n, Dh), BF16),
                        pltpu.VMEM((P, RESIDUES, n, Dh), BF16),
                        pltpu.VMEM((P, RESIDUES, n, Dh + LANES), BF16),
                        pltpu.VMEM((P, S, Dh + LANES), BF16),
                        pltpu.VMEM((P, S, Dh), F32),
                        pltpu.VMEM((P, S, LANES), F32),
                        pltpu.VMEM((P, S, LANES), F32)],
        compiler_params=pltpu.CompilerParams(dimension_semantics=("parallel", "parallel"),
                                             vmem_limit_bytes=VMEM_LIMIT),
        name="attention",
    )(q, k, v, _local_bias(), _strided_bias(n))


def _route(logits):
    tm = logits.shape[0]
    lane = lax.broadcasted_iota(jnp.int32, logits.shape, 1)
    lanef = lane.astype(F32)

    def masked_top(mask):
        top = jnp.max(jnp.where(mask, logits, -jnp.inf), axis=-1, keepdims=True)
        idx = jnp.min(jnp.where(mask & (logits == top), lanef, float(LANES)), axis=-1, keepdims=True)
        return top, idx

    gmask = lane < N_GROUPS
    gmax, gidx = masked_top(gmask)
    p_group = 1.0 / jnp.sum(jnp.where(gmask, jnp.exp(logits - gmax), 0.0), axis=-1, keepdims=True)
    e_lo = EXPERT_LANE0 + gidx * EXPERTS_PER_GROUP
    emask = (lanef >= e_lo) & (lanef < e_lo + EXPERTS_PER_GROUP)
    v1, i1 = masked_top(emask)
    v2, i2 = masked_top(emask & (lanef != i1))
    e21 = jnp.exp(v2 - v1)
    p1 = p_group / (1.0 + e21)
    p2 = p_group * e21 / (1.0 + e21)
    first_is_lo = i1 < i2
    lo = jnp.minimum(i1, i2) - e_lo
    hi = jnp.maximum(i1, i2) - e_lo
    pair = lo * (7.0 - lo) * 0.5 + (hi - lo - 1.0)
    cls = gidx * float(len(PAIRS)) + pair
    p_lo = jnp.where(first_is_lo, p1, p2)
    p_hi = jnp.where(first_is_lo, p2, p1)

    onehot = lanef == cls
    onehot_b = jnp.where(onehot, 1.0, 0.0).astype(BF16)
    r = lax.broadcasted_iota(jnp.int32, (tm, tm), 0)
    c = lax.broadcasted_iota(jnp.int32, (tm, tm), 1)
    before = jnp.where(c < r, 1.0, 0.0).astype(BF16)
    rank = jnp.dot(before, onehot_b, preferred_element_type=F32)
    counts = jnp.sum(jnp.where(onehot, 1.0, 0.0), axis=0, keepdims=True)
    units = jnp.ceil(counts * (1.0 / SUBLANES))
    ur = lax.broadcasted_iota(jnp.int32, (LANES, LANES), 0)
    uc = lax.broadcasted_iota(jnp.int32, (LANES, LANES), 1)
    upper = jnp.where(ur < uc, 1.0, 0.0).astype(BF16)
    start = jnp.dot(jnp.broadcast_to(units, (SUBLANES, LANES)).astype(BF16), upper,
                    preferred_element_type=F32)[0:1] * float(SUBLANES)
    dest = jnp.sum(jnp.where(onehot, start + rank, 0.0), axis=-1, keepdims=True)

    def hi_part(p):
        return p.astype(BF16).astype(F32)

    info = jnp.where(lane == INFO_DEST, dest, 0.0)
    info = jnp.where(lane == INFO_PLO_H, hi_part(p_lo), info)
    info = jnp.where(lane == INFO_PHI_H, hi_part(p_hi), info)
    info = jnp.where(lane == INFO_PLO_L, p_lo - hi_part(p_lo), info)
    info = jnp.where(lane == INFO_PHI_L, p_hi - hi_part(p_hi), info)
    return info, counts


def _merge_kernel(x_ref, u_ref, v_ref, o_ref, ga_ref, gb_ref, gate1_ref, shift2_ref, scale2_ref,
                  n2g_ref, sw_ref, sb_ref, wa_ref, wb_ref, wo_ref, wr_ref, br_ref,
                  x1_ref, h2_ref, info_ref, counts_ref, s_scr):
    tm = x_ref.shape[0]
    row = lax.broadcasted_iota(jnp.int32, (CHUNK, CHUNK), 0)
    colm = lax.broadcasted_iota(jnp.int32, (CHUNK, CHUNK), 1)
    causal = colm <= row
    for g in range(SGU_GROUPS):
        w = jnp.where(causal, sw_ref[g], 0.0).astype(BF16)
        gs = slice(g * CHUNK, (g + 1) * CHUNK)
        chunks = [slice(c * CHUNK, (c + 1) * CHUNK) for c in range(tm // CHUNK)]
        mixed = jnp.dot(w, jnp.concatenate([v_ref[cs, gs] for cs in chunks], axis=1),
                        preferred_element_type=F32)
        for cs in chunks:
            s_scr[cs, gs] = (u_ref[cs, gs].astype(F32) * (mixed[:, cs] + sb_ref[:, gs])).astype(BF16)

    y_a = jnp.dot(s_scr[...], wa_ref[...], preferred_element_type=F32)
    o = jnp.concatenate([o_ref[0, hd] for hd in range(ATT_HEADS)], axis=-1)
    y_b = jnp.dot(o, wb_ref[...], preferred_element_type=F32)
    merged = ga_ref[...].astype(F32) * y_a + gb_ref[...].astype(F32) * y_b
    y = jnp.dot(merged.astype(BF16), wo_ref[...], preferred_element_type=F32)
    x1 = x_ref[...] + gate1_ref[0] * y
    x1_ref[...] = x1

    h2 = _rms(x1, n2g_ref[...]) * (1.0 + scale2_ref[0]) + shift2_ref[0]
    h2_ref[...] = h2.astype(BF16)
    h_hi = h2.astype(BF16)
    h_lo = (h2 - h_hi.astype(F32)).astype(BF16)
    r_hi = jnp.dot(h_hi, wr_ref[...], preferred_element_type=F32)
    r_lo = jnp.dot(h_lo, wr_ref[:, :LANES], preferred_element_type=F32)
    logits = r_hi[:, :LANES] + r_hi[:, LANES:] + r_lo + br_ref[...]
    info, counts = _route(logits)
    info_ref[...] = info
    counts_ref[0] = counts


def _merge_call(x2, u, v, o, ga, gb, gate1, shift2, scale2, n2g, sgu_w, sgu_bias, wa, wb, wo,
                w_router, b_router, B, S):
    T, D = x2.shape
    tm = TOKEN_TILE
    tpb = S // tm
    tok = pl.BlockSpec((tm, D), lambda i: (i, 0))
    per_b = pl.BlockSpec((1, 1, D), lambda i: (i // tpb, 0, 0))
    head = pl.BlockSpec((1, ATT_HEADS, tm, ATT_HEAD_DIM), lambda i: (i // tpb, 0, i % tpb, 0))
    full = lambda *shape: pl.BlockSpec(shape, lambda i: (0,) * len(shape))
    return pl.pallas_call(
        _merge_kernel,
        grid=(T // tm,),
        in_specs=[tok, tok, tok, head, tok, tok, per_b, per_b, per_b, full(1, D),
                  full(SGU_GROUPS, CHUNK, CHUNK), full(CHUNK, D),
                  full(D, D), full(D, D), full(D, D), full(D, 2 * LANES), full(1, LANES)],
        out_specs=[tok, tok, pl.BlockSpec((tm, LANES), lambda i: (i, 0)),
                   pl.BlockSpec((1, 1, LANES), lambda i: (i, 0, 0))],
        out_shape=[jax.ShapeDtypeStruct((T, D), F32), jax.ShapeDtypeStruct((T, D), BF16),
                   jax.ShapeDtypeStruct((T, LANES), F32),
                   jax.ShapeDtypeStruct((T // tm, 1, LANES), F32)],
        scratch_shapes=[pltpu.VMEM((tm, D), BF16)],
        compiler_params=pltpu.CompilerParams(dimension_semantics=("parallel",),
                                             vmem_limit_bytes=VMEM_LIMIT),
        name="merge",
    )(x2, u, v, o, ga, gb, gate1, shift2, scale2, n2g, sgu_w, sgu_bias, wa, wb, wo,
      w_router, b_router)


def _dispatch_plan(counts, n_ffn_tiles):
    units = (counts + SUBLANES - 1) // SUBLANES
    rows = units * SUBLANES
    local_start = jnp.cumsum(rows, axis=1) - rows
    class_rows = jnp.sum(rows, axis=0)
    class_tiles = (class_rows + FFN_TILE - 1) // FFN_TILE
    cum_tiles = jnp.cumsum(class_tiles)
    seg_start = (cum_tiles - class_tiles) * FFN_TILE
    run_start = seg_start[None, :] + jnp.cumsum(rows, axis=0) - rows
    n_used = cum_tiles[-1]
    tile_idx = jnp.maximum(jnp.minimum(jnp.arange(n_ffn_tiles, dtype=jnp.int32), n_used - 1), 0)
    tile_cls = jnp.sum((tile_idx[:, None] >= cum_tiles[None, :]).astype(jnp.int32), axis=1)
    pair = tile_cls % len(PAIRS)
    group = tile_cls // len(PAIRS)
    pairs = jnp.asarray(PAIRS, jnp.int32)
    i32 = lambda a: a.astype(jnp.int32).reshape(-1)

    n_token_tiles = counts.shape[0]
    piece_row = jnp.arange(PIECES, dtype=jnp.int32)[None, :] * SUBLANES
    local_end = jnp.cumsum(rows, axis=1)
    piece_cls = jnp.sum((piece_row[:, :, None] >= local_end[:, None, :]).astype(jnp.int32), axis=2)
    covered = piece_cls < N_CLASSES
    onehot = (jnp.minimum(piece_cls, N_CLASSES - 1)[:, :, None]
              == jnp.arange(N_CLASSES, dtype=jnp.int32)[None, None, :]).astype(jnp.int32)
    global_row = jnp.sum(onehot * (run_start - local_start)[:, None, :], axis=2) + piece_row
    slot = (jnp.arange(n_token_tiles, dtype=jnp.int32) % 2)[:, None]
    spare_row = n_ffn_tiles * FFN_TILE + slot * SPARE_ROWS + piece_row - TOKEN_TILE
    return dict(
        piece_dst=i32(jnp.where(covered, global_row, spare_row)),
        run_start=i32(run_start), local_start=i32(local_start), units=i32(units),
        tile_units=i32(jnp.sum(units, axis=1)),
        tail_start=i32(seg_start + class_rows),
        tail_units=i32((class_tiles * FFN_TILE - class_rows) // SUBLANES),
        tile_idx=i32(tile_idx), n_used=i32(n_used),
        tile_ea=i32(group * EXPERTS_PER_GROUP + pairs[pair, 0]),
        tile_eb=i32(group * EXPERTS_PER_GROUP + pairs[pair, 1]))


def _start_run_pieces(tile, run_start, local_start, units, make_copy):
    def per_class(c, carry):
        k = tile * N_CLASSES + c
        g0 = run_start[k]
        l0 = local_start[k]

        def per_piece(u, carry2):
            off = u * SUBLANES
            make_copy(pl.multiple_of(l0 + off, SUBLANES), pl.multiple_of(g0 + off, SUBLANES)).start()
            return carry2

        return lax.fori_loop(0, units[k], per_piece, carry)

    lax.fori_loop(0, N_CLASSES, per_class, 0)


def _wait_run_pieces(n, make_copy):
    def body(_, carry):
        make_copy(0, 0).wait()
        return carry

    lax.fori_loop(0, n, body, 0)


def _start_pieces(tile, piece_rows, make_copy):
    for k in range(PIECES):
        make_copy(k, pl.multiple_of(piece_rows[tile * PIECES + k], SUBLANES)).start()


def _wait_pieces(make_copy):
    for k in range(PIECES):
        make_copy(k, 0).wait()


def _dispatch_kernel(piece_dst, tail_start, tail_units, n_used,
                     h2_ref, info_ref, hs_ref, sorted_scr, zero_scr, sems, tail_sem, unused_sem,
                     spare_sem):
    i = pl.program_id(0)
    nt = pl.num_programs(0)
    slot = i % 2
    n_ffn_tiles = (hs_ref.shape[0] - 2 * SPARE_ROWS) // FFN_TILE

    def piece_copy(s):
        def make(k, global_row):
            return pltpu.make_async_copy(sorted_scr.at[s, pl.ds(k * SUBLANES, SUBLANES), :],
                                         hs_ref.at[pl.ds(global_row, SUBLANES), :], sems.at[s])
        return make

    def tail_copy(global_row):
        return pltpu.make_async_copy(zero_scr.at[pl.ds(0, SUBLANES), :],
                                     hs_ref.at[pl.ds(global_row, SUBLANES), :], tail_sem)

    def unused_copy(t):
        return pltpu.make_async_copy(
            zero_scr, hs_ref.at[pl.ds(pl.multiple_of(t * FFN_TILE, FFN_TILE), FFN_TILE), :],
            unused_sem)

    def spare_copy():
        return pltpu.make_async_copy(
            zero_scr.at[pl.ds(0, 2 * SPARE_ROWS), :],
            hs_ref.at[pl.ds(n_ffn_tiles * FFN_TILE, 2 * SPARE_ROWS), :], spare_sem)

    def for_slot(fn):
        for s in range(2):
            pl.when(slot == s)(lambda s=s: fn(s))

    @pl.when(i >= 2)
    def _():
        for_slot(lambda s: _wait_pieces(piece_copy(s)))

    @pl.when(i == 0)
    def _():
        zero_scr[...] = jnp.zeros_like(zero_scr)
        spare_copy().start()
        spare_copy().wait()

        def per_class(c, carry):
            def per_piece(u, carry2):
                tail_copy(pl.multiple_of(tail_start[c] + u * SUBLANES, SUBLANES)).start()
                return carry2
            return lax.fori_loop(0, tail_units[c], per_piece, carry)

        lax.fori_loop(0, N_CLASSES, per_class, 0)

        def per_unused(t, carry):
            unused_copy(t).start()
            return carry

        lax.fori_loop(n_used[0], n_ffn_tiles, per_unused, 0)

    info = info_ref[...]
    dest_row = info.T[INFO_DEST:INFO_DEST + 1, :]
    rows = lax.broadcasted_iota(jnp.int32, (SORT_ROWS, TOKEN_TILE), 0).astype(F32)
    perm = jnp.where(rows == dest_row, 1.0, 0.0).astype(BF16)
    lane = lax.broadcasted_iota(jnp.int32, info.shape, 1)
    weights = jnp.where(lane == INFO_DEST, 0.0, info).astype(BF16)

    def sort_and_send(s):
        sorted_scr[s, :, :D_MODEL] = jnp.dot(perm, h2_ref[...], preferred_element_type=F32)
        sorted_scr[s, :, D_MODEL:] = jnp.dot(perm, weights, preferred_element_type=F32)
        _start_pieces(i, piece_dst, piece_copy(s))

    for_slot(sort_and_send)

    @pl.when(i == nt - 1)
    def _():
        for_slot(lambda s: _wait_pieces(piece_copy(s)))

        @pl.when(nt >= 2)
        def _():
            for_slot(lambda s: _wait_pieces(piece_copy(1 - s)))

        def per_class(c, carry):
            def per_piece(u, carry2):
                tail_copy(0).wait()
                return carry2
            return lax.fori_loop(0, tail_units[c], per_piece, carry)

        lax.fori_loop(0, N_CLASSES, per_class, 0)

        def per_unused(t, carry):
            unused_copy(0).wait()
            return carry

        lax.fori_loop(n_used[0], n_ffn_tiles, per_unused, 0)


def _dispatch_call(plan, h2, info, n_ffn_tiles):
    T, D = h2.shape
    tm = TOKEN_TILE
    grid_spec = pltpu.PrefetchScalarGridSpec(
        num_scalar_prefetch=4,
        grid=(T // tm,),
        in_specs=[pl.BlockSpec((tm, D), lambda i, *_: (i, 0)),
                  pl.BlockSpec((tm, LANES), lambda i, *_: (i, 0))],
        out_specs=pl.BlockSpec(memory_space=pl.ANY),
        scratch_shapes=[pltpu.VMEM((2, SORT_ROWS, ROW_WIDTH), F32),
                        pltpu.VMEM((FFN_TILE, ROW_WIDTH), F32),
                        pltpu.SemaphoreType.DMA((2,)),
                        pltpu.SemaphoreType.DMA(()),
                        pltpu.SemaphoreType.DMA(()),
                        pltpu.SemaphoreType.DMA(())])
    return pl.pallas_call(
        _dispatch_kernel,
        grid_spec=grid_spec,
        out_shape=jax.ShapeDtypeStruct((n_ffn_tiles * FFN_TILE + 2 * SPARE_ROWS, ROW_WIDTH), F32),
        compiler_params=pltpu.CompilerParams(dimension_semantics=("arbitrary",),
                                             vmem_limit_bytes=VMEM_LIMIT),
        name="dispatch",
    )(plan["piece_dst"], plan["tail_start"], plan["tail_units"], plan["n_used"], h2, info)


def _ffn_kernel(tile_idx, tile_ea, tile_eb, n_used, hs_ref, wga_ref, wua_ref, wda_ref,
                wgb_ref, wub_ref, wdb_ref, ys_ref):
    @pl.when(pl.program_id(0) >= n_used[0])
    def _():
        ys_ref[...] = jnp.zeros_like(ys_ref)

    @pl.when(pl.program_id(0) < n_used[0])
    def _():
        h = hs_ref[:, :D_MODEL].astype(BF16)
        pv = hs_ref[:, D_MODEL:]
        p_lo = pv[:, INFO_PLO_H:INFO_PLO_H + 1] + pv[:, INFO_PLO_L:INFO_PLO_L + 1]
        p_hi = pv[:, INFO_PHI_H:INFO_PHI_H + 1] + pv[:, INFO_PHI_L:INFO_PHI_L + 1]

        def hidden(wg_ref, wu_ref, p):
            a = jnp.dot(h, wg_ref[0], preferred_element_type=F32)
            b = jnp.dot(h, wu_ref[0], preferred_element_type=F32)
            return (a * jax.nn.sigmoid(a) * b * p).astype(BF16)

        ys_ref[...] = (jnp.dot(hidden(wga_ref, wua_ref, p_lo), wda_ref[0], preferred_element_type=F32)
                       + jnp.dot(hidden(wgb_ref, wub_ref, p_hi), wdb_ref[0], preferred_element_type=F32))


def _ffn_call(plan, hs, wg, wu, wd, n_ffn_tiles):
    D = D_MODEL
    rows = pl.BlockSpec((FFN_TILE, ROW_WIDTH), lambda t, idx, ea, eb, n: (idx[t], 0))
    w_in_a = pl.BlockSpec((1, D, D_EXPERT), lambda t, idx, ea, eb, n: (ea[t], 0, 0))
    w_out_a = pl.BlockSpec((1, D_EXPERT, D), lambda t, idx, ea, eb, n: (ea[t], 0, 0))
    w_in_b = pl.BlockSpec((1, D, D_EXPERT), lambda t, idx, ea, eb, n: (eb[t], 0, 0))
    w_out_b = pl.BlockSpec((1, D_EXPERT, D), lambda t, idx, ea, eb, n: (eb[t], 0, 0))
    grid_spec = pltpu.PrefetchScalarGridSpec(
        num_scalar_prefetch=4,
        grid=(n_ffn_tiles,),
        in_specs=[rows, w_in_a, w_in_a, w_out_a, w_in_b, w_in_b, w_out_b],
        out_specs=pl.BlockSpec((FFN_TILE, D), lambda t, idx, ea, eb, n: (t, 0)))
    return pl.pallas_call(
        _ffn_kernel,
        grid_spec=grid_spec,
        out_shape=jax.ShapeDtypeStruct((n_ffn_tiles * FFN_TILE, D), F32),
        compiler_params=pltpu.CompilerParams(dimension_semantics=("arbitrary",),
                                             vmem_limit_bytes=VMEM_LIMIT),
        name="ffn",
    )(plan["tile_idx"], plan["tile_ea"], plan["tile_eb"], plan["n_used"],
      hs, wg, wu, wd, wg, wu, wd)


def _combine_kernel(run_start, local_start, units, tile_units,
                    x1_ref, info_ref, gate2_ref, ys_ref, o_ref, ybuf, sems):
    i = pl.program_id(0)
    nt = pl.num_programs(0)
    slot = i % 2

    def piece_copy(s):
        def make(local_row, global_row):
            return pltpu.make_async_copy(ys_ref.at[pl.ds(global_row, SUBLANES), :],
                                         ybuf.at[s, pl.ds(local_row, SUBLANES), :], sems.at[s])
        return make

    def for_slot(fn):
        for s in range(2):
            pl.when(slot == s)(lambda s=s: fn(s))

    def fetch(tile, s):
        _start_run_pieces(tile, run_start, local_start, units, piece_copy(s))

    @pl.when(i == 0)
    def _():
        ybuf[...] = jnp.zeros_like(ybuf)
        fetch(0, 0)

    @pl.when(i + 1 < nt)
    def _():
        for_slot(lambda s: fetch(i + 1, 1 - s))

    dest_col = info_ref[:, INFO_DEST:INFO_DEST + 1]
    cols = lax.broadcasted_iota(jnp.int32, (TOKEN_TILE, SORT_ROWS), 1).astype(F32)
    unperm = jnp.where(cols == dest_col, 1.0, 0.0).astype(BF16)

    def finish(s):
        _wait_run_pieces(tile_units[i], piece_copy(s))
        y = jnp.dot(unperm, ybuf[s].astype(BF16), preferred_element_type=F32)
        o_ref[...] = x1_ref[...] + gate2_ref[0] * y

    for_slot(finish)


def _combine_call(plan, x1, info, gate2, ys, B, S):
    T, D = x1.shape
    tm = TOKEN_TILE
    tpb = S // tm
    grid_spec = pltpu.PrefetchScalarGridSpec(
        num_scalar_prefetch=4,
        grid=(T // tm,),
        in_specs=[pl.BlockSpec((tm, D), lambda i, *_: (i, 0)),
                  pl.BlockSpec((tm, LANES), lambda i, *_: (i, 0)),
                  pl.BlockSpec((1, 1, D), lambda i, *_: (i // tpb, 0, 0)),
                  pl.BlockSpec(memory_space=pl.ANY)],
        out_specs=pl.BlockSpec((tm, D), lambda i, *_: (i, 0)),
        scratch_shapes=[pltpu.VMEM((2, SORT_ROWS, D), F32),
                        pltpu.SemaphoreType.DMA((2,))])
    return pl.pallas_call(
        _combine_kernel,
        grid_spec=grid_spec,
        out_shape=jax.ShapeDtypeStruct((T, D), F32),
        compiler_params=pltpu.CompilerParams(dimension_semantics=("arbitrary",),
                                             vmem_limit_bytes=VMEM_LIMIT),
        name="combine",
    )(plan["run_start"], plan["local_start"], plan["units"], plan["tile_units"],
      x1, info, gate2, ys)


def kernel(x, c, w_ada, b_ada, norm1_g, w_in, sgu_norm_g, sgu_w, sgu_b, q_norm_g, k_norm_g,
           w_proj_a, w_proj_b, w_out, norm2_g, w_router_group, b_router_group,
           w_router_expert, b_router_expert, w_gate, w_up, w_down):
    B, S, D = x.shape
    T = B * S
    depth = w_ada.shape[0]
    n_token_tiles = T // TOKEN_TILE
    max_rows = T + n_token_tiles * N_CLASSES * (SUBLANES - 1) + N_CLASSES * (FFN_TILE - SUBLANES)
    n_ffn_tiles = -(-max_rows // FFN_TILE)
    x2 = x.reshape(T, D)
    for l in range(depth):
        mod = _ada_call(c, w_ada[l], b_ada[l])
        shift1, scale1, gate1, shift2, scale2, gate2 = [
            m.reshape(B, 1, D) for m in jnp.split(mod, 6, axis=-1)]

        u, v, q, k, va, ga, gb = _inproj_call(
            x2, shift1, scale1, norm1_g[l].reshape(1, D), w_in[l].astype(BF16),
            sgu_norm_g[l].reshape(1, D), q_norm_g[l].reshape(1, ATT_HEAD_DIM),
            k_norm_g[l].reshape(1, ATT_HEAD_DIM), B, S, tm=512)

        o = _attention_call(q, k, va)

        w_router = jnp.concatenate(
            [w_router_group[l],
             jnp.transpose(w_router_expert[l], (1, 0, 2)).reshape(D, N_EXPERTS)], axis=1)
        w_router = jnp.pad(w_router, ((0, 0), (0, LANES - w_router.shape[1])))
        w_router_hi = w_router.astype(BF16)
        w_router_lo = (w_router - w_router_hi.astype(F32)).astype(BF16)
        w_router = jnp.concatenate([w_router_hi, w_router_lo], axis=1)
        b_router = jnp.concatenate([b_router_group[l], b_router_expert[l].reshape(N_EXPERTS)])
        b_router = jnp.pad(b_router, (0, LANES - b_router.shape[0])).reshape(1, LANES)
        sgu_bias = jnp.repeat(sgu_b[l].T, D // SGU_GROUPS, axis=1)

        x1, h2, info, counts = _merge_call(
            x2, u, v, o, ga, gb, gate1, shift2, scale2, norm2_g[l].reshape(1, D), sgu_w[l],
            sgu_bias, w_proj_a[l].astype(BF16), w_proj_b[l].astype(BF16), w_out[l].astype(BF16),
            w_router, b_router, B, S)

        plan = _dispatch_plan(counts[:, 0, :N_CLASSES].astype(jnp.int32), n_ffn_tiles)
        hs = _dispatch_call(plan, h2, info, n_ffn_tiles)
        ys = _ffn_call(plan, hs, w_gate[l], w_up[l], w_down[l], n_ffn_tiles)
        x2 = _combine_call(plan, x1, info, gate2, ys, B, S)
    return x2.reshape(B, S, D)
```

```python
import jax
import jax.numpy as jnp
import numpy as np
from jax import lax
from jax.experimental import pallas as pl
from jax.experimental.pallas import tpu as pltpu

D_MODEL = 1024
CHUNK = 128
SGU_GROUPS = 8
ATT_HEADS = 8
ATT_HEAD_DIM = 128
DILATED_PATTERNS = ((128, 1), (512, 4), (2048, 16))
N_GROUPS = 4
EXPERTS_PER_GROUP = 4
N_EXPERTS = N_GROUPS * EXPERTS_PER_GROUP
D_EXPERT = 512
EPS = 1e-6
NEG_INF = -1e30

N_IN_SPLITS = 7
LANES = 128
SUBLANES = 8
EXPERT_LANE0 = N_GROUPS

PAIRS = ((0, 1), (0, 2), (0, 3), (1, 2), (1, 3), (2, 3))
N_CLASSES = N_GROUPS * len(PAIRS)
TOKEN_TILE = 512
SORT_ROWS = TOKEN_TILE + 3 * 64
ROW_WIDTH = D_MODEL + LANES
FFN_TILE = 512
INFO_DEST, INFO_PLO_H, INFO_PHI_H, INFO_PLO_L, INFO_PHI_L = 0, 1, 2, 3, 4

VMEM_LIMIT = 56 * 1024 * 1024
BF16 = jnp.bfloat16
F32 = jnp.float32

assert SORT_ROWS >= TOKEN_TILE + N_CLASSES * (SUBLANES - 1)


def _rms(x, g):
    return x * lax.rsqrt(jnp.mean(x * x, axis=-1, keepdims=True) + EPS) * g


def _ada_kernel(c_ref, w_ref, b_ref, o_ref):
    c = c_ref[...]
    cond = c * jax.nn.sigmoid(c)
    o_ref[...] = jnp.dot(cond, w_ref[...], preferred_element_type=F32,
                         precision=lax.Precision.HIGHEST) + b_ref[...]


def _ada_call(c, w_ada, b_ada):
    B, D = c.shape
    N = w_ada.shape[1]
    tn = 1024
    return pl.pallas_call(
        _ada_kernel,
        grid=(N // tn,),
        in_specs=[pl.BlockSpec((B, D), lambda j: (0, 0)),
                  pl.BlockSpec((D, tn), lambda j: (0, j)),
                  pl.BlockSpec((1, tn), lambda j: (0, j))],
        out_specs=pl.BlockSpec((B, tn), lambda j: (0, j)),
        out_shape=jax.ShapeDtypeStruct((B, N), F32),
        compiler_params=pltpu.CompilerParams(dimension_semantics=("arbitrary",),
                                             vmem_limit_bytes=VMEM_LIMIT),
        name="adaln",
    )(c, w_ada, b_ada.reshape(1, N))


def _inproj_kernel(x_ref, shift_ref, scale_ref, n1g_ref, w_ref, sgug_ref, qg_ref, kg_ref,
                   u_ref, v_ref, q_ref, k_ref, va_ref, ga_ref, gb_ref):
    x = x_ref[...]
    h = _rms(x, n1g_ref[...]) * (1.0 + scale_ref[0]) + shift_ref[0]
    h = h.astype(BF16)

    def proj(i):
        return jnp.dot(h, w_ref[:, i * D_MODEL:(i + 1) * D_MODEL], preferred_element_type=F32)

    u_ref[...] = jax.nn.gelu(proj(0)).astype(BF16)
    v_ref[...] = _rms(jax.nn.gelu(proj(1)), sgug_ref[...]).astype(BF16)

    q = proj(2)
    qscale = ATT_HEAD_DIM ** -0.5
    for hd in range(ATT_HEADS):
        sl = slice(hd * ATT_HEAD_DIM, (hd + 1) * ATT_HEAD_DIM)
        q_ref[0, hd] = (_rms(q[:, sl], qg_ref[...]) * qscale).astype(BF16)
    k = proj(3)
    for hd in range(ATT_HEADS):
        sl = slice(hd * ATT_HEAD_DIM, (hd + 1) * ATT_HEAD_DIM)
        k_ref[0, hd] = _rms(k[:, sl], kg_ref[...]).astype(BF16)
    va = proj(4)
    for hd in range(ATT_HEADS):
        sl = slice(hd * ATT_HEAD_DIM, (hd + 1) * ATT_HEAD_DIM)
        va_ref[0, hd] = va[:, sl].astype(BF16)
    ga_ref[...] = jax.nn.sigmoid(proj(5)).astype(BF16)
    gb_ref[...] = jax.nn.sigmoid(proj(6)).astype(BF16)


def _inproj_call(x2, shift1, scale1, n1g, w_in, sgug, qg, kg, B, S, tm):
    T, D = x2.shape
    tpb = S // tm
    tok = pl.BlockSpec((tm, D), lambda i: (i, 0))
    per_b = pl.BlockSpec((1, 1, D), lambda i: (i // tpb, 0, 0))
    row = lambda n: pl.BlockSpec((1, n), lambda i: (0, 0))
    head = pl.BlockSpec((1, ATT_HEADS, tm, ATT_HEAD_DIM), lambda i: (i // tpb, 0, i % tpb, 0))
    tok_sds = jax.ShapeDtypeStruct((T, D), BF16)
    head_sds = jax.ShapeDtypeStruct((B, ATT_HEADS, S, ATT_HEAD_DIM), BF16)
    return pl.pallas_call(
        _inproj_kernel,
        grid=(T // tm,),
        in_specs=[tok, per_b, per_b, row(D),
                  pl.BlockSpec((D, N_IN_SPLITS * D), lambda i: (0, 0), pipeline_mode=pl.Buffered(1)),
                  row(D), row(ATT_HEAD_DIM), row(ATT_HEAD_DIM)],
        out_specs=[tok, tok, head, head, head, tok, tok],
        out_shape=[tok_sds, tok_sds, head_sds, head_sds, head_sds, tok_sds, tok_sds],
        compiler_params=pltpu.CompilerParams(dimension_semantics=("parallel",),
                                             vmem_limit_bytes=VMEM_LIMIT),
        name="inproj",
    )(x2, shift1, scale1, n1g, w_in, sgug, qg, kg)


ATT_BLK = 128
STRIDED_BLK = 256
HEADS_PER_STEP = 2
RESIDUES = 4
LOCAL_PATTERNS = tuple(p for p in DILATED_PATTERNS if p[1] % RESIDUES != 0)
STRIDED_PATTERNS = tuple(p for p in DILATED_PATTERNS if p[1] % RESIDUES == 0)
assert all(d == 1 and w <= ATT_BLK for w, d in LOCAL_PATTERNS)


def _log_count_bias(count):
    return jnp.asarray(np.where(count > 0, np.log(np.maximum(count, 1)), NEG_INF), F32)


def _local_bias():
    a = np.arange(ATT_BLK)[:, None]
    col = np.arange(2 * ATT_BLK)[None, :]
    delta = ATT_BLK + a - col
    count = np.zeros(delta.shape, np.int64)
    for window, _ in LOCAL_PATTERNS:
        count += (delta >= 0) & (delta <= window)
    return _log_count_bias(count)


def _strided_bias(n):
    nblk = n // STRIDED_BLK
    a = np.arange(STRIDED_BLK)[:, None]
    col = np.arange(n)[None, :]
    delta = (STRIDED_BLK * (nblk - 1 - col // STRIDED_BLK) + a - col % STRIDED_BLK) * RESIDUES
    count = np.zeros(delta.shape, np.int64)
    for window, dilation in STRIDED_PATTERNS:
        count += (delta >= 0) & (delta <= window) & (delta % dilation == 0)
    return _log_count_bias(count)


def _qk(q, k):
    return lax.dot_general(q, k, (((1,), (1,)), ((), ())), preferred_element_type=F32)


def _attention_kernel(q_ref, k_ref, v_ref, lbias_ref, sbias_ref, o_ref,
                      stage, q4, k4, v4, ve, acc_scr, m_scr, l_scr):
    S = q_ref.shape[2]
    n = S // RESIDUES
    nblk_s = n // STRIDED_BLK
    nblk = S // ATT_BLK
    Dh = ATT_HEAD_DIM
    ones = jnp.ones((S, LANES), BF16)

    for hd in range(HEADS_PER_STEP):
        ve[hd, :, Dh:] = ones
        ve[hd, :, :Dh] = v_ref[0, hd]
        for r in range(RESIDUES):
            v4[hd, r, :, Dh:] = ones[:n]
        for src, dst in ((q_ref, q4), (k_ref, k4), (v_ref, v4)):
            stage[hd] = src[0, hd].astype(F32)
            for r in range(RESIDUES):
                dst[hd, r, :, :Dh] = stage[hd, pl.ds(r, n, stride=RESIDUES), :].astype(BF16)

        for r in range(RESIDUES):
            for jb in range(nblk_s):
                nk = (jb + 1) * STRIDED_BLK
                s = _qk(q4[hd, r, jb * STRIDED_BLK:(jb + 1) * STRIDED_BLK, :], k4[hd, r, :nk, :])
                s = s + sbias_ref[:, (nblk_s - 1 - jb) * STRIDED_BLK:]
                m = jnp.max(s, axis=-1, keepdims=True)
                p = jnp.exp(s - m)
                acc = jnp.dot(p.astype(BF16), v4[hd, r, :nk, :], preferred_element_type=F32)
                rows = pl.ds(RESIDUES * jb * STRIDED_BLK + r, STRIDED_BLK, stride=RESIDUES)
                acc_scr[hd, rows, :] = acc[:, :Dh]
                m_scr[hd, rows, :] = jnp.broadcast_to(m, (STRIDED_BLK, LANES))
                l_scr[hd, rows, :] = acc[:, Dh:]

        for c in range(nblk):
            k0 = max(c - 1, 0) * ATT_BLK
            nk = (c + 1) * ATT_BLK - k0
            blk = slice(c * ATT_BLK, (c + 1) * ATT_BLK)
            s = _qk(q_ref[0, hd, blk, :], k_ref[0, hd, k0:k0 + nk, :]) + lbias_ref[:, 2 * ATT_BLK - nk:]
            m_s = m_scr[hd, blk, :]
            m = jnp.maximum(jnp.max(s, axis=-1, keepdims=True), m_s)
            w = jnp.exp(m_s - m)
            p = jnp.exp(s - jnp.concatenate([m] * (nk // LANES), axis=-1))
            acc = jnp.dot(p.astype(BF16), ve[hd, k0:k0 + nk, :], preferred_element_type=F32)
            l = acc[:, Dh:] + l_scr[hd, blk, :] * w
            o_ref[0, hd, blk, :] = ((acc[:, :Dh] + acc_scr[hd, blk, :] * w) / l).astype(BF16)


def _attention_call(q, k, v):
    B, H, S, Dh = q.shape
    n = S // RESIDUES
    P = HEADS_PER_STEP
    blk = pl.BlockSpec((1, P, S, Dh), lambda b, h: (b, h, 0, 0))
    return pl.pallas_call(
        _attention_kernel,
        grid=(B, H // P),
        in_specs=[blk, blk, blk,
                  pl.BlockSpec((ATT_BLK, 2 * ATT_BLK), lambda b, h: (0, 0)),
                  pl.BlockSpec((STRIDED_BLK, n), lambda b, h: (0, 0))],
        out_specs=blk,
        out_shape=jax.ShapeDtypeStruct((B, H, S, Dh), BF16),
        scratch_shapes=[pltpu.VMEM((P, S, Dh), F32),
                        pltpu.VMEM((P, RESIDUES, n, Dh), BF16),
                        pltpu.VMEM((P, RESIDUES, n, Dh), BF16),
                        pltpu.VMEM((P, RESIDUES, n, Dh + LANES), BF16),
                        pltpu.VMEM((P, S, Dh + LANES), BF16),
                        pltpu.VMEM((P, S, Dh), F32),
                        pltpu.VMEM((P, S, LANES), F32),
                        pltpu.VMEM((P, S, LANES), F32)],
        compiler_params=pltpu.CompilerParams(dimension_semantics=("parallel", "parallel"),
                                             vmem_limit_bytes=VMEM_LIMIT),
        name="attention",
    )(q, k, v, _local_bias(), _strided_bias(n))


def _route(logits):
    tm = logits.shape[0]
    lane = lax.broadcasted_iota(jnp.int32, logits.shape, 1)
    lanef = lane.astype(F32)

    def masked_top(mask):
        top = jnp.max(jnp.where(mask, logits, -jnp.inf), axis=-1, keepdims=True)
        idx = jnp.min(jnp.where(mask & (logits == top), lanef, float(LANES)), axis=-1, keepdims=True)
        return top, idx

    gmask = lane < N_GROUPS
    gmax, gidx = masked_top(gmask)
    p_group = 1.0 / jnp.sum(jnp.where(gmask, jnp.exp(logits - gmax), 0.0), axis=-1, keepdims=True)
    e_lo = EXPERT_LANE0 + gidx * EXPERTS_PER_GROUP
    emask = (lanef >= e_lo) & (lanef < e_lo + EXPERTS_PER_GROUP)
    v1, i1 = masked_top(emask)
    v2, i2 = masked_top(emask & (lanef != i1))
    e21 = jnp.exp(v2 - v1)
    p1 = p_group / (1.0 + e21)
    p2 = p_group * e21 / (1.0 + e21)
    first_is_lo = i1 < i2
    lo = jnp.minimum(i1, i2) - e_lo
    hi = jnp.maximum(i1, i2) - e_lo
    pair = lo * (7.0 - lo) * 0.5 + (hi - lo - 1.0)
    cls = gidx * float(len(PAIRS)) + pair
    p_lo = jnp.where(first_is_lo, p1, p2)
    p_hi = jnp.where(first_is_lo, p2, p1)

    onehot = lanef == cls
    onehot_b = jnp.where(onehot, 1.0, 0.0).astype(BF16)
    r = lax.broadcasted_iota(jnp.int32, (tm, tm), 0)
    c = lax.broadcasted_iota(jnp.int32, (tm, tm), 1)
    before = jnp.where(c < r, 1.0, 0.0).astype(BF16)
    rank = jnp.dot(before, onehot_b, preferred_element_type=F32)
    counts = jnp.sum(jnp.where(onehot, 1.0, 0.0), axis=0, keepdims=True)
    units = jnp.ceil(counts * (1.0 / SUBLANES))
    ur = lax.broadcasted_iota(jnp.int32, (LANES, LANES), 0)
    uc = lax.broadcasted_iota(jnp.int32, (LANES, LANES), 1)
    upper = jnp.where(ur < uc, 1.0, 0.0).astype(BF16)
    start = jnp.dot(jnp.broadcast_to(units, (SUBLANES, LANES)).astype(BF16), upper,
                    preferred_element_type=F32)[0:1] * float(SUBLANES)
    dest = jnp.sum(jnp.where(onehot, start + rank, 0.0), axis=-1, keepdims=True)

    def hi_part(p):
        return p.astype(BF16).astype(F32)

    info = jnp.where(lane == INFO_DEST, dest, 0.0)
    info = jnp.where(lane == INFO_PLO_H, hi_part(p_lo), info)
    info = jnp.where(lane == INFO_PHI_H, hi_part(p_hi), info)
    info = jnp.where(lane == INFO_PLO_L, p_lo - hi_part(p_lo), info)
    info = jnp.where(lane == INFO_PHI_L, p_hi - hi_part(p_hi), info)
    return info, counts


def _merge_kernel(x_ref, u_ref, v_ref, o_ref, ga_ref, gb_ref, gate1_ref, shift2_ref, scale2_ref,
                  n2g_ref, sw_ref, sb_ref, wa_ref, wb_ref, wo_ref, wr_ref, br_ref,
                  x1_ref, h2_ref, info_ref, counts_ref, s_scr):
    tm = x_ref.shape[0]
    row = lax.broadcasted_iota(jnp.int32, (CHUNK, CHUNK), 0)
    colm = lax.broadcasted_iota(jnp.int32, (CHUNK, CHUNK), 1)
    causal = colm <= row
    for g in range(SGU_GROUPS):
        w = jnp.where(causal, sw_ref[g], 0.0).astype(BF16)
        gs = slice(g * CHUNK, (g + 1) * CHUNK)
        chunks = [slice(c * CHUNK, (c + 1) * CHUNK) for c in range(tm // CHUNK)]
        mixed = jnp.dot(w, jnp.concatenate([v_ref[cs, gs] for cs in chunks], axis=1),
                        preferred_element_type=F32)
        for cs in chunks:
            s_scr[cs, gs] = (u_ref[cs, gs].astype(F32) * (mixed[:, cs] + sb_ref[:, gs])).astype(BF16)

    y_a = jnp.dot(s_scr[...], wa_ref[...], preferred_element_type=F32)
    o = jnp.concatenate([o_ref[0, hd] for hd in range(ATT_HEADS)], axis=-1)
    y_b = jnp.dot(o, wb_ref[...], preferred_element_type=F32)
    merged = ga_ref[...].astype(F32) * y_a + gb_ref[...].astype(F32) * y_b
    y = jnp.dot(merged.astype(BF16), wo_ref[...], preferred_element_type=F32)
    x1 = x_ref[...] + gate1_ref[0] * y
    x1_ref[...] = x1

    h2 = _rms(x1, n2g_ref[...]) * (1.0 + scale2_ref[0]) + shift2_ref[0]
    h2_ref[...] = h2.astype(BF16)
    h_hi = h2.astype(BF16)
    h_lo = (h2 - h_hi.astype(F32)).astype(BF16)
    r_hi = jnp.dot(h_hi, wr_ref[...], preferred_element_type=F32)
    r_lo = jnp.dot(h_lo, wr_ref[:, :LANES], preferred_element_type=F32)
    logits = r_hi[:, :LANES] + r_hi[:, LANES:] + r_lo + br_ref[...]
    info, counts = _route(logits)
    info_ref[...] = info
    counts_ref[0] = counts


def _merge_call(x2, u, v, o, ga, gb, gate1, shift2, scale2, n2g, sgu_w, sgu_bias, wa, wb, wo,
                w_router, b_router, B, S):
    T, D = x2.shape
    tm = TOKEN_TILE
    tpb = S // tm
    tok = pl.BlockSpec((tm, D), lambda i: (i, 0))
    per_b = pl.BlockSpec((1, 1, D), lambda i: (i // tpb, 0, 0))
    head = pl.BlockSpec((1, ATT_HEADS, tm, ATT_HEAD_DIM), lambda i: (i // tpb, 0, i % tpb, 0))
    full = lambda *shape: pl.BlockSpec(shape, lambda i: (0,) * len(shape))
    return pl.pallas_call(
        _merge_kernel,
        grid=(T // tm,),
        in_specs=[tok, tok, tok, head, tok, tok, per_b, per_b, per_b, full(1, D),
                  full(SGU_GROUPS, CHUNK, CHUNK), full(CHUNK, D),
                  full(D, D), full(D, D), full(D, D), full(D, 2 * LANES), full(1, LANES)],
        out_specs=[tok, tok, pl.BlockSpec((tm, LANES), lambda i: (i, 0)),
                   pl.BlockSpec((1, 1, LANES), lambda i: (i, 0, 0))],
        out_shape=[jax.ShapeDtypeStruct((T, D), F32), jax.ShapeDtypeStruct((T, D), BF16),
                   jax.ShapeDtypeStruct((T, LANES), F32),
                   jax.ShapeDtypeStruct((T // tm, 1, LANES), F32)],
        scratch_shapes=[pltpu.VMEM((tm, D), BF16)],
        compiler_params=pltpu.CompilerParams(dimension_semantics=("parallel",),
                                             vmem_limit_bytes=VMEM_LIMIT),
        name="merge",
    )(x2, u, v, o, ga, gb, gate1, shift2, scale2, n2g, sgu_w, sgu_bias, wa, wb, wo,
      w_router, b_router)


def _dispatch_plan(counts, n_ffn_tiles):
    units = (counts + SUBLANES - 1) // SUBLANES
    rows = units * SUBLANES
    local_start = jnp.cumsum(rows, axis=1) - rows
    class_rows = jnp.sum(rows, axis=0)
    class_tiles = (class_rows + FFN_TILE - 1) // FFN_TILE
    cum_tiles = jnp.cumsum(class_tiles)
    seg_start = (cum_tiles - class_tiles) * FFN_TILE
    run_start = seg_start[None, :] + jnp.cumsum(rows, axis=0) - rows
    n_used = cum_tiles[-1]
    tile_idx = jnp.maximum(jnp.minimum(jnp.arange(n_ffn_tiles, dtype=jnp.int32), n_used - 1), 0)
    tile_cls = jnp.sum((tile_idx[:, None] >= cum_tiles[None, :]).astype(jnp.int32), axis=1)
    pair = tile_cls % len(PAIRS)
    group = tile_cls // len(PAIRS)
    pairs = jnp.asarray(PAIRS, jnp.int32)
    i32 = lambda a: a.astype(jnp.int32).reshape(-1)

    return dict(
        run_start=i32(run_start), local_start=i32(local_start), units=i32(units),
        tile_units=i32(jnp.sum(units, axis=1)),
        tail_start=i32(seg_start + class_rows),
        tail_units=i32((class_tiles * FFN_TILE - class_rows) // SUBLANES),
        tile_idx=i32(tile_idx), n_used=i32(n_used),
        tile_ea=i32(group * EXPERTS_PER_GROUP + pairs[pair, 0]),
        tile_eb=i32(group * EXPERTS_PER_GROUP + pairs[pair, 1]))


def _rows(units):
    return pl.multiple_of(units * SUBLANES, SUBLANES)


def _start_runs(tile, run_start, local_start, units, make_copy):
    def per_class(c, carry):
        k = tile * N_CLASSES + c

        @pl.when(units[k] > 0)
        def _():
            make_copy(pl.multiple_of(local_start[k], SUBLANES),
                      pl.multiple_of(run_start[k], SUBLANES), _rows(units[k])).start()

        return carry

    lax.fori_loop(0, N_CLASSES, per_class, 0)


def _wait_runs(tile, tile_units, make_copy):
    make_copy(0, 0, _rows(tile_units[tile])).wait()


def _dispatch_kernel(run_start, local_start, units, tile_units, tail_start, tail_units, n_used,
                     h2_ref, info_ref, hs_ref, sorted_scr, zero_scr, sems, tail_sem, unused_sem):
    i = pl.program_id(0)
    nt = pl.num_programs(0)
    slot = i % 2
    n_ffn_tiles = hs_ref.shape[0] // FFN_TILE

    def run_copy(s):
        def make(local_row, global_row, rows):
            return pltpu.make_async_copy(sorted_scr.at[s, pl.ds(local_row, rows), :],
                                         hs_ref.at[pl.ds(global_row, rows), :], sems.at[s])
        return make

    def tail_copy(c):
        rows = _rows(tail_units[c])
        return pltpu.make_async_copy(
            zero_scr.at[pl.ds(0, rows), :],
            hs_ref.at[pl.ds(pl.multiple_of(tail_start[c], SUBLANES), rows), :], tail_sem)

    def unused_copy(t):
        return pltpu.make_async_copy(
            zero_scr, hs_ref.at[pl.ds(pl.multiple_of(t * FFN_TILE, FFN_TILE), FFN_TILE), :],
            unused_sem)

    def for_slot(fn):
        for s in range(2):
            pl.when(slot == s)(lambda s=s: fn(s))

    def for_tails(fn):
        def per_class(c, carry):
            pl.when(tail_units[c] > 0)(lambda: fn(c))
            return carry

        lax.fori_loop(0, N_CLASSES, per_class, 0)

    @pl.when(i >= 2)
    def _():
        for_slot(lambda s: _wait_runs(i - 2, tile_units, run_copy(s)))

    @pl.when(i == 0)
    def _():
        zero_scr[...] = jnp.zeros_like(zero_scr)
        for_tails(lambda c: tail_copy(c).start())

        def per_unused(t, carry):
            unused_copy(t).start()
            return carry

        lax.fori_loop(n_used[0], n_ffn_tiles, per_unused, 0)

    info = info_ref[...]
    dest_row = info.T[INFO_DEST:INFO_DEST + 1, :]
    rows = lax.broadcasted_iota(jnp.int32, (SORT_ROWS, TOKEN_TILE), 0).astype(F32)
    perm = jnp.where(rows == dest_row, 1.0, 0.0).astype(BF16)
    lane = lax.broadcasted_iota(jnp.int32, info.shape, 1)
    weights = jnp.where(lane == INFO_DEST, 0.0, info).astype(BF16)

    def sort_and_send(s):
        sorted_scr[s, :, :D_MODEL] = jnp.dot(perm, h2_ref[...], preferred_element_type=F32)
        sorted_scr[s, :, D_MODEL:] = jnp.dot(perm, weights, preferred_element_type=F32)
        _start_runs(i, run_start, local_start, units, run_copy(s))

    for_slot(sort_and_send)

    @pl.when(i == nt - 1)
    def _():
        for_slot(lambda s: _wait_runs(i, tile_units, run_copy(s)))

        @pl.when(nt >= 2)
        def _():
            for_slot(lambda s: _wait_runs(i - 1, tile_units, run_copy(1 - s)))

        for_tails(lambda c: tail_copy(c).wait())

        def per_unused(t, carry):
            unused_copy(0).wait()
            return carry

        lax.fori_loop(n_used[0], n_ffn_tiles, per_unused, 0)


def _dispatch_call(plan, h2, info, n_ffn_tiles):
    T, D = h2.shape
    tm = TOKEN_TILE
    grid_spec = pltpu.PrefetchScalarGridSpec(
        num_scalar_prefetch=7,
        grid=(T // tm,),
        in_specs=[pl.BlockSpec((tm, D), lambda i, *_: (i, 0)),
                  pl.BlockSpec((tm, LANES), lambda i, *_: (i, 0))],
        out_specs=pl.BlockSpec(memory_space=pl.ANY),
        scratch_shapes=[pltpu.VMEM((2, SORT_ROWS, ROW_WIDTH), F32),
                        pltpu.VMEM((FFN_TILE, ROW_WIDTH), F32),
                        pltpu.SemaphoreType.DMA((2,)),
                        pltpu.SemaphoreType.DMA(()),
                        pltpu.SemaphoreType.DMA(())])
    return pl.pallas_call(
        _dispatch_kernel,
        grid_spec=grid_spec,
        out_shape=jax.ShapeDtypeStruct((n_ffn_tiles * FFN_TILE, ROW_WIDTH), F32),
        compiler_params=pltpu.CompilerParams(dimension_semantics=("arbitrary",),
                                             vmem_limit_bytes=VMEM_LIMIT),
        name="dispatch",
    )(plan["run_start"], plan["local_start"], plan["units"], plan["tile_units"],
      plan["tail_start"], plan["tail_units"], plan["n_used"], h2, info)


def _ffn_kernel(tile_idx, tile_ea, tile_eb, n_used, hs_ref, wga_ref, wua_ref, wda_ref,
                wgb_ref, wub_ref, wdb_ref, ys_ref):
    @pl.when(pl.program_id(0) >= n_used[0])
    def _():
        ys_ref[...] = jnp.zeros_like(ys_ref)

    @pl.when(pl.program_id(0) < n_used[0])
    def _():
        h = hs_ref[:, :D_MODEL].astype(BF16)
        pv = hs_ref[:, D_MODEL:]
        p_lo = pv[:, INFO_PLO_H:INFO_PLO_H + 1] + pv[:, INFO_PLO_L:INFO_PLO_L + 1]
        p_hi = pv[:, INFO_PHI_H:INFO_PHI_H + 1] + pv[:, INFO_PHI_L:INFO_PHI_L + 1]

        def hidden(wg_ref, wu_ref, p):
            a = jnp.dot(h, wg_ref[0], preferred_element_type=F32)
            b = jnp.dot(h, wu_ref[0], preferred_element_type=F32)
            return (a * jax.nn.sigmoid(a) * b * p).astype(BF16)

        ys_ref[...] = (jnp.dot(hidden(wga_ref, wua_ref, p_lo), wda_ref[0], preferred_element_type=F32)
                       + jnp.dot(hidden(wgb_ref, wub_ref, p_hi), wdb_ref[0], preferred_element_type=F32))


def _ffn_call(plan, hs, wg, wu, wd, n_ffn_tiles):
    D = D_MODEL
    rows = pl.BlockSpec((FFN_TILE, ROW_WIDTH), lambda t, idx, ea, eb, n: (idx[t], 0))
    w_in_a = pl.BlockSpec((1, D, D_EXPERT), lambda t, idx, ea, eb, n: (ea[t], 0, 0))
    w_out_a = pl.BlockSpec((1, D_EXPERT, D), lambda t, idx, ea, eb, n: (ea[t], 0, 0))
    w_in_b = pl.BlockSpec((1, D, D_EXPERT), lambda t, idx, ea, eb, n: (eb[t], 0, 0))
    w_out_b = pl.BlockSpec((1, D_EXPERT, D), lambda t, idx, ea, eb, n: (eb[t], 0, 0))
    grid_spec = pltpu.PrefetchScalarGridSpec(
        num_scalar_prefetch=4,
        grid=(n_ffn_tiles,),
        in_specs=[rows, w_in_a, w_in_a, w_out_a, w_in_b, w_in_b, w_out_b],
        out_specs=pl.BlockSpec((FFN_TILE, D), lambda t, idx, ea, eb, n: (t, 0)))
    return pl.pallas_call(
        _ffn_kernel,
        grid_spec=grid_spec,
        out_shape=jax.ShapeDtypeStruct((n_ffn_tiles * FFN_TILE, D), F32),
        compiler_params=pltpu.CompilerParams(dimension_semantics=("arbitrary",),
                                             vmem_limit_bytes=VMEM_LIMIT),
        name="ffn",
    )(plan["tile_idx"], plan["tile_ea"], plan["tile_eb"], plan["n_used"],
      hs, wg, wu, wd, wg, wu, wd)


def _combine_kernel(run_start, local_start, units, tile_units,
                    x1_ref, info_ref, gate2_ref, ys_ref, o_ref, ybuf, sems):
    i = pl.program_id(0)
    nt = pl.num_programs(0)
    slot = i % 2

    def run_copy(s):
        def make(local_row, global_row, rows):
            return pltpu.make_async_copy(ys_ref.at[pl.ds(global_row, rows), :],
                                         ybuf.at[s, pl.ds(local_row, rows), :], sems.at[s])
        return make

    def for_slot(fn):
        for s in range(2):
            pl.when(slot == s)(lambda s=s: fn(s))

    def fetch(tile, s):
        _start_runs(tile, run_start, local_start, units, run_copy(s))

    @pl.when(i == 0)
    def _():
        ybuf[...] = jnp.zeros_like(ybuf)
        fetch(0, 0)

    @pl.when(i + 1 < nt)
    def _():
        for_slot(lambda s: fetch(i + 1, 1 - s))

    dest_col = info_ref[:, INFO_DEST:INFO_DEST + 1]
    cols = lax.broadcasted_iota(jnp.int32, (TOKEN_TILE, SORT_ROWS), 1).astype(F32)
    unperm = jnp.where(cols == dest_col, 1.0, 0.0).astype(BF16)

    def finish(s):
        _wait_runs(i, tile_units, run_copy(s))
        y = jnp.dot(unperm, ybuf[s].astype(BF16), preferred_element_type=F32)
        o_ref[...] = x1_ref[...] + gate2_ref[0] * y

    for_slot(finish)


def _combine_call(plan, x1, info, gate2, ys, B, S):
    T, D = x1.shape
    tm = TOKEN_TILE
    tpb = S // tm
    grid_spec = pltpu.PrefetchScalarGridSpec(
        num_scalar_prefetch=4,
        grid=(T // tm,),
        in_specs=[pl.BlockSpec((tm, D), lambda i, *_: (i, 0)),
                  pl.BlockSpec((tm, LANES), lambda i, *_: (i, 0)),
                  pl.BlockSpec((1, 1, D), lambda i, *_: (i // tpb, 0, 0)),
                  pl.BlockSpec(memory_space=pl.ANY)],
        out_specs=pl.BlockSpec((tm, D), lambda i, *_: (i, 0)),
        scratch_shapes=[pltpu.VMEM((2, SORT_ROWS, D), F32),
                        pltpu.SemaphoreType.DMA((2,))])
    return pl.pallas_call(
        _combine_kernel,
        grid_spec=grid_spec,
        out_shape=jax.ShapeDtypeStruct((T, D), F32),
        compiler_params=pltpu.CompilerParams(dimension_semantics=("arbitrary",),
                                             vmem_limit_bytes=VMEM_LIMIT),
        name="combine",
    )(plan["run_start"], plan["local_start"], plan["units"], plan["tile_units"],
      x1, info, gate2, ys)


def kernel(x, c, w_ada, b_ada, norm1_g, w_in, sgu_norm_g, sgu_w, sgu_b, q_norm_g, k_norm_g,
           w_proj_a, w_proj_b, w_out, norm2_g, w_router_group, b_router_group,
           w_router_expert, b_router_expert, w_gate, w_up, w_down):
    B, S, D = x.shape
    T = B * S
    depth = w_ada.shape[0]
    n_token_tiles = T // TOKEN_TILE
    max_rows = T + n_token_tiles * N_CLASSES * (SUBLANES - 1) + N_CLASSES * (FFN_TILE - SUBLANES)
    n_ffn_tiles = -(-max_rows // FFN_TILE)
    x2 = x.reshape(T, D)
    for l in range(depth):
        mod = _ada_call(c, w_ada[l], b_ada[l])
        shift1, scale1, gate1, shift2, scale2, gate2 = [
            m.reshape(B, 1, D) for m in jnp.split(mod, 6, axis=-1)]

        u, v, q, k, va, ga, gb = _inproj_call(
            x2, shift1, scale1, norm1_g[l].reshape(1, D), w_in[l].astype(BF16),
            sgu_norm_g[l].reshape(1, D), q_norm_g[l].reshape(1, ATT_HEAD_DIM),
            k_norm_g[l].reshape(1, ATT_HEAD_DIM), B, S, tm=512)

        o = _attention_call(q, k, va)

        w_router = jnp.concatenate(
            [w_router_group[l],
             jnp.transpose(w_router_expert[l], (1, 0, 2)).reshape(D, N_EXPERTS)], axis=1)
        w_router = jnp.pad(w_router, ((0, 0), (0, LANES - w_router.shape[1])))
        w_router_hi = w_router.astype(BF16)
        w_router_lo = (w_router - w_router_hi.astype(F32)).astype(BF16)
        w_router = jnp.concatenate([w_router_hi, w_router_lo], axis=1)
        b_router = jnp.concatenate([b_router_group[l], b_router_expert[l].reshape(N_EXPERTS)])
        b_router = jnp.pad(b_router, (0, LANES - b_router.shape[0])).reshape(1, LANES)
        sgu_bias = jnp.repeat(sgu_b[l].T, D // SGU_GROUPS, axis=1)

        x1, h2, info, counts = _merge_call(
            x2, u, v, o, ga, gb, gate1, shift2, scale2, norm2_g[l].reshape(1, D), sgu_w[l],
            sgu_bias, w_proj_a[l].astype(BF16), w_proj_b[l].astype(BF16), w_out[l].astype(BF16),
            w_router, b_router, B, S)

        plan = _dispatch_plan(counts[:, 0, :N_CLASSES].astype(jnp.int32), n_ffn_tiles)
        hs = _dispatch_call(plan, h2, info, n_ffn_tiles)
        ys = _ffn_call(plan, hs, w_gate[l], w_up[l], w_down[l], n_ffn_tiles)
        x2 = _combine_call(plan, x1, info, gate2, ys, B, S)
    return x2.reshape(B, S, D)
```

```python
import jax
import jax.numpy as jnp
import numpy as np
from jax import lax
from jax.experimental import pallas as pl
from jax.experimental.pallas import tpu as pltpu

D_MODEL = 1024
CHUNK = 128
SGU_GROUPS = 8
ATT_HEADS = 8
ATT_HEAD_DIM = 128
DILATED_PATTERNS = ((128, 1), (512, 4), (2048, 16))
N_GROUPS = 4
EXPERTS_PER_GROUP = 4
N_EXPERTS = N_GROUPS * EXPERTS_PER_GROUP
D_EXPERT = 512
EPS = 1e-6
NEG_INF = -1e30

N_IN_SPLITS = 7
LANES = 128
SUBLANES = 8
EXPERT_LANE0 = N_GROUPS

PAIRS = ((0, 1), (0, 2), (0, 3), (1, 2), (1, 3), (2, 3))
N_CLASSES = N_GROUPS * len(PAIRS)
TOKEN_TILE = 512
SORT_ROWS = TOKEN_TILE + 3 * 64
ROW_WIDTH = D_MODEL + LANES
FFN_TILE = 512
INFO_DEST, INFO_PLO_H, INFO_PHI_H, INFO_PLO_L, INFO_PHI_L = 0, 1, 2, 3, 4

VMEM_LIMIT = 56 * 1024 * 1024
BF16 = jnp.bfloat16
F32 = jnp.float32

assert SORT_ROWS >= TOKEN_TILE + N_CLASSES * (SUBLANES - 1)


def _rms(x, g):
    return x * lax.rsqrt(jnp.mean(x * x, axis=-1, keepdims=True) + EPS) * g


def _ada_kernel(c_ref, w_ref, b_ref, o_ref):
    c = c_ref[...]
    cond = c * jax.nn.sigmoid(c)
    o_ref[...] = jnp.dot(cond, w_ref[...], preferred_element_type=F32,
                         precision=lax.Precision.HIGHEST) + b_ref[...]


def _ada_call(c, w_ada, b_ada):
    B, D = c.shape
    N = w_ada.shape[1]
    tn = 1024
    return pl.pallas_call(
        _ada_kernel,
        grid=(N // tn,),
        in_specs=[pl.BlockSpec((B, D), lambda j: (0, 0)),
                  pl.BlockSpec((D, tn), lambda j: (0, j)),
                  pl.BlockSpec((1, tn), lambda j: (0, j))],
        out_specs=pl.BlockSpec((B, tn), lambda j: (0, j)),
        out_shape=jax.ShapeDtypeStruct((B, N), F32),
        compiler_params=pltpu.CompilerParams(dimension_semantics=("arbitrary",),
                                             vmem_limit_bytes=VMEM_LIMIT),
        name="adaln",
    )(c, w_ada, b_ada.reshape(1, N))


def _inproj_kernel(x_ref, shift_ref, scale_ref, n1g_ref, w_ref, sgug_ref, qg_ref, kg_ref,
                   u_ref, v_ref, q_ref, k_ref, va_ref, ga_ref, gb_ref):
    x = x_ref[...]
    h = _rms(x, n1g_ref[...]) * (1.0 + scale_ref[0]) + shift_ref[0]
    h = h.astype(BF16)

    def proj(i):
        return jnp.dot(h, w_ref[:, i * D_MODEL:(i + 1) * D_MODEL], preferred_element_type=F32)

    u_ref[...] = jax.nn.gelu(proj(0)).astype(BF16)
    v_ref[...] = _rms(jax.nn.gelu(proj(1)), sgug_ref[...]).astype(BF16)

    q = proj(2)
    qscale = ATT_HEAD_DIM ** -0.5
    for hd in range(ATT_HEADS):
        sl = slice(hd * ATT_HEAD_DIM, (hd + 1) * ATT_HEAD_DIM)
        q_ref[0, hd] = (_rms(q[:, sl], qg_ref[...]) * qscale).astype(BF16)
    k = proj(3)
    for hd in range(ATT_HEADS):
        sl = slice(hd * ATT_HEAD_DIM, (hd + 1) * ATT_HEAD_DIM)
        k_ref[0, hd] = _rms(k[:, sl], kg_ref[...]).astype(BF16)
    va = proj(4)
    for hd in range(ATT_HEADS):
        sl = slice(hd * ATT_HEAD_DIM, (hd + 1) * ATT_HEAD_DIM)
        va_ref[0, hd] = va[:, sl].astype(BF16)
    ga_ref[...] = jax.nn.sigmoid(proj(5)).astype(BF16)
    gb_ref[...] = jax.nn.sigmoid(proj(6)).astype(BF16)


def _inproj_call(x2, shift1, scale1, n1g, w_in, sgug, qg, kg, B, S, tm):
    T, D = x2.shape
    tpb = S // tm
    tok = pl.BlockSpec((tm, D), lambda i: (i, 0))
    per_b = pl.BlockSpec((1, 1, D), lambda i: (i // tpb, 0, 0))
    row = lambda n: pl.BlockSpec((1, n), lambda i: (0, 0))
    head = pl.BlockSpec((1, ATT_HEADS, tm, ATT_HEAD_DIM), lambda i: (i // tpb, 0, i % tpb, 0))
    tok_sds = jax.ShapeDtypeStruct((T, D), BF16)
    head_sds = jax.ShapeDtypeStruct((B, ATT_HEADS, S, ATT_HEAD_DIM), BF16)
    return pl.pallas_call(
        _inproj_kernel,
        grid=(T // tm,),
        in_specs=[tok, per_b, per_b, row(D),
                  pl.BlockSpec((D, N_IN_SPLITS * D), lambda i: (0, 0), pipeline_mode=pl.Buffered(1)),
                  row(D), row(ATT_HEAD_DIM), row(ATT_HEAD_DIM)],
        out_specs=[tok, tok, head, head, head, tok, tok],
        out_shape=[tok_sds, tok_sds, head_sds, head_sds, head_sds, tok_sds, tok_sds],
        compiler_params=pltpu.CompilerParams(dimension_semantics=("parallel",),
                                             vmem_limit_bytes=VMEM_LIMIT),
        name="inproj",
    )(x2, shift1, scale1, n1g, w_in, sgug, qg, kg)


ATT_BLK = 128
STRIDED_BLK = 256
HEADS_PER_STEP = 2
RESIDUES = 4
LOCAL_PATTERNS = tuple(p for p in DILATED_PATTERNS if p[1] % RESIDUES != 0)
STRIDED_PATTERNS = tuple(p for p in DILATED_PATTERNS if p[1] % RESIDUES == 0)
assert all(d == 1 and w <= ATT_BLK for w, d in LOCAL_PATTERNS)


def _log_count_bias(count):
    return jnp.asarray(np.where(count > 0, np.log(np.maximum(count, 1)), NEG_INF), F32)


def _local_bias():
    a = np.arange(ATT_BLK)[:, None]
    col = np.arange(2 * ATT_BLK)[None, :]
    delta = ATT_BLK + a - col
    count = np.zeros(delta.shape, np.int64)
    for window, _ in LOCAL_PATTERNS:
        count += (delta >= 0) & (delta <= window)
    return _log_count_bias(count)


def _strided_bias(n):
    nblk = n // STRIDED_BLK
    a = np.arange(STRIDED_BLK)[:, None]
    col = np.arange(n)[None, :]
    delta = (STRIDED_BLK * (nblk - 1 - col // STRIDED_BLK) + a - col % STRIDED_BLK) * RESIDUES
    count = np.zeros(delta.shape, np.int64)
    for window, dilation in STRIDED_PATTERNS:
        count += (delta >= 0) & (delta <= window) & (delta % dilation == 0)
    return _log_count_bias(count)


def _qk(q, k):
    return lax.dot_general(q, k, (((1,), (1,)), ((), ())), preferred_element_type=F32)


def _attention_kernel(q_ref, k_ref, v_ref, lbias_ref, sbias_ref, o_ref,
                      stage, q4, k4, v4, ve, acc_scr, m_scr, l_scr):
    S = q_ref.shape[2]
    n = S // RESIDUES
    nblk_s = n // STRIDED_BLK
    nblk = S // ATT_BLK
    Dh = ATT_HEAD_DIM
    ones = jnp.ones((S, LANES), BF16)

    def relayout(hd):
        ve[hd, :, Dh:] = ones
        ve[hd, :, :Dh] = v_ref[0, hd]
        for r in range(RESIDUES):
            v4[hd, r, :, Dh:] = ones[:n]
        for src, dst in ((q_ref, q4), (k_ref, k4), (v_ref, v4)):
            stage[hd] = src[0, hd].astype(F32)
            for r in range(RESIDUES):
                dst[hd, r, :, :Dh] = stage[hd, pl.ds(r, n, stride=RESIDUES), :].astype(BF16)

    def strided_block(hd, r, jb):
        nk = (jb + 1) * STRIDED_BLK

        def scores():
            s = _qk(q4[hd, r, jb * STRIDED_BLK:(jb + 1) * STRIDED_BLK, :], k4[hd, r, :nk, :])
            return s + sbias_ref[:, (nblk_s - 1 - jb) * STRIDED_BLK:]

        def rest(s):
            m = jnp.max(s, axis=-1, keepdims=True)
            p = jnp.exp(s - m)
            acc = jnp.dot(p.astype(BF16), v4[hd, r, :nk, :], preferred_element_type=F32)
            rows = pl.ds(RESIDUES * jb * STRIDED_BLK + r, STRIDED_BLK, stride=RESIDUES)
            acc_scr[hd, rows, :] = acc[:, :Dh]
            m_scr[hd, rows, :] = jnp.broadcast_to(m, (STRIDED_BLK, LANES))
            l_scr[hd, rows, :] = acc[:, Dh:]

        return scores, rest

    def local_block(hd, c):
        k0 = max(c - 1, 0) * ATT_BLK
        nk = (c + 1) * ATT_BLK - k0
        blk = slice(c * ATT_BLK, (c + 1) * ATT_BLK)

        def scores():
            s = _qk(q_ref[0, hd, blk, :], k_ref[0, hd, k0:k0 + nk, :])
            return s + lbias_ref[:, 2 * ATT_BLK - nk:]

        def rest(s):
            m_s = m_scr[hd, blk, :]
            m = jnp.maximum(jnp.max(s, axis=-1, keepdims=True), m_s)
            w = jnp.exp(m_s - m)
            p = jnp.exp(s - jnp.concatenate([m] * (nk // LANES), axis=-1))
            acc = jnp.dot(p.astype(BF16), ve[hd, k0:k0 + nk, :], preferred_element_type=F32)
            l = acc[:, Dh:] + l_scr[hd, blk, :] * w
            o_ref[0, hd, blk, :] = ((acc[:, :Dh] + acc_scr[hd, blk, :] * w) / l).astype(BF16)

        return scores, rest

    for hd in range(HEADS_PER_STEP):
        relayout(hd)
        blocks = [strided_block(hd, r, jb) for r in range(RESIDUES) for jb in range(nblk_s)]
        blocks += [local_block(hd, c) for c in range(nblk)]
        s_next = blocks[0][0]()
        for b, (_, rest) in enumerate(blocks):
            s_cur = s_next
            if b + 1 < len(blocks):
                s_next = blocks[b + 1][0]()
            rest(s_cur)


def _attention_call(q, k, v):
    B, H, S, Dh = q.shape
    n = S // RESIDUES
    P = HEADS_PER_STEP
    blk = pl.BlockSpec((1, P, S, Dh), lambda b, h: (b, h, 0, 0))
    return pl.pallas_call(
        _attention_kernel,
        grid=(B, H // P),
        in_specs=[blk, blk, blk,
                  pl.BlockSpec((ATT_BLK, 2 * ATT_BLK), lambda b, h: (0, 0)),
                  pl.BlockSpec((STRIDED_BLK, n), lambda b, h: (0, 0))],
        out_specs=blk,
        out_shape=jax.ShapeDtypeStruct((B, H, S, Dh), BF16),
        scratch_shapes=[pltpu.VMEM((P, S, Dh), F32),
                        pltpu.VMEM((P, RESIDUES, n, Dh), BF16),
                        pltpu.VMEM((P, RESIDUES, n, Dh), BF16),
                        pltpu.VMEM((P, RESIDUES, n, Dh + LANES), BF16),
                        pltpu.VMEM((P, S, Dh + LANES), BF16),
                        pltpu.VMEM((P, S, Dh), F32),
                        pltpu.VMEM((P, S, LANES), F32),
                        pltpu.VMEM((P, S, LANES), F32)],
        compiler_params=pltpu.CompilerParams(dimension_semantics=("parallel", "parallel"),
                                             vmem_limit_bytes=VMEM_LIMIT),
        name="attention",
    )(q, k, v, _local_bias(), _strided_bias(n))


def _route(logits):
    tm = logits.shape[0]
    lane = lax.broadcasted_iota(jnp.int32, logits.shape, 1)
    lanef = lane.astype(F32)

    def masked_top(mask):
        top = jnp.max(jnp.where(mask, logits, -jnp.inf), axis=-1, keepdims=True)
        idx = jnp.min(jnp.where(mask & (logits == top), lanef, float(LANES)), axis=-1, keepdims=True)
        return top, idx

    gmask = lane < N_GROUPS
    gmax, gidx = masked_top(gmask)
    p_group = 1.0 / jnp.sum(jnp.where(gmask, jnp.exp(logits - gmax), 0.0), axis=-1, keepdims=True)
    e_lo = EXPERT_LANE0 + gidx * EXPERTS_PER_GROUP
    emask = (lanef >= e_lo) & (lanef < e_lo + EXPERTS_PER_GROUP)
    v1, i1 = masked_top(emask)
    v2, i2 = masked_top(emask & (lanef != i1))
    e21 = jnp.exp(v2 - v1)
    p1 = p_group / (1.0 + e21)
    p2 = p_group * e21 / (1.0 + e21)
    first_is_lo = i1 < i2
    lo = jnp.minimum(i1, i2) - e_lo
    hi = jnp.maximum(i1, i2) - e_lo
    pair = lo * (7.0 - lo) * 0.5 + (hi - lo - 1.0)
    cls = gidx * float(len(PAIRS)) + pair
    p_lo = jnp.where(first_is_lo, p1, p2)
    p_hi = jnp.where(first_is_lo, p2, p1)

    onehot = lanef == cls
    onehot_b = jnp.where(onehot, 1.0, 0.0).astype(BF16)
    r = lax.broadcasted_iota(jnp.int32, (tm, tm), 0)
    c = lax.broadcasted_iota(jnp.int32, (tm, tm), 1)
    before = jnp.where(c < r, 1.0, 0.0).astype(BF16)
    rank = jnp.dot(before, onehot_b, preferred_element_type=F32)
    counts = jnp.sum(jnp.where(onehot, 1.0, 0.0), axis=0, keepdims=True)
    units = jnp.ceil(counts * (1.0 / SUBLANES))
    ur = lax.broadcasted_iota(jnp.int32, (LANES, LANES), 0)
    uc = lax.broadcasted_iota(jnp.int32, (LANES, LANES), 1)
    upper = jnp.where(ur < uc, 1.0, 0.0).astype(BF16)
    start = jnp.dot(jnp.broadcast_to(units, (SUBLANES, LANES)).astype(BF16), upper,
                    preferred_element_type=F32)[0:1] * float(SUBLANES)
    dest = jnp.sum(jnp.where(onehot, start + rank, 0.0), axis=-1, keepdims=True)

    def hi_part(p):
        return p.astype(BF16).astype(F32)

    info = jnp.where(lane == INFO_DEST, dest, 0.0)
    info = jnp.where(lane == INFO_PLO_H, hi_part(p_lo), info)
    info = jnp.where(lane == INFO_PHI_H, hi_part(p_hi), info)
    info = jnp.where(lane == INFO_PLO_L, p_lo - hi_part(p_lo), info)
    info = jnp.where(lane == INFO_PHI_L, p_hi - hi_part(p_hi), info)
    return info, counts


def _merge_kernel(x_ref, u_ref, v_ref, o_ref, ga_ref, gb_ref, gate1_ref, shift2_ref, scale2_ref,
                  n2g_ref, sw_ref, sb_ref, wa_ref, wb_ref, wo_ref, wr_ref, br_ref,
                  x1_ref, h2_ref, info_ref, counts_ref, s_scr, logit_scr):
    tm = x_ref.shape[0]
    step = pl.program_id(0)

    @pl.when(step == 0)
    def _():
        logit_scr[...] = jnp.zeros_like(logit_scr)

    prev_logits = logit_scr[1 - step % 2]
    row = lax.broadcasted_iota(jnp.int32, (CHUNK, CHUNK), 0)
    colm = lax.broadcasted_iota(jnp.int32, (CHUNK, CHUNK), 1)
    causal = colm <= row
    for g in range(SGU_GROUPS):
        w = jnp.where(causal, sw_ref[g], 0.0).astype(BF16)
        gs = slice(g * CHUNK, (g + 1) * CHUNK)
        chunks = [slice(c * CHUNK, (c + 1) * CHUNK) for c in range(tm // CHUNK)]
        mixed = jnp.dot(w, jnp.concatenate([v_ref[cs, gs] for cs in chunks], axis=1),
                        preferred_element_type=F32)
        for cs in chunks:
            s_scr[cs, gs] = (u_ref[cs, gs].astype(F32) * (mixed[:, cs] + sb_ref[:, gs])).astype(BF16)

    y_a = jnp.dot(s_scr[...], wa_ref[...], preferred_element_type=F32)
    info, counts = _route(prev_logits)
    info_ref[...] = info
    counts_ref[0] = counts
    o = jnp.concatenate([o_ref[0, hd] for hd in range(ATT_HEADS)], axis=-1)
    y_b = jnp.dot(o, wb_ref[...], preferred_element_type=F32)
    merged = ga_ref[...].astype(F32) * y_a + gb_ref[...].astype(F32) * y_b
    y = jnp.dot(merged.astype(BF16), wo_ref[...], preferred_element_type=F32)
    x1 = x_ref[...] + gate1_ref[0] * y
    x1_ref[...] = x1

    h2 = _rms(x1, n2g_ref[...]) * (1.0 + scale2_ref[0]) + shift2_ref[0]
    h2_ref[...] = h2.astype(BF16)
    h_hi = h2.astype(BF16)
    h_lo = (h2 - h_hi.astype(F32)).astype(BF16)
    r_hi = jnp.dot(h_hi, wr_ref[...], preferred_element_type=F32)
    r_lo = jnp.dot(h_lo, wr_ref[:, :LANES], preferred_element_type=F32)
    logit_scr[step % 2] = r_hi[:, :LANES] + r_hi[:, LANES:] + r_lo + br_ref[...]


def _merge_call(x2, u, v, o, ga, gb, gate1, shift2, scale2, n2g, sgu_w, sgu_bias, wa, wb, wo,
                w_router, b_router, B, S):
    T, D = x2.shape
    tm = TOKEN_TILE
    tpb = S // tm
    n_tiles = T // tm
    tile = lambda i: jnp.minimum(i, n_tiles - 1)
    routed = lambda i: jnp.maximum(i - 1, 0)
    tok = pl.BlockSpec((tm, D), lambda i: (tile(i), 0))
    per_b = pl.BlockSpec((1, 1, D), lambda i: (tile(i) // tpb, 0, 0))
    head = pl.BlockSpec((1, ATT_HEADS, tm, ATT_HEAD_DIM),
                        lambda i: (tile(i) // tpb, 0, tile(i) % tpb, 0))
    full = lambda *shape: pl.BlockSpec(shape, lambda i: (0,) * len(shape))
    return pl.pallas_call(
        _merge_kernel,
        grid=(n_tiles + 1,),
        in_specs=[tok, tok, tok, head, tok, tok, per_b, per_b, per_b, full(1, D),
                  full(SGU_GROUPS, CHUNK, CHUNK), full(CHUNK, D),
                  full(D, D), full(D, D), full(D, D), full(D, 2 * LANES), full(1, LANES)],
        out_specs=[tok, tok, pl.BlockSpec((tm, LANES), lambda i: (routed(i), 0)),
                   pl.BlockSpec((1, 1, LANES), lambda i: (routed(i), 0, 0))],
        out_shape=[jax.ShapeDtypeStruct((T, D), F32), jax.ShapeDtypeStruct((T, D), BF16),
                   jax.ShapeDtypeStruct((T, LANES), F32),
                   jax.ShapeDtypeStruct((n_tiles, 1, LANES), F32)],
        scratch_shapes=[pltpu.VMEM((tm, D), BF16), pltpu.VMEM((2, tm, LANES), F32)],
        compiler_params=pltpu.CompilerParams(dimension_semantics=("arbitrary",),
                                             vmem_limit_bytes=VMEM_LIMIT),
        name="merge",
    )(x2, u, v, o, ga, gb, gate1, shift2, scale2, n2g, sgu_w, sgu_bias, wa, wb, wo,
      w_router, b_router)


def _dispatch_plan(counts, n_ffn_tiles):
    units = (counts + SUBLANES - 1) // SUBLANES
    rows = units * SUBLANES
    local_start = jnp.cumsum(rows, axis=1) - rows
    class_rows = jnp.sum(rows, axis=0)
    class_tiles = (class_rows + FFN_TILE - 1) // FFN_TILE
    cum_tiles = jnp.cumsum(class_tiles)
    seg_start = (cum_tiles - class_tiles) * FFN_TILE
    run_start = seg_start[None, :] + jnp.cumsum(rows, axis=0) - rows
    n_used = cum_tiles[-1]
    tile_idx = jnp.maximum(jnp.minimum(jnp.arange(n_ffn_tiles, dtype=jnp.int32), n_used - 1), 0)
    tile_cls = jnp.sum((tile_idx[:, None] >= cum_tiles[None, :]).astype(jnp.int32), axis=1)
    pair = tile_cls % len(PAIRS)
    group = tile_cls // len(PAIRS)
    pairs = jnp.asarray(PAIRS, jnp.int32)
    i32 = lambda a: a.astype(jnp.int32).reshape(-1)

    return dict(
        run_start=i32(run_start), local_start=i32(local_start), units=i32(units),
        tile_units=i32(jnp.sum(units, axis=1)),
        tail_start=i32(seg_start + class_rows),
        tail_units=i32((class_tiles * FFN_TILE - class_rows) // SUBLANES),
        tile_idx=i32(tile_idx), n_used=i32(n_used),
        tile_ea=i32(group * EXPERTS_PER_GROUP + pairs[pair, 0]),
        tile_eb=i32(group * EXPERTS_PER_GROUP + pairs[pair, 1]))


def _rows(units):
    return pl.multiple_of(units * SUBLANES, SUBLANES)


def _start_runs(tile, run_start, local_start, units, make_copy):
    def per_class(c, carry):
        k = tile * N_CLASSES + c

        @pl.when(units[k] > 0)
        def _():
            make_copy(pl.multiple_of(local_start[k], SUBLANES),
                      pl.multiple_of(run_start[k], SUBLANES), _rows(units[k])).start()

        return carry

    lax.fori_loop(0, N_CLASSES, per_class, 0)


def _wait_runs(tile, tile_units, make_copy):
    make_copy(0, 0, _rows(tile_units[tile])).wait()


def _dispatch_kernel(run_start, local_start, units, tile_units, tail_start, tail_units, n_used,
                     h2_ref, info_ref, hs_ref, sorted_scr, zero_scr, sems, tail_sem, unused_sem):
    i = pl.program_id(0)
    nt = pl.num_programs(0)
    slot = i % 2
    n_ffn_tiles = hs_ref.shape[0] // FFN_TILE

    def run_copy(s):
        def make(local_row, global_row, rows):
            return pltpu.make_async_copy(sorted_scr.at[s, pl.ds(local_row, rows), :],
                                         hs_ref.at[pl.ds(global_row, rows), :], sems.at[s])
        return make

    def tail_copy(c):
        rows = _rows(tail_units[c])
        return pltpu.make_async_copy(
            zero_scr.at[pl.ds(0, rows), :],
            hs_ref.at[pl.ds(pl.multiple_of(tail_start[c], SUBLANES), rows), :], tail_sem)

    def unused_copy(t):
        return pltpu.make_async_copy(
            zero_scr, hs_ref.at[pl.ds(pl.multiple_of(t * FFN_TILE, FFN_TILE), FFN_TILE), :],
            unused_sem)

    def for_slot(fn):
        for s in range(2):
            pl.when(slot == s)(lambda s=s: fn(s))

    def for_tails(fn):
        def per_class(c, carry):
            pl.when(tail_units[c] > 0)(lambda: fn(c))
            return carry

        lax.fori_loop(0, N_CLASSES, per_class, 0)

    @pl.when(i >= 2)
    def _():
        for_slot(lambda s: _wait_runs(i - 2, tile_units, run_copy(s)))

    @pl.when(i == 0)
    def _():
        zero_scr[...] = jnp.zeros_like(zero_scr)
        for_tails(lambda c: tail_copy(c).start())

        def per_unused(t, carry):
            unused_copy(t).start()
            return carry

        lax.fori_loop(n_used[0], n_ffn_tiles, per_unused, 0)

    info = info_ref[...]
    dest_row = info.T[INFO_DEST:INFO_DEST + 1, :]
    rows = lax.broadcasted_iota(jnp.int32, (SORT_ROWS, TOKEN_TILE), 0).astype(F32)
    perm = jnp.where(rows == dest_row, 1.0, 0.0).astype(BF16)
    lane = lax.broadcasted_iota(jnp.int32, info.shape, 1)
    weights = jnp.where(lane == INFO_DEST, 0.0, info).astype(BF16)

    def sort_and_send(s):
        sorted_scr[s, :, :D_MODEL] = jnp.dot(perm, h2_ref[...], preferred_element_type=F32)
        sorted_scr[s, :, D_MODEL:] = jnp.dot(perm, weights, preferred_element_type=F32)
        _start_runs(i, run_start, local_start, units, run_copy(s))

    for_slot(sort_and_send)

    @pl.when(i == nt - 1)
    def _():
        for_slot(lambda s: _wait_runs(i, tile_units, run_copy(s)))

        @pl.when(nt >= 2)
        def _():
            for_slot(lambda s: _wait_runs(i - 1, tile_units, run_copy(1 - s)))

        for_tails(lambda c: tail_copy(c).wait())

        def per_unused(t, carry):
            unused_copy(0).wait()
            return carry

        lax.fori_loop(n_used[0], n_ffn_tiles, per_unused, 0)


def _dispatch_call(plan, h2, info, n_ffn_tiles):
    T, D = h2.shape
    tm = TOKEN_TILE
    grid_spec = pltpu.PrefetchScalarGridSpec(
        num_scalar_prefetch=7,
        grid=(T // tm,),
        in_specs=[pl.BlockSpec((tm, D), lambda i, *_: (i, 0)),
                  pl.BlockSpec((tm, LANES), lambda i, *_: (i, 0))],
        out_specs=pl.BlockSpec(memory_space=pl.ANY),
        scratch_shapes=[pltpu.VMEM((2, SORT_ROWS, ROW_WIDTH), F32),
                        pltpu.VMEM((FFN_TILE, ROW_WIDTH), F32),
                        pltpu.SemaphoreType.DMA((2,)),
                        pltpu.SemaphoreType.DMA(()),
                        pltpu.SemaphoreType.DMA(())])
    return pl.pallas_call(
        _dispatch_kernel,
        grid_spec=grid_spec,
        out_shape=jax.ShapeDtypeStruct((n_ffn_tiles * FFN_TILE, ROW_WIDTH), F32),
        compiler_params=pltpu.CompilerParams(dimension_semantics=("arbitrary",),
                                             vmem_limit_bytes=VMEM_LIMIT),
        name="dispatch",
    )(plan["run_start"], plan["local_start"], plan["units"], plan["tile_units"],
      plan["tail_start"], plan["tail_units"], plan["n_used"], h2, info)


def _ffn_kernel(tile_idx, tile_ea, tile_eb, n_used, hs_ref, wga_ref, wua_ref, wda_ref,
                wgb_ref, wub_ref, wdb_ref, ys_ref):
    @pl.when(pl.program_id(0) >= n_used[0])
    def _():
        ys_ref[...] = jnp.zeros_like(ys_ref)

    @pl.when(pl.program_id(0) < n_used[0])
    def _():
        h = hs_ref[:, :D_MODEL].astype(BF16)
        pv = hs_ref[:, D_MODEL:]
        p_lo = pv[:, INFO_PLO_H:INFO_PLO_H + 1] + pv[:, INFO_PLO_L:INFO_PLO_L + 1]
        p_hi = pv[:, INFO_PHI_H:INFO_PHI_H + 1] + pv[:, INFO_PHI_L:INFO_PHI_L + 1]

        def hidden(wg_ref, wu_ref, p):
            a = jnp.dot(h, wg_ref[0], preferred_element_type=F32)
            b = jnp.dot(h, wu_ref[0], preferred_element_type=F32)
            return (a * jax.nn.sigmoid(a) * b * p).astype(BF16)

        he_a = hidden(wga_ref, wua_ref, p_lo)
        he_b = hidden(wgb_ref, wub_ref, p_hi)
        ys_ref[...] = (jnp.dot(he_a, wda_ref[0], preferred_element_type=F32)
                       + jnp.dot(he_b, wdb_ref[0], preferred_element_type=F32))


def _ffn_call(plan, hs, wg, wu, wd, n_ffn_tiles):
    D = D_MODEL
    rows = pl.BlockSpec((FFN_TILE, ROW_WIDTH), lambda t, idx, ea, eb, n: (idx[t], 0))
    w_in_a = pl.BlockSpec((1, D, D_EXPERT), lambda t, idx, ea, eb, n: (ea[t], 0, 0))
    w_out_a = pl.BlockSpec((1, D_EXPERT, D), lambda t, idx, ea, eb, n: (ea[t], 0, 0))
    w_in_b = pl.BlockSpec((1, D, D_EXPERT), lambda t, idx, ea, eb, n: (eb[t], 0, 0))
    w_out_b = pl.BlockSpec((1, D_EXPERT, D), lambda t, idx, ea, eb, n: (eb[t], 0, 0))
    grid_spec = pltpu.PrefetchScalarGridSpec(
        num_scalar_prefetch=4,
        grid=(n_ffn_tiles,),
        in_specs=[rows, w_in_a, w_in_a, w_out_a, w_in_b, w_in_b, w_out_b],
        out_specs=pl.BlockSpec((FFN_TILE, D), lambda t, idx, ea, eb, n: (t, 0)))
    return pl.pallas_call(
        _ffn_kernel,
        grid_spec=grid_spec,
        out_shape=jax.ShapeDtypeStruct((n_ffn_tiles * FFN_TILE, D), F32),
        compiler_params=pltpu.CompilerParams(dimension_semantics=("arbitrary",),
                                             vmem_limit_bytes=VMEM_LIMIT),
        name="ffn",
    )(plan["tile_idx"], plan["tile_ea"], plan["tile_eb"], plan["n_used"],
      hs, wg, wu, wd, wg, wu, wd)


def _combine_kernel(run_start, local_start, units, tile_units,
                    x1_ref, info_ref, gate2_ref, ys_ref, o_ref, ybuf, sems):
    i = pl.program_id(0)
    nt = pl.num_programs(0)
    slot = i % 2

    def run_copy(s):
        def make(local_row, global_row, rows):
            return pltpu.make_async_copy(ys_ref.at[pl.ds(global_row, rows), :],
                                         ybuf.at[s, pl.ds(local_row, rows), :], sems.at[s])
        return make

    def for_slot(fn):
        for s in range(2):
            pl.when(slot == s)(lambda s=s: fn(s))

    def fetch(tile, s):
        _start_runs(tile, run_start, local_start, units, run_copy(s))

    @pl.when(i == 0)
    def _():
        ybuf[...] = jnp.zeros_like(ybuf)
        fetch(0, 0)

    @pl.when(i + 1 < nt)
    def _():
        for_slot(lambda s: fetch(i + 1, 1 - s))

    dest_col = info_ref[:, INFO_DEST:INFO_DEST + 1]
    cols = lax.broadcasted_iota(jnp.int32, (TOKEN_TILE, SORT_ROWS), 1).astype(F32)
    unperm = jnp.where(cols == dest_col, 1.0, 0.0).astype(BF16)

    def finish(s):
        _wait_runs(i, tile_units, run_copy(s))
        y = jnp.dot(unperm, ybuf[s].astype(BF16), preferred_element_type=F32)
        o_ref[...] = x1_ref[...] + gate2_ref[0] * y

    for_slot(finish)


def _combine_call(plan, x1, info, gate2, ys, B, S):
    T, D = x1.shape
    tm = TOKEN_TILE
    tpb = S // tm
    grid_spec = pltpu.PrefetchScalarGridSpec(
        num_scalar_prefetch=4,
        grid=(T // tm,),
        in_specs=[pl.BlockSpec((tm, D), lambda i, *_: (i, 0)),
                  pl.BlockSpec((tm, LANES), lambda i, *_: (i, 0)),
                  pl.BlockSpec((1, 1, D), lambda i, *_: (i // tpb, 0, 0)),
                  pl.BlockSpec(memory_space=pl.ANY)],
        out_specs=pl.BlockSpec((tm, D), lambda i, *_: (i, 0)),
        scratch_shapes=[pltpu.VMEM((2, SORT_ROWS, D), F32),
                        pltpu.SemaphoreType.DMA((2,))])
    return pl.pallas_call(
        _combine_kernel,
        grid_spec=grid_spec,
        out_shape=jax.ShapeDtypeStruct((T, D), F32),
        compiler_params=pltpu.CompilerParams(dimension_semantics=("arbitrary",),
                                             vmem_limit_bytes=VMEM_LIMIT),
        name="combine",
    )(plan["run_start"], plan["local_start"], plan["units"], plan["tile_units"],
      x1, info, gate2, ys)


def kernel(x, c, w_ada, b_ada, norm1_g, w_in, sgu_norm_g, sgu_w, sgu_b, q_norm_g, k_norm_g,
           w_proj_a, w_proj_b, w_out, norm2_g, w_router_group, b_router_group,
           w_router_expert, b_router_expert, w_gate, w_up, w_down):
    B, S, D = x.shape
    T = B * S
    depth = w_ada.shape[0]
    n_token_tiles = T // TOKEN_TILE
    max_rows = T + n_token_tiles * N_CLASSES * (SUBLANES - 1) + N_CLASSES * (FFN_TILE - SUBLANES)
    n_ffn_tiles = -(-max_rows // FFN_TILE)
    x2 = x.reshape(T, D)
    for l in range(depth):
        mod = _ada_call(c, w_ada[l], b_ada[l])
        shift1, scale1, gate1, shift2, scale2, gate2 = [
            m.reshape(B, 1, D) for m in jnp.split(mod, 6, axis=-1)]

        u, v, q, k, va, ga, gb = _inproj_call(
            x2, shift1, scale1, norm1_g[l].reshape(1, D), w_in[l].astype(BF16),
            sgu_norm_g[l].reshape(1, D), q_norm_g[l].reshape(1, ATT_HEAD_DIM),
            k_norm_g[l].reshape(1, ATT_HEAD_DIM), B, S, tm=512)

        o = _attention_call(q, k, va)

        w_router = jnp.concatenate(
            [w_router_group[l],
             jnp.transpose(w_router_expert[l], (1, 0, 2)).reshape(D, N_EXPERTS)], axis=1)
        w_router = jnp.pad(w_router, ((0, 0), (0, LANES - w_router.shape[1])))
        w_router_hi = w_router.astype(BF16)
        w_router_lo = (w_router - w_router_hi.astype(F32)).astype(BF16)
        w_router = jnp.concatenate([w_router_hi, w_router_lo], axis=1)
        b_router = jnp.concatenate([b_router_group[l], b_router_expert[l].reshape(N_EXPERTS)])
        b_router = jnp.pad(b_router, (0, LANES - b_router.shape[0])).reshape(1, LANES)
        sgu_bias = jnp.repeat(sgu_b[l].T, D // SGU_GROUPS, axis=1)

        x1, h2, info, counts = _merge_call(
            x2, u, v, o, ga, gb, gate1, shift2, scale2, norm2_g[l].reshape(1, D), sgu_w[l],
            sgu_bias, w_proj_a[l].astype(BF16), w_proj_b[l].astype(BF16), w_out[l].astype(BF16),
            w_router, b_router, B, S)

        plan = _dispatch_plan(counts[:, 0, :N_CLASSES].astype(jnp.int32), n_ffn_tiles)
        hs = _dispatch_call(plan, h2, info, n_ffn_tiles)
        ys = _ffn_call(plan, hs, w_gate[l], w_up[l], w_down[l], n_ffn_tiles)
        x2 = _combine_call(plan, x1, info, gate2, ys, B, S)
    return x2.reshape(B, S, D)
```

```python
import math

import jax
import jax.numpy as jnp
import numpy as np
from jax import lax
from jax.experimental import pallas as pl
from jax.experimental.pallas import tpu as pltpu

D_MODEL = 1024
CHUNK = 128
SGU_GROUPS = 8
ATT_HEADS = 8
ATT_HEAD_DIM = 128
DILATED_PATTERNS = ((128, 1), (512, 4), (2048, 16))
N_GROUPS = 4
EXPERTS_PER_GROUP = 4
N_EXPERTS = N_GROUPS * EXPERTS_PER_GROUP
D_EXPERT = 512
EPS = 1e-6
NEG_INF = -1e30

N_IN_SPLITS = 7
LANES = 128
SUBLANES = 8
EXPERT_LANE0 = N_GROUPS

PAIRS = ((0, 1), (0, 2), (0, 3), (1, 2), (1, 3), (2, 3))
N_CLASSES = N_GROUPS * len(PAIRS)
TOKEN_TILE = 512
SORT_ROWS = TOKEN_TILE + 3 * 64
ROW_WIDTH = D_MODEL + LANES
FFN_TILE = 512
INFO_DEST, INFO_PLO_H, INFO_PHI_H, INFO_PLO_L, INFO_PHI_L = 0, 1, 2, 3, 4

VMEM_LIMIT = 56 * 1024 * 1024
BF16 = jnp.bfloat16
F32 = jnp.float32

assert SORT_ROWS >= TOKEN_TILE + N_CLASSES * (SUBLANES - 1)


def _rms(x, g):
    return x * lax.rsqrt(jnp.mean(x * x, axis=-1, keepdims=True) + EPS) * g


def _gelu_tanh(x):
    k = -2.0 * math.sqrt(2.0 / math.pi) * math.log2(math.e)
    return x / (1.0 + jnp.exp2(x * (k + (k * 0.044715) * (x * x))))


def _ada_kernel(c_ref, w_ref, b_ref, o_ref):
    c = c_ref[...]
    cond = c * jax.nn.sigmoid(c)
    o_ref[...] = jnp.dot(cond, w_ref[...], preferred_element_type=F32,
                         precision=lax.Precision.HIGHEST) + b_ref[...]


def _ada_call(c, w_ada, b_ada):
    B, D = c.shape
    N = w_ada.shape[1]
    tn = 3072
    return pl.pallas_call(
        _ada_kernel,
        grid=(N // tn,),
        in_specs=[pl.BlockSpec((B, D), lambda j: (0, 0)),
                  pl.BlockSpec((D, tn), lambda j: (0, j)),
                  pl.BlockSpec((1, tn), lambda j: (0, j))],
        out_specs=pl.BlockSpec((B, tn), lambda j: (0, j)),
        out_shape=jax.ShapeDtypeStruct((B, N), F32),
        compiler_params=pltpu.CompilerParams(dimension_semantics=("arbitrary",),
                                             vmem_limit_bytes=VMEM_LIMIT),
        name="adaln",
    )(c, w_ada, b_ada.reshape(1, N))


def _inproj_kernel(x_ref, shift_ref, scale_ref, n1g_ref, w_ref, sgug_ref, qg_ref, kg_ref,
                   u_ref, v_ref, q_ref, k_ref, va_ref, ga_ref, gb_ref):
    x = x_ref[...]
    h = _rms(x, n1g_ref[...]) * (1.0 + scale_ref[0]) + shift_ref[0]
    h = h.astype(BF16)

    def proj(i):
        return jnp.dot(h, w_ref[:, i * D_MODEL:(i + 1) * D_MODEL], preferred_element_type=F32)

    u_ref[...] = _gelu_tanh(proj(0)).astype(BF16)
    v_ref[...] = _rms(_gelu_tanh(proj(1)), sgug_ref[...]).astype(BF16)

    q = proj(2)
    qscale = ATT_HEAD_DIM ** -0.5
    for hd in range(ATT_HEADS):
        sl = slice(hd * ATT_HEAD_DIM, (hd + 1) * ATT_HEAD_DIM)
        q_ref[0, hd] = (_rms(q[:, sl], qg_ref[...]) * qscale).astype(BF16)
    k = proj(3)
    for hd in range(ATT_HEADS):
        sl = slice(hd * ATT_HEAD_DIM, (hd + 1) * ATT_HEAD_DIM)
        k_ref[0, hd] = _rms(k[:, sl], kg_ref[...]).astype(BF16)
    ga_ref[...] = jax.nn.sigmoid(proj(5)).astype(BF16)
    gb_ref[...] = jax.nn.sigmoid(proj(6)).astype(BF16)
    va = proj(4)
    for hd in range(ATT_HEADS):
        sl = slice(hd * ATT_HEAD_DIM, (hd + 1) * ATT_HEAD_DIM)
        va_ref[0, hd] = va[:, sl].astype(BF16)


def _inproj_call(x2, shift1, scale1, n1g, w_in, sgug, qg, kg, B, S, tm):
    T, D = x2.shape
    tpb = S // tm
    tok = pl.BlockSpec((tm, D), lambda i: (i, 0))
    per_b = pl.BlockSpec((1, 1, D), lambda i: (i // tpb, 0, 0))
    row = lambda n: pl.BlockSpec((1, n), lambda i: (0, 0))
    head = pl.BlockSpec((1, ATT_HEADS, tm, ATT_HEAD_DIM), lambda i: (i // tpb, 0, i % tpb, 0))
    tok_sds = jax.ShapeDtypeStruct((T, D), BF16)
    head_sds = jax.ShapeDtypeStruct((B, ATT_HEADS, S, ATT_HEAD_DIM), BF16)
    return pl.pallas_call(
        _inproj_kernel,
        grid=(T // tm,),
        in_specs=[tok, per_b, per_b, row(D),
                  pl.BlockSpec((D, N_IN_SPLITS * D), lambda i: (0, 0), pipeline_mode=pl.Buffered(1)),
                  row(D), row(ATT_HEAD_DIM), row(ATT_HEAD_DIM)],
        out_specs=[tok, tok, head, head, head, tok, tok],
        out_shape=[tok_sds, tok_sds, head_sds, head_sds, head_sds, tok_sds, tok_sds],
        compiler_params=pltpu.CompilerParams(dimension_semantics=("parallel",),
                                             vmem_limit_bytes=VMEM_LIMIT),
        name="inproj",
    )(x2, shift1, scale1, n1g, w_in, sgug, qg, kg)


ATT_BLK = 128
STRIDED_BLK = 256
HEADS_PER_STEP = 4
RESIDUES = 4
LOCAL_PATTERNS = tuple(p for p in DILATED_PATTERNS if p[1] % RESIDUES != 0)
STRIDED_PATTERNS = tuple(p for p in DILATED_PATTERNS if p[1] % RESIDUES == 0)
assert all(d == 1 and w <= ATT_BLK for w, d in LOCAL_PATTERNS)


def _log_count_bias(count):
    return jnp.asarray(np.where(count > 0, np.log(np.maximum(count, 1)), NEG_INF), F32)


def _local_bias():
    a = np.arange(ATT_BLK)[:, None]
    col = np.arange(2 * ATT_BLK)[None, :]
    delta = ATT_BLK + a - col
    count = np.zeros(delta.shape, np.int64)
    for window, _ in LOCAL_PATTERNS:
        count += (delta >= 0) & (delta <= window)
    return _log_count_bias(count)


def _strided_bias(n):
    nblk = n // STRIDED_BLK
    a = np.arange(STRIDED_BLK)[:, None]
    col = np.arange(n)[None, :]
    delta = (STRIDED_BLK * (nblk - 1 - col // STRIDED_BLK) + a - col % STRIDED_BLK) * RESIDUES
    count = np.zeros(delta.shape, np.int64)
    for window, dilation in STRIDED_PATTERNS:
        count += (delta >= 0) & (delta <= window) & (delta % dilation == 0)
    return _log_count_bias(count)


def _qk(q, k):
    return lax.dot_general(q, k, (((1,), (1,)), ((), ())), preferred_element_type=F32)


def _attention_kernel(q_ref, k_ref, v_ref, lbias_ref, sbias_ref, o_ref,
                      stage, q4, k4, v4, ve, acc_scr, m_scr, l_scr):
    S = q_ref.shape[2]
    n = S // RESIDUES
    nblk_s = n // STRIDED_BLK
    nblk = S // ATT_BLK
    Dh = ATT_HEAD_DIM
    ones = jnp.ones((S, LANES), BF16)

    def relayout(hd):
        ve[hd, :, Dh:] = ones
        ve[hd, :, :Dh] = v_ref[0, hd]
        for r in range(RESIDUES):
            v4[hd, r, :, Dh:] = ones[:n]
        for src, dst in ((q_ref, q4), (k_ref, k4), (v_ref, v4)):
            stage[hd] = src[0, hd].astype(F32)
            for r in range(RESIDUES):
                dst[hd, r, :, :Dh] = stage[hd, pl.ds(r, n, stride=RESIDUES), :].astype(BF16)

    def strided_block(hd, r, jb):
        nk = (jb + 1) * STRIDED_BLK

        def scores():
            s = _qk(q4[hd, r, jb * STRIDED_BLK:(jb + 1) * STRIDED_BLK, :], k4[hd, r, :nk, :])
            return s + sbias_ref[:, (nblk_s - 1 - jb) * STRIDED_BLK:]

        def rest(s):
            m = jnp.max(s, axis=-1, keepdims=True)
            p = jnp.exp(s - m)
            acc = jnp.dot(p.astype(BF16), v4[hd, r, :nk, :], preferred_element_type=F32)
            rows = pl.ds(RESIDUES * jb * STRIDED_BLK + r, STRIDED_BLK, stride=RESIDUES)
            acc_scr[hd, rows, :] = acc[:, :Dh]
            m_scr[hd, rows, :] = jnp.broadcast_to(m, (STRIDED_BLK, LANES))
            l_scr[hd, rows, :] = acc[:, Dh:]

        return scores, rest

    def local_block(hd, c):
        k0 = max(c - 1, 0) * ATT_BLK
        nk = (c + 1) * ATT_BLK - k0
        blk = slice(c * ATT_BLK, (c + 1) * ATT_BLK)

        def scores():
            s = _qk(q_ref[0, hd, blk, :], k_ref[0, hd, k0:k0 + nk, :])
            return s + lbias_ref[:, 2 * ATT_BLK - nk:]

        def rest(s):
            m_s = m_scr[hd, blk, :]
            m = jnp.maximum(jnp.max(s, axis=-1, keepdims=True), m_s)
            w = jnp.exp(m_s - m)
            p = jnp.exp(s - jnp.concatenate([m] * (nk // LANES), axis=-1))
            acc = jnp.dot(p.astype(BF16), ve[hd, k0:k0 + nk, :], preferred_element_type=F32)
            l = acc[:, Dh:] + l_scr[hd, blk, :] * w
            o_ref[0, hd, blk, :] = ((acc[:, :Dh] + acc_scr[hd, blk, :] * w) / l).astype(BF16)

        return scores, rest

    for hd in range(HEADS_PER_STEP):
        relayout(hd)
        blocks = [strided_block(hd, r, jb) for r in range(RESIDUES) for jb in range(nblk_s)]
        blocks += [local_block(hd, c) for c in range(nblk)]
        s_next = blocks[0][0]()
        for b, (_, rest) in enumerate(blocks):
            s_cur = s_next
            if b + 1 < len(blocks):
                s_next = blocks[b + 1][0]()
            rest(s_cur)


def _attention_call(q, k, v):
    B, H, S, Dh = q.shape
    n = S // RESIDUES
    P = HEADS_PER_STEP
    blk = pl.BlockSpec((1, P, S, Dh), lambda b, h: (b, h, 0, 0))
    return pl.pallas_call(
        _attention_kernel,
        grid=(B, H // P),
        in_specs=[blk, blk, blk,
                  pl.BlockSpec((ATT_BLK, 2 * ATT_BLK), lambda b, h: (0, 0)),
                  pl.BlockSpec((STRIDED_BLK, n), lambda b, h: (0, 0))],
        out_specs=blk,
        out_shape=jax.ShapeDtypeStruct((B, H, S, Dh), BF16),
        scratch_shapes=[pltpu.VMEM((P, S, Dh), F32),
                        pltpu.VMEM((P, RESIDUES, n, Dh), BF16),
                        pltpu.VMEM((P, RESIDUES, n, Dh), BF16),
                        pltpu.VMEM((P, RESIDUES, n, Dh + LANES), BF16),
                        pltpu.VMEM((P, S, Dh + LANES), BF16),
                        pltpu.VMEM((P, S, Dh), F32),
                        pltpu.VMEM((P, S, LANES), F32),
                        pltpu.VMEM((P, S, LANES), F32)],
        compiler_params=pltpu.CompilerParams(dimension_semantics=("parallel", "parallel"),
                                             vmem_limit_bytes=VMEM_LIMIT),
        name="attention",
    )(q, k, v, _local_bias(), _strided_bias(n))


def _route(logits):
    tm = logits.shape[0]
    lane = lax.broadcasted_iota(jnp.int32, logits.shape, 1)
    lanef = lane.astype(F32)

    def masked_top(mask):
        top = jnp.max(jnp.where(mask, logits, -jnp.inf), axis=-1, keepdims=True)
        idx = jnp.min(jnp.where(mask & (logits == top), lanef, float(LANES)), axis=-1, keepdims=True)
        return top, idx

    gmask = lane < N_GROUPS
    gmax, gidx = masked_top(gmask)
    p_group = 1.0 / jnp.sum(jnp.where(gmask, jnp.exp(logits - gmax), 0.0), axis=-1, keepdims=True)
    e_lo = EXPERT_LANE0 + gidx * EXPERTS_PER_GROUP
    emask = (lanef >= e_lo) & (lanef < e_lo + EXPERTS_PER_GROUP)
    v1, i1 = masked_top(emask)
    v2, i2 = masked_top(emask & (lanef != i1))
    e21 = jnp.exp(v2 - v1)
    p1 = p_group / (1.0 + e21)
    p2 = p_group * e21 / (1.0 + e21)
    first_is_lo = i1 < i2
    lo = jnp.minimum(i1, i2) - e_lo
    hi = jnp.maximum(i1, i2) - e_lo
    pair = lo * (7.0 - lo) * 0.5 + (hi - lo - 1.0)
    cls = gidx * float(len(PAIRS)) + pair
    p_lo = jnp.where(first_is_lo, p1, p2)
    p_hi = jnp.where(first_is_lo, p2, p1)

    onehot = lanef == cls
    onehot_b = jnp.where(onehot, 1.0, 0.0).astype(BF16)
    r = lax.broadcasted_iota(jnp.int32, (tm, tm), 0)
    c = lax.broadcasted_iota(jnp.int32, (tm, tm), 1)
    before = jnp.where(c < r, 1.0, 0.0).astype(BF16)
    rank = jnp.dot(before, onehot_b, preferred_element_type=F32)
    counts = jnp.sum(jnp.where(onehot, 1.0, 0.0), axis=0, keepdims=True)
    units = jnp.ceil(counts * (1.0 / SUBLANES))
    ur = lax.broadcasted_iota(jnp.int32, (LANES, LANES), 0)
    uc = lax.broadcasted_iota(jnp.int32, (LANES, LANES), 1)
    upper = jnp.where(ur < uc, 1.0, 0.0).astype(BF16)
    start = jnp.dot(jnp.broadcast_to(units, (SUBLANES, LANES)).astype(BF16), upper,
                    preferred_element_type=F32)[0:1] * float(SUBLANES)
    dest = jnp.sum(jnp.where(onehot, start + rank, 0.0), axis=-1, keepdims=True)

    def hi_part(p):
        return p.astype(BF16).astype(F32)

    info = jnp.where(lane == INFO_DEST, dest, 0.0)
    info = jnp.where(lane == INFO_PLO_H, hi_part(p_lo), info)
    info = jnp.where(lane == INFO_PHI_H, hi_part(p_hi), info)
    info = jnp.where(lane == INFO_PLO_L, p_lo - hi_part(p_lo), info)
    info = jnp.where(lane == INFO_PHI_L, p_hi - hi_part(p_hi), info)
    return info, counts


def _merge_kernel(x_ref, u_ref, v_ref, o_ref, ga_ref, gb_ref, gate1_ref, shift2_ref, scale2_ref,
                  n2g_ref, sw_ref, sb_ref, wa_ref, wb_ref, wo_ref, wr_ref, br_ref,
                  x1_ref, h2_ref, info_ref, counts_ref, s_scr, logit_scr):
    tm = x_ref.shape[0]
    step = pl.program_id(0)

    @pl.when(step == 0)
    def _():
        logit_scr[...] = jnp.zeros_like(logit_scr)

    prev_logits = logit_scr[1 - step % 2]
    row = lax.broadcasted_iota(jnp.int32, (CHUNK, CHUNK), 0)
    colm = lax.broadcasted_iota(jnp.int32, (CHUNK, CHUNK), 1)
    causal = colm <= row
    for g in range(SGU_GROUPS):
        w = jnp.where(causal, sw_ref[g], 0.0).astype(BF16)
        gs = slice(g * CHUNK, (g + 1) * CHUNK)
        chunks = [slice(c * CHUNK, (c + 1) * CHUNK) for c in range(tm // CHUNK)]
        mixed = jnp.dot(w, jnp.concatenate([v_ref[cs, gs] for cs in chunks], axis=1),
                        preferred_element_type=F32)
        for cs in chunks:
            s_scr[cs, gs] = (u_ref[cs, gs].astype(F32) * (mixed[:, cs] + sb_ref[:, gs])).astype(BF16)

    y_a = jnp.dot(s_scr[...], wa_ref[...], preferred_element_type=F32)
    info, counts = _route(prev_logits)
    info_ref[...] = info
    counts_ref[0] = counts
    o = jnp.concatenate([o_ref[0, hd] for hd in range(ATT_HEADS)], axis=-1)
    y_b = jnp.dot(o, wb_ref[...], preferred_element_type=F32)
    merged = ga_ref[...].astype(F32) * y_a + gb_ref[...].astype(F32) * y_b
    y = jnp.dot(merged.astype(BF16), wo_ref[...], preferred_element_type=F32)
    x1 = x_ref[...] + gate1_ref[0] * y
    x1_ref[...] = x1

    h2 = _rms(x1, n2g_ref[...]) * (1.0 + scale2_ref[0]) + shift2_ref[0]
    h2_ref[...] = h2.astype(BF16)
    h_hi = h2.astype(BF16)
    h_lo = (h2 - h_hi.astype(F32)).astype(BF16)
    r_hi = jnp.dot(h_hi, wr_ref[...], preferred_element_type=F32)
    r_lo = jnp.dot(h_lo, wr_ref[:, :LANES], preferred_element_type=F32)
    logit_scr[step % 2] = r_hi[:, :LANES] + r_hi[:, LANES:] + r_lo + br_ref[...]


def _merge_call(x2, u, v, o, ga, gb, gate1, shift2, scale2, n2g, sgu_w, sgu_bias, wa, wb, wo,
                w_router, b_router, B, S):
    T, D = x2.shape
    tm = TOKEN_TILE
    tpb = S // tm
    n_tiles = T // tm
    tile = lambda i: jnp.minimum(i, n_tiles - 1)
    routed = lambda i: jnp.maximum(i - 1, 0)
    tok = pl.BlockSpec((tm, D), lambda i: (tile(i), 0))
    per_b = pl.BlockSpec((1, 1, D), lambda i: (tile(i) // tpb, 0, 0))
    head = pl.BlockSpec((1, ATT_HEADS, tm, ATT_HEAD_DIM),
                        lambda i: (tile(i) // tpb, 0, tile(i) % tpb, 0))
    full = lambda *shape: pl.BlockSpec(shape, lambda i: (0,) * len(shape))
    return pl.pallas_call(
        _merge_kernel,
        grid=(n_tiles + 1,),
        in_specs=[tok, tok, tok, head, tok, tok, per_b, per_b, per_b, full(1, D),
                  full(SGU_GROUPS, CHUNK, CHUNK), full(CHUNK, D),
                  full(D, D), full(D, D), full(D, D), full(D, 2 * LANES), full(1, LANES)],
        out_specs=[tok, tok, pl.BlockSpec((tm, LANES), lambda i: (routed(i), 0)),
                   pl.BlockSpec((1, 1, LANES), lambda i: (routed(i), 0, 0))],
        out_shape=[jax.ShapeDtypeStruct((T, D), F32), jax.ShapeDtypeStruct((T, D), BF16),
                   jax.ShapeDtypeStruct((T, LANES), F32),
                   jax.ShapeDtypeStruct((n_tiles, 1, LANES), F32)],
        scratch_shapes=[pltpu.VMEM((tm, D), BF16), pltpu.VMEM((2, tm, LANES), F32)],
        compiler_params=pltpu.CompilerParams(dimension_semantics=("arbitrary",),
                                             vmem_limit_bytes=VMEM_LIMIT),
        name="merge",
    )(x2, u, v, o, ga, gb, gate1, shift2, scale2, n2g, sgu_w, sgu_bias, wa, wb, wo,
      w_router, b_router)


def _dispatch_plan(counts, n_ffn_tiles):
    units = (counts + SUBLANES - 1) // SUBLANES
    rows = units * SUBLANES
    local_start = jnp.cumsum(rows, axis=1) - rows
    class_rows = jnp.sum(rows, axis=0)
    class_tiles = (class_rows + FFN_TILE - 1) // FFN_TILE
    cum_tiles = jnp.cumsum(class_tiles)
    seg_start = (cum_tiles - class_tiles) * FFN_TILE
    run_start = seg_start[None, :] + jnp.cumsum(rows, axis=0) - rows
    n_used = cum_tiles[-1]
    tile_idx = jnp.maximum(jnp.minimum(jnp.arange(n_ffn_tiles, dtype=jnp.int32), n_used - 1), 0)
    tile_cls = jnp.sum((tile_idx[:, None] >= cum_tiles[None, :]).astype(jnp.int32), axis=1)
    pair = tile_cls % len(PAIRS)
    group = tile_cls // len(PAIRS)
    pairs = jnp.asarray(PAIRS, jnp.int32)
    i32 = lambda a: a.astype(jnp.int32).reshape(-1)

    return dict(
        run_start=i32(run_start), local_start=i32(local_start), units=i32(units),
        tile_units=i32(jnp.sum(units, axis=1)),
        tail_start=i32(seg_start + class_rows),
        tail_units=i32((class_tiles * FFN_TILE - class_rows) // SUBLANES),
        tile_idx=i32(tile_idx), n_used=i32(n_used),
        tile_ea=i32(group * EXPERTS_PER_GROUP + pairs[pair, 0]),
        tile_eb=i32(group * EXPERTS_PER_GROUP + pairs[pair, 1]))


def _rows(units):
    return pl.multiple_of(units * SUBLANES, SUBLANES)


def _start_runs(tile, run_start, local_start, units, make_copy):
    def per_class(c, carry):
        k = tile * N_CLASSES + c

        @pl.when(units[k] > 0)
        def _():
            make_copy(pl.multiple_of(local_start[k], SUBLANES),
                      pl.multiple_of(run_start[k], SUBLANES), _rows(units[k])).start()

        return carry

    lax.fori_loop(0, N_CLASSES, per_class, 0)


def _wait_runs(tile, tile_units, make_copy):
    make_copy(0, 0, _rows(tile_units[tile])).wait()


def _dispatch_kernel(run_start, local_start, units, tile_units, tail_start, tail_units, n_used,
                     h2_ref, info_ref, hs_ref, sorted_scr, zero_scr, sems, tail_sem, unused_sem):
    i = pl.program_id(0)
    nt = pl.num_programs(0)
    slot = i % 2
    n_ffn_tiles = hs_ref.shape[0] // FFN_TILE

    def run_copy(s):
        def make(local_row, global_row, rows):
            return pltpu.make_async_copy(sorted_scr.at[s, pl.ds(local_row, rows), :],
                                         hs_ref.at[pl.ds(global_row, rows), :], sems.at[s])
        return make

    def tail_copy(c):
        rows = _rows(tail_units[c])
        return pltpu.make_async_copy(
            zero_scr.at[pl.ds(0, rows), :],
            hs_ref.at[pl.ds(pl.multiple_of(tail_start[c], SUBLANES), rows), :], tail_sem)

    def unused_copy(t):
        return pltpu.make_async_copy(
            zero_scr, hs_ref.at[pl.ds(pl.multiple_of(t * FFN_TILE, FFN_TILE), FFN_TILE), :],
            unused_sem)

    def for_slot(fn):
        for s in range(2):
            pl.when(slot == s)(lambda s=s: fn(s))

    def for_tails(fn):
        def per_class(c, carry):
            pl.when(tail_units[c] > 0)(lambda: fn(c))
            return carry

        lax.fori_loop(0, N_CLASSES, per_class, 0)

    @pl.when(i >= 2)
    def _():
        for_slot(lambda s: _wait_runs(i - 2, tile_units, run_copy(s)))

    @pl.when(i == 0)
    def _():
        zero_scr[...] = jnp.zeros_like(zero_scr)
        for_tails(lambda c: tail_copy(c).start())

        def per_unused(t, carry):
            unused_copy(t).start()
            return carry

        lax.fori_loop(n_used[0], n_ffn_tiles, per_unused, 0)

    info = info_ref[...]
    dest_row = info.T[INFO_DEST:INFO_DEST + 1, :]
    rows = lax.broadcasted_iota(jnp.int32, (SORT_ROWS, TOKEN_TILE), 0).astype(F32)
    perm = jnp.where(rows == dest_row, 1.0, 0.0).astype(BF16)
    lane = lax.broadcasted_iota(jnp.int32, info.shape, 1)
    weights = jnp.where(lane == INFO_DEST, 0.0, info).astype(BF16)

    def sort_and_send(s):
        sorted_scr[s, :, :D_MODEL] = jnp.dot(perm, h2_ref[...], preferred_element_type=F32)
        sorted_scr[s, :, D_MODEL:] = jnp.dot(perm, weights, preferred_element_type=F32)
        _start_runs(i, run_start, local_start, units, run_copy(s))

    for_slot(sort_and_send)

    @pl.when(i == nt - 1)
    def _():
        for_slot(lambda s: _wait_runs(i, tile_units, run_copy(s)))

        @pl.when(nt >= 2)
        def _():
            for_slot(lambda s: _wait_runs(i - 1, tile_units, run_copy(1 - s)))

        for_tails(lambda c: tail_copy(c).wait())

        def per_unused(t, carry):
            unused_copy(0).wait()
            return carry

        lax.fori_loop(n_used[0], n_ffn_tiles, per_unused, 0)


def _dispatch_call(plan, h2, info, n_ffn_tiles):
    T, D = h2.shape
    tm = TOKEN_TILE
    grid_spec = pltpu.PrefetchScalarGridSpec(
        num_scalar_prefetch=7,
        grid=(T // tm,),
        in_specs=[pl.BlockSpec((tm, D), lambda i, *_: (i, 0)),
                  pl.BlockSpec((tm, LANES), lambda i, *_: (i, 0))],
        out_specs=pl.BlockSpec(memory_space=pl.ANY),
        scratch_shapes=[pltpu.VMEM((2, SORT_ROWS, ROW_WIDTH), F32),
                        pltpu.VMEM((FFN_TILE, ROW_WIDTH), F32),
                        pltpu.SemaphoreType.DMA((2,)),
                        pltpu.SemaphoreType.DMA(()),
                        pltpu.SemaphoreType.DMA(())])
    return pl.pallas_call(
        _dispatch_kernel,
        grid_spec=grid_spec,
        out_shape=jax.ShapeDtypeStruct((n_ffn_tiles * FFN_TILE, ROW_WIDTH), F32),
        compiler_params=pltpu.CompilerParams(dimension_semantics=("arbitrary",),
                                             vmem_limit_bytes=VMEM_LIMIT),
        name="dispatch",
    )(plan["run_start"], plan["local_start"], plan["units"], plan["tile_units"],
      plan["tail_start"], plan["tail_units"], plan["n_used"], h2, info)


def _ffn_kernel(tile_idx, tile_ea, tile_eb, n_used, hs_ref, wga_ref, wua_ref, wda_ref,
                wgb_ref, wub_ref, wdb_ref, ys_ref):
    @pl.when(pl.program_id(0) >= n_used[0])
    def _():
        ys_ref[...] = jnp.zeros_like(ys_ref)

    @pl.when(pl.program_id(0) < n_used[0])
    def _():
        h = hs_ref[:, :D_MODEL].astype(BF16)
        pv = hs_ref[:, D_MODEL:]
        p_lo = pv[:, INFO_PLO_H:INFO_PLO_H + 1] + pv[:, INFO_PLO_L:INFO_PLO_L + 1]
        p_hi = pv[:, INFO_PHI_H:INFO_PHI_H + 1] + pv[:, INFO_PHI_L:INFO_PHI_L + 1]

        def hidden(wg_ref, wu_ref, p):
            a = jnp.dot(h, wg_ref[0], preferred_element_type=F32)
            b = jnp.dot(h, wu_ref[0], preferred_element_type=F32)
            return (a * jax.nn.sigmoid(a) * b * p).astype(BF16)

        he_a = hidden(wga_ref, wua_ref, p_lo)
        he_b = hidden(wgb_ref, wub_ref, p_hi)
        ys_ref[...] = (jnp.dot(he_a, wda_ref[0], preferred_element_type=F32)
                       + jnp.dot(he_b, wdb_ref[0], preferred_element_type=F32))


def _ffn_call(plan, hs, wg, wu, wd, n_ffn_tiles):
    D = D_MODEL
    rows = pl.BlockSpec((FFN_TILE, ROW_WIDTH), lambda t, idx, ea, eb, n: (idx[t], 0))
    w_in_a = pl.BlockSpec((1, D, D_EXPERT), lambda t, idx, ea, eb, n: (ea[t], 0, 0))
    w_out_a = pl.BlockSpec((1, D_EXPERT, D), lambda t, idx, ea, eb, n: (ea[t], 0, 0))
    w_in_b = pl.BlockSpec((1, D, D_EXPERT), lambda t, idx, ea, eb, n: (eb[t], 0, 0))
    w_out_b = pl.BlockSpec((1, D_EXPERT, D), lambda t, idx, ea, eb, n: (eb[t], 0, 0))
    grid_spec = pltpu.PrefetchScalarGridSpec(
        num_scalar_prefetch=4,
        grid=(n_ffn_tiles,),
        in_specs=[rows, w_in_a, w_in_a, w_out_a, w_in_b, w_in_b, w_out_b],
        out_specs=pl.BlockSpec((FFN_TILE, D), lambda t, idx, ea, eb, n: (t, 0)))
    return pl.pallas_call(
        _ffn_kernel,
        grid_spec=grid_spec,
        out_shape=jax.ShapeDtypeStruct((n_ffn_tiles * FFN_TILE, D), F32),
        compiler_params=pltpu.CompilerParams(dimension_semantics=("arbitrary",),
                                             vmem_limit_bytes=VMEM_LIMIT),
        name="ffn",
    )(plan["tile_idx"], plan["tile_ea"], plan["tile_eb"], plan["n_used"],
      hs, wg, wu, wd, wg, wu, wd)


def _combine_kernel(run_start, local_start, units, tile_units,
                    x1_ref, info_ref, gate2_ref, ys_ref, o_ref, ybuf, sems):
    i = pl.program_id(0)
    nt = pl.num_programs(0)
    slot = i % 2

    def run_copy(s):
        def make(local_row, global_row, rows):
            return pltpu.make_async_copy(ys_ref.at[pl.ds(global_row, rows), :],
                                         ybuf.at[s, pl.ds(local_row, rows), :], sems.at[s])
        return make

    def for_slot(fn):
        for s in range(2):
            pl.when(slot == s)(lambda s=s: fn(s))

    def fetch(tile, s):
        _start_runs(tile, run_start, local_start, units, run_copy(s))

    @pl.when(i == 0)
    def _():
        ybuf[...] = jnp.zeros_like(ybuf)
        fetch(0, 0)

    @pl.when(i + 1 < nt)
    def _():
        for_slot(lambda s: fetch(i + 1, 1 - s))

    dest_col = info_ref[:, INFO_DEST:INFO_DEST + 1]
    cols = lax.broadcasted_iota(jnp.int32, (TOKEN_TILE, SORT_ROWS), 1).astype(F32)
    unperm = jnp.where(cols == dest_col, 1.0, 0.0).astype(BF16)

    def finish(s):
        _wait_runs(i, tile_units, run_copy(s))
        y = jnp.dot(unperm, ybuf[s].astype(BF16), preferred_element_type=F32)
        o_ref[...] = x1_ref[...] + gate2_ref[0] * y

    for_slot(finish)


def _combine_call(plan, x1, info, gate2, ys, B, S):
    T, D = x1.shape
    tm = TOKEN_TILE
    tpb = S // tm
    grid_spec = pltpu.PrefetchScalarGridSpec(
        num_scalar_prefetch=4,
        grid=(T // tm,),
        in_specs=[pl.BlockSpec((tm, D), lambda i, *_: (i, 0)),
                  pl.BlockSpec((tm, LANES), lambda i, *_: (i, 0)),
                  pl.BlockSpec((1, 1, D), lambda i, *_: (i // tpb, 0, 0)),
                  pl.BlockSpec(memory_space=pl.ANY)],
        out_specs=pl.BlockSpec((tm, D), lambda i, *_: (i, 0)),
        scratch_shapes=[pltpu.VMEM((2, SORT_ROWS, D), F32),
                        pltpu.SemaphoreType.DMA((2,))])
    return pl.pallas_call(
        _combine_kernel,
        grid_spec=grid_spec,
        out_shape=jax.ShapeDtypeStruct((T, D), F32),
        compiler_params=pltpu.CompilerParams(dimension_semantics=("arbitrary",),
                                             vmem_limit_bytes=VMEM_LIMIT),
        name="combine",
    )(plan["run_start"], plan["local_start"], plan["units"], plan["tile_units"],
      x1, info, gate2, ys)


def kernel(x, c, w_ada, b_ada, norm1_g, w_in, sgu_norm_g, sgu_w, sgu_b, q_norm_g, k_norm_g,
           w_proj_a, w_proj_b, w_out, norm2_g, w_router_group, b_router_group,
           w_router_expert, b_router_expert, w_gate, w_up, w_down):
    B, S, D = x.shape
    T = B * S
    depth = w_ada.shape[0]
    n_token_tiles = T // TOKEN_TILE
    max_rows = T + n_token_tiles * N_CLASSES * (SUBLANES - 1) + N_CLASSES * (FFN_TILE - SUBLANES)
    n_ffn_tiles = -(-max_rows // FFN_TILE)
    x2 = x.reshape(T, D)
    for l in range(depth):
        mod = _ada_call(c, w_ada[l], b_ada[l])
        shift1, scale1, gate1, shift2, scale2, gate2 = [
            m.reshape(B, 1, D) for m in jnp.split(mod, 6, axis=-1)]

        u, v, q, k, va, ga, gb = _inproj_call(
            x2, shift1, scale1, norm1_g[l].reshape(1, D), w_in[l].astype(BF16),
            sgu_norm_g[l].reshape(1, D), q_norm_g[l].reshape(1, ATT_HEAD_DIM),
            k_norm_g[l].reshape(1, ATT_HEAD_DIM), B, S, tm=512)

        o = _attention_call(q, k, va)

        w_router = jnp.concatenate(
            [w_router_group[l],
             jnp.transpose(w_router_expert[l], (1, 0, 2)).reshape(D, N_EXPERTS)], axis=1)
        w_router = jnp.pad(w_router, ((0, 0), (0, LANES - w_router.shape[1])))
        w_router_hi = w_router.astype(BF16)
        w_router_lo = (w_router - w_router_hi.astype(F32)).astype(BF16)
        w_router = jnp.concatenate([w_router_hi, w_router_lo], axis=1)
        b_router = jnp.concatenate([b_router_group[l], b_router_expert[l].reshape(N_EXPERTS)])
        b_router = jnp.pad(b_router, (0, LANES - b_router.shape[0])).reshape(1, LANES)
        sgu_bias = jnp.repeat(sgu_b[l].T, D // SGU_GROUPS, axis=1)

        x1, h2, info, counts = _merge_call(
            x2, u, v, o, ga, gb, gate1, shift2, scale2, norm2_g[l].reshape(1, D), sgu_w[l],
            sgu_bias, w_proj_a[l].astype(BF16), w_proj_b[l].astype(BF16), w_out[l].astype(BF16),
            w_router, b_router, B, S)

        plan = _dispatch_plan(counts[:, 0, :N_CLASSES].astype(jnp.int32), n_ffn_tiles)
        hs = _dispatch_call(plan, h2, info, n_ffn_tiles)
        ys = _ffn_call(plan, hs, w_gate[l], w_up[l], w_down[l], n_ffn_tiles)
        x2 = _combine_call(plan, x1, info, gate2, ys, B, S)
    return x2.reshape(B, S, D)
```

```python
import math

import jax
import jax.numpy as jnp
import numpy as np
from jax import lax
from jax.experimental import pallas as pl
from jax.experimental.pallas import tpu as pltpu

D_MODEL = 1024
CHUNK = 128
SGU_GROUPS = 8
ATT_HEADS = 8
ATT_HEAD_DIM = 128
DILATED_PATTERNS = ((128, 1), (512, 4), (2048, 16))
N_GROUPS = 4
EXPERTS_PER_GROUP = 4
N_EXPERTS = N_GROUPS * EXPERTS_PER_GROUP
D_EXPERT = 512
EPS = 1e-6
NEG_INF = -1e30

N_IN_SPLITS = 7
LANES = 128
SUBLANES = 8
EXPERT_LANE0 = N_GROUPS

PAIRS = ((0, 1), (0, 2), (0, 3), (1, 2), (1, 3), (2, 3))
N_CLASSES = N_GROUPS * len(PAIRS)
TOKEN_TILE = 512
SORT_ROWS = TOKEN_TILE + 3 * 64
ROW_WIDTH = D_MODEL + LANES
FFN_TILE = 512
INFO_DEST, INFO_PLO_H, INFO_PHI_H, INFO_PLO_L, INFO_PHI_L = 0, 1, 2, 3, 4

VMEM_LIMIT = 56 * 1024 * 1024
BF16 = jnp.bfloat16
F32 = jnp.float32

assert SORT_ROWS >= TOKEN_TILE + N_CLASSES * (SUBLANES - 1)


def _rms(x, g):
    return x * lax.rsqrt(jnp.mean(x * x, axis=-1, keepdims=True) + EPS) * g


def _gelu_tanh(x):
    k = -2.0 * math.sqrt(2.0 / math.pi) * math.log2(math.e)
    return x / (1.0 + jnp.exp2(x * (k + (k * 0.044715) * (x * x))))


def _ada_kernel(c_ref, w_ref, b_ref, o_ref):
    c = c_ref[...]
    cond = c * jax.nn.sigmoid(c)
    o_ref[...] = jnp.dot(cond, w_ref[...], preferred_element_type=F32,
                         precision=lax.Precision.HIGHEST) + b_ref[...]


def _ada_call(c, w_ada, b_ada):
    B, D = c.shape
    N = w_ada.shape[1]
    tn = 1024
    return pl.pallas_call(
        _ada_kernel,
        grid=(N // tn,),
        in_specs=[pl.BlockSpec((B, D), lambda j: (0, 0)),
                  pl.BlockSpec((D, tn), lambda j: (0, j)),
                  pl.BlockSpec((1, tn), lambda j: (0, j))],
        out_specs=pl.BlockSpec((B, tn), lambda j: (0, j)),
        out_shape=jax.ShapeDtypeStruct((B, N), F32),
        compiler_params=pltpu.CompilerParams(dimension_semantics=("arbitrary",),
                                             vmem_limit_bytes=VMEM_LIMIT),
        name="adaln",
    )(c, w_ada, b_ada.reshape(1, N))


def _inproj_kernel(x_ref, shift_ref, scale_ref, n1g_ref, w_ref, sgug_ref, qg_ref, kg_ref,
                   u_ref, v_ref, q_ref, k_ref, va_ref, ga_ref, gb_ref):
    x = x_ref[...]
    h = _rms(x, n1g_ref[...]) * (1.0 + scale_ref[0]) + shift_ref[0]
    h = h.astype(BF16)

    def proj(i):
        return jnp.dot(h, w_ref[:, i * D_MODEL:(i + 1) * D_MODEL], preferred_element_type=F32)

    u_ref[...] = _gelu_tanh(proj(0)).astype(BF16)
    v_ref[...] = _rms(_gelu_tanh(proj(1)), sgug_ref[...]).astype(BF16)

    q = proj(2)
    qscale = ATT_HEAD_DIM ** -0.5
    for hd in range(ATT_HEADS):
        sl = slice(hd * ATT_HEAD_DIM, (hd + 1) * ATT_HEAD_DIM)
        q_ref[0, hd] = (_rms(q[:, sl], qg_ref[...]) * qscale).astype(BF16)
    k = proj(3)
    for hd in range(ATT_HEADS):
        sl = slice(hd * ATT_HEAD_DIM, (hd + 1) * ATT_HEAD_DIM)
        k_ref[0, hd] = _rms(k[:, sl], kg_ref[...]).astype(BF16)
    ga_ref[...] = jax.nn.sigmoid(proj(5)).astype(BF16)
    gb_ref[...] = jax.nn.sigmoid(proj(6)).astype(BF16)
    va = proj(4)
    for hd in range(ATT_HEADS):
        sl = slice(hd * ATT_HEAD_DIM, (hd + 1) * ATT_HEAD_DIM)
        va_ref[0, hd] = va[:, sl].astype(BF16)


def _inproj_call(x2, shift1, scale1, n1g, w_in, sgug, qg, kg, B, S, tm):
    T, D = x2.shape
    tpb = S // tm
    tok = pl.BlockSpec((tm, D), lambda i: (i, 0))
    per_b = pl.BlockSpec((1, 1, D), lambda i: (i // tpb, 0, 0))
    row = lambda n: pl.BlockSpec((1, n), lambda i: (0, 0))
    head = pl.BlockSpec((1, ATT_HEADS, tm, ATT_HEAD_DIM), lambda i: (i // tpb, 0, i % tpb, 0))
    tok_sds = jax.ShapeDtypeStruct((T, D), BF16)
    head_sds = jax.ShapeDtypeStruct((B, ATT_HEADS, S, ATT_HEAD_DIM), BF16)
    return pl.pallas_call(
        _inproj_kernel,
        grid=(T // tm,),
        in_specs=[tok, per_b, per_b, row(D),
                  pl.BlockSpec((D, N_IN_SPLITS * D), lambda i: (0, 0), pipeline_mode=pl.Buffered(1)),
                  row(D), row(ATT_HEAD_DIM), row(ATT_HEAD_DIM)],
        out_specs=[tok, tok, head, head, head, tok, tok],
        out_shape=[tok_sds, tok_sds, head_sds, head_sds, head_sds, tok_sds, tok_sds],
        compiler_params=pltpu.CompilerParams(dimension_semantics=("parallel",),
                                             vmem_limit_bytes=VMEM_LIMIT),
        name="inproj",
    )(x2, shift1, scale1, n1g, w_in, sgug, qg, kg)


ATT_BLK = 128
STRIDED_BLK = 256
HEADS_PER_STEP = 4
RESIDUES = 4
LOCAL_PATTERNS = tuple(p for p in DILATED_PATTERNS if p[1] % RESIDUES != 0)
STRIDED_PATTERNS = tuple(p for p in DILATED_PATTERNS if p[1] % RESIDUES == 0)
assert all(d == 1 and w <= ATT_BLK for w, d in LOCAL_PATTERNS)


def _log_count_bias(count):
    return jnp.asarray(np.where(count > 0, np.log(np.maximum(count, 1)), NEG_INF), F32)


def _local_bias():
    a = np.arange(ATT_BLK)[:, None]
    col = np.arange(2 * ATT_BLK)[None, :]
    delta = ATT_BLK + a - col
    count = np.zeros(delta.shape, np.int64)
    for window, _ in LOCAL_PATTERNS:
        count += (delta >= 0) & (delta <= window)
    return _log_count_bias(count)


def _strided_bias(n):
    nblk = n // STRIDED_BLK
    a = np.arange(STRIDED_BLK)[:, None]
    col = np.arange(n)[None, :]
    delta = (STRIDED_BLK * (nblk - 1 - col // STRIDED_BLK) + a - col % STRIDED_BLK) * RESIDUES
    count = np.zeros(delta.shape, np.int64)
    for window, dilation in STRIDED_PATTERNS:
        count += (delta >= 0) & (delta <= window) & (delta % dilation == 0)
    return _log_count_bias(count)


def _qk(q, k):
    return lax.dot_general(q, k, (((1,), (1,)), ((), ())), preferred_element_type=F32)


def _attention_kernel(q_ref, k_ref, v_ref, lbias_ref, sbias_ref, o_ref,
                      stage, q4, k4, v4, ve, o_scr, lse_scr):
    S = q_ref.shape[2]
    n = S // RESIDUES
    nblk_s = n // STRIDED_BLK
    nblk = S // ATT_BLK
    Dh = ATT_HEAD_DIM
    ones = jnp.ones((S, LANES), BF16)

    def relayout(hd):
        ve[hd, :, Dh:] = ones
        ve[hd, :, :Dh] = v_ref[0, hd]
        for r in range(RESIDUES):
            v4[hd, r, :, Dh:] = ones[:n]
        for src, dst in ((q_ref, q4), (k_ref, k4), (v_ref, v4)):
            stage[hd] = src[0, hd].astype(F32)
            for r in range(RESIDUES):
                dst[hd, r, :, :Dh] = stage[hd, pl.ds(r, n, stride=RESIDUES), :].astype(BF16)

    def strided_block(hd, r, jb):
        nk = (jb + 1) * STRIDED_BLK

        def scores():
            s = _qk(q4[hd, r, jb * STRIDED_BLK:(jb + 1) * STRIDED_BLK, :], k4[hd, r, :nk, :])
            return s + sbias_ref[:, (nblk_s - 1 - jb) * STRIDED_BLK:]

        def rest(s):
            m = jnp.max(s, axis=-1, keepdims=True)
            p = jnp.exp(s - m)
            acc = jnp.dot(p.astype(BF16), v4[hd, r, :nk, :], preferred_element_type=F32)
            rows = pl.ds(RESIDUES * jb * STRIDED_BLK + r, STRIDED_BLK, stride=RESIDUES)
            o_scr[hd, rows, :] = acc[:, :Dh] / acc[:, Dh:]
            lse_scr[hd, rows, :] = m + jnp.log(acc[:, Dh:])

        return scores, rest

    def local_block(hd, c):
        k0 = max(c - 1, 0) * ATT_BLK
        nk = (c + 1) * ATT_BLK - k0
        blk = slice(c * ATT_BLK, (c + 1) * ATT_BLK)

        def scores():
            s = _qk(q_ref[0, hd, blk, :], k_ref[0, hd, k0:k0 + nk, :])
            return s + lbias_ref[:, 2 * ATT_BLK - nk:]

        def rest(s):
            lse_s = lse_scr[hd, blk, :]
            m = jnp.maximum(jnp.max(s, axis=-1, keepdims=True), lse_s)
            w = jnp.exp(lse_s - m)
            p = jnp.exp(s - jnp.concatenate([m] * (nk // LANES), axis=-1))
            acc = jnp.dot(p.astype(BF16), ve[hd, k0:k0 + nk, :], preferred_element_type=F32)
            l = acc[:, Dh:] + w
            o_ref[0, hd, blk, :] = ((acc[:, :Dh] + o_scr[hd, blk, :] * w) / l).astype(BF16)

        return scores, rest

    for hd in range(HEADS_PER_STEP):
        relayout(hd)
        blocks = [strided_block(hd, r, jb) for r in range(RESIDUES) for jb in range(nblk_s)]
        blocks += [local_block(hd, c) for c in range(nblk)]
        s_next = blocks[0][0]()
        for b, (_, rest) in enumerate(blocks):
            s_cur = s_next
            if b + 1 < len(blocks):
                s_next = blocks[b + 1][0]()
            rest(s_cur)


def _attention_call(q, k, v):
    B, H, S, Dh = q.shape
    n = S // RESIDUES
    P = HEADS_PER_STEP
    blk = pl.BlockSpec((1, P, S, Dh), lambda b, h: (b, h, 0, 0))
    return pl.pallas_call(
        _attention_kernel,
        grid=(B, H // P),
        in_specs=[blk, blk, blk,
                  pl.BlockSpec((ATT_BLK, 2 * ATT_BLK), lambda b, h: (0, 0)),
                  pl.BlockSpec((STRIDED_BLK, n), lambda b, h: (0, 0))],
        out_specs=blk,
        out_shape=jax.ShapeDtypeStruct((B, H, S, Dh), BF16),
        scratch_shapes=[pltpu.VMEM((P, S, Dh), F32),
                        pltpu.VMEM((P, RESIDUES, n, Dh), BF16),
                        pltpu.VMEM((P, RESIDUES, n, Dh), BF16),
                        pltpu.VMEM((P, RESIDUES, n, Dh + LANES), BF16),
                        pltpu.VMEM((P, S, Dh + LANES), BF16),
                        pltpu.VMEM((P, S, Dh), F32),
                        pltpu.VMEM((P, S, LANES), F32)],
        compiler_params=pltpu.CompilerParams(dimension_semantics=("parallel", "parallel"),
                                             vmem_limit_bytes=VMEM_LIMIT),
        name="attention",
    )(q, k, v, _local_bias(), _strided_bias(n))


def _route(logits):
    tm = logits.shape[0]
    lane = lax.broadcasted_iota(jnp.int32, logits.shape, 1)
    lanef = lane.astype(F32)

    def masked_top(mask):
        top = jnp.max(jnp.where(mask, logits, -jnp.inf), axis=-1, keepdims=True)
        idx = jnp.min(jnp.where(mask & (logits == top), lanef, float(LANES)), axis=-1, keepdims=True)
        return top, idx

    gmask = lane < N_GROUPS
    gmax, gidx = masked_top(gmask)
    p_group = 1.0 / jnp.sum(jnp.where(gmask, jnp.exp(logits - gmax), 0.0), axis=-1, keepdims=True)
    e_lo = EXPERT_LANE0 + gidx * EXPERTS_PER_GROUP
    emask = (lanef >= e_lo) & (lanef < e_lo + EXPERTS_PER_GROUP)
    v1, i1 = masked_top(emask)
    v2, i2 = masked_top(emask & (lanef != i1))
    e21 = jnp.exp(v2 - v1)
    p1 = p_group / (1.0 + e21)
    p2 = p_group * e21 / (1.0 + e21)
    first_is_lo = i1 < i2
    lo = jnp.minimum(i1, i2) - e_lo
    hi = jnp.maximum(i1, i2) - e_lo
    pair = lo * (7.0 - lo) * 0.5 + (hi - lo - 1.0)
    cls = gidx * float(len(PAIRS)) + pair
    p_lo = jnp.where(first_is_lo, p1, p2)
    p_hi = jnp.where(first_is_lo, p2, p1)

    onehot = lanef == cls
    onehot_b = jnp.where(onehot, 1.0, 0.0).astype(BF16)
    r = lax.broadcasted_iota(jnp.int32, (tm, tm), 0)
    c = lax.broadcasted_iota(jnp.int32, (tm, tm), 1)
    before = jnp.where(c < r, 1.0, 0.0).astype(BF16)
    rank = jnp.dot(before, onehot_b, preferred_element_type=F32)
    counts = jnp.sum(jnp.where(onehot, 1.0, 0.0), axis=0, keepdims=True)
    units = jnp.ceil(counts * (1.0 / SUBLANES))
    ur = lax.broadcasted_iota(jnp.int32, (LANES, LANES), 0)
    uc = lax.broadcasted_iota(jnp.int32, (LANES, LANES), 1)
    upper = jnp.where(ur < uc, 1.0, 0.0).astype(BF16)
    start = jnp.dot(jnp.broadcast_to(units, (SUBLANES, LANES)).astype(BF16), upper,
                    preferred_element_type=F32)[0:1] * float(SUBLANES)
    dest = jnp.sum(jnp.where(onehot, start + rank, 0.0), axis=-1, keepdims=True)

    def hi_part(p):
        return p.astype(BF16).astype(F32)

    info = jnp.where(lane == INFO_DEST, dest, 0.0)
    info = jnp.where(lane == INFO_PLO_H, hi_part(p_lo), info)
    info = jnp.where(lane == INFO_PHI_H, hi_part(p_hi), info)
    info = jnp.where(lane == INFO_PLO_L, p_lo - hi_part(p_lo), info)
    info = jnp.where(lane == INFO_PHI_L, p_hi - hi_part(p_hi), info)
    return info, counts


def _merge_kernel(x_ref, u_ref, v_ref, o_ref, ga_ref, gb_ref, gate1_ref, shift2_ref, scale2_ref,
                  n2g_ref, sw_ref, sb_ref, wa_ref, wb_ref, wo_ref, wr_ref, br_ref,
                  x1_ref, h2_ref, info_ref, counts_ref, s_scr, logit_scr):
    tm = x_ref.shape[0]
    step = pl.program_id(0)

    @pl.when(step == 0)
    def _():
        logit_scr[...] = jnp.zeros_like(logit_scr)

    prev_logits = logit_scr[1 - step % 2]
    row = lax.broadcasted_iota(jnp.int32, (CHUNK, CHUNK), 0)
    colm = lax.broadcasted_iota(jnp.int32, (CHUNK, CHUNK), 1)
    causal = colm <= row
    for g in range(SGU_GROUPS):
        w = jnp.where(causal, sw_ref[g], 0.0).astype(BF16)
        gs = slice(g * CHUNK, (g + 1) * CHUNK)
        chunks = [slice(c * CHUNK, (c + 1) * CHUNK) for c in range(tm // CHUNK)]
        mixed = jnp.dot(w, jnp.concatenate([v_ref[cs, gs] for cs in chunks], axis=1),
                        preferred_element_type=F32)
        for cs in chunks:
            s_scr[cs, gs] = (u_ref[cs, gs].astype(F32) * (mixed[:, cs] + sb_ref[:, gs])).astype(BF16)

    y_a = jnp.dot(s_scr[...], wa_ref[...], preferred_element_type=F32)
    info, counts = _route(prev_logits)
    info_ref[...] = info
    counts_ref[0] = counts
    o = jnp.concatenate([o_ref[0, hd] for hd in range(ATT_HEADS)], axis=-1)
    y_b = jnp.dot(o, wb_ref[...], preferred_element_type=F32)
    merged = ga_ref[...].astype(F32) * y_a + gb_ref[...].astype(F32) * y_b
    y = jnp.dot(merged.astype(BF16), wo_ref[...], preferred_element_type=F32)
    x1 = x_ref[...] + gate1_ref[0] * y
    x1_ref[...] = x1

    h2 = _rms(x1, n2g_ref[...]) * (1.0 + scale2_ref[0]) + shift2_ref[0]
    h2_ref[...] = h2.astype(BF16)
    h_hi = h2.astype(BF16)
    h_lo = (h2 - h_hi.astype(F32)).astype(BF16)
    r_hi = jnp.dot(h_hi, wr_ref[...], preferred_element_type=F32)
    r_lo = jnp.dot(h_lo, wr_ref[:, :LANES], preferred_element_type=F32)
    logit_scr[step % 2] = r_hi[:, :LANES] + r_hi[:, LANES:] + r_lo + br_ref[...]


def _merge_call(x2, u, v, o, ga, gb, gate1, shift2, scale2, n2g, sgu_w, sgu_bias, wa, wb, wo,
                w_router, b_router, B, S):
    T, D = x2.shape
    tm = TOKEN_TILE
    tpb = S // tm
    n_tiles = T // tm
    tile = lambda i: jnp.minimum(i, n_tiles - 1)
    routed = lambda i: jnp.maximum(i - 1, 0)
    tok = pl.BlockSpec((tm, D), lambda i: (tile(i), 0))
    per_b = pl.BlockSpec((1, 1, D), lambda i: (tile(i) // tpb, 0, 0))
    head = pl.BlockSpec((1, ATT_HEADS, tm, ATT_HEAD_DIM),
                        lambda i: (tile(i) // tpb, 0, tile(i) % tpb, 0))
    full = lambda *shape: pl.BlockSpec(shape, lambda i: (0,) * len(shape))
    return pl.pallas_call(
        _merge_kernel,
        grid=(n_tiles + 1,),
        in_specs=[tok, tok, tok, head, tok, tok, per_b, per_b, per_b, full(1, D),
                  full(SGU_GROUPS, CHUNK, CHUNK), full(CHUNK, D),
                  full(D, D), full(D, D), full(D, D), full(D, 2 * LANES), full(1, LANES)],
        out_specs=[tok, tok, pl.BlockSpec((tm, LANES), lambda i: (routed(i), 0)),
                   pl.BlockSpec((1, 1, LANES), lambda i: (routed(i), 0, 0))],
        out_shape=[jax.ShapeDtypeStruct((T, D), F32), jax.ShapeDtypeStruct((T, D), BF16),
                   jax.ShapeDtypeStruct((T, LANES), F32),
                   jax.ShapeDtypeStruct((n_tiles, 1, LANES), F32)],
        scratch_shapes=[pltpu.VMEM((tm, D), BF16), pltpu.VMEM((2, tm, LANES), F32)],
        compiler_params=pltpu.CompilerParams(dimension_semantics=("arbitrary",),
                                             vmem_limit_bytes=VMEM_LIMIT),
        name="merge",
    )(x2, u, v, o, ga, gb, gate1, shift2, scale2, n2g, sgu_w, sgu_bias, wa, wb, wo,
      w_router, b_router)


def _dispatch_plan(counts, n_ffn_tiles):
    units = (counts + SUBLANES - 1) // SUBLANES
    rows = units * SUBLANES
    local_start = jnp.cumsum(rows, axis=1) - rows
    class_rows = jnp.sum(rows, axis=0)
    class_tiles = (class_rows + FFN_TILE - 1) // FFN_TILE
    cum_tiles = jnp.cumsum(class_tiles)
    seg_start = (cum_tiles - class_tiles) * FFN_TILE
    run_start = seg_start[None, :] + jnp.cumsum(rows, axis=0) - rows
    n_used = cum_tiles[-1]
    tile_idx = jnp.maximum(jnp.minimum(jnp.arange(n_ffn_tiles, dtype=jnp.int32), n_used - 1), 0)
    tile_cls = jnp.sum((tile_idx[:, None] >= cum_tiles[None, :]).astype(jnp.int32), axis=1)
    pair = tile_cls % len(PAIRS)
    group = tile_cls // len(PAIRS)
    pairs = jnp.asarray(PAIRS, jnp.int32)
    i32 = lambda a: a.astype(jnp.int32).reshape(-1)

    return dict(
        run_start=i32(run_start), local_start=i32(local_start), units=i32(units),
        tile_units=i32(jnp.sum(units, axis=1)),
        tail_start=i32(seg_start + class_rows),
        tail_units=i32((class_tiles * FFN_TILE - class_rows) // SUBLANES),
        tile_idx=i32(tile_idx), n_used=i32(n_used),
        tile_ea=i32(group * EXPERTS_PER_GROUP + pairs[pair, 0]),
        tile_eb=i32(group * EXPERTS_PER_GROUP + pairs[pair, 1]))


def _rows(units):
    return pl.multiple_of(units * SUBLANES, SUBLANES)


def _start_runs(tile, run_start, local_start, units, make_copy):
    def per_class(c, carry):
        k = tile * N_CLASSES + c

        @pl.when(units[k] > 0)
        def _():
            make_copy(pl.multiple_of(local_start[k], SUBLANES),
                      pl.multiple_of(run_start[k], SUBLANES), _rows(units[k])).start()

        return carry

    lax.fori_loop(0, N_CLASSES, per_class, 0)


def _wait_runs(tile, tile_units, make_copy):
    make_copy(0, 0, _rows(tile_units[tile])).wait()


def _dispatch_kernel(run_start, local_start, units, tile_units, tail_start, tail_units, n_used,
                     h2_ref, info_ref, hs_ref, sorted_scr, zero_scr, sems, tail_sem, unused_sem):
    i = pl.program_id(0)
    nt = pl.num_programs(0)
    slot = i % 2
    n_ffn_tiles = hs_ref.shape[0] // FFN_TILE

    def run_copy(s):
        def make(local_row, global_row, rows):
            return pltpu.make_async_copy(sorted_scr.at[s, pl.ds(local_row, rows), :],
                                         hs_ref.at[pl.ds(global_row, rows), :], sems.at[s])
        return make

    def tail_copy(c):
        rows = _rows(tail_units[c])
        return pltpu.make_async_copy(
            zero_scr.at[pl.ds(0, rows), :],
            hs_ref.at[pl.ds(pl.multiple_of(tail_start[c], SUBLANES), rows), :], tail_sem)

    def unused_copy(t):
        return pltpu.make_async_copy(
            zero_scr, hs_ref.at[pl.ds(pl.multiple_of(t * FFN_TILE, FFN_TILE), FFN_TILE), :],
            unused_sem)

    def for_slot(fn):
        for s in range(2):
            pl.when(slot == s)(lambda s=s: fn(s))

    def for_tails(fn):
        def per_class(c, carry):
            pl.when(tail_units[c] > 0)(lambda: fn(c))
            return carry

        lax.fori_loop(0, N_CLASSES, per_class, 0)

    @pl.when(i >= 2)
    def _():
        for_slot(lambda s: _wait_runs(i - 2, tile_units, run_copy(s)))

    @pl.when(i == 0)
    def _():
        zero_scr[...] = jnp.zeros_like(zero_scr)
        for_tails(lambda c: tail_copy(c).start())

        def per_unused(t, carry):
            unused_copy(t).start()
            return carry

        lax.fori_loop(n_used[0], n_ffn_tiles, per_unused, 0)

    info = info_ref[...]
    dest_row = info.T[INFO_DEST:INFO_DEST + 1, :]
    rows = lax.broadcasted_iota(jnp.int32, (SORT_ROWS, TOKEN_TILE), 0).astype(F32)
    perm = jnp.where(rows == dest_row, 1.0, 0.0).astype(BF16)
    lane = lax.broadcasted_iota(jnp.int32, info.shape, 1)
    weights = jnp.where(lane == INFO_DEST, 0.0, info).astype(BF16)

    def sort_and_send(s):
        sorted_scr[s, :, :D_MODEL] = jnp.dot(perm, h2_ref[...], preferred_element_type=F32)
        sorted_scr[s, :, D_MODEL:] = jnp.dot(perm, weights, preferred_element_type=F32)
        _start_runs(i, run_start, local_start, units, run_copy(s))

    for_slot(sort_and_send)

    @pl.when(i == nt - 1)
    def _():
        for_slot(lambda s: _wait_runs(i, tile_units, run_copy(s)))

        @pl.when(nt >= 2)
        def _():
            for_slot(lambda s: _wait_runs(i - 1, tile_units, run_copy(1 - s)))

        for_tails(lambda c: tail_copy(c).wait())

        def per_unused(t, carry):
            unused_copy(0).wait()
            return carry

        lax.fori_loop(n_used[0], n_ffn_tiles, per_unused, 0)


def _dispatch_call(plan, h2, info, n_ffn_tiles):
    T, D = h2.shape
    tm = TOKEN_TILE
    grid_spec = pltpu.PrefetchScalarGridSpec(
        num_scalar_prefetch=7,
        grid=(T // tm,),
        in_specs=[pl.BlockSpec((tm, D), lambda i, *_: (i, 0)),
                  pl.BlockSpec((tm, LANES), lambda i, *_: (i, 0))],
        out_specs=pl.BlockSpec(memory_space=pl.ANY),
        scratch_shapes=[pltpu.VMEM((2, SORT_ROWS, ROW_WIDTH), F32),
                        pltpu.VMEM((FFN_TILE, ROW_WIDTH), F32),
                        pltpu.SemaphoreType.DMA((2,)),
                        pltpu.SemaphoreType.DMA(()),
                        pltpu.SemaphoreType.DMA(())])
    return pl.pallas_call(
        _dispatch_kernel,
        grid_spec=grid_spec,
        out_shape=jax.ShapeDtypeStruct((n_ffn_tiles * FFN_TILE, ROW_WIDTH), F32),
        compiler_params=pltpu.CompilerParams(dimension_semantics=("arbitrary",),
                                             vmem_limit_bytes=VMEM_LIMIT),
        name="dispatch",
    )(plan["run_start"], plan["local_start"], plan["units"], plan["tile_units"],
      plan["tail_start"], plan["tail_units"], plan["n_used"], h2, info)


def _ffn_kernel(tile_idx, tile_ea, tile_eb, n_used, hs_ref, wga_ref, wua_ref, wda_ref,
                wgb_ref, wub_ref, wdb_ref, ys_ref):
    @pl.when(pl.program_id(0) >= n_used[0])
    def _():
        ys_ref[...] = jnp.zeros_like(ys_ref)

    @pl.when(pl.program_id(0) < n_used[0])
    def _():
        h = hs_ref[:, :D_MODEL].astype(BF16)
        pv = hs_ref[:, D_MODEL:]
        p_lo = pv[:, INFO_PLO_H:INFO_PLO_H + 1] + pv[:, INFO_PLO_L:INFO_PLO_L + 1]
        p_hi = pv[:, INFO_PHI_H:INFO_PHI_H + 1] + pv[:, INFO_PHI_L:INFO_PHI_L + 1]

        def hidden(wg_ref, wu_ref, p):
            a = jnp.dot(h, wg_ref[0], preferred_element_type=F32)
            b = jnp.dot(h, wu_ref[0], preferred_element_type=F32)
            return (a * jax.nn.sigmoid(a) * b * p).astype(BF16)

        he_a = hidden(wga_ref, wua_ref, p_lo)
        he_b = hidden(wgb_ref, wub_ref, p_hi)
        ys_ref[...] = (jnp.dot(he_a, wda_ref[0], preferred_element_type=F32)
                       + jnp.dot(he_b, wdb_ref[0], preferred_element_type=F32))


def _ffn_call(plan, hs, wg, wu, wd, n_ffn_tiles):
    D = D_MODEL
    rows = pl.BlockSpec((FFN_TILE, ROW_WIDTH), lambda t, idx, ea, eb, n: (idx[t], 0))
    w_in_a = pl.BlockSpec((1, D, D_EXPERT), lambda t, idx, ea, eb, n: (ea[t], 0, 0))
    w_out_a = pl.BlockSpec((1, D_EXPERT, D), lambda t, idx, ea, eb, n: (ea[t], 0, 0))
    w_in_b = pl.BlockSpec((1, D, D_EXPERT), lambda t, idx, ea, eb, n: (eb[t], 0, 0))
    w_out_b = pl.BlockSpec((1, D_EXPERT, D), lambda t, idx, ea, eb, n: (eb[t], 0, 0))
    grid_spec = pltpu.PrefetchScalarGridSpec(
        num_scalar_prefetch=4,
        grid=(n_ffn_tiles,),
        in_specs=[rows, w_in_a, w_in_a, w_out_a, w_in_b, w_in_b, w_out_b],
        out_specs=pl.BlockSpec((FFN_TILE, D), lambda t, idx, ea, eb, n: (t, 0)))
    return pl.pallas_call(
        _ffn_kernel,
        grid_spec=grid_spec,
        out_shape=jax.ShapeDtypeStruct((n_ffn_tiles * FFN_TILE, D), F32),
        compiler_params=pltpu.CompilerParams(dimension_semantics=("arbitrary",),
                                             vmem_limit_bytes=VMEM_LIMIT),
        name="ffn",
    )(plan["tile_idx"], plan["tile_ea"], plan["tile_eb"], plan["n_used"],
      hs, wg, wu, wd, wg, wu, wd)


def _combine_kernel(run_start, local_start, units, tile_units,
                    x1_ref, info_ref, gate2_ref, ys_ref, o_ref, ybuf, sems):
    i = pl.program_id(0)
    nt = pl.num_programs(0)
    slot = i % 2

    def run_copy(s):
        def make(local_row, global_row, rows):
            return pltpu.make_async_copy(ys_ref.at[pl.ds(global_row, rows), :],
                                         ybuf.at[s, pl.ds(local_row, rows), :], sems.at[s])
        return make

    def for_slot(fn):
        for s in range(2):
            pl.when(slot == s)(lambda s=s: fn(s))

    def fetch(tile, s):
        _start_runs(tile, run_start, local_start, units, run_copy(s))

    @pl.when(i == 0)
    def _():
        ybuf[...] = jnp.zeros_like(ybuf)
        fetch(0, 0)

    @pl.when(i + 1 < nt)
    def _():
        for_slot(lambda s: fetch(i + 1, 1 - s))

    dest_col = info_ref[:, INFO_DEST:INFO_DEST + 1]
    cols = lax.broadcasted_iota(jnp.int32, (TOKEN_TILE, SORT_ROWS), 1).astype(F32)
    unperm = jnp.where(cols == dest_col, 1.0, 0.0).astype(BF16)

    def finish(s):
        _wait_runs(i, tile_units, run_copy(s))
        y = jnp.dot(unperm, ybuf[s].astype(BF16), preferred_element_type=F32)
        o_ref[...] = x1_ref[...] + gate2_ref[0] * y

    for_slot(finish)


def _combine_call(plan, x1, info, gate2, ys, B, S):
    T, D = x1.shape
    tm = TOKEN_TILE
    tpb = S // tm
    grid_spec = pltpu.PrefetchScalarGridSpec(
        num_scalar_prefetch=4,
        grid=(T // tm,),
        in_specs=[pl.BlockSpec((tm, D), lambda i, *_: (i, 0)),
                  pl.BlockSpec((tm, LANES), lambda i, *_: (i, 0)),
                  pl.BlockSpec((1, 1, D), lambda i, *_: (i // tpb, 0, 0)),
                  pl.BlockSpec(memory_space=pl.ANY)],
        out_specs=pl.BlockSpec((tm, D), lambda i, *_: (i, 0)),
        scratch_shapes=[pltpu.VMEM((2, SORT_ROWS, D), F32),
                        pltpu.SemaphoreType.DMA((2,))])
    return pl.pallas_call(
        _combine_kernel,
        grid_spec=grid_spec,
        out_shape=jax.ShapeDtypeStruct((T, D), F32),
        compiler_params=pltpu.CompilerParams(dimension_semantics=("arbitrary",),
                                             vmem_limit_bytes=VMEM_LIMIT),
        name="combine",
    )(plan["run_start"], plan["local_start"], plan["units"], plan["tile_units"],
      x1, info, gate2, ys)


def kernel(x, c, w_ada, b_ada, norm1_g, w_in, sgu_norm_g, sgu_w, sgu_b, q_norm_g, k_norm_g,
           w_proj_a, w_proj_b, w_out, norm2_g, w_router_group, b_router_group,
           w_router_expert, b_router_expert, w_gate, w_up, w_down):
    B, S, D = x.shape
    T = B * S
    depth = w_ada.shape[0]
    n_token_tiles = T // TOKEN_TILE
    max_rows = T + n_token_tiles * N_CLASSES * (SUBLANES - 1) + N_CLASSES * (FFN_TILE - SUBLANES)
    n_ffn_tiles = -(-max_rows // FFN_TILE)
    x2 = x.reshape(T, D)
    for l in range(depth):
        mod = _ada_call(c, w_ada[l], b_ada[l])
        shift1, scale1, gate1, shift2, scale2, gate2 = [
            m.reshape(B, 1, D) for m in jnp.split(mod, 6, axis=-1)]

        u, v, q, k, va, ga, gb = _inproj_call(
            x2, shift1, scale1, norm1_g[l].reshape(1, D), w_in[l].astype(BF16),
            sgu_norm_g[l].reshape(1, D), q_norm_g[l].reshape(1, ATT_HEAD_DIM),
            k_norm_g[l].reshape(1, ATT_HEAD_DIM), B, S, tm=512)

        o = _attention_call(q, k, va)

        w_router = jnp.concatenate(
            [w_router_group[l],
             jnp.transpose(w_router_expert[l], (1, 0, 2)).reshape(D, N_EXPERTS)], axis=1)
        w_router = jnp.pad(w_router, ((0, 0), (0, LANES - w_router.shape[1])))
        w_router_hi = w_router.astype(BF16)
        w_router_lo = (w_router - w_router_hi.astype(F32)).astype(BF16)
        w_router = jnp.concatenate([w_router_hi, w_router_lo], axis=1)
        b_router = jnp.concatenate([b_router_group[l], b_router_expert[l].reshape(N_EXPERTS)])
        b_router = jnp.pad(b_router, (0, LANES - b_router.shape[0])).reshape(1, LANES)
        sgu_bias = jnp.repeat(sgu_b[l].T, D // SGU_GROUPS, axis=1)

        x1, h2, info, counts = _merge_call(
            x2, u, v, o, ga, gb, gate1, shift2, scale2, norm2_g[l].reshape(1, D), sgu_w[l],
            sgu_bias, w_proj_a[l].astype(BF16), w_proj_b[l].astype(BF16), w_out[l].astype(BF16),
            w_router, b_router, B, S)

        plan = _dispatch_plan(counts[:, 0, :N_CLASSES].astype(jnp.int32), n_ffn_tiles)
        hs = _dispatch_call(plan, h2, info, n_ffn_tiles)
        ys = _ffn_call(plan, hs, w_gate[l], w_up[l], w_down[l], n_ffn_tiles)
        x2 = _combine_call(plan, x1, info, gate2, ys, B, S)
    return x2.reshape(B, S, D)
```

```python
import math

import jax
import jax.numpy as jnp
import numpy as np
from jax import lax
from jax.experimental import pallas as pl
from jax.experimental.pallas import tpu as pltpu

D_MODEL = 1024
CHUNK = 128
SGU_GROUPS = 8
ATT_HEADS = 8
ATT_HEAD_DIM = 128
DILATED_PATTERNS = ((128, 1), (512, 4), (2048, 16))
N_GROUPS = 4
EXPERTS_PER_GROUP = 4
N_EXPERTS = N_GROUPS * EXPERTS_PER_GROUP
D_EXPERT = 512
EPS = 1e-6
NEG_INF = -1e30

N_IN_SPLITS = 7
LANES = 128
SUBLANES = 8
EXPERT_LANE0 = N_GROUPS

PAIRS = ((0, 1), (0, 2), (0, 3), (1, 2), (1, 3), (2, 3))
N_CLASSES = N_GROUPS * len(PAIRS)
TOKEN_TILE = 512
SORT_ROWS = TOKEN_TILE + 3 * 64
ROW_WIDTH = D_MODEL + LANES
FFN_TILE = 512
INFO_DEST, INFO_PLO_H, INFO_PHI_H, INFO_PLO_L, INFO_PHI_L = 0, 1, 2, 3, 4

VMEM_LIMIT = 56 * 1024 * 1024
BF16 = jnp.bfloat16
F32 = jnp.float32

assert SORT_ROWS >= TOKEN_TILE + N_CLASSES * (SUBLANES - 1)


def _rms(x, g):
    return x * lax.rsqrt(jnp.mean(x * x, axis=-1, keepdims=True) + EPS) * g


def _gelu_tanh(x):
    k = -2.0 * math.sqrt(2.0 / math.pi) * math.log2(math.e)
    return x / (1.0 + jnp.exp2(x * (k + (k * 0.044715) * (x * x))))


def _ada_kernel(c_ref, w_ref, b_ref, o_ref):
    c = c_ref[...]
    cond = c * jax.nn.sigmoid(c)
    o_ref[...] = jnp.dot(cond, w_ref[...], preferred_element_type=F32,
                         precision=lax.Precision.HIGHEST) + b_ref[...]


def _ada_call(c, w_ada, b_ada):
    B, D = c.shape
    N = w_ada.shape[1]
    tn = 1024
    return pl.pallas_call(
        _ada_kernel,
        grid=(N // tn,),
        in_specs=[pl.BlockSpec((B, D), lambda j: (0, 0)),
                  pl.BlockSpec((D, tn), lambda j: (0, j)),
                  pl.BlockSpec((1, tn), lambda j: (0, j))],
        out_specs=pl.BlockSpec((B, tn), lambda j: (0, j)),
        out_shape=jax.ShapeDtypeStruct((B, N), F32),
        compiler_params=pltpu.CompilerParams(dimension_semantics=("arbitrary",),
                                             vmem_limit_bytes=VMEM_LIMIT),
        name="adaln",
    )(c, w_ada, b_ada.reshape(1, N))


def _inproj_kernel(x_ref, shift_ref, scale_ref, n1g_ref, w_ref, sgug_ref, qg_ref, kg_ref,
                   u_ref, v_ref, q_ref, k_ref, va_ref, ga_ref, gb_ref):
    x = x_ref[...]
    h = _rms(x, n1g_ref[...]) * (1.0 + scale_ref[0]) + shift_ref[0]
    h = h.astype(BF16)

    def proj(i):
        return jnp.dot(h, w_ref[:, i * D_MODEL:(i + 1) * D_MODEL], preferred_element_type=F32)

    u_ref[...] = _gelu_tanh(proj(0)).astype(BF16)
    v_ref[...] = _rms(_gelu_tanh(proj(1)), sgug_ref[...]).astype(BF16)

    q = proj(2)
    qscale = ATT_HEAD_DIM ** -0.5
    for hd in range(ATT_HEADS):
        sl = slice(hd * ATT_HEAD_DIM, (hd + 1) * ATT_HEAD_DIM)
        q_ref[0, hd] = (_rms(q[:, sl], qg_ref[...]) * qscale).astype(BF16)
    k = proj(3)
    for hd in range(ATT_HEADS):
        sl = slice(hd * ATT_HEAD_DIM, (hd + 1) * ATT_HEAD_DIM)
        k_ref[0, hd] = _rms(k[:, sl], kg_ref[...]).astype(BF16)
    ga_ref[...] = jax.nn.sigmoid(proj(5)).astype(BF16)
    gb_ref[...] = jax.nn.sigmoid(proj(6)).astype(BF16)
    va = proj(4)
    for hd in range(ATT_HEADS):
        sl = slice(hd * ATT_HEAD_DIM, (hd + 1) * ATT_HEAD_DIM)
        va_ref[0, hd] = va[:, sl].astype(BF16)


def _inproj_call(x2, shift1, scale1, n1g, w_in, sgug, qg, kg, B, S, tm):
    T, D = x2.shape
    tpb = S // tm
    tok = pl.BlockSpec((tm, D), lambda i: (i, 0))
    per_b = pl.BlockSpec((1, 1, D), lambda i: (i // tpb, 0, 0))
    row = lambda n: pl.BlockSpec((1, n), lambda i: (0, 0))
    head = pl.BlockSpec((1, ATT_HEADS, tm, ATT_HEAD_DIM), lambda i: (i // tpb, 0, i % tpb, 0))
    tok_sds = jax.ShapeDtypeStruct((T, D), BF16)
    head_sds = jax.ShapeDtypeStruct((B, ATT_HEADS, S, ATT_HEAD_DIM), BF16)
    return pl.pallas_call(
        _inproj_kernel,
        grid=(T // tm,),
        in_specs=[tok, per_b, per_b, row(D),
                  pl.BlockSpec((D, N_IN_SPLITS * D), lambda i: (0, 0), pipeline_mode=pl.Buffered(1)),
                  row(D), row(ATT_HEAD_DIM), row(ATT_HEAD_DIM)],
        out_specs=[tok, tok, head, head, head, tok, tok],
        out_shape=[tok_sds, tok_sds, head_sds, head_sds, head_sds, tok_sds, tok_sds],
        compiler_params=pltpu.CompilerParams(dimension_semantics=("parallel",),
                                             vmem_limit_bytes=VMEM_LIMIT),
        name="inproj",
    )(x2, shift1, scale1, n1g, w_in, sgug, qg, kg)


ATT_BLK = 128
STRIDED_BLK = 256
HEADS_PER_STEP = 4
RESIDUES = 4
LOCAL_PATTERNS = tuple(p for p in DILATED_PATTERNS if p[1] % RESIDUES != 0)
STRIDED_PATTERNS = tuple(p for p in DILATED_PATTERNS if p[1] % RESIDUES == 0)
assert all(d == 1 and w <= ATT_BLK for w, d in LOCAL_PATTERNS)


def _log_count_bias(count):
    return jnp.asarray(np.where(count > 0, np.log(np.maximum(count, 1)), NEG_INF), F32)


def _local_bias():
    a = np.arange(ATT_BLK)[:, None]
    col = np.arange(2 * ATT_BLK)[None, :]
    delta = ATT_BLK + a - col
    count = np.zeros(delta.shape, np.int64)
    for window, _ in LOCAL_PATTERNS:
        count += (delta >= 0) & (delta <= window)
    return _log_count_bias(count)


def _strided_bias(n):
    nblk = n // STRIDED_BLK
    a = np.arange(STRIDED_BLK)[:, None]
    col = np.arange(n)[None, :]
    delta = (STRIDED_BLK * (nblk - 1 - col // STRIDED_BLK) + a - col % STRIDED_BLK) * RESIDUES
    count = np.zeros(delta.shape, np.int64)
    for window, dilation in STRIDED_PATTERNS:
        count += (delta >= 0) & (delta <= window) & (delta % dilation == 0)
    return _log_count_bias(count)


def _qk(q, k):
    return lax.dot_general(q, k, (((1,), (1,)), ((), ())), preferred_element_type=F32)


def _attention_kernel(q_ref, k_ref, v_ref, lbias_ref, sbias_ref, o_ref,
                      stage, q4, k4, v4, ve, acc_scr, m_scr, l_scr):
    S = q_ref.shape[2]
    n = S // RESIDUES
    nblk_s = n // STRIDED_BLK
    nblk = S // ATT_BLK
    Dh = ATT_HEAD_DIM
    ones = jnp.ones((S, LANES), BF16)

    def relayout(hd):
        ve[hd, :, Dh:] = ones
        ve[hd, :, :Dh] = v_ref[0, hd]
        for r in range(RESIDUES):
            v4[hd, r, :, Dh:] = ones[:n]
        for src, dst in ((q_ref, q4), (k_ref, k4), (v_ref, v4)):
            stage[hd] = src[0, hd].astype(F32)
            for r in range(RESIDUES):
                dst[hd, r, :, :Dh] = stage[hd, pl.ds(r, n, stride=RESIDUES), :].astype(BF16)

    def strided_block(hd, r, jb):
        nk = (jb + 1) * STRIDED_BLK

        def scores():
            s = _qk(q4[hd, r, jb * STRIDED_BLK:(jb + 1) * STRIDED_BLK, :], k4[hd, r, :nk, :])
            return s + sbias_ref[:, (nblk_s - 1 - jb) * STRIDED_BLK:]

        def rest(s):
            m = jnp.max(s, axis=-1, keepdims=True)
            p = jnp.exp(s - m)
            acc = jnp.dot(p.astype(BF16), v4[hd, r, :nk, :], preferred_element_type=F32)
            rows = pl.ds(RESIDUES * jb * STRIDED_BLK + r, STRIDED_BLK, stride=RESIDUES)
            acc_scr[hd, rows, :] = acc[:, :Dh]
            m_scr[hd, rows, :] = jnp.broadcast_to(m, (STRIDED_BLK, LANES))
            l_scr[hd, rows, :] = acc[:, Dh:]

        return scores, rest

    def local_block(hd, c):
        k0 = max(c - 1, 0) * ATT_BLK
        nk = (c + 1) * ATT_BLK - k0
        blk = slice(c * ATT_BLK, (c + 1) * ATT_BLK)

        def scores():
            s = _qk(q_ref[0, hd, blk, :], k_ref[0, hd, k0:k0 + nk, :])
            return s + lbias_ref[:, 2 * ATT_BLK - nk:]

        def rest(s):
            m_s = m_scr[hd, blk, :]
            m = jnp.maximum(jnp.max(s, axis=-1, keepdims=True), m_s)
            w = jnp.exp(m_s - m)
            p = jnp.exp(s - jnp.concatenate([m] * (nk // LANES), axis=-1))
            acc = jnp.dot(p.astype(BF16), ve[hd, k0:k0 + nk, :], preferred_element_type=F32)
            l = acc[:, Dh:] + l_scr[hd, blk, :] * w
            o_ref[0, hd, blk, :] = ((acc[:, :Dh] + acc_scr[hd, blk, :] * w) / l).astype(BF16)

        return scores, rest

    for hd in range(HEADS_PER_STEP):
        relayout(hd)
        blocks = [strided_block(hd, r, jb) for r in range(RESIDUES) for jb in range(nblk_s)]
        blocks += [local_block(hd, c) for c in range(nblk)]
        s_next = blocks[0][0]()
        for b, (_, rest) in enumerate(blocks):
            s_cur = s_next
            if b + 1 < len(blocks):
                s_next = blocks[b + 1][0]()
            rest(s_cur)


def _attention_call(q, k, v):
    B, H, S, Dh = q.shape
    n = S // RESIDUES
    P = HEADS_PER_STEP
    blk = pl.BlockSpec((1, P, S, Dh), lambda b, h: (b, h, 0, 0))
    return pl.pallas_call(
        _attention_kernel,
        grid=(B, H // P),
        in_specs=[blk, blk, blk,
                  pl.BlockSpec((ATT_BLK, 2 * ATT_BLK), lambda b, h: (0, 0)),
                  pl.BlockSpec((STRIDED_BLK, n), lambda b, h: (0, 0))],
        out_specs=blk,
        out_shape=jax.ShapeDtypeStruct((B, H, S, Dh), BF16),
        scratch_shapes=[pltpu.VMEM((P, S, Dh), F32),
                        pltpu.VMEM((P, RESIDUES, n, Dh), BF16),
                        pltpu.VMEM((P, RESIDUES, n, Dh), BF16),
                        pltpu.VMEM((P, RESIDUES, n, Dh + LANES), BF16),
                        pltpu.VMEM((P, S, Dh + LANES), BF16),
                        pltpu.VMEM((P, S, Dh), F32),
                        pltpu.VMEM((P, S, LANES), F32),
                        pltpu.VMEM((P, S, LANES), F32)],
        compiler_params=pltpu.CompilerParams(dimension_semantics=("parallel", "parallel"),
                                             vmem_limit_bytes=VMEM_LIMIT),
        name="attention",
    )(q, k, v, _local_bias(), _strided_bias(n))


def _route(logits):
    tm = logits.shape[0]
    lane = lax.broadcasted_iota(jnp.int32, logits.shape, 1)
    lanef = lane.astype(F32)

    def masked_top(mask):
        top = jnp.max(jnp.where(mask, logits, -jnp.inf), axis=-1, keepdims=True)
        idx = jnp.min(jnp.where(mask & (logits == top), lanef, float(LANES)), axis=-1, keepdims=True)
        return top, idx

    gmask = lane < N_GROUPS
    gmax, gidx = masked_top(gmask)
    p_group = 1.0 / jnp.sum(jnp.where(gmask, jnp.exp(logits - gmax), 0.0), axis=-1, keepdims=True)
    e_lo = EXPERT_LANE0 + gidx * EXPERTS_PER_GROUP
    emask = (lanef >= e_lo) & (lanef < e_lo + EXPERTS_PER_GROUP)
    v1, i1 = masked_top(emask)
    v2, i2 = masked_top(emask & (lanef != i1))
    e21 = jnp.exp(v2 - v1)
    p1 = p_group / (1.0 + e21)
    p2 = p_group * e21 / (1.0 + e21)
    first_is_lo = i1 < i2
    lo = jnp.minimum(i1, i2) - e_lo
    hi = jnp.maximum(i1, i2) - e_lo
    pair = lo * (7.0 - lo) * 0.5 + (hi - lo - 1.0)
    cls = gidx * float(len(PAIRS)) + pair
    p_lo = jnp.where(first_is_lo, p1, p2)
    p_hi = jnp.where(first_is_lo, p2, p1)

    onehot = lanef == cls
    onehot_b = jnp.where(onehot, 1.0, 0.0).astype(BF16)
    r = lax.broadcasted_iota(jnp.int32, (tm, tm), 0)
    c = lax.broadcasted_iota(jnp.int32, (tm, tm), 1)
    before = jnp.where(c < r, 1.0, 0.0).astype(BF16)
    rank = jnp.dot(before, onehot_b, preferred_element_type=F32)
    counts = jnp.sum(jnp.where(onehot, 1.0, 0.0), axis=0, keepdims=True)
    units = jnp.ceil(counts * (1.0 / SUBLANES))
    ur = lax.broadcasted_iota(jnp.int32, (LANES, LANES), 0)
    uc = lax.broadcasted_iota(jnp.int32, (LANES, LANES), 1)
    upper = jnp.where(ur < uc, 1.0, 0.0).astype(BF16)
    start = jnp.dot(jnp.broadcast_to(units, (SUBLANES, LANES)).astype(BF16), upper,
                    preferred_element_type=F32)[0:1] * float(SUBLANES)
    dest = jnp.sum(jnp.where(onehot, start + rank, 0.0), axis=-1, keepdims=True)

    def hi_part(p):
        return p.astype(BF16).astype(F32)

    info = jnp.where(lane == INFO_DEST, dest, 0.0)
    info = jnp.where(lane == INFO_PLO_H, hi_part(p_lo), info)
    info = jnp.where(lane == INFO_PHI_H, hi_part(p_hi), info)
    info = jnp.where(lane == INFO_PLO_L, p_lo - hi_part(p_lo), info)
    info = jnp.where(lane == INFO_PHI_L, p_hi - hi_part(p_hi), info)
    return info, counts


def _merge_kernel(x_ref, u_ref, v_ref, o_ref, ga_ref, gb_ref, gate1_ref, shift2_ref, scale2_ref,
                  n2g_ref, sw_ref, sb_ref, wa_ref, wb_ref, wo_ref, wr_ref, br_ref,
                  x1_ref, h2_ref, info_ref, counts_ref, s_scr, logit_scr):
    tm = x_ref.shape[0]
    step = pl.program_id(0)

    @pl.when(step == 0)
    def _():
        logit_scr[...] = jnp.zeros_like(logit_scr)

    prev_logits = logit_scr[1 - step % 2]
    row = lax.broadcasted_iota(jnp.int32, (CHUNK, CHUNK), 0)
    colm = lax.broadcasted_iota(jnp.int32, (CHUNK, CHUNK), 1)
    causal = colm <= row
    for g in range(SGU_GROUPS):
        w = jnp.where(causal, sw_ref[g], 0.0).astype(BF16)
        gs = slice(g * CHUNK, (g + 1) * CHUNK)
        chunks = [slice(c * CHUNK, (c + 1) * CHUNK) for c in range(tm // CHUNK)]
        mixed = jnp.dot(w, jnp.concatenate([v_ref[cs, gs] for cs in chunks], axis=1),
                        preferred_element_type=F32)
        for cs in chunks:
            s_scr[cs, gs] = (u_ref[cs, gs].astype(F32) * (mixed[:, cs] + sb_ref[:, gs])).astype(BF16)

    y_a = jnp.dot(s_scr[...], wa_ref[...], preferred_element_type=F32)
    info, counts = _route(prev_logits)
    info_ref[...] = info
    counts_ref[0] = counts
    o = jnp.concatenate([o_ref[0, hd] for hd in range(ATT_HEADS)], axis=-1)
    y_b = jnp.dot(o, wb_ref[...], preferred_element_type=F32)
    merged = ga_ref[...].astype(F32) * y_a + gb_ref[...].astype(F32) * y_b
    y = jnp.dot(merged.astype(BF16), wo_ref[...], preferred_element_type=F32)
    x1 = x_ref[...] + gate1_ref[0] * y
    x1_ref[...] = x1

    h2 = _rms(x1, n2g_ref[...]) * (1.0 + scale2_ref[0]) + shift2_ref[0]
    h2_ref[...] = h2.astype(BF16)
    h_hi = h2.astype(BF16)
    h_lo = (h2 - h_hi.astype(F32)).astype(BF16)
    r_hi = jnp.dot(h_hi, wr_ref[...], preferred_element_type=F32)
    r_lo = jnp.dot(h_lo, wr_ref[:, :LANES], preferred_element_type=F32)
    logit_scr[step % 2] = r_hi[:, :LANES] + r_hi[:, LANES:] + r_lo + br_ref[...]


def _merge_call(x2, u, v, o, ga, gb, gate1, shift2, scale2, n2g, sgu_w, sgu_bias, wa, wb, wo,
                w_router, b_router, B, S):
    T, D = x2.shape
    tm = TOKEN_TILE
    tpb = S // tm
    n_tiles = T // tm
    tile = lambda i: jnp.minimum(i, n_tiles - 1)
    routed = lambda i: jnp.maximum(i - 1, 0)
    tok = pl.BlockSpec((tm, D), lambda i: (tile(i), 0))
    per_b = pl.BlockSpec((1, 1, D), lambda i: (tile(i) // tpb, 0, 0))
    head = pl.BlockSpec((1, ATT_HEADS, tm, ATT_HEAD_DIM),
                        lambda i: (tile(i) // tpb, 0, tile(i) % tpb, 0))
    full = lambda *shape: pl.BlockSpec(shape, lambda i: (0,) * len(shape))
    return pl.pallas_call(
        _merge_kernel,
        grid=(n_tiles + 1,),
        in_specs=[tok, tok, tok, head, tok, tok, per_b, per_b, per_b, full(1, D),
                  full(SGU_GROUPS, CHUNK, CHUNK), full(CHUNK, D),
                  full(D, D), full(D, D), full(D, D), full(D, 2 * LANES), full(1, LANES)],
        out_specs=[tok, tok, pl.BlockSpec((tm, LANES), lambda i: (routed(i), 0)),
                   pl.BlockSpec((1, 1, LANES), lambda i: (routed(i), 0, 0))],
        out_shape=[jax.ShapeDtypeStruct((T, D), F32), jax.ShapeDtypeStruct((T, D), BF16),
                   jax.ShapeDtypeStruct((T, LANES), F32),
                   jax.ShapeDtypeStruct((n_tiles, 1, LANES), F32)],
        scratch_shapes=[pltpu.VMEM((tm, D), BF16), pltpu.VMEM((2, tm, LANES), F32)],
        compiler_params=pltpu.CompilerParams(dimension_semantics=("arbitrary",),
                                             vmem_limit_bytes=VMEM_LIMIT),
        name="merge",
    )(x2, u, v, o, ga, gb, gate1, shift2, scale2, n2g, sgu_w, sgu_bias, wa, wb, wo,
      w_router, b_router)


def _dispatch_plan(counts, n_ffn_tiles):
    units = (counts + SUBLANES - 1) // SUBLANES
    rows = units * SUBLANES
    local_start = jnp.cumsum(rows, axis=1) - rows
    class_rows = jnp.sum(rows, axis=0)
    class_tiles = (class_rows + FFN_TILE - 1) // FFN_TILE
    cum_tiles = jnp.cumsum(class_tiles)
    seg_start = (cum_tiles - class_tiles) * FFN_TILE
    run_start = seg_start[None, :] + jnp.cumsum(rows, axis=0) - rows
    n_used = cum_tiles[-1]
    tile_idx = jnp.maximum(jnp.minimum(jnp.arange(n_ffn_tiles, dtype=jnp.int32), n_used - 1), 0)
    tile_cls = jnp.sum((tile_idx[:, None] >= cum_tiles[None, :]).astype(jnp.int32), axis=1)
    pair = tile_cls % len(PAIRS)
    group = tile_cls // len(PAIRS)
    pairs = jnp.asarray(PAIRS, jnp.int32)
    i32 = lambda a: a.astype(jnp.int32).reshape(-1)

    return dict(
        run_start=i32(run_start), local_start=i32(local_start), units=i32(units),
        tile_units=i32(jnp.sum(units, axis=1)),
        tail_start=i32(seg_start + class_rows),
        tail_units=i32((class_tiles * FFN_TILE - class_rows) // SUBLANES),
        tile_idx=i32(tile_idx), n_used=i32(n_used),
        tile_ea=i32(group * EXPERTS_PER_GROUP + pairs[pair, 0]),
        tile_eb=i32(group * EXPERTS_PER_GROUP + pairs[pair, 1]))


def _rows(units):
    return pl.multiple_of(units * SUBLANES, SUBLANES)


def _start_runs(tile, run_start, local_start, units, make_copy):
    def per_class(c, carry):
        k = tile * N_CLASSES + c

        @pl.when(units[k] > 0)
        def _():
            make_copy(pl.multiple_of(local_start[k], SUBLANES),
                      pl.multiple_of(run_start[k], SUBLANES), _rows(units[k])).start()

        return carry

    lax.fori_loop(0, N_CLASSES, per_class, 0)


def _wait_runs(tile, tile_units, make_copy):
    make_copy(0, 0, _rows(tile_units[tile])).wait()


def _dispatch_kernel(run_start, local_start, units, tile_units, tail_start, tail_units, n_used,
                     h2_ref, info_ref, hs_ref, sorted_scr, zero_scr, sems, tail_sem, unused_sem):
    i = pl.program_id(0)
    nt = pl.num_programs(0)
    slot = i % 2
    n_ffn_tiles = hs_ref.shape[0] // FFN_TILE

    def run_copy(s):
        def make(local_row, global_row, rows):
            return pltpu.make_async_copy(sorted_scr.at[s, pl.ds(local_row, rows), :],
                                         hs_ref.at[pl.ds(global_row, rows), :], sems.at[s])
        return make

    def tail_copy(c):
        rows = _rows(tail_units[c])
        return pltpu.make_async_copy(
            zero_scr.at[pl.ds(0, rows), :],
            hs_ref.at[pl.ds(pl.multiple_of(tail_start[c], SUBLANES), rows), :], tail_sem)

    def unused_copy(t):
        return pltpu.make_async_copy(
            zero_scr, hs_ref.at[pl.ds(pl.multiple_of(t * FFN_TILE, FFN_TILE), FFN_TILE), :],
            unused_sem)

    def for_slot(fn):
        for s in range(2):
            pl.when(slot == s)(lambda s=s: fn(s))

    def for_tails(fn):
        def per_class(c, carry):
            pl.when(tail_units[c] > 0)(lambda: fn(c))
            return carry

        lax.fori_loop(0, N_CLASSES, per_class, 0)

    @pl.when(i >= 2)
    def _():
        for_slot(lambda s: _wait_runs(i - 2, tile_units, run_copy(s)))

    @pl.when(i == 0)
    def _():
        zero_scr[...] = jnp.zeros_like(zero_scr)
        for_tails(lambda c: tail_copy(c).start())

        def per_unused(t, carry):
            unused_copy(t).start()
            return carry

        lax.fori_loop(n_used[0], n_ffn_tiles, per_unused, 0)

    info = info_ref[...]
    dest_row = info.T[INFO_DEST:INFO_DEST + 1, :]
    rows = lax.broadcasted_iota(jnp.int32, (SORT_ROWS, TOKEN_TILE), 0).astype(F32)
    perm = jnp.where(rows == dest_row, 1.0, 0.0).astype(BF16)
    lane = lax.broadcasted_iota(jnp.int32, info.shape, 1)
    weights = jnp.where(lane == INFO_DEST, 0.0, info).astype(BF16)

    def sort_and_send(s):
        sorted_scr[s, :, :D_MODEL] = jnp.dot(perm, h2_ref[...], preferred_element_type=F32)
        sorted_scr[s, :, D_MODEL:] = jnp.dot(perm, weights, preferred_element_type=F32)
        _start_runs(i, run_start, local_start, units, run_copy(s))

    for_slot(sort_and_send)

    @pl.when(i == nt - 1)
    def _():
        for_slot(lambda s: _wait_runs(i, tile_units, run_copy(s)))

        @pl.when(nt >= 2)
        def _():
            for_slot(lambda s: _wait_runs(i - 1, tile_units, run_copy(1 - s)))

        for_tails(lambda c: tail_copy(c).wait())

        def per_unused(t, carry):
            unused_copy(0).wait()
            return carry

        lax.fori_loop(n_used[0], n_ffn_tiles, per_unused, 0)


def _dispatch_call(plan, h2, info, n_ffn_tiles):
    T, D = h2.shape
    tm = TOKEN_TILE
    grid_spec = pltpu.PrefetchScalarGridSpec(
        num_scalar_prefetch=7,
        grid=(T // tm,),
        in_specs=[pl.BlockSpec((tm, D), lambda i, *_: (i, 0)),
                  pl.BlockSpec((tm, LANES), lambda i, *_: (i, 0))],
        out_specs=pl.BlockSpec(memory_space=pl.ANY),
        scratch_shapes=[pltpu.VMEM((2, SORT_ROWS, ROW_WIDTH), F32),
                        pltpu.VMEM((FFN_TILE, ROW_WIDTH), F32),
                        pltpu.SemaphoreType.DMA((2,)),
                        pltpu.SemaphoreType.DMA(()),
                        pltpu.SemaphoreType.DMA(())])
    return pl.pallas_call(
        _dispatch_kernel,
        grid_spec=grid_spec,
        out_shape=jax.ShapeDtypeStruct((n_ffn_tiles * FFN_TILE, ROW_WIDTH), F32),
        compiler_params=pltpu.CompilerParams(dimension_semantics=("arbitrary",),
                                             vmem_limit_bytes=VMEM_LIMIT),
        name="dispatch",
    )(plan["run_start"], plan["local_start"], plan["units"], plan["tile_units"],
      plan["tail_start"], plan["tail_units"], plan["n_used"], h2, info)


def _ffn_kernel(tile_idx, tile_ea, tile_eb, n_used, hs_ref, wga_ref, wua_ref, wda_ref,
                wgb_ref, wub_ref, wdb_ref, ys_ref):
    @pl.when(pl.program_id(0) >= n_used[0])
    def _():
        ys_ref[...] = jnp.zeros_like(ys_ref)

    @pl.when(pl.program_id(0) < n_used[0])
    def _():
        h = hs_ref[:, :D_MODEL].astype(BF16)
        pv = hs_ref[:, D_MODEL:]
        p_lo = pv[:, INFO_PLO_H:INFO_PLO_H + 1] + pv[:, INFO_PLO_L:INFO_PLO_L + 1]
        p_hi = pv[:, INFO_PHI_H:INFO_PHI_H + 1] + pv[:, INFO_PHI_L:INFO_PHI_L + 1]

        def hidden(wg_ref, wu_ref, p):
            a = jnp.dot(h, wg_ref[0], preferred_element_type=F32)
            b = jnp.dot(h, wu_ref[0], preferred_element_type=F32)
            return (a * jax.nn.sigmoid(a) * b * p).astype(BF16)

        he_a = hidden(wga_ref, wua_ref, p_lo)
        he_b = hidden(wgb_ref, wub_ref, p_hi)
        ys_ref[...] = (jnp.dot(he_a, wda_ref[0], preferred_element_type=F32)
                       + jnp.dot(he_b, wdb_ref[0], preferred_element_type=F32))


def _ffn_call(plan, hs, wg, wu, wd, n_ffn_tiles):
    D = D_MODEL
    rows = pl.BlockSpec((FFN_TILE, ROW_WIDTH), lambda t, idx, ea, eb, n: (idx[t], 0))
    w_in_a = pl.BlockSpec((1, D, D_EXPERT), lambda t, idx, ea, eb, n: (ea[t], 0, 0))
    w_out_a = pl.BlockSpec((1, D_EXPERT, D), lambda t, idx, ea, eb, n: (ea[t], 0, 0))
    w_in_b = pl.BlockSpec((1, D, D_EXPERT), lambda t, idx, ea, eb, n: (eb[t], 0, 0))
    w_out_b = pl.BlockSpec((1, D_EXPERT, D), lambda t, idx, ea, eb, n: (eb[t], 0, 0))
    grid_spec = pltpu.PrefetchScalarGridSpec(
        num_scalar_prefetch=4,
        grid=(n_ffn_tiles,),
        in_specs=[rows, w_in_a, w_in_a, w_out_a, w_in_b, w_in_b, w_out_b],
        out_specs=pl.BlockSpec((FFN_TILE, D), lambda t, idx, ea, eb, n: (t, 0)))
    return pl.pallas_call(
        _ffn_kernel,
        grid_spec=grid_spec,
        out_shape=jax.ShapeDtypeStruct((n_ffn_tiles * FFN_TILE, D), F32),
        compiler_params=pltpu.CompilerParams(dimension_semantics=("arbitrary",),
                                             vmem_limit_bytes=VMEM_LIMIT),
        name="ffn",
    )(plan["tile_idx"], plan["tile_ea"], plan["tile_eb"], plan["n_used"],
      hs, wg, wu, wd, wg, wu, wd)


def _combine_kernel(run_start, local_start, units, tile_units,
                    x1_ref, info_ref, gate2_ref, ys_ref, o_ref, ybuf, sems):
    i = pl.program_id(0)
    nt = pl.num_programs(0)
    slot = i % 2

    def run_copy(s):
        def make(local_row, global_row, rows):
            return pltpu.make_async_copy(ys_ref.at[pl.ds(global_row, rows), :],
                                         ybuf.at[s, pl.ds(local_row, rows), :], sems.at[s])
        return make

    def for_slot(fn):
        for s in range(2):
            pl.when(slot == s)(lambda s=s: fn(s))

    def fetch(tile, s):
        _start_runs(tile, run_start, local_start, units, run_copy(s))

    @pl.when(i == 0)
    def _():
        ybuf[...] = jnp.zeros_like(ybuf)
        fetch(0, 0)

    @pl.when(i + 1 < nt)
    def _():
        for_slot(lambda s: fetch(i + 1, 1 - s))

    dest_col = info_ref[:, INFO_DEST:INFO_DEST + 1]
    cols = lax.broadcasted_iota(jnp.int32, (TOKEN_TILE, SORT_ROWS), 1).astype(F32)
    unperm = jnp.where(cols == dest_col, 1.0, 0.0).astype(BF16)

    def finish(s):
        _wait_runs(i, tile_units, run_copy(s))
        y = jnp.dot(unperm, ybuf[s].astype(BF16), preferred_element_type=F32)
        o_ref[...] = x1_ref[...] + gate2_ref[0] * y

    for_slot(finish)


def _combine_call(plan, x1, info, gate2, ys, B, S):
    T, D = x1.shape
    tm = TOKEN_TILE
    tpb = S // tm
    grid_spec = pltpu.PrefetchScalarGridSpec(
        num_scalar_prefetch=4,
        grid=(T // tm,),
        in_specs=[pl.BlockSpec((tm, D), lambda i, *_: (i, 0)),
                  pl.BlockSpec((tm, LANES), lambda i, *_: (i, 0)),
                  pl.BlockSpec((1, 1, D), lambda i, *_: (i // tpb, 0, 0)),
                  pl.BlockSpec(memory_space=pl.ANY)],
        out_specs=pl.BlockSpec((tm, D), lambda i, *_: (i, 0)),
        scratch_shapes=[pltpu.VMEM((2, SORT_ROWS, D), F32),
                        pltpu.SemaphoreType.DMA((2,))])
    return pl.pallas_call(
        _combine_kernel,
        grid_spec=grid_spec,
        out_shape=jax.ShapeDtypeStruct((T, D), F32),
        compiler_params=pltpu.CompilerParams(dimension_semantics=("arbitrary",),
                                             vmem_limit_bytes=VMEM_LIMIT),
        name="combine",
    )(plan["run_start"], plan["local_start"], plan["units"], plan["tile_units"],
      x1, info, gate2, ys)


def kernel(x, c, w_ada, b_ada, norm1_g, w_in, sgu_norm_g, sgu_w, sgu_b, q_norm_g, k_norm_g,
           w_proj_a, w_proj_b, w_out, norm2_g, w_router_group, b_router_group,
           w_router_expert, b_router_expert, w_gate, w_up, w_down):
    B, S, D = x.shape
    T = B * S
    depth = w_ada.shape[0]
    n_token_tiles = T // TOKEN_TILE
    max_rows = T + n_token_tiles * N_CLASSES * (SUBLANES - 1) + N_CLASSES * (FFN_TILE - SUBLANES)
    n_ffn_tiles = -(-max_rows // FFN_TILE)
    x2 = x.reshape(T, D)
    for l in range(depth):
        mod = _ada_call(c, w_ada[l], b_ada[l])
        shift1, scale1, gate1, shift2, scale2, gate2 = [
            m.reshape(B, 1, D) for m in jnp.split(mod, 6, axis=-1)]

        u, v, q, k, va, ga, gb = _inproj_call(
            x2, shift1, scale1, norm1_g[l].reshape(1, D), w_in[l].astype(BF16),
            sgu_norm_g[l].reshape(1, D), q_norm_g[l].reshape(1, ATT_HEAD_DIM),
            k_norm_g[l].reshape(1, ATT_HEAD_DIM), B, S, tm=512)

        o = _attention_call(q, k, va)

        w_router = jnp.concatenate(
            [w_router_group[l],
             jnp.transpose(w_router_expert[l], (1, 0, 2)).reshape(D, N_EXPERTS)], axis=1)
        w_router = jnp.pad(w_router, ((0, 0), (0, LANES - w_router.shape[1])))
        w_router_hi = w_router.astype(BF16)
        w_router_lo = (w_router - w_router_hi.astype(F32)).astype(BF16)
        w_router = jnp.concatenate([w_router_hi, w_router_lo], axis=1)
        b_router = jnp.concatenate([b_router_group[l], b_router_expert[l].reshape(N_EXPERTS)])
        b_router = jnp.pad(b_router, (0, LANES - b_router.shape[0])).reshape(1, LANES)
        sgu_bias = jnp.repeat(sgu_b[l].T, D // SGU_GROUPS, axis=1)

        x1, h2, info, counts = _merge_call(
            x2, u, v, o, ga, gb, gate1, shift2, scale2, norm2_g[l].reshape(1, D), sgu_w[l],
            sgu_bias, w_proj_a[l].astype(BF16), w_proj_b[l].astype(BF16), w_out[l].astype(BF16),
            w_router, b_router, B, S)

        plan = _dispatch_plan(counts[:, 0, :N_CLASSES].astype(jnp.int32), n_ffn_tiles)
        hs = _dispatch_call(plan, h2, info, n_ffn_tiles)
        ys = _ffn_call(plan, hs, w_gate[l], w_up[l], w_down[l], n_ffn_tiles)
        x2 = _combine_call(plan, x1, info, gate2, ys, B, S)
    return x2.reshape(B, S, D)
```

```python
import math

import jax
import jax.numpy as jnp
import numpy as np
from jax import lax
from jax.experimental import pallas as pl
from jax.experimental.pallas import tpu as pltpu

D_MODEL = 1024
CHUNK = 128
SGU_GROUPS = 8
ATT_HEADS = 8
ATT_HEAD_DIM = 128
DILATED_PATTERNS = ((128, 1), (512, 4), (2048, 16))
N_GROUPS = 4
EXPERTS_PER_GROUP = 4
N_EXPERTS = N_GROUPS * EXPERTS_PER_GROUP
D_EXPERT = 512
EPS = 1e-6
NEG_INF = -1e30

N_IN_SPLITS = 7
LANES = 128
SUBLANES = 8
EXPERT_LANE0 = N_GROUPS

PAIRS = ((0, 1), (0, 2), (0, 3), (1, 2), (1, 3), (2, 3))
N_CLASSES = N_GROUPS * len(PAIRS)
TOKEN_TILE = 512
SORT_ROWS = TOKEN_TILE + 3 * 128
CARRY_ROWS = 2 * LANES
ROW_WIDTH = D_MODEL + LANES
FFN_TILE = 512
INFO_RANK, INFO_PLO_H, INFO_PHI_H, INFO_PLO_L, INFO_PHI_L, INFO_CLS = 0, 1, 2, 3, 4, 5

VMEM_LIMIT = 56 * 1024 * 1024
BF16 = jnp.bfloat16
F32 = jnp.float32

assert SORT_ROWS >= TOKEN_TILE + 2 * N_CLASSES * (SUBLANES - 1) + SUBLANES
assert CARRY_ROWS >= N_CLASSES * SUBLANES


def _rms(x, g):
    return x * lax.rsqrt(jnp.mean(x * x, axis=-1, keepdims=True) + EPS) * g


def _gelu_tanh(x):
    k = -2.0 * math.sqrt(2.0 / math.pi) * math.log2(math.e)
    return x / (1.0 + jnp.exp2(x * (k + (k * 0.044715) * (x * x))))


def _ada_kernel(c_ref, w_ref, b_ref, o_ref):
    c = c_ref[...]
    cond = c * jax.nn.sigmoid(c)
    o_ref[...] = jnp.dot(cond, w_ref[...], preferred_element_type=F32,
                         precision=lax.Precision.HIGHEST) + b_ref[...]


def _ada_call(c, w_ada, b_ada):
    B, D = c.shape
    N = w_ada.shape[1]
    tn = 1024
    return pl.pallas_call(
        _ada_kernel,
        grid=(N // tn,),
        in_specs=[pl.BlockSpec((B, D), lambda j: (0, 0)),
                  pl.BlockSpec((D, tn), lambda j: (0, j)),
                  pl.BlockSpec((1, tn), lambda j: (0, j))],
        out_specs=pl.BlockSpec((B, tn), lambda j: (0, j)),
        out_shape=jax.ShapeDtypeStruct((B, N), F32),
        compiler_params=pltpu.CompilerParams(dimension_semantics=("arbitrary",),
                                             vmem_limit_bytes=VMEM_LIMIT),
        name="adaln",
    )(c, w_ada, b_ada.reshape(1, N))


def _inproj_kernel(x_ref, shift_ref, scale_ref, n1g_ref, w_ref, sgug_ref, qg_ref, kg_ref,
                   u_ref, v_ref, q_ref, k_ref, va_ref, ga_ref, gb_ref):
    x = x_ref[...]
    h = _rms(x, n1g_ref[...]) * (1.0 + scale_ref[0]) + shift_ref[0]
    h = h.astype(BF16)

    def proj(i):
        return jnp.dot(h, w_ref[:, i * D_MODEL:(i + 1) * D_MODEL], preferred_element_type=F32)

    u_ref[...] = _gelu_tanh(proj(0)).astype(BF16)
    v_ref[...] = _rms(_gelu_tanh(proj(1)), sgug_ref[...]).astype(BF16)

    q = proj(2)
    qscale = ATT_HEAD_DIM ** -0.5
    for hd in range(ATT_HEADS):
        sl = slice(hd * ATT_HEAD_DIM, (hd + 1) * ATT_HEAD_DIM)
        q_ref[0, hd] = (_rms(q[:, sl], qg_ref[...]) * qscale).astype(BF16)
    k = proj(3)
    for hd in range(ATT_HEADS):
        sl = slice(hd * ATT_HEAD_DIM, (hd + 1) * ATT_HEAD_DIM)
        k_ref[0, hd] = _rms(k[:, sl], kg_ref[...]).astype(BF16)
    ga_ref[...] = jax.nn.sigmoid(proj(5)).astype(BF16)
    gb_ref[...] = jax.nn.sigmoid(proj(6)).astype(BF16)
    va = proj(4)
    for hd in range(ATT_HEADS):
        sl = slice(hd * ATT_HEAD_DIM, (hd + 1) * ATT_HEAD_DIM)
        va_ref[0, hd] = va[:, sl].astype(BF16)


def _inproj_call(x2, shift1, scale1, n1g, w_in, sgug, qg, kg, B, S, tm):
    T, D = x2.shape
    tpb = S // tm
    tok = pl.BlockSpec((tm, D), lambda i: (i, 0))
    per_b = pl.BlockSpec((1, 1, D), lambda i: (i // tpb, 0, 0))
    row = lambda n: pl.BlockSpec((1, n), lambda i: (0, 0))
    head = pl.BlockSpec((1, ATT_HEADS, tm, ATT_HEAD_DIM), lambda i: (i // tpb, 0, i % tpb, 0))
    tok_sds = jax.ShapeDtypeStruct((T, D), BF16)
    head_sds = jax.ShapeDtypeStruct((B, ATT_HEADS, S, ATT_HEAD_DIM), BF16)
    return pl.pallas_call(
        _inproj_kernel,
        grid=(T // tm,),
        in_specs=[tok, per_b, per_b, row(D),
                  pl.BlockSpec((D, N_IN_SPLITS * D), lambda i: (0, 0), pipeline_mode=pl.Buffered(1)),
                  row(D), row(ATT_HEAD_DIM), row(ATT_HEAD_DIM)],
        out_specs=[tok, tok, head, head, head, tok, tok],
        out_shape=[tok_sds, tok_sds, head_sds, head_sds, head_sds, tok_sds, tok_sds],
        compiler_params=pltpu.CompilerParams(dimension_semantics=("parallel",),
                                             vmem_limit_bytes=VMEM_LIMIT),
        name="inproj",
    )(x2, shift1, scale1, n1g, w_in, sgug, qg, kg)


ATT_BLK = 128
STRIDED_BLK = 256
HEADS_PER_STEP = 4
RESIDUES = 4
LOCAL_PATTERNS = tuple(p for p in DILATED_PATTERNS if p[1] % RESIDUES != 0)
STRIDED_PATTERNS = tuple(p for p in DILATED_PATTERNS if p[1] % RESIDUES == 0)
assert all(d == 1 and w <= ATT_BLK for w, d in LOCAL_PATTERNS)


def _log_count_bias(count):
    return jnp.asarray(np.where(count > 0, np.log(np.maximum(count, 1)), NEG_INF), F32)


def _local_bias():
    a = np.arange(ATT_BLK)[:, None]
    col = np.arange(2 * ATT_BLK)[None, :]
    delta = ATT_BLK + a - col
    count = np.zeros(delta.shape, np.int64)
    for window, _ in LOCAL_PATTERNS:
        count += (delta >= 0) & (delta <= window)
    return _log_count_bias(count)


def _strided_bias(n):
    nblk = n // STRIDED_BLK
    a = np.arange(STRIDED_BLK)[:, None]
    col = np.arange(n)[None, :]
    delta = (STRIDED_BLK * (nblk - 1 - col // STRIDED_BLK) + a - col % STRIDED_BLK) * RESIDUES
    count = np.zeros(delta.shape, np.int64)
    for window, dilation in STRIDED_PATTERNS:
        count += (delta >= 0) & (delta <= window) & (delta % dilation == 0)
    return _log_count_bias(count)


def _qk(q, k):
    return lax.dot_general(q, k, (((1,), (1,)), ((), ())), preferred_element_type=F32)


def _attention_kernel(q_ref, k_ref, v_ref, lbias_ref, sbias_ref, o_ref,
                      stage, q4, k4, v4, ve, acc_scr, m_scr, l_scr):
    S = q_ref.shape[2]
    n = S // RESIDUES
    nblk_s = n // STRIDED_BLK
    nblk = S // ATT_BLK
    Dh = ATT_HEAD_DIM
    ones = jnp.ones((S, LANES), BF16)

    def relayout(hd):
        ve[hd, :, Dh:] = ones
        ve[hd, :, :Dh] = v_ref[0, hd]
        for r in range(RESIDUES):
            v4[hd, r, :, Dh:] = ones[:n]
        for src, dst in ((q_ref, q4), (k_ref, k4), (v_ref, v4)):
            stage[hd] = src[0, hd].astype(F32)
            for r in range(RESIDUES):
                dst[hd, r, :, :Dh] = stage[hd, pl.ds(r, n, stride=RESIDUES), :].astype(BF16)

    def strided_block(hd, r, jb):
        nk = (jb + 1) * STRIDED_BLK

        def scores():
            s = _qk(q4[hd, r, jb * STRIDED_BLK:(jb + 1) * STRIDED_BLK, :], k4[hd, r, :nk, :])
            return s + sbias_ref[:, (nblk_s - 1 - jb) * STRIDED_BLK:]

        def rest(s):
            m = jnp.max(s, axis=-1, keepdims=True)
            p = jnp.exp(s - m)
            acc = jnp.dot(p.astype(BF16), v4[hd, r, :nk, :], preferred_element_type=F32)
            rows = pl.ds(RESIDUES * jb * STRIDED_BLK + r, STRIDED_BLK, stride=RESIDUES)
            acc_scr[hd, rows, :] = acc[:, :Dh]
            m_scr[hd, rows, :] = jnp.broadcast_to(m, (STRIDED_BLK, LANES))
            l_scr[hd, rows, :] = acc[:, Dh:]

        return scores, rest

    def local_block(hd, c):
        k0 = max(c - 1, 0) * ATT_BLK
        nk = (c + 1) * ATT_BLK - k0
        blk = slice(c * ATT_BLK, (c + 1) * ATT_BLK)

        def scores():
            s = _qk(q_ref[0, hd, blk, :], k_ref[0, hd, k0:k0 + nk, :])
            return s + lbias_ref[:, 2 * ATT_BLK - nk:]

        def rest(s):
            m_s = m_scr[hd, blk, :]
            m = jnp.maximum(jnp.max(s, axis=-1, keepdims=True), m_s)
            w = jnp.exp(m_s - m)
            p = jnp.exp(s - jnp.concatenate([m] * (nk // LANES), axis=-1))
            acc = jnp.dot(p.astype(BF16), ve[hd, k0:k0 + nk, :], preferred_element_type=F32)
            l = acc[:, Dh:] + l_scr[hd, blk, :] * w
            o_ref[0, hd, blk, :] = ((acc[:, :Dh] + acc_scr[hd, blk, :] * w) / l).astype(BF16)

        return scores, rest

    for hd in range(HEADS_PER_STEP):
        relayout(hd)
        blocks = [strided_block(hd, r, jb) for r in range(RESIDUES) for jb in range(nblk_s)]
        blocks += [local_block(hd, c) for c in range(nblk)]
        s_next = blocks[0][0]()
        for b, (_, rest) in enumerate(blocks):
            s_cur = s_next
            if b + 1 < len(blocks):
                s_next = blocks[b + 1][0]()
            rest(s_cur)


def _attention_call(q, k, v):
    B, H, S, Dh = q.shape
    n = S // RESIDUES
    P = HEADS_PER_STEP
    blk = pl.BlockSpec((1, P, S, Dh), lambda b, h: (b, h, 0, 0))
    return pl.pallas_call(
        _attention_kernel,
        grid=(B, H // P),
        in_specs=[blk, blk, blk,
                  pl.BlockSpec((ATT_BLK, 2 * ATT_BLK), lambda b, h: (0, 0)),
                  pl.BlockSpec((STRIDED_BLK, n), lambda b, h: (0, 0))],
        out_specs=blk,
        out_shape=jax.ShapeDtypeStruct((B, H, S, Dh), BF16),
        scratch_shapes=[pltpu.VMEM((P, S, Dh), F32),
                        pltpu.VMEM((P, RESIDUES, n, Dh), BF16),
                        pltpu.VMEM((P, RESIDUES, n, Dh), BF16),
                        pltpu.VMEM((P, RESIDUES, n, Dh + LANES), BF16),
                        pltpu.VMEM((P, S, Dh + LANES), BF16),
                        pltpu.VMEM((P, S, Dh), F32),
                        pltpu.VMEM((P, S, LANES), F32),
                        pltpu.VMEM((P, S, LANES), F32)],
        compiler_params=pltpu.CompilerParams(dimension_semantics=("parallel", "parallel"),
                                             vmem_limit_bytes=VMEM_LIMIT),
        name="attention",
    )(q, k, v, _local_bias(), _strided_bias(n))


def _route(logits):
    tm = logits.shape[0]
    lane = lax.broadcasted_iota(jnp.int32, logits.shape, 1)
    lanef = lane.astype(F32)

    def masked_top(mask):
        top = jnp.max(jnp.where(mask, logits, -jnp.inf), axis=-1, keepdims=True)
        idx = jnp.min(jnp.where(mask & (logits == top), lanef, float(LANES)), axis=-1, keepdims=True)
        return top, idx

    gmask = lane < N_GROUPS
    gmax, gidx = masked_top(gmask)
    p_group = 1.0 / jnp.sum(jnp.where(gmask, jnp.exp(logits - gmax), 0.0), axis=-1, keepdims=True)
    e_lo = EXPERT_LANE0 + gidx * EXPERTS_PER_GROUP
    emask = (lanef >= e_lo) & (lanef < e_lo + EXPERTS_PER_GROUP)
    v1, i1 = masked_top(emask)
    v2, i2 = masked_top(emask & (lanef != i1))
    e21 = jnp.exp(v2 - v1)
    p1 = p_group / (1.0 + e21)
    p2 = p_group * e21 / (1.0 + e21)
    first_is_lo = i1 < i2
    lo = jnp.minimum(i1, i2) - e_lo
    hi = jnp.maximum(i1, i2) - e_lo
    pair = lo * (7.0 - lo) * 0.5 + (hi - lo - 1.0)
    cls = gidx * float(len(PAIRS)) + pair
    p_lo = jnp.where(first_is_lo, p1, p2)
    p_hi = jnp.where(first_is_lo, p2, p1)

    onehot = lanef == cls
    onehot_b = jnp.where(onehot, 1.0, 0.0).astype(BF16)
    r = lax.broadcasted_iota(jnp.int32, (tm, tm), 0)
    c = lax.broadcasted_iota(jnp.int32, (tm, tm), 1)
    before = jnp.where(c < r, 1.0, 0.0).astype(BF16)
    before_me = jnp.dot(before, onehot_b, preferred_element_type=F32)
    counts = jnp.sum(jnp.where(onehot, 1.0, 0.0), axis=0, keepdims=True)
    rank = jnp.sum(jnp.where(onehot, before_me, 0.0), axis=-1, keepdims=True)

    def hi_part(p):
        return p.astype(BF16).astype(F32)

    info = jnp.where(lane == INFO_RANK, rank, 0.0)
    info = jnp.where(lane == INFO_CLS, cls, info)
    info = jnp.where(lane == INFO_PLO_H, hi_part(p_lo), info)
    info = jnp.where(lane == INFO_PHI_H, hi_part(p_hi), info)
    info = jnp.where(lane == INFO_PLO_L, p_lo - hi_part(p_lo), info)
    info = jnp.where(lane == INFO_PHI_L, p_hi - hi_part(p_hi), info)
    return info, counts


def _merge_kernel(x_ref, u_ref, v_ref, o_ref, ga_ref, gb_ref, gate1_ref, shift2_ref, scale2_ref,
                  n2g_ref, sw_ref, sb_ref, wa_ref, wb_ref, wo_ref, wr_ref, br_ref,
                  x1_ref, h2_ref, info_ref, counts_ref, s_scr, logit_scr):
    tm = x_ref.shape[0]
    step = pl.program_id(0)

    @pl.when(step == 0)
    def _():
        logit_scr[...] = jnp.zeros_like(logit_scr)

    prev_logits = logit_scr[1 - step % 2]
    row = lax.broadcasted_iota(jnp.int32, (CHUNK, CHUNK), 0)
    colm = lax.broadcasted_iota(jnp.int32, (CHUNK, CHUNK), 1)
    causal = colm <= row
    for g in range(SGU_GROUPS):
        w = jnp.where(causal, sw_ref[g], 0.0).astype(BF16)
        gs = slice(g * CHUNK, (g + 1) * CHUNK)
        chunks = [slice(c * CHUNK, (c + 1) * CHUNK) for c in range(tm // CHUNK)]
        mixed = jnp.dot(w, jnp.concatenate([v_ref[cs, gs] for cs in chunks], axis=1),
                        preferred_element_type=F32)
        for cs in chunks:
            s_scr[cs, gs] = (u_ref[cs, gs].astype(F32) * (mixed[:, cs] + sb_ref[:, gs])).astype(BF16)

    y_a = jnp.dot(s_scr[...], wa_ref[...], preferred_element_type=F32)
    info, counts = _route(prev_logits)
    info_ref[...] = info
    counts_ref[0] = counts
    o = jnp.concatenate([o_ref[0, hd] for hd in range(ATT_HEADS)], axis=-1)
    y_b = jnp.dot(o, wb_ref[...], preferred_element_type=F32)
    merged = ga_ref[...].astype(F32) * y_a + gb_ref[...].astype(F32) * y_b
    y = jnp.dot(merged.astype(BF16), wo_ref[...], preferred_element_type=F32)
    x1 = x_ref[...] + gate1_ref[0] * y
    x1_ref[...] = x1

    h2 = _rms(x1, n2g_ref[...]) * (1.0 + scale2_ref[0]) + shift2_ref[0]
    h2_ref[...] = h2.astype(BF16)
    h_hi = h2.astype(BF16)
    h_lo = (h2 - h_hi.astype(F32)).astype(BF16)
    r_hi = jnp.dot(h_hi, wr_ref[...], preferred_element_type=F32)
    r_lo = jnp.dot(h_lo, wr_ref[:, :LANES], preferred_element_type=F32)
    logit_scr[step % 2] = r_hi[:, :LANES] + r_hi[:, LANES:] + r_lo + br_ref[...]


def _merge_call(x2, u, v, o, ga, gb, gate1, shift2, scale2, n2g, sgu_w, sgu_bias, wa, wb, wo,
                w_router, b_router, B, S):
    T, D = x2.shape
    tm = TOKEN_TILE
    tpb = S // tm
    n_tiles = T // tm
    tile = lambda i: jnp.minimum(i, n_tiles - 1)
    routed = lambda i: jnp.maximum(i - 1, 0)
    tok = pl.BlockSpec((tm, D), lambda i: (tile(i), 0))
    per_b = pl.BlockSpec((1, 1, D), lambda i: (tile(i) // tpb, 0, 0))
    head = pl.BlockSpec((1, ATT_HEADS, tm, ATT_HEAD_DIM),
                        lambda i: (tile(i) // tpb, 0, tile(i) % tpb, 0))
    full = lambda *shape: pl.BlockSpec(shape, lambda i: (0,) * len(shape))
    return pl.pallas_call(
        _merge_kernel,
        grid=(n_tiles + 1,),
        in_specs=[tok, tok, tok, head, tok, tok, per_b, per_b, per_b, full(1, D),
                  full(SGU_GROUPS, CHUNK, CHUNK), full(CHUNK, D),
                  full(D, D), full(D, D), full(D, D), full(D, 2 * LANES), full(1, LANES)],
        out_specs=[tok, tok, pl.BlockSpec((tm, LANES), lambda i: (routed(i), 0)),
                   pl.BlockSpec((1, 1, LANES), lambda i: (routed(i), 0, 0))],
        out_shape=[jax.ShapeDtypeStruct((T, D), F32), jax.ShapeDtypeStruct((T, D), BF16),
                   jax.ShapeDtypeStruct((T, LANES), F32),
                   jax.ShapeDtypeStruct((n_tiles, 1, LANES), F32)],
        scratch_shapes=[pltpu.VMEM((tm, D), BF16), pltpu.VMEM((2, tm, LANES), F32)],
        compiler_params=pltpu.CompilerParams(dimension_semantics=("arbitrary",),
                                             vmem_limit_bytes=VMEM_LIMIT),
        name="merge",
    )(x2, u, v, o, ga, gb, gate1, shift2, scale2, n2g, sgu_w, sgu_bias, wa, wb, wo,
      w_router, b_router)


def _dispatch_plan(counts, n_ffn_tiles):
    n = counts
    before = jnp.cumsum(n, axis=0) - n
    class_tokens = jnp.sum(n, axis=0)
    class_rows = (class_tokens + SUBLANES - 1) // SUBLANES * SUBLANES
    class_tiles = (class_rows + FFN_TILE - 1) // FFN_TILE
    cum_tiles = jnp.cumsum(class_tiles)
    seg_start = (cum_tiles - class_tiles) * FFN_TILE
    ahead = before % SUBLANES
    block_start = seg_start[None, :] + before - ahead
    span = (ahead + n + SUBLANES - 1) // SUBLANES
    local_start = (jnp.cumsum(span, axis=1) - span) * SUBLANES
    full = (ahead + n) // SUBLANES
    fetch = jnp.where(n > 0, span, 0)
    pad = ((0, 0), (0, LANES - N_CLASSES))
    base = jnp.stack([jnp.pad(local_start // SUBLANES, pad), jnp.pad(ahead, pad)], axis=1)
    n_used = cum_tiles[-1]
    tile_idx = jnp.maximum(jnp.minimum(jnp.arange(n_ffn_tiles, dtype=jnp.int32), n_used - 1), 0)
    tile_cls = jnp.sum((tile_idx[:, None] >= cum_tiles[None, :]).astype(jnp.int32), axis=1)
    pair = tile_cls % len(PAIRS)
    group = tile_cls // len(PAIRS)
    pairs = jnp.asarray(PAIRS, jnp.int32)
    i32 = lambda a: a.astype(jnp.int32).reshape(-1)

    return dict(
        block_start=i32(block_start), local_start=i32(local_start),
        send_units=i32(full), tile_send_units=i32(jnp.sum(full, axis=1)),
        fetch_units=i32(fetch), tile_fetch_units=i32(jnp.sum(fetch, axis=1)),
        carried_rows=i32(ahead), carry_src=i32(local_start + full * SUBLANES),
        flush_dst=i32(seg_start + class_tokens // SUBLANES * SUBLANES),
        flush_flag=i32(class_tokens % SUBLANES),
        base_rows=jnp.pad(base, ((0, 0), (0, SUBLANES - 2), (0, 0))).astype(BF16),
        base_cols=jnp.repeat(jnp.transpose(base, (0, 2, 1)), LANES, axis=2).astype(BF16),
        tail_start=i32(seg_start + class_rows),
        tail_units=i32((class_tiles * FFN_TILE - class_rows) // SUBLANES),
        tile_idx=i32(tile_idx), n_used=i32(n_used),
        tile_ea=i32(group * EXPERTS_PER_GROUP + pairs[pair, 0]),
        tile_eb=i32(group * EXPERTS_PER_GROUP + pairs[pair, 1]))


def _rows(units):
    return pl.multiple_of(units * SUBLANES, SUBLANES)


def _start_runs(tile, run_start, local_start, units, make_copy):
    def per_class(c, carry):
        k = tile * N_CLASSES + c

        @pl.when(units[k] > 0)
        def _():
            make_copy(pl.multiple_of(local_start[k], SUBLANES),
                      pl.multiple_of(run_start[k], SUBLANES), _rows(units[k])).start()

        return carry

    lax.fori_loop(0, N_CLASSES, per_class, 0)


def _wait_runs(tile, tile_units, make_copy):
    make_copy(0, 0, _rows(tile_units[tile])).wait()


def _dispatch_kernel(block_start, local_start, send_units, tile_send_units, carried_rows, carry_src,
                     flush_dst, flush_flag, tail_start, tail_units, n_used,
                     h2_ref, info_ref, base_ref, hs_ref,
                     sorted_scr, carry_scr, zero_scr, sems, flush_sem, tail_sem, unused_sem):
    i = pl.program_id(0)
    nt = pl.num_programs(0)
    slot = i % 2
    n_ffn_tiles = hs_ref.shape[0] // FFN_TILE

    def run_copy(s):
        def make(local_row, global_row, rows):
            return pltpu.make_async_copy(sorted_scr.at[s, pl.ds(local_row, rows), :],
                                         hs_ref.at[pl.ds(global_row, rows), :], sems.at[s])
        return make

    def flush_copy(c):
        return pltpu.make_async_copy(
            carry_scr.at[pl.ds(pl.multiple_of(c * SUBLANES, SUBLANES), SUBLANES), :],
            hs_ref.at[pl.ds(pl.multiple_of(flush_dst[c], SUBLANES), SUBLANES), :], flush_sem)

    def for_flushes(fn):
        def per_class(c, carry):
            pl.when(flush_flag[c] > 0)(lambda: fn(c))
            return carry

        lax.fori_loop(0, N_CLASSES, per_class, 0)

    def tail_copy(c):
        rows = _rows(tail_units[c])
        return pltpu.make_async_copy(
            zero_scr.at[pl.ds(0, rows), :],
            hs_ref.at[pl.ds(pl.multiple_of(tail_start[c], SUBLANES), rows), :], tail_sem)

    def unused_copy(t):
        return pltpu.make_async_copy(
            zero_scr, hs_ref.at[pl.ds(pl.multiple_of(t * FFN_TILE, FFN_TILE), FFN_TILE), :],
            unused_sem)

    def for_slot(fn):
        for s in range(2):
            pl.when(slot == s)(lambda s=s: fn(s))

    def for_tails(fn):
        def per_class(c, carry):
            pl.when(tail_units[c] > 0)(lambda: fn(c))
            return carry

        lax.fori_loop(0, N_CLASSES, per_class, 0)

    @pl.when(i >= 2)
    def _():
        for_slot(lambda s: _wait_runs(i - 2, tile_send_units, run_copy(s)))

    @pl.when(i == 0)
    def _():
        zero_scr[...] = jnp.zeros_like(zero_scr)
        carry_scr[...] = jnp.zeros_like(carry_scr)
        for_tails(lambda c: tail_copy(c).start())

        def per_unused(t, carry):
            unused_copy(t).start()
            return carry

        lax.fori_loop(n_used[0], n_ffn_tiles, per_unused, 0)

    info = info_ref[...]
    info_t = info.T
    class_iota = lax.broadcasted_iota(jnp.int32, (LANES, TOKEN_TILE), 0).astype(F32)
    class_onehot = jnp.where(class_iota == info_t[INFO_CLS:INFO_CLS + 1, :], 1.0, 0.0).astype(BF16)
    base = jnp.dot(base_ref[0], class_onehot, preferred_element_type=F32)
    dest_row = base[0:1, :] * float(SUBLANES) + base[1:2, :] + info_t[INFO_RANK:INFO_RANK + 1, :]
    rows = lax.broadcasted_iota(jnp.int32, (SORT_ROWS, TOKEN_TILE), 0).astype(F32)
    perm = jnp.where(rows == dest_row, 1.0, 0.0).astype(BF16)
    lane = lax.broadcasted_iota(jnp.int32, info.shape, 1)
    weights = jnp.where((lane == INFO_RANK) | (lane == INFO_CLS), 0.0, info).astype(BF16)

    def sort_and_send(s):
        sorted_scr[s, :, :D_MODEL] = jnp.dot(perm, h2_ref[...], preferred_element_type=F32)
        sorted_scr[s, :, D_MODEL:] = jnp.dot(perm, weights, preferred_element_type=F32)

        def per_class(c, carry):
            k = i * N_CLASSES + c
            mine = pl.ds(pl.multiple_of(c * SUBLANES, SUBLANES), SUBLANES)
            first = pl.ds(pl.multiple_of(local_start[k], SUBLANES), SUBLANES)

            @pl.when(carried_rows[k] > 0)
            def _():
                sorted_scr[s, first, :] = sorted_scr[s, first, :] + carry_scr[mine, :]

            last = pl.ds(pl.multiple_of(carry_src[k], SUBLANES), SUBLANES)
            carry_scr[mine, :] = sorted_scr[s, last, :]
            return carry

        lax.fori_loop(0, N_CLASSES, per_class, 0)
        _start_runs(i, block_start, local_start, send_units, run_copy(s))

    for_slot(sort_and_send)

    @pl.when(i == nt - 1)
    def _():
        for_flushes(lambda c: flush_copy(c).start())
        for_slot(lambda s: _wait_runs(i, tile_send_units, run_copy(s)))

        @pl.when(nt >= 2)
        def _():
            for_slot(lambda s: _wait_runs(i - 1, tile_send_units, run_copy(1 - s)))

        for_flushes(lambda c: flush_copy(c).wait())
        for_tails(lambda c: tail_copy(c).wait())

        def per_unused(t, carry):
            unused_copy(0).wait()
            return carry

        lax.fori_loop(n_used[0], n_ffn_tiles, per_unused, 0)


def _dispatch_call(plan, h2, info, n_ffn_tiles):
    T, D = h2.shape
    tm = TOKEN_TILE
    grid_spec = pltpu.PrefetchScalarGridSpec(
        num_scalar_prefetch=11,
        grid=(T // tm,),
        in_specs=[pl.BlockSpec((tm, D), lambda i, *_: (i, 0)),
                  pl.BlockSpec((tm, LANES), lambda i, *_: (i, 0)),
                  pl.BlockSpec((1, SUBLANES, LANES), lambda i, *_: (i, 0, 0))],
        out_specs=pl.BlockSpec(memory_space=pl.ANY),
        scratch_shapes=[pltpu.VMEM((2, SORT_ROWS, ROW_WIDTH), F32),
                        pltpu.VMEM((CARRY_ROWS, ROW_WIDTH), F32),
                        pltpu.VMEM((FFN_TILE, ROW_WIDTH), F32),
                        pltpu.SemaphoreType.DMA((2,)),
                        pltpu.SemaphoreType.DMA(()),
                        pltpu.SemaphoreType.DMA(()),
                        pltpu.SemaphoreType.DMA(())])
    return pl.pallas_call(
        _dispatch_kernel,
        grid_spec=grid_spec,
        out_shape=jax.ShapeDtypeStruct((n_ffn_tiles * FFN_TILE, ROW_WIDTH), F32),
        compiler_params=pltpu.CompilerParams(dimension_semantics=("arbitrary",),
                                             vmem_limit_bytes=VMEM_LIMIT),
        name="dispatch",
    )(plan["block_start"], plan["local_start"], plan["send_units"], plan["tile_send_units"],
      plan["carried_rows"], plan["carry_src"], plan["flush_dst"], plan["flush_flag"],
      plan["tail_start"], plan["tail_units"], plan["n_used"],
      h2, info, plan["base_rows"])


def _ffn_kernel(tile_idx, tile_ea, tile_eb, n_used, hs_ref, wga_ref, wua_ref, wda_ref,
                wgb_ref, wub_ref, wdb_ref, ys_ref):
    @pl.when(pl.program_id(0) >= n_used[0])
    def _():
        ys_ref[...] = jnp.zeros_like(ys_ref)

    @pl.when(pl.program_id(0) < n_used[0])
    def _():
        h = hs_ref[:, :D_MODEL].astype(BF16)
        pv = hs_ref[:, D_MODEL:]
        p_lo = pv[:, INFO_PLO_H:INFO_PLO_H + 1] + pv[:, INFO_PLO_L:INFO_PLO_L + 1]
        p_hi = pv[:, INFO_PHI_H:INFO_PHI_H + 1] + pv[:, INFO_PHI_L:INFO_PHI_L + 1]

        def hidden(wg_ref, wu_ref, p):
            a = jnp.dot(h, wg_ref[0], preferred_element_type=F32)
            b = jnp.dot(h, wu_ref[0], preferred_element_type=F32)
            return (a * jax.nn.sigmoid(a) * b * p).astype(BF16)

        he_a = hidden(wga_ref, wua_ref, p_lo)
        he_b = hidden(wgb_ref, wub_ref, p_hi)
        ys_ref[...] = (jnp.dot(he_a, wda_ref[0], preferred_element_type=F32)
                       + jnp.dot(he_b, wdb_ref[0], preferred_element_type=F32))


def _ffn_call(plan, hs, wg, wu, wd, n_ffn_tiles):
    D = D_MODEL
    rows = pl.BlockSpec((FFN_TILE, ROW_WIDTH), lambda t, idx, ea, eb, n: (idx[t], 0))
    w_in_a = pl.BlockSpec((1, D, D_EXPERT), lambda t, idx, ea, eb, n: (ea[t], 0, 0))
    w_out_a = pl.BlockSpec((1, D_EXPERT, D), lambda t, idx, ea, eb, n: (ea[t], 0, 0))
    w_in_b = pl.BlockSpec((1, D, D_EXPERT), lambda t, idx, ea, eb, n: (eb[t], 0, 0))
    w_out_b = pl.BlockSpec((1, D_EXPERT, D), lambda t, idx, ea, eb, n: (eb[t], 0, 0))
    grid_spec = pltpu.PrefetchScalarGridSpec(
        num_scalar_prefetch=4,
        grid=(n_ffn_tiles,),
        in_specs=[rows, w_in_a, w_in_a, w_out_a, w_in_b, w_in_b, w_out_b],
        out_specs=pl.BlockSpec((FFN_TILE, D), lambda t, idx, ea, eb, n: (t, 0)))
    return pl.pallas_call(
        _ffn_kernel,
        grid_spec=grid_spec,
        out_shape=jax.ShapeDtypeStruct((n_ffn_tiles * FFN_TILE, D), F32),
        compiler_params=pltpu.CompilerParams(dimension_semantics=("arbitrary",),
                                             vmem_limit_bytes=VMEM_LIMIT),
        name="ffn",
    )(plan["tile_idx"], plan["tile_ea"], plan["tile_eb"], plan["n_used"],
      hs, wg, wu, wd, wg, wu, wd)


def _combine_kernel(run_start, local_start, units, tile_units,
                    x1_ref, info_ref, base_ref, gate2_ref, ys_ref, o_ref, ybuf, sems):
    i = pl.program_id(0)
    nt = pl.num_programs(0)
    slot = i % 2

    def run_copy(s):
        def make(local_row, global_row, rows):
            return pltpu.make_async_copy(ys_ref.at[pl.ds(global_row, rows), :],
                                         ybuf.at[s, pl.ds(local_row, rows), :], sems.at[s])
        return make

    def for_slot(fn):
        for s in range(2):
            pl.when(slot == s)(lambda s=s: fn(s))

    def fetch(tile, s):
        _start_runs(tile, run_start, local_start, units, run_copy(s))

    @pl.when(i == 0)
    def _():
        ybuf[...] = jnp.zeros_like(ybuf)
        fetch(0, 0)

    @pl.when(i + 1 < nt)
    def _():
        for_slot(lambda s: fetch(i + 1, 1 - s))

    lane = lax.broadcasted_iota(jnp.int32, (TOKEN_TILE, LANES), 1).astype(F32)
    class_onehot = jnp.where(lane == info_ref[:, INFO_CLS:INFO_CLS + 1], 1.0, 0.0).astype(BF16)
    base = jnp.dot(class_onehot, base_ref[0], preferred_element_type=F32)
    dest = (base[:, :LANES] * float(SUBLANES) + base[:, LANES:]
            + info_ref[:, INFO_RANK:INFO_RANK + 1])
    cols = lax.broadcasted_iota(jnp.int32, (TOKEN_TILE, SORT_ROWS), 1).astype(F32)
    unperm = jnp.where(cols == jnp.concatenate([dest] * (SORT_ROWS // LANES), axis=1),
                       1.0, 0.0).astype(BF16)

    def finish(s):
        _wait_runs(i, tile_units, run_copy(s))
        y = jnp.dot(unperm, ybuf[s].astype(BF16), preferred_element_type=F32)
        o_ref[...] = x1_ref[...] + gate2_ref[0] * y

    for_slot(finish)


def _combine_call(plan, x1, info, gate2, ys, B, S):
    T, D = x1.shape
    tm = TOKEN_TILE
    tpb = S // tm
    grid_spec = pltpu.PrefetchScalarGridSpec(
        num_scalar_prefetch=4,
        grid=(T // tm,),
        in_specs=[pl.BlockSpec((tm, D), lambda i, *_: (i, 0)),
                  pl.BlockSpec((tm, LANES), lambda i, *_: (i, 0)),
                  pl.BlockSpec((1, LANES, 2 * LANES), lambda i, *_: (i, 0, 0)),
                  pl.BlockSpec((1, 1, D), lambda i, *_: (i // tpb, 0, 0)),
                  pl.BlockSpec(memory_space=pl.ANY)],
        out_specs=pl.BlockSpec((tm, D), lambda i, *_: (i, 0)),
        scratch_shapes=[pltpu.VMEM((2, SORT_ROWS, D), F32),
                        pltpu.SemaphoreType.DMA((2,))])
    return pl.pallas_call(
        _combine_kernel,
        grid_spec=grid_spec,
        out_shape=jax.ShapeDtypeStruct((T, D), F32),
        compiler_params=pltpu.CompilerParams(dimension_semantics=("arbitrary",),
                                             vmem_limit_bytes=VMEM_LIMIT),
        name="combine",
    )(plan["block_start"], plan["local_start"], plan["fetch_units"], plan["tile_fetch_units"],
      x1, info, plan["base_cols"], gate2, ys)


def kernel(x, c, w_ada, b_ada, norm1_g, w_in, sgu_norm_g, sgu_w, sgu_b, q_norm_g, k_norm_g,
           w_proj_a, w_proj_b, w_out, norm2_g, w_router_group, b_router_group,
           w_router_expert, b_router_expert, w_gate, w_up, w_down):
    B, S, D = x.shape
    T = B * S
    depth = w_ada.shape[0]
    n_token_tiles = T // TOKEN_TILE
    max_rows = T + N_CLASSES * (SUBLANES - 1) + N_CLASSES * (FFN_TILE - SUBLANES)
    n_ffn_tiles = -(-max_rows // FFN_TILE)
    x2 = x.reshape(T, D)
    for l in range(depth):
        mod = _ada_call(c, w_ada[l], b_ada[l])
        shift1, scale1, gate1, shift2, scale2, gate2 = [
            m.reshape(B, 1, D) for m in jnp.split(mod, 6, axis=-1)]

        u, v, q, k, va, ga, gb = _inproj_call(
            x2, shift1, scale1, norm1_g[l].reshape(1, D), w_in[l].astype(BF16),
            sgu_norm_g[l].reshape(1, D), q_norm_g[l].reshape(1, ATT_HEAD_DIM),
            k_norm_g[l].reshape(1, ATT_HEAD_DIM), B, S, tm=512)

        o = _attention_call(q, k, va)

        w_router = jnp.concatenate(
            [w_router_group[l],
             jnp.transpose(w_router_expert[l], (1, 0, 2)).reshape(D, N_EXPERTS)], axis=1)
        w_router = jnp.pad(w_router, ((0, 0), (0, LANES - w_router.shape[1])))
        w_router_hi = w_router.astype(BF16)
        w_router_lo = (w_router - w_router_hi.astype(F32)).astype(BF16)
        w_router = jnp.concatenate([w_router_hi, w_router_lo], axis=1)
        b_router = jnp.concatenate([b_router_group[l], b_router_expert[l].reshape(N_EXPERTS)])
        b_router = jnp.pad(b_router, (0, LANES - b_router.shape[0])).reshape(1, LANES)
        sgu_bias = jnp.repeat(sgu_b[l].T, D // SGU_GROUPS, axis=1)

        x1, h2, info, counts = _merge_call(
            x2, u, v, o, ga, gb, gate1, shift2, scale2, norm2_g[l].reshape(1, D), sgu_w[l],
            sgu_bias, w_proj_a[l].astype(BF16), w_proj_b[l].astype(BF16), w_out[l].astype(BF16),
            w_router, b_router, B, S)

        plan = _dispatch_plan(counts[:, 0, :N_CLASSES].astype(jnp.int32), n_ffn_tiles)
        hs = _dispatch_call(plan, h2, info, n_ffn_tiles)
        ys = _ffn_call(plan, hs, w_gate[l], w_up[l], w_down[l], n_ffn_tiles)
        x2 = _combine_call(plan, x1, info, gate2, ys, B, S)
    return x2.reshape(B, S, D)
```

```python
import math

import jax
import jax.numpy as jnp
import numpy as np
from jax import lax
from jax.experimental import pallas as pl
from jax.experimental.pallas import tpu as pltpu

D_MODEL = 1024
CHUNK = 128
SGU_GROUPS = 8
ATT_HEADS = 8
ATT_HEAD_DIM = 128
DILATED_PATTERNS = ((128, 1), (512, 4), (2048, 16))
N_GROUPS = 4
EXPERTS_PER_GROUP = 4
N_EXPERTS = N_GROUPS * EXPERTS_PER_GROUP
D_EXPERT = 512
EPS = 1e-6
NEG_INF = -1e30

N_IN_SPLITS = 7
LANES = 128
SUBLANES = 8
EXPERT_LANE0 = N_GROUPS

PAIRS = ((0, 1), (0, 2), (0, 3), (1, 2), (1, 3), (2, 3))
N_CLASSES = N_GROUPS * len(PAIRS)
TOKEN_TILE = 512
SORT_ROWS = TOKEN_TILE + 3 * 64
ROW_WIDTH = D_MODEL + LANES
FFN_TILE = 512
FFN_ROW_STEP = 128
INFO_DEST, INFO_PLO_H, INFO_PHI_H, INFO_PLO_L, INFO_PHI_L = 0, 1, 2, 3, 4

VMEM_LIMIT = 56 * 1024 * 1024
BF16 = jnp.bfloat16
F32 = jnp.float32

assert SORT_ROWS >= TOKEN_TILE + N_CLASSES * (SUBLANES - 1)


def _rms(x, g):
    return x * lax.rsqrt(jnp.mean(x * x, axis=-1, keepdims=True) + EPS) * g


def _gelu_tanh(x):
    k = -2.0 * math.sqrt(2.0 / math.pi) * math.log2(math.e)
    return x / (1.0 + jnp.exp2(x * (k + (k * 0.044715) * (x * x))))


def _ada_kernel(c_ref, w_ref, b_ref, o_ref):
    c = c_ref[...]
    cond = c * jax.nn.sigmoid(c)
    o_ref[...] = jnp.dot(cond, w_ref[...], preferred_element_type=F32,
                         precision=lax.Precision.HIGHEST) + b_ref[...]


def _ada_call(c, w_ada, b_ada):
    B, D = c.shape
    N = w_ada.shape[1]
    tn = 1024
    return pl.pallas_call(
        _ada_kernel,
        grid=(N // tn,),
        in_specs=[pl.BlockSpec((B, D), lambda j: (0, 0)),
                  pl.BlockSpec((D, tn), lambda j: (0, j)),
                  pl.BlockSpec((1, tn), lambda j: (0, j))],
        out_specs=pl.BlockSpec((B, tn), lambda j: (0, j)),
        out_shape=jax.ShapeDtypeStruct((B, N), F32),
        compiler_params=pltpu.CompilerParams(dimension_semantics=("arbitrary",),
                                             vmem_limit_bytes=VMEM_LIMIT),
        name="adaln",
    )(c, w_ada, b_ada.reshape(1, N))


def _inproj_kernel(x_ref, shift_ref, scale_ref, n1g_ref, w_ref, sgug_ref, qg_ref, kg_ref,
                   u_ref, v_ref, q_ref, k_ref, va_ref, ga_ref, gb_ref):
    x = x_ref[...]
    h = _rms(x, n1g_ref[...]) * (1.0 + scale_ref[0]) + shift_ref[0]
    h = h.astype(BF16)

    def proj(i):
        return jnp.dot(h, w_ref[:, i * D_MODEL:(i + 1) * D_MODEL], preferred_element_type=F32)

    u_ref[...] = _gelu_tanh(proj(0)).astype(BF16)
    v_ref[...] = _rms(_gelu_tanh(proj(1)), sgug_ref[...]).astype(BF16)

    q = proj(2)
    qscale = ATT_HEAD_DIM ** -0.5
    for hd in range(ATT_HEADS):
        sl = slice(hd * ATT_HEAD_DIM, (hd + 1) * ATT_HEAD_DIM)
        q_ref[0, hd] = (_rms(q[:, sl], qg_ref[...]) * qscale).astype(BF16)
    k = proj(3)
    for hd in range(ATT_HEADS):
        sl = slice(hd * ATT_HEAD_DIM, (hd + 1) * ATT_HEAD_DIM)
        k_ref[0, hd] = _rms(k[:, sl], kg_ref[...]).astype(BF16)
    ga_ref[...] = jax.nn.sigmoid(proj(5)).astype(BF16)
    gb_ref[...] = jax.nn.sigmoid(proj(6)).astype(BF16)
    va = proj(4)
    for hd in range(ATT_HEADS):
        sl = slice(hd * ATT_HEAD_DIM, (hd + 1) * ATT_HEAD_DIM)
        va_ref[0, hd] = va[:, sl].astype(BF16)


def _inproj_call(x2, shift1, scale1, n1g, w_in, sgug, qg, kg, B, S, tm):
    T, D = x2.shape
    tpb = S // tm
    tok = pl.BlockSpec((tm, D), lambda i: (i, 0))
    per_b = pl.BlockSpec((1, 1, D), lambda i: (i // tpb, 0, 0))
    row = lambda n: pl.BlockSpec((1, n), lambda i: (0, 0))
    head = pl.BlockSpec((1, ATT_HEADS, tm, ATT_HEAD_DIM), lambda i: (i // tpb, 0, i % tpb, 0))
    tok_sds = jax.ShapeDtypeStruct((T, D), BF16)
    head_sds = jax.ShapeDtypeStruct((B, ATT_HEADS, S, ATT_HEAD_DIM), BF16)
    return pl.pallas_call(
        _inproj_kernel,
        grid=(T // tm,),
        in_specs=[tok, per_b, per_b, row(D),
                  pl.BlockSpec((D, N_IN_SPLITS * D), lambda i: (0, 0), pipeline_mode=pl.Buffered(1)),
                  row(D), row(ATT_HEAD_DIM), row(ATT_HEAD_DIM)],
        out_specs=[tok, tok, head, head, head, tok, tok],
        out_shape=[tok_sds, tok_sds, head_sds, head_sds, head_sds, tok_sds, tok_sds],
        compiler_params=pltpu.CompilerParams(dimension_semantics=("parallel",),
                                             vmem_limit_bytes=VMEM_LIMIT),
        name="inproj",
    )(x2, shift1, scale1, n1g, w_in, sgug, qg, kg)


ATT_BLK = 128
STRIDED_BLK = 256
HEADS_PER_STEP = 4
RESIDUES = 4
LOCAL_PATTERNS = tuple(p for p in DILATED_PATTERNS if p[1] % RESIDUES != 0)
STRIDED_PATTERNS = tuple(p for p in DILATED_PATTERNS if p[1] % RESIDUES == 0)
assert all(d == 1 and w <= ATT_BLK for w, d in LOCAL_PATTERNS)


def _log_count_bias(count):
    return jnp.asarray(np.where(count > 0, np.log(np.maximum(count, 1)), NEG_INF), F32)


def _local_bias():
    a = np.arange(ATT_BLK)[:, None]
    col = np.arange(2 * ATT_BLK)[None, :]
    delta = ATT_BLK + a - col
    count = np.zeros(delta.shape, np.int64)
    for window, _ in LOCAL_PATTERNS:
        count += (delta >= 0) & (delta <= window)
    return _log_count_bias(count)


def _strided_bias(n):
    nblk = n // STRIDED_BLK
    a = np.arange(STRIDED_BLK)[:, None]
    col = np.arange(n)[None, :]
    delta = (STRIDED_BLK * (nblk - 1 - col // STRIDED_BLK) + a - col % STRIDED_BLK) * RESIDUES
    count = np.zeros(delta.shape, np.int64)
    for window, dilation in STRIDED_PATTERNS:
        count += (delta >= 0) & (delta <= window) & (delta % dilation == 0)
    return _log_count_bias(count)


def _qk(q, k):
    return lax.dot_general(q, k, (((1,), (1,)), ((), ())), preferred_element_type=F32)


def _attention_kernel(q_ref, k_ref, v_ref, lbias_ref, sbias_ref, o_ref,
                      stage, q4, k4, v4, ve, acc_scr, m_scr, l_scr):
    S = q_ref.shape[2]
    n = S // RESIDUES
    nblk_s = n // STRIDED_BLK
    nblk = S // ATT_BLK
    Dh = ATT_HEAD_DIM
    ones = jnp.ones((S, LANES), BF16)

    def relayout(hd):
        ve[hd, :, Dh:] = ones
        ve[hd, :, :Dh] = v_ref[0, hd]
        for r in range(RESIDUES):
            v4[hd, r, :, Dh:] = ones[:n]
        for src, dst in ((q_ref, q4), (k_ref, k4), (v_ref, v4)):
            stage[hd] = src[0, hd].astype(F32)
            for r in range(RESIDUES):
                dst[hd, r, :, :Dh] = stage[hd, pl.ds(r, n, stride=RESIDUES), :].astype(BF16)

    def strided_block(hd, r, jb):
        nk = (jb + 1) * STRIDED_BLK

        def scores():
            s = _qk(q4[hd, r, jb * STRIDED_BLK:(jb + 1) * STRIDED_BLK, :], k4[hd, r, :nk, :])
            return s + sbias_ref[:, (nblk_s - 1 - jb) * STRIDED_BLK:]

        def rest(s):
            m = jnp.max(s, axis=-1, keepdims=True)
            p = jnp.exp(s - m)
            acc = jnp.dot(p.astype(BF16), v4[hd, r, :nk, :], preferred_element_type=F32)
            rows = pl.ds(RESIDUES * jb * STRIDED_BLK + r, STRIDED_BLK, stride=RESIDUES)
            acc_scr[hd, rows, :] = acc[:, :Dh]
            m_scr[hd, rows, :] = jnp.broadcast_to(m, (STRIDED_BLK, LANES))
            l_scr[hd, rows, :] = acc[:, Dh:]

        return scores, rest

    def local_block(hd, c):
        k0 = max(c - 1, 0) * ATT_BLK
        nk = (c + 1) * ATT_BLK - k0
        blk = slice(c * ATT_BLK, (c + 1) * ATT_BLK)

        def scores():
            s = _qk(q_ref[0, hd, blk, :], k_ref[0, hd, k0:k0 + nk, :])
            return s + lbias_ref[:, 2 * ATT_BLK - nk:]

        def rest(s):
            m_s = m_scr[hd, blk, :]
            m = jnp.maximum(jnp.max(s, axis=-1, keepdims=True), m_s)
            w = jnp.exp(m_s - m)
            p = jnp.exp(s - jnp.concatenate([m] * (nk // LANES), axis=-1))
            acc = jnp.dot(p.astype(BF16), ve[hd, k0:k0 + nk, :], preferred_element_type=F32)
            l = acc[:, Dh:] + l_scr[hd, blk, :] * w
            o_ref[0, hd, blk, :] = ((acc[:, :Dh] + acc_scr[hd, blk, :] * w) / l).astype(BF16)

        return scores, rest

    for hd in range(HEADS_PER_STEP):
        relayout(hd)
        blocks = [strided_block(hd, r, jb) for r in range(RESIDUES) for jb in range(nblk_s)]
        blocks += [local_block(hd, c) for c in range(nblk)]
        s_next = blocks[0][0]()
        for b, (_, rest) in enumerate(blocks):
            s_cur = s_next
            if b + 1 < len(blocks):
                s_next = blocks[b + 1][0]()
            rest(s_cur)


def _attention_call(q, k, v):
    B, H, S, Dh = q.shape
    n = S // RESIDUES
    P = HEADS_PER_STEP
    blk = pl.BlockSpec((1, P, S, Dh), lambda b, h: (b, h, 0, 0))
    return pl.pallas_call(
        _attention_kernel,
        grid=(B, H // P),
        in_specs=[blk, blk, blk,
                  pl.BlockSpec((ATT_BLK, 2 * ATT_BLK), lambda b, h: (0, 0)),
                  pl.BlockSpec((STRIDED_BLK, n), lambda b, h: (0, 0))],
        out_specs=blk,
        out_shape=jax.ShapeDtypeStruct((B, H, S, Dh), BF16),
        scratch_shapes=[pltpu.VMEM((P, S, Dh), F32),
                        pltpu.VMEM((P, RESIDUES, n, Dh), BF16),
                        pltpu.VMEM((P, RESIDUES, n, Dh), BF16),
                        pltpu.VMEM((P, RESIDUES, n, Dh + LANES), BF16),
                        pltpu.VMEM((P, S, Dh + LANES), BF16),
                        pltpu.VMEM((P, S, Dh), F32),
                        pltpu.VMEM((P, S, LANES), F32),
                        pltpu.VMEM((P, S, LANES), F32)],
        compiler_params=pltpu.CompilerParams(dimension_semantics=("parallel", "parallel"),
                                             vmem_limit_bytes=VMEM_LIMIT),
        name="attention",
    )(q, k, v, _local_bias(), _strided_bias(n))


def _route(logits):
    tm = logits.shape[0]
    lane = lax.broadcasted_iota(jnp.int32, logits.shape, 1)
    lanef = lane.astype(F32)

    def masked_top(mask):
        top = jnp.max(jnp.where(mask, logits, -jnp.inf), axis=-1, keepdims=True)
        idx = jnp.min(jnp.where(mask & (logits == top), lanef, float(LANES)), axis=-1, keepdims=True)
        return top, idx

    gmask = lane < N_GROUPS
    gmax, gidx = masked_top(gmask)
    p_group = 1.0 / jnp.sum(jnp.where(gmask, jnp.exp(logits - gmax), 0.0), axis=-1, keepdims=True)
    e_lo = EXPERT_LANE0 + gidx * EXPERTS_PER_GROUP
    emask = (lanef >= e_lo) & (lanef < e_lo + EXPERTS_PER_GROUP)
    v1, i1 = masked_top(emask)
    v2, i2 = masked_top(emask & (lanef != i1))
    e21 = jnp.exp(v2 - v1)
    p1 = p_group / (1.0 + e21)
    p2 = p_group * e21 / (1.0 + e21)
    first_is_lo = i1 < i2
    lo = jnp.minimum(i1, i2) - e_lo
    hi = jnp.maximum(i1, i2) - e_lo
    pair = lo * (7.0 - lo) * 0.5 + (hi - lo - 1.0)
    cls = gidx * float(len(PAIRS)) + pair
    p_lo = jnp.where(first_is_lo, p1, p2)
    p_hi = jnp.where(first_is_lo, p2, p1)

    onehot = lanef == cls
    onehot_b = jnp.where(onehot, 1.0, 0.0).astype(BF16)
    r = lax.broadcasted_iota(jnp.int32, (tm, tm), 0)
    c = lax.broadcasted_iota(jnp.int32, (tm, tm), 1)
    before = jnp.where(c < r, 1.0, 0.0).astype(BF16)
    rank = jnp.dot(before, onehot_b, preferred_element_type=F32)
    counts = jnp.sum(jnp.where(onehot, 1.0, 0.0), axis=0, keepdims=True)
    units = jnp.ceil(counts * (1.0 / SUBLANES))
    ur = lax.broadcasted_iota(jnp.int32, (LANES, LANES), 0)
    uc = lax.broadcasted_iota(jnp.int32, (LANES, LANES), 1)
    upper = jnp.where(ur < uc, 1.0, 0.0).astype(BF16)
    start = jnp.dot(jnp.broadcast_to(units, (SUBLANES, LANES)).astype(BF16), upper,
                    preferred_element_type=F32)[0:1] * float(SUBLANES)
    dest = jnp.sum(jnp.where(onehot, start + rank, 0.0), axis=-1, keepdims=True)

    def hi_part(p):
        return p.astype(BF16).astype(F32)

    info = jnp.where(lane == INFO_DEST, dest, 0.0)
    info = jnp.where(lane == INFO_PLO_H, hi_part(p_lo), info)
    info = jnp.where(lane == INFO_PHI_H, hi_part(p_hi), info)
    info = jnp.where(lane == INFO_PLO_L, p_lo - hi_part(p_lo), info)
    info = jnp.where(lane == INFO_PHI_L, p_hi - hi_part(p_hi), info)
    return info, counts


def _merge_kernel(x_ref, u_ref, v_ref, o_ref, ga_ref, gb_ref, gate1_ref, shift2_ref, scale2_ref,
                  n2g_ref, sw_ref, sb_ref, wa_ref, wb_ref, wo_ref, wr_ref, br_ref,
                  x1_ref, h2_ref, info_ref, counts_ref, s_scr, logit_scr):
    tm = x_ref.shape[0]
    step = pl.program_id(0)

    @pl.when(step == 0)
    def _():
        logit_scr[...] = jnp.zeros_like(logit_scr)

    prev_logits = logit_scr[1 - step % 2]
    row = lax.broadcasted_iota(jnp.int32, (CHUNK, CHUNK), 0)
    colm = lax.broadcasted_iota(jnp.int32, (CHUNK, CHUNK), 1)
    causal = colm <= row
    for g in range(SGU_GROUPS):
        w = jnp.where(causal, sw_ref[g], 0.0).astype(BF16)
        gs = slice(g * CHUNK, (g + 1) * CHUNK)
        chunks = [slice(c * CHUNK, (c + 1) * CHUNK) for c in range(tm // CHUNK)]
        mixed = jnp.dot(w, jnp.concatenate([v_ref[cs, gs] for cs in chunks], axis=1),
                        preferred_element_type=F32)
        for cs in chunks:
            s_scr[cs, gs] = (u_ref[cs, gs].astype(F32) * (mixed[:, cs] + sb_ref[:, gs])).astype(BF16)

    y_a = jnp.dot(s_scr[...], wa_ref[...], preferred_element_type=F32)
    info, counts = _route(prev_logits)
    info_ref[...] = info
    counts_ref[0] = counts
    o = jnp.concatenate([o_ref[0, hd] for hd in range(ATT_HEADS)], axis=-1)
    y_b = jnp.dot(o, wb_ref[...], preferred_element_type=F32)
    merged = ga_ref[...].astype(F32) * y_a + gb_ref[...].astype(F32) * y_b
    y = jnp.dot(merged.astype(BF16), wo_ref[...], preferred_element_type=F32)
    x1 = x_ref[...] + gate1_ref[0] * y
    x1_ref[...] = x1

    h2 = _rms(x1, n2g_ref[...]) * (1.0 + scale2_ref[0]) + shift2_ref[0]
    h2_ref[...] = h2.astype(BF16)
    h_hi = h2.astype(BF16)
    h_lo = (h2 - h_hi.astype(F32)).astype(BF16)
    r_hi = jnp.dot(h_hi, wr_ref[...], preferred_element_type=F32)
    r_lo = jnp.dot(h_lo, wr_ref[:, :LANES], preferred_element_type=F32)
    logit_scr[step % 2] = r_hi[:, :LANES] + r_hi[:, LANES:] + r_lo + br_ref[...]


def _merge_call(x2, u, v, o, ga, gb, gate1, shift2, scale2, n2g, sgu_w, sgu_bias, wa, wb, wo,
                w_router, b_router, B, S):
    T, D = x2.shape
    tm = TOKEN_TILE
    tpb = S // tm
    n_tiles = T // tm
    tile = lambda i: jnp.minimum(i, n_tiles - 1)
    routed = lambda i: jnp.maximum(i - 1, 0)
    tok = pl.BlockSpec((tm, D), lambda i: (tile(i), 0))
    per_b = pl.BlockSpec((1, 1, D), lambda i: (tile(i) // tpb, 0, 0))
    head = pl.BlockSpec((1, ATT_HEADS, tm, ATT_HEAD_DIM),
                        lambda i: (tile(i) // tpb, 0, tile(i) % tpb, 0))
    full = lambda *shape: pl.BlockSpec(shape, lambda i: (0,) * len(shape))
    return pl.pallas_call(
        _merge_kernel,
        grid=(n_tiles + 1,),
        in_specs=[tok, tok, tok, head, tok, tok, per_b, per_b, per_b, full(1, D),
                  full(SGU_GROUPS, CHUNK, CHUNK), full(CHUNK, D),
                  full(D, D), full(D, D), full(D, D), full(D, 2 * LANES), full(1, LANES)],
        out_specs=[tok, tok, pl.BlockSpec((tm, LANES), lambda i: (routed(i), 0)),
                   pl.BlockSpec((1, 1, LANES), lambda i: (routed(i), 0, 0))],
        out_shape=[jax.ShapeDtypeStruct((T, D), F32), jax.ShapeDtypeStruct((T, D), BF16),
                   jax.ShapeDtypeStruct((T, LANES), F32),
                   jax.ShapeDtypeStruct((n_tiles, 1, LANES), F32)],
        scratch_shapes=[pltpu.VMEM((tm, D), BF16), pltpu.VMEM((2, tm, LANES), F32)],
        compiler_params=pltpu.CompilerParams(dimension_semantics=("arbitrary",),
                                             vmem_limit_bytes=VMEM_LIMIT),
        name="merge",
    )(x2, u, v, o, ga, gb, gate1, shift2, scale2, n2g, sgu_w, sgu_bias, wa, wb, wo,
      w_router, b_router)


def _dispatch_plan(counts, n_ffn_tiles):
    units = (counts + SUBLANES - 1) // SUBLANES
    rows = units * SUBLANES
    local_start = jnp.cumsum(rows, axis=1) - rows
    class_rows = jnp.sum(rows, axis=0)
    class_tiles = (class_rows + FFN_TILE - 1) // FFN_TILE
    cum_tiles = jnp.cumsum(class_tiles)
    seg_start = (cum_tiles - class_tiles) * FFN_TILE
    run_start = seg_start[None, :] + jnp.cumsum(rows, axis=0) - rows
    n_used = cum_tiles[-1]
    tile_idx = jnp.maximum(jnp.minimum(jnp.arange(n_ffn_tiles, dtype=jnp.int32), n_used - 1), 0)
    tile_cls = jnp.sum((tile_idx[:, None] >= cum_tiles[None, :]).astype(jnp.int32), axis=1)
    tile_no = jnp.arange(n_ffn_tiles, dtype=jnp.int32)
    first_tile = (cum_tiles - class_tiles)[tile_cls]
    tile_rows = jnp.clip(class_rows[tile_cls] - (tile_no - first_tile) * FFN_TILE, 0, FFN_TILE)
    tile_rows = jnp.where(tile_no < n_used, tile_rows, 0)
    pair = tile_cls % len(PAIRS)
    group = tile_cls // len(PAIRS)
    pairs = jnp.asarray(PAIRS, jnp.int32)
    i32 = lambda a: a.astype(jnp.int32).reshape(-1)

    return dict(
        run_start=i32(run_start), local_start=i32(local_start), units=i32(units),
        tile_units=i32(jnp.sum(units, axis=1)),
        tail_start=i32(seg_start + class_rows),
        tail_units=i32((class_tiles * FFN_TILE - class_rows) // SUBLANES),
        tile_idx=i32(tile_idx), n_used=i32(n_used), tile_rows=i32(tile_rows),
        tile_ea=i32(group * EXPERTS_PER_GROUP + pairs[pair, 0]),
        tile_eb=i32(group * EXPERTS_PER_GROUP + pairs[pair, 1]))


def _rows(units):
    return pl.multiple_of(units * SUBLANES, SUBLANES)


def _start_runs(tile, run_start, local_start, units, make_copy):
    def per_class(c, carry):
        k = tile * N_CLASSES + c

        @pl.when(units[k] > 0)
        def _():
            make_copy(pl.multiple_of(local_start[k], SUBLANES),
                      pl.multiple_of(run_start[k], SUBLANES), _rows(units[k])).start()

        return carry

    lax.fori_loop(0, N_CLASSES, per_class, 0)


def _wait_runs(tile, tile_units, make_copy):
    make_copy(0, 0, _rows(tile_units[tile])).wait()


def _dispatch_kernel(run_start, local_start, units, tile_units, tail_start, tail_units, n_used,
                     h2_ref, info_ref, hs_ref, sorted_scr, zero_scr, sems, tail_sem, unused_sem):
    i = pl.program_id(0)
    nt = pl.num_programs(0)
    slot = i % 2
    n_ffn_tiles = hs_ref.shape[0] // FFN_TILE

    def run_copy(s):
        def make(local_row, global_row, rows):
            return pltpu.make_async_copy(sorted_scr.at[s, pl.ds(local_row, rows), :],
                                         hs_ref.at[pl.ds(global_row, rows), :], sems.at[s])
        return make

    def tail_copy(c):
        rows = _rows(tail_units[c])
        return pltpu.make_async_copy(
            zero_scr.at[pl.ds(0, rows), :],
            hs_ref.at[pl.ds(pl.multiple_of(tail_start[c], SUBLANES), rows), :], tail_sem)

    def unused_copy(t):
        return pltpu.make_async_copy(
            zero_scr, hs_ref.at[pl.ds(pl.multiple_of(t * FFN_TILE, FFN_TILE), FFN_TILE), :],
            unused_sem)

    def for_slot(fn):
        for s in range(2):
            pl.when(slot == s)(lambda s=s: fn(s))

    def for_tails(fn):
        def per_class(c, carry):
            pl.when(tail_units[c] > 0)(lambda: fn(c))
            return carry

        lax.fori_loop(0, N_CLASSES, per_class, 0)

    @pl.when(i >= 2)
    def _():
        for_slot(lambda s: _wait_runs(i - 2, tile_units, run_copy(s)))

    @pl.when(i == 0)
    def _():
        zero_scr[...] = jnp.zeros_like(zero_scr)
        for_tails(lambda c: tail_copy(c).start())

        def per_unused(t, carry):
            unused_copy(t).start()
            return carry

        lax.fori_loop(n_used[0], n_ffn_tiles, per_unused, 0)

    info = info_ref[...]
    dest_row = info.T[INFO_DEST:INFO_DEST + 1, :]
    rows = lax.broadcasted_iota(jnp.int32, (SORT_ROWS, TOKEN_TILE), 0).astype(F32)
    perm = jnp.where(rows == dest_row, 1.0, 0.0).astype(BF16)
    lane = lax.broadcasted_iota(jnp.int32, info.shape, 1)
    weights = jnp.where(lane == INFO_DEST, 0.0, info).astype(BF16)

    def sort_and_send(s):
        sorted_scr[s, :, :D_MODEL] = jnp.dot(perm, h2_ref[...], preferred_element_type=F32)
        sorted_scr[s, :, D_MODEL:] = jnp.dot(perm, weights, preferred_element_type=F32)
        _start_runs(i, run_start, local_start, units, run_copy(s))

    for_slot(sort_and_send)

    @pl.when(i == nt - 1)
    def _():
        for_slot(lambda s: _wait_runs(i, tile_units, run_copy(s)))

        @pl.when(nt >= 2)
        def _():
            for_slot(lambda s: _wait_runs(i - 1, tile_units, run_copy(1 - s)))

        for_tails(lambda c: tail_copy(c).wait())

        def per_unused(t, carry):
            unused_copy(0).wait()
            return carry

        lax.fori_loop(n_used[0], n_ffn_tiles, per_unused, 0)


def _dispatch_call(plan, h2, info, n_ffn_tiles):
    T, D = h2.shape
    tm = TOKEN_TILE
    grid_spec = pltpu.PrefetchScalarGridSpec(
        num_scalar_prefetch=7,
        grid=(T // tm,),
        in_specs=[pl.BlockSpec((tm, D), lambda i, *_: (i, 0)),
                  pl.BlockSpec((tm, LANES), lambda i, *_: (i, 0))],
        out_specs=pl.BlockSpec(memory_space=pl.ANY),
        scratch_shapes=[pltpu.VMEM((2, SORT_ROWS, ROW_WIDTH), F32),
                        pltpu.VMEM((FFN_TILE, ROW_WIDTH), F32),
                        pltpu.SemaphoreType.DMA((2,)),
                        pltpu.SemaphoreType.DMA(()),
                        pltpu.SemaphoreType.DMA(())])
    return pl.pallas_call(
        _dispatch_kernel,
        grid_spec=grid_spec,
        out_shape=jax.ShapeDtypeStruct((n_ffn_tiles * FFN_TILE, ROW_WIDTH), F32),
        compiler_params=pltpu.CompilerParams(dimension_semantics=("arbitrary",),
                                             vmem_limit_bytes=VMEM_LIMIT),
        name="dispatch",
    )(plan["run_start"], plan["local_start"], plan["units"], plan["tile_units"],
      plan["tail_start"], plan["tail_units"], plan["n_used"], h2, info)


def _ffn_kernel(tile_idx, tile_ea, tile_eb, tile_rows, hs_ref, wga_ref, wua_ref, wda_ref,
                wgb_ref, wub_ref, wdb_ref, ys_ref):
    rows = tile_rows[pl.program_id(0)]

    def compute(m):
        h = hs_ref[:m, :D_MODEL].astype(BF16)
        pv = hs_ref[:m, D_MODEL:]
        p_lo = pv[:, INFO_PLO_H:INFO_PLO_H + 1] + pv[:, INFO_PLO_L:INFO_PLO_L + 1]
        p_hi = pv[:, INFO_PHI_H:INFO_PHI_H + 1] + pv[:, INFO_PHI_L:INFO_PHI_L + 1]

        def hidden(wg_ref, wu_ref, p):
            a = jnp.dot(h, wg_ref[0], preferred_element_type=F32)
            b = jnp.dot(h, wu_ref[0], preferred_element_type=F32)
            return (a * jax.nn.sigmoid(a) * b * p).astype(BF16)

        he_a = hidden(wga_ref, wua_ref, p_lo)
        he_b = hidden(wgb_ref, wub_ref, p_hi)
        ys_ref[:m, :] = (jnp.dot(he_a, wda_ref[0], preferred_element_type=F32)
                         + jnp.dot(he_b, wdb_ref[0], preferred_element_type=F32))
        if m < FFN_TILE:
            ys_ref[m:, :] = jnp.zeros((FFN_TILE - m, D_MODEL), F32)

    @pl.when(rows == 0)
    def _():
        ys_ref[...] = jnp.zeros_like(ys_ref)

    for m in range(FFN_ROW_STEP, FFN_TILE + 1, FFN_ROW_STEP):
        pl.when((rows > m - FFN_ROW_STEP) & (rows <= m))(lambda m=m: compute(m))


def _ffn_call(plan, hs, wg, wu, wd, n_ffn_tiles):
    D = D_MODEL
    rows = pl.BlockSpec((FFN_TILE, ROW_WIDTH), lambda t, idx, ea, eb, n: (idx[t], 0))
    w_in_a = pl.BlockSpec((1, D, D_EXPERT), lambda t, idx, ea, eb, n: (ea[t], 0, 0))
    w_out_a = pl.BlockSpec((1, D_EXPERT, D), lambda t, idx, ea, eb, n: (ea[t], 0, 0))
    w_in_b = pl.BlockSpec((1, D, D_EXPERT), lambda t, idx, ea, eb, n: (eb[t], 0, 0))
    w_out_b = pl.BlockSpec((1, D_EXPERT, D), lambda t, idx, ea, eb, n: (eb[t], 0, 0))
    grid_spec = pltpu.PrefetchScalarGridSpec(
        num_scalar_prefetch=4,
        grid=(n_ffn_tiles,),
        in_specs=[rows, w_in_a, w_in_a, w_out_a, w_in_b, w_in_b, w_out_b],
        out_specs=pl.BlockSpec((FFN_TILE, D), lambda t, idx, ea, eb, n: (t, 0)))
    return pl.pallas_call(
        _ffn_kernel,
        grid_spec=grid_spec,
        out_shape=jax.ShapeDtypeStruct((n_ffn_tiles * FFN_TILE, D), F32),
        compiler_params=pltpu.CompilerParams(dimension_semantics=("arbitrary",),
                                             vmem_limit_bytes=VMEM_LIMIT),
        name="ffn",
    )(plan["tile_idx"], plan["tile_ea"], plan["tile_eb"], plan["tile_rows"],
      hs, wg, wu, wd, wg, wu, wd)


def _combine_kernel(run_start, local_start, units, tile_units,
                    x1_ref, info_ref, gate2_ref, ys_ref, o_ref, ybuf, sems):
    i = pl.program_id(0)
    nt = pl.num_programs(0)
    slot = i % 2

    def run_copy(s):
        def make(local_row, global_row, rows):
            return pltpu.make_async_copy(ys_ref.at[pl.ds(global_row, rows), :],
                                         ybuf.at[s, pl.ds(local_row, rows), :], sems.at[s])
        return make

    def for_slot(fn):
        for s in range(2):
            pl.when(slot == s)(lambda s=s: fn(s))

    def fetch(tile, s):
        _start_runs(tile, run_start, local_start, units, run_copy(s))

    @pl.when(i == 0)
    def _():
        ybuf[...] = jnp.zeros_like(ybuf)
        fetch(0, 0)

    @pl.when(i + 1 < nt)
    def _():
        for_slot(lambda s: fetch(i + 1, 1 - s))

    dest_col = info_ref[:, INFO_DEST:INFO_DEST + 1]
    cols = lax.broadcasted_iota(jnp.int32, (TOKEN_TILE, SORT_ROWS), 1).astype(F32)
    unperm = jnp.where(cols == dest_col, 1.0, 0.0).astype(BF16)

    def finish(s):
        _wait_runs(i, tile_units, run_copy(s))
        y = jnp.dot(unperm, ybuf[s].astype(BF16), preferred_element_type=F32)
        o_ref[...] = x1_ref[...] + gate2_ref[0] * y

    for_slot(finish)


def _combine_call(plan, x1, info, gate2, ys, B, S):
    T, D = x1.shape
    tm = TOKEN_TILE
    tpb = S // tm
    grid_spec = pltpu.PrefetchScalarGridSpec(
        num_scalar_prefetch=4,
        grid=(T // tm,),
        in_specs=[pl.BlockSpec((tm, D), lambda i, *_: (i, 0)),
                  pl.BlockSpec((tm, LANES), lambda i, *_: (i, 0)),
                  pl.BlockSpec((1, 1, D), lambda i, *_: (i // tpb, 0, 0)),
                  pl.BlockSpec(memory_space=pl.ANY)],
        out_specs=pl.BlockSpec((tm, D), lambda i, *_: (i, 0)),
        scratch_shapes=[pltpu.VMEM((2, SORT_ROWS, D), F32),
                        pltpu.SemaphoreType.DMA((2,))])
    return pl.pallas_call(
        _combine_kernel,
        grid_spec=grid_spec,
        out_shape=jax.ShapeDtypeStruct((T, D), F32),
        compiler_params=pltpu.CompilerParams(dimension_semantics=("arbitrary",),
                                             vmem_limit_bytes=VMEM_LIMIT),
        name="combine",
    )(plan["run_start"], plan["local_start"], plan["units"], plan["tile_units"],
      x1, info, gate2, ys)


def kernel(x, c, w_ada, b_ada, norm1_g, w_in, sgu_norm_g, sgu_w, sgu_b, q_norm_g, k_norm_g,
           w_proj_a, w_proj_b, w_out, norm2_g, w_router_group, b_router_group,
           w_router_expert, b_router_expert, w_gate, w_up, w_down):
    B, S, D = x.shape
    T = B * S
    depth = w_ada.shape[0]
    n_token_tiles = T // TOKEN_TILE
    max_rows = T + n_token_tiles * N_CLASSES * (SUBLANES - 1) + N_CLASSES * (FFN_TILE - SUBLANES)
    n_ffn_tiles = -(-max_rows // FFN_TILE)
    x2 = x.reshape(T, D)
    for l in range(depth):
        mod = _ada_call(c, w_ada[l], b_ada[l])
        shift1, scale1, gate1, shift2, scale2, gate2 = [
            m.reshape(B, 1, D) for m in jnp.split(mod, 6, axis=-1)]

        u, v, q, k, va, ga, gb = _inproj_call(
            x2, shift1, scale1, norm1_g[l].reshape(1, D), w_in[l].astype(BF16),
            sgu_norm_g[l].reshape(1, D), q_norm_g[l].reshape(1, ATT_HEAD_DIM),
            k_norm_g[l].reshape(1, ATT_HEAD_DIM), B, S, tm=512)

        o = _attention_call(q, k, va)

        w_router = jnp.concatenate(
            [w_router_group[l],
             jnp.transpose(w_router_expert[l], (1, 0, 2)).reshape(D, N_EXPERTS)], axis=1)
        w_router = jnp.pad(w_router, ((0, 0), (0, LANES - w_router.shape[1])))
        w_router_hi = w_router.astype(BF16)
        w_router_lo = (w_router - w_router_hi.astype(F32)).astype(BF16)
        w_router = jnp.concatenate([w_router_hi, w_router_lo], axis=1)
        b_router = jnp.concatenate([b_router_group[l], b_router_expert[l].reshape(N_EXPERTS)])
        b_router = jnp.pad(b_router, (0, LANES - b_router.shape[0])).reshape(1, LANES)
        sgu_bias = jnp.repeat(sgu_b[l].T, D // SGU_GROUPS, axis=1)

        x1, h2, info, counts = _merge_call(
            x2, u, v, o, ga, gb, gate1, shift2, scale2, norm2_g[l].reshape(1, D), sgu_w[l],
            sgu_bias, w_proj_a[l].astype(BF16), w_proj_b[l].astype(BF16), w_out[l].astype(BF16),
            w_router, b_router, B, S)

        plan = _dispatch_plan(counts[:, 0, :N_CLASSES].astype(jnp.int32), n_ffn_tiles)
        hs = _dispatch_call(plan, h2, info, n_ffn_tiles)
        ys = _ffn_call(plan, hs, w_gate[l], w_up[l], w_down[l], n_ffn_tiles)
        x2 = _combine_call(plan, x1, info, gate2, ys, B, S)
    return x2.reshape(B, S, D)
```

```python
import math

import jax
import jax.numpy as jnp
import numpy as np
from jax import lax
from jax.experimental import pallas as pl
from jax.experimental.pallas import tpu as pltpu

D_MODEL = 1024
CHUNK = 128
SGU_GROUPS = 8
ATT_HEADS = 8
ATT_HEAD_DIM = 128
DILATED_PATTERNS = ((128, 1), (512, 4), (2048, 16))
N_GROUPS = 4
EXPERTS_PER_GROUP = 4
N_EXPERTS = N_GROUPS * EXPERTS_PER_GROUP
D_EXPERT = 512
EPS = 1e-6
NEG_INF = -1e30

N_IN_SPLITS = 7
LANES = 128
SUBLANES = 8
EXPERT_LANE0 = N_GROUPS

PAIRS = ((0, 1), (0, 2), (0, 3), (1, 2), (1, 3), (2, 3))
N_CLASSES = N_GROUPS * len(PAIRS)
TOKEN_TILE = 512
SORT_ROWS = TOKEN_TILE + 3 * 64
ROW_WIDTH = D_MODEL + LANES
FFN_TILE = 512
FFN_ROW_STEP = 128
INFO_DEST, INFO_PLO_H, INFO_PHI_H, INFO_PLO_L, INFO_PHI_L = 0, 1, 2, 3, 4

VMEM_LIMIT = 56 * 1024 * 1024
BF16 = jnp.bfloat16
F32 = jnp.float32

assert SORT_ROWS >= TOKEN_TILE + N_CLASSES * (SUBLANES - 1)


def _rms(x, g):
    return x * lax.rsqrt(jnp.mean(x * x, axis=-1, keepdims=True) + EPS) * g


def _gelu_tanh(x):
    k = -2.0 * math.sqrt(2.0 / math.pi) * math.log2(math.e)
    return x / (1.0 + jnp.exp2(x * (k + (k * 0.044715) * (x * x))))


def _ada_kernel(c_ref, w_ref, b_ref, o_ref):
    c = c_ref[...]
    cond = c * jax.nn.sigmoid(c)
    o_ref[...] = jnp.dot(cond, w_ref[...], preferred_element_type=F32,
                         precision=lax.Precision.HIGHEST) + b_ref[...]


def _ada_call(c, w_ada, b_ada):
    B, D = c.shape
    N = w_ada.shape[1]
    tn = 1024
    return pl.pallas_call(
        _ada_kernel,
        grid=(N // tn,),
        in_specs=[pl.BlockSpec((B, D), lambda j: (0, 0)),
                  pl.BlockSpec((D, tn), lambda j: (0, j)),
                  pl.BlockSpec((1, tn), lambda j: (0, j))],
        out_specs=pl.BlockSpec((B, tn), lambda j: (0, j)),
        out_shape=jax.ShapeDtypeStruct((B, N), F32),
        compiler_params=pltpu.CompilerParams(dimension_semantics=("arbitrary",),
                                             vmem_limit_bytes=VMEM_LIMIT),
        name="adaln",
    )(c, w_ada, b_ada.reshape(1, N))


def _inproj_kernel(x_ref, shift_ref, scale_ref, n1g_ref, w_ref, sgug_ref, qg_ref, kg_ref,
                   u_ref, v_ref, q_ref, k_ref, va_ref, ga_ref, gb_ref):
    x = x_ref[...]
    h = _rms(x, n1g_ref[...]) * (1.0 + scale_ref[0]) + shift_ref[0]
    h = h.astype(BF16)

    def proj(i):
        return jnp.dot(h, w_ref[:, i * D_MODEL:(i + 1) * D_MODEL], preferred_element_type=F32)

    u_ref[...] = _gelu_tanh(proj(0)).astype(BF16)
    v_ref[...] = _rms(_gelu_tanh(proj(1)), sgug_ref[...]).astype(BF16)

    q = proj(2)
    qscale = ATT_HEAD_DIM ** -0.5
    for hd in range(ATT_HEADS):
        sl = slice(hd * ATT_HEAD_DIM, (hd + 1) * ATT_HEAD_DIM)
        q_ref[0, hd] = (_rms(q[:, sl], qg_ref[...]) * qscale).astype(BF16)
    k = proj(3)
    for hd in range(ATT_HEADS):
        sl = slice(hd * ATT_HEAD_DIM, (hd + 1) * ATT_HEAD_DIM)
        k_ref[0, hd] = _rms(k[:, sl], kg_ref[...]).astype(BF16)
    ga_ref[...] = jax.nn.sigmoid(proj(5)).astype(BF16)
    gb_ref[...] = jax.nn.sigmoid(proj(6)).astype(BF16)
    va = proj(4)
    for hd in range(ATT_HEADS):
        sl = slice(hd * ATT_HEAD_DIM, (hd + 1) * ATT_HEAD_DIM)
        va_ref[0, hd] = va[:, sl].astype(BF16)


def _inproj_call(x2, shift1, scale1, n1g, w_in, sgug, qg, kg, B, S, tm):
    T, D = x2.shape
    tpb = S // tm
    tok = pl.BlockSpec((tm, D), lambda i: (i, 0))
    per_b = pl.BlockSpec((1, 1, D), lambda i: (i // tpb, 0, 0))
    row = lambda n: pl.BlockSpec((1, n), lambda i: (0, 0))
    head = pl.BlockSpec((1, ATT_HEADS, tm, ATT_HEAD_DIM), lambda i: (i // tpb, 0, i % tpb, 0))
    tok_sds = jax.ShapeDtypeStruct((T, D), BF16)
    head_sds = jax.ShapeDtypeStruct((B, ATT_HEADS, S, ATT_HEAD_DIM), BF16)
    return pl.pallas_call(
        _inproj_kernel,
        grid=(T // tm,),
        in_specs=[tok, per_b, per_b, row(D),
                  pl.BlockSpec((D, N_IN_SPLITS * D), lambda i: (0, 0), pipeline_mode=pl.Buffered(1)),
                  row(D), row(ATT_HEAD_DIM), row(ATT_HEAD_DIM)],
        out_specs=[tok, tok, head, head, head, tok, tok],
        out_shape=[tok_sds, tok_sds, head_sds, head_sds, head_sds, tok_sds, tok_sds],
        compiler_params=pltpu.CompilerParams(dimension_semantics=("parallel",),
                                             vmem_limit_bytes=VMEM_LIMIT),
        name="inproj",
    )(x2, shift1, scale1, n1g, w_in, sgug, qg, kg)


ATT_BLK = 128
STRIDED_BLK = 256
HEADS_PER_STEP = 4
RESIDUES = 4
LOCAL_PATTERNS = tuple(p for p in DILATED_PATTERNS if p[1] % RESIDUES != 0)
STRIDED_PATTERNS = tuple(p for p in DILATED_PATTERNS if p[1] % RESIDUES == 0)
assert all(d == 1 and w <= ATT_BLK for w, d in LOCAL_PATTERNS)


def _log_count_bias(count):
    return jnp.asarray(np.where(count > 0, np.log(np.maximum(count, 1)), NEG_INF), F32)


def _local_bias():
    a = np.arange(ATT_BLK)[:, None]
    col = np.arange(2 * ATT_BLK)[None, :]
    delta = ATT_BLK + a - col
    count = np.zeros(delta.shape, np.int64)
    for window, _ in LOCAL_PATTERNS:
        count += (delta >= 0) & (delta <= window)
    return _log_count_bias(count)


def _strided_bias(n):
    nblk = n // STRIDED_BLK
    a = np.arange(STRIDED_BLK)[:, None]
    col = np.arange(n)[None, :]
    delta = (STRIDED_BLK * (nblk - 1 - col // STRIDED_BLK) + a - col % STRIDED_BLK) * RESIDUES
    count = np.zeros(delta.shape, np.int64)
    for window, dilation in STRIDED_PATTERNS:
        count += (delta >= 0) & (delta <= window) & (delta % dilation == 0)
    return _log_count_bias(count)


def _qk(q, k):
    return lax.dot_general(q, k, (((1,), (1,)), ((), ())), preferred_element_type=F32)


def _attention_kernel(q_ref, k_ref, v_ref, lbias_ref, sbias_ref, o_ref,
                      stage, q4, k4, v4, ve, acc_scr, m_scr, l_scr):
    S = q_ref.shape[2]
    n = S // RESIDUES
    nblk_s = n // STRIDED_BLK
    nblk = S // ATT_BLK
    Dh = ATT_HEAD_DIM
    ones = jnp.ones((S, LANES), BF16)

    def relayout(hd):
        ve[hd, :, Dh:] = ones
        ve[hd, :, :Dh] = v_ref[0, hd]
        for r in range(RESIDUES):
            v4[hd, r, :, Dh:] = ones[:n]
        for src, dst in ((q_ref, q4), (k_ref, k4), (v_ref, v4)):
            stage[hd] = src[0, hd].astype(F32)
            for r in range(RESIDUES):
                dst[hd, r, :, :Dh] = stage[hd, pl.ds(r, n, stride=RESIDUES), :].astype(BF16)

    def strided_block(hd, r, jb):
        nk = (jb + 1) * STRIDED_BLK

        def scores():
            s = _qk(q4[hd, r, jb * STRIDED_BLK:(jb + 1) * STRIDED_BLK, :], k4[hd, r, :nk, :])
            return s + sbias_ref[:, (nblk_s - 1 - jb) * STRIDED_BLK:]

        def rest(s):
            m = jnp.max(s, axis=-1, keepdims=True)
            p = jnp.exp(s - m)
            acc = jnp.dot(p.astype(BF16), v4[hd, r, :nk, :], preferred_element_type=F32)
            rows = pl.ds(RESIDUES * jb * STRIDED_BLK + r, STRIDED_BLK, stride=RESIDUES)
            acc_scr[hd, rows, :] = acc[:, :Dh]
            m_scr[hd, rows, :] = jnp.broadcast_to(m, (STRIDED_BLK, LANES))
            l_scr[hd, rows, :] = acc[:, Dh:]

        return scores, rest

    def local_block(hd, c):
        k0 = max(c - 1, 0) * ATT_BLK
        nk = (c + 1) * ATT_BLK - k0
        blk = slice(c * ATT_BLK, (c + 1) * ATT_BLK)

        def scores():
            s = _qk(q_ref[0, hd, blk, :], k_ref[0, hd, k0:k0 + nk, :])
            return s + lbias_ref[:, 2 * ATT_BLK - nk:]

        def rest(s):
            m_s = m_scr[hd, blk, :]
            m = jnp.maximum(jnp.max(s, axis=-1, keepdims=True), m_s)
            w = jnp.exp(m_s - m)
            p = jnp.exp(s - jnp.concatenate([m] * (nk // LANES), axis=-1))
            acc = jnp.dot(p.astype(BF16), ve[hd, k0:k0 + nk, :], preferred_element_type=F32)
            l = acc[:, Dh:] + l_scr[hd, blk, :] * w
            o_ref[0, hd, blk, :] = ((acc[:, :Dh] + acc_scr[hd, blk, :] * w) / l).astype(BF16)

        return scores, rest

    for hd in range(HEADS_PER_STEP):
        relayout(hd)
        blocks = [strided_block(hd, r, jb) for r in range(RESIDUES) for jb in range(nblk_s)]
        blocks += [local_block(hd, c) for c in range(nblk)]
        s_next = blocks[0][0]()
        for b, (_, rest) in enumerate(blocks):
            s_cur = s_next
            if b + 1 < len(blocks):
                s_next = blocks[b + 1][0]()
            rest(s_cur)


def _attention_call(q, k, v):
    B, H, S, Dh = q.shape
    n = S // RESIDUES
    P = HEADS_PER_STEP
    blk = pl.BlockSpec((1, P, S, Dh), lambda b, h: (b, h, 0, 0))
    return pl.pallas_call(
        _attention_kernel,
        grid=(B, H // P),
        in_specs=[blk, blk, blk,
                  pl.BlockSpec((ATT_BLK, 2 * ATT_BLK), lambda b, h: (0, 0)),
                  pl.BlockSpec((STRIDED_BLK, n), lambda b, h: (0, 0))],
        out_specs=blk,
        out_shape=jax.ShapeDtypeStruct((B, H, S, Dh), BF16),
        scratch_shapes=[pltpu.VMEM((P, S, Dh), F32),
                        pltpu.VMEM((P, RESIDUES, n, Dh), BF16),
                        pltpu.VMEM((P, RESIDUES, n, Dh), BF16),
                        pltpu.VMEM((P, RESIDUES, n, Dh + LANES), BF16),
                        pltpu.VMEM((P, S, Dh + LANES), BF16),
                        pltpu.VMEM((P, S, Dh), F32),
                        pltpu.VMEM((P, S, LANES), F32),
                        pltpu.VMEM((P, S, LANES), F32)],
        compiler_params=pltpu.CompilerParams(dimension_semantics=("parallel", "parallel"),
                                             vmem_limit_bytes=VMEM_LIMIT),
        name="attention",
    )(q, k, v, _local_bias(), _strided_bias(n))


def _route(logits):
    tm = logits.shape[0]
    lane = lax.broadcasted_iota(jnp.int32, logits.shape, 1)
    lanef = lane.astype(F32)

    def masked_top(mask):
        top = jnp.max(jnp.where(mask, logits, -jnp.inf), axis=-1, keepdims=True)
        idx = jnp.min(jnp.where(mask & (logits == top), lanef, float(LANES)), axis=-1, keepdims=True)
        return top, idx

    gmask = lane < N_GROUPS
    gmax, gidx = masked_top(gmask)
    p_group = 1.0 / jnp.sum(jnp.where(gmask, jnp.exp(logits - gmax), 0.0), axis=-1, keepdims=True)
    e_lo = EXPERT_LANE0 + gidx * EXPERTS_PER_GROUP
    emask = (lanef >= e_lo) & (lanef < e_lo + EXPERTS_PER_GROUP)
    v1, i1 = masked_top(emask)
    v2, i2 = masked_top(emask & (lanef != i1))
    e21 = jnp.exp(v2 - v1)
    p1 = p_group / (1.0 + e21)
    p2 = p_group * e21 / (1.0 + e21)
    first_is_lo = i1 < i2
    lo = jnp.minimum(i1, i2) - e_lo
    hi = jnp.maximum(i1, i2) - e_lo
    pair = lo * (7.0 - lo) * 0.5 + (hi - lo - 1.0)
    cls = gidx * float(len(PAIRS)) + pair
    p_lo = jnp.where(first_is_lo, p1, p2)
    p_hi = jnp.where(first_is_lo, p2, p1)

    onehot = lanef == cls
    onehot_b = jnp.where(onehot, 1.0, 0.0).astype(BF16)
    r = lax.broadcasted_iota(jnp.int32, (tm, tm), 0)
    c = lax.broadcasted_iota(jnp.int32, (tm, tm), 1)
    before = jnp.where(c < r, 1.0, 0.0).astype(BF16)
    rank = jnp.dot(before, onehot_b, preferred_element_type=F32)
    counts = jnp.sum(jnp.where(onehot, 1.0, 0.0), axis=0, keepdims=True)
    units = jnp.ceil(counts * (1.0 / SUBLANES))
    ur = lax.broadcasted_iota(jnp.int32, (LANES, LANES), 0)
    uc = lax.broadcasted_iota(jnp.int32, (LANES, LANES), 1)
    upper = jnp.where(ur < uc, 1.0, 0.0).astype(BF16)
    start = jnp.dot(jnp.broadcast_to(units, (SUBLANES, LANES)).astype(BF16), upper,
                    preferred_element_type=F32)[0:1] * float(SUBLANES)
    dest = jnp.sum(jnp.where(onehot, start + rank, 0.0), axis=-1, keepdims=True)

    def hi_part(p):
        return p.astype(BF16).astype(F32)

    info = jnp.where(lane == INFO_DEST, dest, 0.0)
    info = jnp.where(lane == INFO_PLO_H, hi_part(p_lo), info)
    info = jnp.where(lane == INFO_PHI_H, hi_part(p_hi), info)
    info = jnp.where(lane == INFO_PLO_L, p_lo - hi_part(p_lo), info)
    info = jnp.where(lane == INFO_PHI_L, p_hi - hi_part(p_hi), info)
    return info, counts


def _merge_kernel(x_ref, u_ref, v_ref, o_ref, ga_ref, gb_ref, gate1_ref, shift2_ref, scale2_ref,
                  n2g_ref, sw_ref, sb_ref, wa_ref, wb_ref, wo_ref, wr_ref, br_ref,
                  x1_ref, h2_ref, info_ref, counts_ref, s_scr, logit_scr):
    tm = x_ref.shape[0]
    step = pl.program_id(0)

    @pl.when(step == 0)
    def _():
        logit_scr[...] = jnp.zeros_like(logit_scr)

    prev_logits = logit_scr[1 - step % 2]
    row = lax.broadcasted_iota(jnp.int32, (CHUNK, CHUNK), 0)
    colm = lax.broadcasted_iota(jnp.int32, (CHUNK, CHUNK), 1)
    causal = colm <= row
    for g in range(SGU_GROUPS):
        w = jnp.where(causal, sw_ref[g], 0.0).astype(BF16)
        gs = slice(g * CHUNK, (g + 1) * CHUNK)
        chunks = [slice(c * CHUNK, (c + 1) * CHUNK) for c in range(tm // CHUNK)]
        mixed = jnp.dot(w, jnp.concatenate([v_ref[cs, gs] for cs in chunks], axis=1),
                        preferred_element_type=F32)
        for cs in chunks:
            s_scr[cs, gs] = (u_ref[cs, gs].astype(F32) * (mixed[:, cs] + sb_ref[:, gs])).astype(BF16)

    y_a = jnp.dot(s_scr[...], wa_ref[...], preferred_element_type=F32)
    info, counts = _route(prev_logits)
    info_ref[...] = info
    counts_ref[0] = counts
    o = jnp.concatenate([o_ref[0, hd] for hd in range(ATT_HEADS)], axis=-1)
    y_b = jnp.dot(o, wb_ref[...], preferred_element_type=F32)
    merged = ga_ref[...].astype(F32) * y_a + gb_ref[...].astype(F32) * y_b
    y = jnp.dot(merged.astype(BF16), wo_ref[...], preferred_element_type=F32)
    x1 = x_ref[...] + gate1_ref[0] * y
    x1_ref[...] = x1

    h2 = _rms(x1, n2g_ref[...]) * (1.0 + scale2_ref[0]) + shift2_ref[0]
    h2_ref[...] = h2.astype(BF16)
    h_hi = h2.astype(BF16)
    h_lo = (h2 - h_hi.astype(F32)).astype(BF16)
    r_hi = jnp.dot(h_hi, wr_ref[...], preferred_element_type=F32)
    r_lo = jnp.dot(h_lo, wr_ref[:, :LANES], preferred_element_type=F32)
    logit_scr[step % 2] = r_hi[:, :LANES] + r_hi[:, LANES:] + r_lo + br_ref[...]


def _merge_call(x2, u, v, o, ga, gb, gate1, shift2, scale2, n2g, sgu_w, sgu_bias, wa, wb, wo,
                w_router, b_router, B, S):
    T, D = x2.shape
    tm = TOKEN_TILE
    tpb = S // tm
    n_tiles = T // tm
    tile = lambda i: jnp.minimum(i, n_tiles - 1)
    routed = lambda i: jnp.maximum(i - 1, 0)
    tok = pl.BlockSpec((tm, D), lambda i: (tile(i), 0))
    per_b = pl.BlockSpec((1, 1, D), lambda i: (tile(i) // tpb, 0, 0))
    head = pl.BlockSpec((1, ATT_HEADS, tm, ATT_HEAD_DIM),
                        lambda i: (tile(i) // tpb, 0, tile(i) % tpb, 0))
    full = lambda *shape: pl.BlockSpec(shape, lambda i: (0,) * len(shape))
    return pl.pallas_call(
        _merge_kernel,
        grid=(n_tiles + 1,),
        in_specs=[tok, tok, tok, head, tok, tok, per_b, per_b, per_b, full(1, D),
                  full(SGU_GROUPS, CHUNK, CHUNK), full(CHUNK, D),
                  full(D, D), full(D, D), full(D, D), full(D, 2 * LANES), full(1, LANES)],
        out_specs=[tok, tok, pl.BlockSpec((tm, LANES), lambda i: (routed(i), 0)),
                   pl.BlockSpec((1, 1, LANES), lambda i: (routed(i), 0, 0))],
        out_shape=[jax.ShapeDtypeStruct((T, D), F32), jax.ShapeDtypeStruct((T, D), BF16),
                   jax.ShapeDtypeStruct((T, LANES), F32),
                   jax.ShapeDtypeStruct((n_tiles, 1, LANES), F32)],
        scratch_shapes=[pltpu.VMEM((tm, D), BF16), pltpu.VMEM((2, tm, LANES), F32)],
        compiler_params=pltpu.CompilerParams(dimension_semantics=("arbitrary",),
                                             vmem_limit_bytes=VMEM_LIMIT),
        name="merge",
    )(x2, u, v, o, ga, gb, gate1, shift2, scale2, n2g, sgu_w, sgu_bias, wa, wb, wo,
      w_router, b_router)


def _dispatch_plan(counts, n_ffn_tiles):
    units = (counts + SUBLANES - 1) // SUBLANES
    rows = units * SUBLANES
    local_start = jnp.cumsum(rows, axis=1) - rows
    class_rows = jnp.sum(rows, axis=0)
    class_tiles = (class_rows + FFN_TILE - 1) // FFN_TILE
    cum_tiles = jnp.cumsum(class_tiles)
    seg_start = (cum_tiles - class_tiles) * FFN_TILE
    run_start = seg_start[None, :] + jnp.cumsum(rows, axis=0) - rows
    n_used = cum_tiles[-1]
    tile_idx = jnp.maximum(jnp.minimum(jnp.arange(n_ffn_tiles, dtype=jnp.int32), n_used - 1), 0)
    tile_cls = jnp.sum((tile_idx[:, None] >= cum_tiles[None, :]).astype(jnp.int32), axis=1)
    tile_no = jnp.arange(n_ffn_tiles, dtype=jnp.int32)
    first_tile = (cum_tiles - class_tiles)[tile_cls]
    tile_rows = jnp.clip(class_rows[tile_cls] - (tile_no - first_tile) * FFN_TILE, 0, FFN_TILE)
    tile_rows = jnp.where(tile_no < n_used, tile_rows, 0)
    pairs = jnp.asarray(PAIRS, jnp.int32)
    i32 = lambda a: a.astype(jnp.int32).reshape(-1)

    def experts(cls):
        return (cls // len(PAIRS) * EXPERTS_PER_GROUP + pairs[cls % len(PAIRS), 0],
                cls // len(PAIRS) * EXPERTS_PER_GROUP + pairs[cls % len(PAIRS), 1])

    class_ids = jnp.arange(N_CLASSES, dtype=jnp.int32)
    has_rows = class_tiles > 0
    later = has_rows[None, :] & (class_ids[None, :] > class_ids[:, None])
    next_cls = jnp.min(jnp.where(later, class_ids[None, :], N_CLASSES), axis=1)
    class_slot = (jnp.cumsum(has_rows.astype(jnp.int32)) - 1) % 2
    tile_next = next_cls[tile_cls]
    ea, eb = experts(tile_cls)
    next_ea, next_eb = experts(jnp.minimum(tile_next, N_CLASSES - 1))
    return dict(
        run_start=i32(run_start), local_start=i32(local_start), units=i32(units),
        tile_units=i32(jnp.sum(units, axis=1)),
        tail_start=i32(seg_start + class_rows),
        tail_units=i32((class_tiles * FFN_TILE - class_rows) // SUBLANES),
        tile_idx=i32(tile_idx), n_used=i32(n_used), tile_rows=i32(tile_rows),
        tile_opens=i32((tile_no == first_tile) & (tile_no < n_used)),
        tile_slot=i32(class_slot[tile_cls]), tile_has_next=i32(tile_next < N_CLASSES),
        tile_ea=i32(ea), tile_eb=i32(eb), next_ea=i32(next_ea), next_eb=i32(next_eb))


def _rows(units):
    return pl.multiple_of(units * SUBLANES, SUBLANES)


def _start_runs(tile, run_start, local_start, units, make_copy):
    def per_class(c, carry):
        k = tile * N_CLASSES + c

        @pl.when(units[k] > 0)
        def _():
            make_copy(pl.multiple_of(local_start[k], SUBLANES),
                      pl.multiple_of(run_start[k], SUBLANES), _rows(units[k])).start()

        return carry

    lax.fori_loop(0, N_CLASSES, per_class, 0)


def _wait_runs(tile, tile_units, make_copy):
    make_copy(0, 0, _rows(tile_units[tile])).wait()


def _dispatch_kernel(run_start, local_start, units, tile_units, tail_start, tail_units, n_used,
                     h2_ref, info_ref, hs_ref, sorted_scr, zero_scr, sems, tail_sem, unused_sem):
    i = pl.program_id(0)
    nt = pl.num_programs(0)
    slot = i % 2
    n_ffn_tiles = hs_ref.shape[0] // FFN_TILE

    def run_copy(s):
        def make(local_row, global_row, rows):
            return pltpu.make_async_copy(sorted_scr.at[s, pl.ds(local_row, rows), :],
                                         hs_ref.at[pl.ds(global_row, rows), :], sems.at[s])
        return make

    def tail_copy(c):
        rows = _rows(tail_units[c])
        return pltpu.make_async_copy(
            zero_scr.at[pl.ds(0, rows), :],
            hs_ref.at[pl.ds(pl.multiple_of(tail_start[c], SUBLANES), rows), :], tail_sem)

    def unused_copy(t):
        return pltpu.make_async_copy(
            zero_scr, hs_ref.at[pl.ds(pl.multiple_of(t * FFN_TILE, FFN_TILE), FFN_TILE), :],
            unused_sem)

    def for_slot(fn):
        for s in range(2):
            pl.when(slot == s)(lambda s=s: fn(s))

    def for_tails(fn):
        def per_class(c, carry):
            pl.when(tail_units[c] > 0)(lambda: fn(c))
            return carry

        lax.fori_loop(0, N_CLASSES, per_class, 0)

    @pl.when(i >= 2)
    def _():
        for_slot(lambda s: _wait_runs(i - 2, tile_units, run_copy(s)))

    @pl.when(i == 0)
    def _():
        zero_scr[...] = jnp.zeros_like(zero_scr)
        for_tails(lambda c: tail_copy(c).start())

        def per_unused(t, carry):
            unused_copy(t).start()
            return carry

        lax.fori_loop(n_used[0], n_ffn_tiles, per_unused, 0)

    info = info_ref[...]
    dest_row = info.T[INFO_DEST:INFO_DEST + 1, :]
    rows = lax.broadcasted_iota(jnp.int32, (SORT_ROWS, TOKEN_TILE), 0).astype(F32)
    perm = jnp.where(rows == dest_row, 1.0, 0.0).astype(BF16)
    lane = lax.broadcasted_iota(jnp.int32, info.shape, 1)
    weights = jnp.where(lane == INFO_DEST, 0.0, info).astype(BF16)

    def sort_and_send(s):
        sorted_scr[s, :, :D_MODEL] = jnp.dot(perm, h2_ref[...], preferred_element_type=F32)
        sorted_scr[s, :, D_MODEL:] = jnp.dot(perm, weights, preferred_element_type=F32)
        _start_runs(i, run_start, local_start, units, run_copy(s))

    for_slot(sort_and_send)

    @pl.when(i == nt - 1)
    def _():
        for_slot(lambda s: _wait_runs(i, tile_units, run_copy(s)))

        @pl.when(nt >= 2)
        def _():
            for_slot(lambda s: _wait_runs(i - 1, tile_units, run_copy(1 - s)))

        for_tails(lambda c: tail_copy(c).wait())

        def per_unused(t, carry):
            unused_copy(0).wait()
            return carry

        lax.fori_loop(n_used[0], n_ffn_tiles, per_unused, 0)


def _dispatch_call(plan, h2, info, n_ffn_tiles):
    T, D = h2.shape
    tm = TOKEN_TILE
    grid_spec = pltpu.PrefetchScalarGridSpec(
        num_scalar_prefetch=7,
        grid=(T // tm,),
        in_specs=[pl.BlockSpec((tm, D), lambda i, *_: (i, 0)),
                  pl.BlockSpec((tm, LANES), lambda i, *_: (i, 0))],
        out_specs=pl.BlockSpec(memory_space=pl.ANY),
        scratch_shapes=[pltpu.VMEM((2, SORT_ROWS, ROW_WIDTH), F32),
                        pltpu.VMEM((FFN_TILE, ROW_WIDTH), F32),
                        pltpu.SemaphoreType.DMA((2,)),
                        pltpu.SemaphoreType.DMA(()),
                        pltpu.SemaphoreType.DMA(())])
    return pl.pallas_call(
        _dispatch_kernel,
        grid_spec=grid_spec,
        out_shape=jax.ShapeDtypeStruct((n_ffn_tiles * FFN_TILE, ROW_WIDTH), F32),
        compiler_params=pltpu.CompilerParams(dimension_semantics=("arbitrary",),
                                             vmem_limit_bytes=VMEM_LIMIT),
        name="dispatch",
    )(plan["run_start"], plan["local_start"], plan["units"], plan["tile_units"],
      plan["tail_start"], plan["tail_units"], plan["n_used"], h2, info)


def _ffn_kernel(tile_idx, tile_rows, tile_opens, tile_slot, tile_has_next,
                tile_ea, tile_eb, next_ea, next_eb,
                hs_ref, wg_hbm, wu_hbm, wd_hbm, ys_ref, wg_buf, wu_buf, wd_buf, sems):
    t = pl.program_id(0)
    rows = tile_rows[t]
    slot = tile_slot[t]

    def weight_copies(s, e_a, e_b):
        copies = []
        for j, e in enumerate((e_a, e_b)):
            copies += [pltpu.make_async_copy(wg_hbm.at[e], wg_buf.at[s, j], sems.at[s]),
                       pltpu.make_async_copy(wu_hbm.at[e], wu_buf.at[s, j], sems.at[s]),
                       pltpu.make_async_copy(wd_hbm.at[e], wd_buf.at[s, j], sems.at[s])]
        return copies

    @pl.when(t == 0)
    def _():
        for cp in weight_copies(slot, tile_ea[0], tile_eb[0]):
            cp.start()

    @pl.when(tile_opens[t] > 0)
    def _():
        @pl.when(tile_has_next[t] > 0)
        def _():
            for cp in weight_copies(1 - slot, next_ea[t], next_eb[t]):
                cp.start()

        for cp in weight_copies(slot, tile_ea[t], tile_eb[t]):
            cp.wait()

    def compute(m):
        h = hs_ref[:m, :D_MODEL].astype(BF16)
        pv = hs_ref[:m, D_MODEL:]
        p_lo = pv[:, INFO_PLO_H:INFO_PLO_H + 1] + pv[:, INFO_PLO_L:INFO_PLO_L + 1]
        p_hi = pv[:, INFO_PHI_H:INFO_PHI_H + 1] + pv[:, INFO_PHI_L:INFO_PHI_L + 1]

        def hidden(j, p):
            a = jnp.dot(h, wg_buf[slot, j].astype(BF16), preferred_element_type=F32)
            b = jnp.dot(h, wu_buf[slot, j].astype(BF16), preferred_element_type=F32)
            return (a * jax.nn.sigmoid(a) * b * p).astype(BF16)

        he_a = hidden(0, p_lo)
        he_b = hidden(1, p_hi)
        ys_ref[:m, :] = (
            jnp.dot(he_a, wd_buf[slot, 0].astype(BF16), preferred_element_type=F32)
            + jnp.dot(he_b, wd_buf[slot, 1].astype(BF16), preferred_element_type=F32))
        if m < FFN_TILE:
            ys_ref[m:, :] = jnp.zeros((FFN_TILE - m, D_MODEL), F32)

    @pl.when(rows == 0)
    def _():
        ys_ref[...] = jnp.zeros_like(ys_ref)

    for m in range(FFN_ROW_STEP, FFN_TILE + 1, FFN_ROW_STEP):
        pl.when((rows > m - FFN_ROW_STEP) & (rows <= m))(lambda m=m: compute(m))


def _ffn_call(plan, hs, wg, wu, wd, n_ffn_tiles):
    D = D_MODEL
    hbm = pl.BlockSpec(memory_space=pl.ANY)
    grid_spec = pltpu.PrefetchScalarGridSpec(
        num_scalar_prefetch=9,
        grid=(n_ffn_tiles,),
        in_specs=[pl.BlockSpec((FFN_TILE, ROW_WIDTH), lambda t, idx, *_: (idx[t], 0)),
                  hbm, hbm, hbm],
        out_specs=pl.BlockSpec((FFN_TILE, D), lambda t, *_: (t, 0)),
        scratch_shapes=[pltpu.VMEM((2, 2, D, D_EXPERT), F32),
                        pltpu.VMEM((2, 2, D, D_EXPERT), F32),
                        pltpu.VMEM((2, 2, D_EXPERT, D), F32),
                        pltpu.SemaphoreType.DMA((2,))])
    return pl.pallas_call(
        _ffn_kernel,
        grid_spec=grid_spec,
        out_shape=jax.ShapeDtypeStruct((n_ffn_tiles * FFN_TILE, D), F32),
        compiler_params=pltpu.CompilerParams(dimension_semantics=("arbitrary",),
                                             vmem_limit_bytes=VMEM_LIMIT),
        name="ffn",
    )(plan["tile_idx"], plan["tile_rows"], plan["tile_opens"], plan["tile_slot"],
      plan["tile_has_next"], plan["tile_ea"], plan["tile_eb"], plan["next_ea"], plan["next_eb"],
      hs, wg, wu, wd)


def _combine_kernel(run_start, local_start, units, tile_units,
                    x1_ref, info_ref, gate2_ref, ys_ref, o_ref, ybuf, sems):
    i = pl.program_id(0)
    nt = pl.num_programs(0)
    slot = i % 2

    def run_copy(s):
        def make(local_row, global_row, rows):
            return pltpu.make_async_copy(ys_ref.at[pl.ds(global_row, rows), :],
                                         ybuf.at[s, pl.ds(local_row, rows), :], sems.at[s])
        return make

    def for_slot(fn):
        for s in range(2):
            pl.when(slot == s)(lambda s=s: fn(s))

    def fetch(tile, s):
        _start_runs(tile, run_start, local_start, units, run_copy(s))

    @pl.when(i == 0)
    def _():
        ybuf[...] = jnp.zeros_like(ybuf)
        fetch(0, 0)

    @pl.when(i + 1 < nt)
    def _():
        for_slot(lambda s: fetch(i + 1, 1 - s))

    dest_col = info_ref[:, INFO_DEST:INFO_DEST + 1]
    cols = lax.broadcasted_iota(jnp.int32, (TOKEN_TILE, SORT_ROWS), 1).astype(F32)
    unperm = jnp.where(cols == dest_col, 1.0, 0.0).astype(BF16)

    def finish(s):
        _wait_runs(i, tile_units, run_copy(s))
        y = jnp.dot(unperm, ybuf[s].astype(BF16), preferred_element_type=F32)
        o_ref[...] = x1_ref[...] + gate2_ref[0] * y

    for_slot(finish)


def _combine_call(plan, x1, info, gate2, ys, B, S):
    T, D = x1.shape
    tm = TOKEN_TILE
    tpb = S // tm
    grid_spec = pltpu.PrefetchScalarGridSpec(
        num_scalar_prefetch=4,
        grid=(T // tm,),
        in_specs=[pl.BlockSpec((tm, D), lambda i, *_: (i, 0)),
                  pl.BlockSpec((tm, LANES), lambda i, *_: (i, 0)),
                  pl.BlockSpec((1, 1, D), lambda i, *_: (i // tpb, 0, 0)),
                  pl.BlockSpec(memory_space=pl.ANY)],
        out_specs=pl.BlockSpec((tm, D), lambda i, *_: (i, 0)),
        scratch_shapes=[pltpu.VMEM((2, SORT_ROWS, D), F32),
                        pltpu.SemaphoreType.DMA((2,))])
    return pl.pallas_call(
        _combine_kernel,
        grid_spec=grid_spec,
        out_shape=jax.ShapeDtypeStruct((T, D), F32),
        compiler_params=pltpu.CompilerParams(dimension_semantics=("arbitrary",),
                                             vmem_limit_bytes=VMEM_LIMIT),
        name="combine",
    )(plan["run_start"], plan["local_start"], plan["units"], plan["tile_units"],
      x1, info, gate2, ys)


def kernel(x, c, w_ada, b_ada, norm1_g, w_in, sgu_norm_g, sgu_w, sgu_b, q_norm_g, k_norm_g,
           w_proj_a, w_proj_b, w_out, norm2_g, w_router_group, b_router_group,
           w_router_expert, b_router_expert, w_gate, w_up, w_down):
    B, S, D = x.shape
    T = B * S
    depth = w_ada.shape[0]
    n_token_tiles = T // TOKEN_TILE
    max_rows = T + n_token_tiles * N_CLASSES * (SUBLANES - 1) + N_CLASSES * (FFN_TILE - SUBLANES)
    n_ffn_tiles = -(-max_rows // FFN_TILE)
    x2 = x.reshape(T, D)
    for l in range(depth):
        mod = _ada_call(c, w_ada[l], b_ada[l])
        shift1, scale1, gate1, shift2, scale2, gate2 = [
            m.reshape(B, 1, D) for m in jnp.split(mod, 6, axis=-1)]

        u, v, q, k, va, ga, gb = _inproj_call(
            x2, shift1, scale1, norm1_g[l].reshape(1, D), w_in[l].astype(BF16),
            sgu_norm_g[l].reshape(1, D), q_norm_g[l].reshape(1, ATT_HEAD_DIM),
            k_norm_g[l].reshape(1, ATT_HEAD_DIM), B, S, tm=512)

        o = _attention_call(q, k, va)

        w_router = jnp.concatenate(
            [w_router_group[l],
             jnp.transpose(w_router_expert[l], (1, 0, 2)).reshape(D, N_EXPERTS)], axis=1)
        w_router = jnp.pad(w_router, ((0, 0), (0, LANES - w_router.shape[1])))
        w_router_hi = w_router.astype(BF16)
        w_router_lo = (w_router - w_router_hi.astype(F32)).astype(BF16)
        w_router = jnp.concatenate([w_router_hi, w_router_lo], axis=1)
        b_router = jnp.concatenate([b_router_group[l], b_router_expert[l].reshape(N_EXPERTS)])
        b_router = jnp.pad(b_router, (0, LANES - b_router.shape[0])).reshape(1, LANES)
        sgu_bias = jnp.repeat(sgu_b[l].T, D // SGU_GROUPS, axis=1)

        x1, h2, info, counts = _merge_call(
            x2, u, v, o, ga, gb, gate1, shift2, scale2, norm2_g[l].reshape(1, D), sgu_w[l],
            sgu_bias, w_proj_a[l].astype(BF16), w_proj_b[l].astype(BF16), w_out[l].astype(BF16),
            w_router, b_router, B, S)

        plan = _dispatch_plan(counts[:, 0, :N_CLASSES].astype(jnp.int32), n_ffn_tiles)
        hs = _dispatch_call(plan, h2, info, n_ffn_tiles)
        ys = _ffn_call(plan, hs, w_gate[l], w_up[l], w_down[l], n_ffn_tiles)
        x2 = _combine_call(plan, x1, info, gate2, ys, B, S)
    return x2.reshape(B, S, D)
```

```python
import math

import jax
import jax.numpy as jnp
import numpy as np
from jax import lax
from jax.experimental import pallas as pl
from jax.experimental.pallas import tpu as pltpu

D_MODEL = 1024
CHUNK = 128
SGU_GROUPS = 8
ATT_HEADS = 8
ATT_HEAD_DIM = 128
DILATED_PATTERNS = ((128, 1), (512, 4), (2048, 16))
N_GROUPS = 4
EXPERTS_PER_GROUP = 4
N_EXPERTS = N_GROUPS * EXPERTS_PER_GROUP
D_EXPERT = 512
EPS = 1e-6
NEG_INF = -1e30

N_IN_SPLITS = 7
LANES = 128
SUBLANES = 8
EXPERT_LANE0 = N_GROUPS

PAIRS = ((0, 1), (0, 2), (0, 3), (1, 2), (1, 3), (2, 3))
N_CLASSES = N_GROUPS * len(PAIRS)
TOKEN_TILE = 512
SORT_ROWS = TOKEN_TILE + 3 * 64
ROW_WIDTH = D_MODEL + LANES
FFN_TILE = 512
FFN_ROW_STEP = 128
INFO_DEST, INFO_PLO_H, INFO_PHI_H, INFO_PLO_L, INFO_PHI_L = 0, 1, 2, 3, 4

VMEM_LIMIT = 56 * 1024 * 1024
BF16 = jnp.bfloat16
F32 = jnp.float32

assert SORT_ROWS >= TOKEN_TILE + N_CLASSES * (SUBLANES - 1)


def _rms(x, g):
    return x * lax.rsqrt(jnp.mean(x * x, axis=-1, keepdims=True) + EPS) * g


def _gelu_tanh(x):
    k = -2.0 * math.sqrt(2.0 / math.pi) * math.log2(math.e)
    return x / (1.0 + jnp.exp2(x * (k + (k * 0.044715) * (x * x))))


def _ada_kernel(c_ref, w_ref, b_ref, o_ref):
    c = c_ref[...]
    cond = c * jax.nn.sigmoid(c)
    o_ref[...] = jnp.dot(cond, w_ref[...], preferred_element_type=F32,
                         precision=lax.Precision.HIGHEST) + b_ref[...]


def _ada_call(c, w_ada, b_ada):
    B, D = c.shape
    N = w_ada.shape[1]
    tn = 1024
    return pl.pallas_call(
        _ada_kernel,
        grid=(N // tn,),
        in_specs=[pl.BlockSpec((B, D), lambda j: (0, 0)),
                  pl.BlockSpec((D, tn), lambda j: (0, j)),
                  pl.BlockSpec((1, tn), lambda j: (0, j))],
        out_specs=pl.BlockSpec((B, tn), lambda j: (0, j)),
        out_shape=jax.ShapeDtypeStruct((B, N), F32),
        compiler_params=pltpu.CompilerParams(dimension_semantics=("arbitrary",),
                                             vmem_limit_bytes=VMEM_LIMIT),
        name="adaln",
    )(c, w_ada, b_ada.reshape(1, N))


def _inproj_kernel(x_ref, shift_ref, scale_ref, n1g_ref, w_ref, sgug_ref, qg_ref, kg_ref,
                   u_ref, v_ref, q_ref, k_ref, va_ref, ga_ref, gb_ref):
    x = x_ref[...]
    h = _rms(x, n1g_ref[...]) * (1.0 + scale_ref[0]) + shift_ref[0]
    h = h.astype(BF16)

    def proj(i):
        return jnp.dot(h, w_ref[:, i * D_MODEL:(i + 1) * D_MODEL], preferred_element_type=F32)

    u_ref[...] = _gelu_tanh(proj(0)).astype(BF16)
    v_ref[...] = _rms(_gelu_tanh(proj(1)), sgug_ref[...]).astype(BF16)

    q = proj(2)
    qscale = ATT_HEAD_DIM ** -0.5
    for hd in range(ATT_HEADS):
        sl = slice(hd * ATT_HEAD_DIM, (hd + 1) * ATT_HEAD_DIM)
        q_ref[0, hd] = (_rms(q[:, sl], qg_ref[...]) * qscale).astype(BF16)
    k = proj(3)
    for hd in range(ATT_HEADS):
        sl = slice(hd * ATT_HEAD_DIM, (hd + 1) * ATT_HEAD_DIM)
        k_ref[0, hd] = _rms(k[:, sl], kg_ref[...]).astype(BF16)
    ga_ref[...] = jax.nn.sigmoid(proj(5)).astype(BF16)
    gb_ref[...] = jax.nn.sigmoid(proj(6)).astype(BF16)
    va = proj(4)
    for hd in range(ATT_HEADS):
        sl = slice(hd * ATT_HEAD_DIM, (hd + 1) * ATT_HEAD_DIM)
        va_ref[0, hd] = va[:, sl].astype(BF16)


def _inproj_call(x2, shift1, scale1, n1g, w_in, sgug, qg, kg, B, S, tm):
    T, D = x2.shape
    tpb = S // tm
    tok = pl.BlockSpec((tm, D), lambda i: (i, 0))
    per_b = pl.BlockSpec((1, 1, D), lambda i: (i // tpb, 0, 0))
    row = lambda n: pl.BlockSpec((1, n), lambda i: (0, 0))
    head = pl.BlockSpec((1, ATT_HEADS, tm, ATT_HEAD_DIM), lambda i: (i // tpb, 0, i % tpb, 0))
    tok_sds = jax.ShapeDtypeStruct((T, D), BF16)
    head_sds = jax.ShapeDtypeStruct((B, ATT_HEADS, S, ATT_HEAD_DIM), BF16)
    return pl.pallas_call(
        _inproj_kernel,
        grid=(T // tm,),
        in_specs=[tok, per_b, per_b, row(D),
                  pl.BlockSpec((D, N_IN_SPLITS * D), lambda i: (0, 0), pipeline_mode=pl.Buffered(1)),
                  row(D), row(ATT_HEAD_DIM), row(ATT_HEAD_DIM)],
        out_specs=[tok, tok, head, head, head, tok, tok],
        out_shape=[tok_sds, tok_sds, head_sds, head_sds, head_sds, tok_sds, tok_sds],
        compiler_params=pltpu.CompilerParams(dimension_semantics=("parallel",),
                                             vmem_limit_bytes=VMEM_LIMIT),
        name="inproj",
    )(x2, shift1, scale1, n1g, w_in, sgug, qg, kg)


ATT_BLK = 128
STRIDED_BLK = 256
HEADS_PER_STEP = 4
RESIDUES = 4
LOCAL_PATTERNS = tuple(p for p in DILATED_PATTERNS if p[1] % RESIDUES != 0)
STRIDED_PATTERNS = tuple(p for p in DILATED_PATTERNS if p[1] % RESIDUES == 0)
assert all(d == 1 and w <= ATT_BLK for w, d in LOCAL_PATTERNS)


def _log_count_bias(count):
    return jnp.asarray(np.where(count > 0, np.log(np.maximum(count, 1)), NEG_INF), F32)


def _local_bias():
    a = np.arange(ATT_BLK)[:, None]
    col = np.arange(2 * ATT_BLK)[None, :]
    delta = ATT_BLK + a - col
    count = np.zeros(delta.shape, np.int64)
    for window, _ in LOCAL_PATTERNS:
        count += (delta >= 0) & (delta <= window)
    return _log_count_bias(count)


def _strided_bias(n):
    nblk = n // STRIDED_BLK
    a = np.arange(STRIDED_BLK)[:, None]
    col = np.arange(n)[None, :]
    delta = (STRIDED_BLK * (nblk - 1 - col // STRIDED_BLK) + a - col % STRIDED_BLK) * RESIDUES
    count = np.zeros(delta.shape, np.int64)
    for window, dilation in STRIDED_PATTERNS:
        count += (delta >= 0) & (delta <= window) & (delta % dilation == 0)
    return _log_count_bias(count)


def _qk(q, k):
    return lax.dot_general(q, k, (((1,), (1,)), ((), ())), preferred_element_type=F32)


def _attention_kernel(q_ref, k_ref, v_ref, lbias_ref, sbias_ref, o_ref,
                      stage, q4, k4, v4, ve, acc_scr, m_scr, l_scr):
    S = q_ref.shape[2]
    n = S // RESIDUES
    nblk_s = n // STRIDED_BLK
    nblk = S // ATT_BLK
    Dh = ATT_HEAD_DIM
    ones = jnp.ones((S, LANES), BF16)

    def relayout(hd):
        ve[hd, :, Dh:] = ones
        ve[hd, :, :Dh] = v_ref[0, hd]
        for r in range(RESIDUES):
            v4[hd, r, :, Dh:] = ones[:n]
        for src, dst in ((q_ref, q4), (k_ref, k4), (v_ref, v4)):
            stage[hd] = src[0, hd].astype(F32)
            for r in range(RESIDUES):
                dst[hd, r, :, :Dh] = stage[hd, pl.ds(r, n, stride=RESIDUES), :].astype(BF16)

    def strided_block(hd, r, jb):
        nk = (jb + 1) * STRIDED_BLK

        def scores():
            s = _qk(q4[hd, r, jb * STRIDED_BLK:(jb + 1) * STRIDED_BLK, :], k4[hd, r, :nk, :])
            return s + sbias_ref[:, (nblk_s - 1 - jb) * STRIDED_BLK:]

        def rest(s):
            m = jnp.max(s, axis=-1, keepdims=True)
            p = jnp.exp(s - m)
            acc = jnp.dot(p.astype(BF16), v4[hd, r, :nk, :], preferred_element_type=F32)
            rows = pl.ds(RESIDUES * jb * STRIDED_BLK + r, STRIDED_BLK, stride=RESIDUES)
            acc_scr[hd, rows, :] = acc[:, :Dh]
            m_scr[hd, rows, :] = jnp.broadcast_to(m, (STRIDED_BLK, LANES))
            l_scr[hd, rows, :] = acc[:, Dh:]

        return scores, rest

    def local_block(hd, c):
        k0 = max(c - 1, 0) * ATT_BLK
        nk = (c + 1) * ATT_BLK - k0
        blk = slice(c * ATT_BLK, (c + 1) * ATT_BLK)

        def scores():
            s = _qk(q_ref[0, hd, blk, :], k_ref[0, hd, k0:k0 + nk, :])
            return s + lbias_ref[:, 2 * ATT_BLK - nk:]

        def rest(s):
            m_s = m_scr[hd, blk, :]
            m = jnp.maximum(jnp.max(s, axis=-1, keepdims=True), m_s)
            w = jnp.exp(m_s - m)
            p = jnp.exp(s - jnp.concatenate([m] * (nk // LANES), axis=-1))
            acc = jnp.dot(p.astype(BF16), ve[hd, k0:k0 + nk, :], preferred_element_type=F32)
            l = acc[:, Dh:] + l_scr[hd, blk, :] * w
            o_ref[0, hd, blk, :] = ((acc[:, :Dh] + acc_scr[hd, blk, :] * w) / l).astype(BF16)

        return scores, rest

    for hd in range(HEADS_PER_STEP):
        relayout(hd)
        blocks = [strided_block(hd, r, jb) for r in range(RESIDUES) for jb in range(nblk_s)]
        blocks += [local_block(hd, c) for c in range(nblk)]
        s_next = blocks[0][0]()
        for b, (_, rest) in enumerate(blocks):
            s_cur = s_next
            if b + 1 < len(blocks):
                s_next = blocks[b + 1][0]()
            rest(s_cur)


def _attention_call(q, k, v):
    B, H, S, Dh = q.shape
    n = S // RESIDUES
    P = HEADS_PER_STEP
    blk = pl.BlockSpec((1, P, S, Dh), lambda b, h: (b, h, 0, 0))
    return pl.pallas_call(
        _attention_kernel,
        grid=(B, H // P),
        in_specs=[blk, blk, blk,
                  pl.BlockSpec((ATT_BLK, 2 * ATT_BLK), lambda b, h: (0, 0)),
                  pl.BlockSpec((STRIDED_BLK, n), lambda b, h: (0, 0))],
        out_specs=blk,
        out_shape=jax.ShapeDtypeStruct((B, H, S, Dh), BF16),
        scratch_shapes=[pltpu.VMEM((P, S, Dh), F32),
                        pltpu.VMEM((P, RESIDUES, n, Dh), BF16),
                        pltpu.VMEM((P, RESIDUES, n, Dh), BF16),
                        pltpu.VMEM((P, RESIDUES, n, Dh + LANES), BF16),
                        pltpu.VMEM((P, S, Dh + LANES), BF16),
                        pltpu.VMEM((P, S, Dh), F32),
                        pltpu.VMEM((P, S, LANES), F32),
                        pltpu.VMEM((P, S, LANES), F32)],
        compiler_params=pltpu.CompilerParams(dimension_semantics=("parallel", "parallel"),
                                             vmem_limit_bytes=VMEM_LIMIT),
        name="attention",
    )(q, k, v, _local_bias(), _strided_bias(n))


def _route(logits):
    tm = logits.shape[0]
    lane = lax.broadcasted_iota(jnp.int32, logits.shape, 1)
    lanef = lane.astype(F32)

    def masked_top(mask):
        top = jnp.max(jnp.where(mask, logits, -jnp.inf), axis=-1, keepdims=True)
        idx = jnp.min(jnp.where(mask & (logits == top), lanef, float(LANES)), axis=-1, keepdims=True)
        return top, idx

    gmask = lane < N_GROUPS
    gmax, gidx = masked_top(gmask)
    p_group = 1.0 / jnp.sum(jnp.where(gmask, jnp.exp(logits - gmax), 0.0), axis=-1, keepdims=True)
    e_lo = EXPERT_LANE0 + gidx * EXPERTS_PER_GROUP
    emask = (lanef >= e_lo) & (lanef < e_lo + EXPERTS_PER_GROUP)
    v1, i1 = masked_top(emask)
    v2, i2 = masked_top(emask & (lanef != i1))
    e21 = jnp.exp(v2 - v1)
    p1 = p_group / (1.0 + e21)
    p2 = p_group * e21 / (1.0 + e21)
    first_is_lo = i1 < i2
    lo = jnp.minimum(i1, i2) - e_lo
    hi = jnp.maximum(i1, i2) - e_lo
    pair = lo * (7.0 - lo) * 0.5 + (hi - lo - 1.0)
    cls = gidx * float(len(PAIRS)) + pair
    p_lo = jnp.where(first_is_lo, p1, p2)
    p_hi = jnp.where(first_is_lo, p2, p1)

    onehot = lanef == cls
    onehot_b = jnp.where(onehot, 1.0, 0.0).astype(BF16)
    r = lax.broadcasted_iota(jnp.int32, (tm, tm), 0)
    c = lax.broadcasted_iota(jnp.int32, (tm, tm), 1)
    before = jnp.where(c < r, 1.0, 0.0).astype(BF16)
    rank = jnp.dot(before, onehot_b, preferred_element_type=F32)
    counts = jnp.sum(jnp.where(onehot, 1.0, 0.0), axis=0, keepdims=True)
    units = jnp.ceil(counts * (1.0 / SUBLANES))
    ur = lax.broadcasted_iota(jnp.int32, (LANES, LANES), 0)
    uc = lax.broadcasted_iota(jnp.int32, (LANES, LANES), 1)
    upper = jnp.where(ur < uc, 1.0, 0.0).astype(BF16)
    start = jnp.dot(jnp.broadcast_to(units, (SUBLANES, LANES)).astype(BF16), upper,
                    preferred_element_type=F32)[0:1] * float(SUBLANES)
    dest = jnp.sum(jnp.where(onehot, start + rank, 0.0), axis=-1, keepdims=True)

    def hi_part(p):
        return p.astype(BF16).astype(F32)

    info = jnp.where(lane == INFO_DEST, dest, 0.0)
    info = jnp.where(lane == INFO_PLO_H, hi_part(p_lo), info)
    info = jnp.where(lane == INFO_PHI_H, hi_part(p_hi), info)
    info = jnp.where(lane == INFO_PLO_L, p_lo - hi_part(p_lo), info)
    info = jnp.where(lane == INFO_PHI_L, p_hi - hi_part(p_hi), info)
    return info, counts


def _merge_kernel(x_ref, u_ref, v_ref, o_ref, ga_ref, gb_ref, gate1_ref, shift2_ref, scale2_ref,
                  n2g_ref, sw_ref, sb_ref, wa_ref, wb_ref, wo_ref, wr_ref, br_ref,
                  x1_ref, h2_ref, info_ref, counts_ref, s_scr, logit_scr):
    tm = x_ref.shape[0]
    step = pl.program_id(0)

    @pl.when(step == 0)
    def _():
        logit_scr[...] = jnp.zeros_like(logit_scr)

    prev_logits = logit_scr[1 - step % 2]
    row = lax.broadcasted_iota(jnp.int32, (CHUNK, CHUNK), 0)
    colm = lax.broadcasted_iota(jnp.int32, (CHUNK, CHUNK), 1)
    causal = colm <= row
    for g in range(SGU_GROUPS):
        w = jnp.where(causal, sw_ref[g], 0.0).astype(BF16)
        gs = slice(g * CHUNK, (g + 1) * CHUNK)
        chunks = [slice(c * CHUNK, (c + 1) * CHUNK) for c in range(tm // CHUNK)]
        mixed = jnp.dot(w, jnp.concatenate([v_ref[cs, gs] for cs in chunks], axis=1),
                        preferred_element_type=F32)
        for cs in chunks:
            s_scr[cs, gs] = (u_ref[cs, gs].astype(F32) * (mixed[:, cs] + sb_ref[:, gs])).astype(BF16)

    y_a = jnp.dot(s_scr[...], wa_ref[...], preferred_element_type=F32)
    info, counts = _route(prev_logits)
    info_ref[...] = info
    counts_ref[0] = counts
    o = jnp.concatenate([o_ref[0, hd] for hd in range(ATT_HEADS)], axis=-1)
    y_b = jnp.dot(o, wb_ref[...], preferred_element_type=F32)
    merged = ga_ref[...].astype(F32) * y_a + gb_ref[...].astype(F32) * y_b
    y = jnp.dot(merged.astype(BF16), wo_ref[...], preferred_element_type=F32)
    x1 = x_ref[...] + gate1_ref[0] * y
    x1_ref[...] = x1

    h2 = _rms(x1, n2g_ref[...]) * (1.0 + scale2_ref[0]) + shift2_ref[0]
    h2_ref[...] = h2.astype(BF16)
    h_hi = h2.astype(BF16)
    h_lo = (h2 - h_hi.astype(F32)).astype(BF16)
    r_hi = jnp.dot(h_hi, wr_ref[...], preferred_element_type=F32)
    r_lo = jnp.dot(h_lo, wr_ref[:, :LANES], preferred_element_type=F32)
    logit_scr[step % 2] = r_hi[:, :LANES] + r_hi[:, LANES:] + r_lo + br_ref[...]


def _merge_call(x2, u, v, o, ga, gb, gate1, shift2, scale2, n2g, sgu_w, sgu_bias, wa, wb, wo,
                w_router, b_router, B, S):
    T, D = x2.shape
    tm = TOKEN_TILE
    tpb = S // tm
    n_tiles = T // tm
    tile = lambda i: jnp.minimum(i, n_tiles - 1)
    routed = lambda i: jnp.maximum(i - 1, 0)
    tok = pl.BlockSpec((tm, D), lambda i: (tile(i), 0))
    per_b = pl.BlockSpec((1, 1, D), lambda i: (tile(i) // tpb, 0, 0))
    head = pl.BlockSpec((1, ATT_HEADS, tm, ATT_HEAD_DIM),
                        lambda i: (tile(i) // tpb, 0, tile(i) % tpb, 0))
    full = lambda *shape: pl.BlockSpec(shape, lambda i: (0,) * len(shape))
    return pl.pallas_call(
        _merge_kernel,
        grid=(n_tiles + 1,),
        in_specs=[tok, tok, tok, head, tok, tok, per_b, per_b, per_b, full(1, D),
                  full(SGU_GROUPS, CHUNK, CHUNK), full(CHUNK, D),
                  full(D, D), full(D, D), full(D, D), full(D, 2 * LANES), full(1, LANES)],
        out_specs=[tok, tok, pl.BlockSpec((tm, LANES), lambda i: (routed(i), 0)),
                   pl.BlockSpec((1, 1, LANES), lambda i: (routed(i), 0, 0))],
        out_shape=[jax.ShapeDtypeStruct((T, D), F32), jax.ShapeDtypeStruct((T, D), BF16),
                   jax.ShapeDtypeStruct((T, LANES), F32),
                   jax.ShapeDtypeStruct((n_tiles, 1, LANES), F32)],
        scratch_shapes=[pltpu.VMEM((tm, D), BF16), pltpu.VMEM((2, tm, LANES), F32)],
        compiler_params=pltpu.CompilerParams(dimension_semantics=("arbitrary",),
                                             vmem_limit_bytes=VMEM_LIMIT),
        name="merge",
    )(x2, u, v, o, ga, gb, gate1, shift2, scale2, n2g, sgu_w, sgu_bias, wa, wb, wo,
      w_router, b_router)


def _dispatch_plan(counts, n_ffn_tiles):
    units = (counts + SUBLANES - 1) // SUBLANES
    rows = units * SUBLANES
    local_start = jnp.cumsum(rows, axis=1) - rows
    class_rows = jnp.sum(rows, axis=0)
    class_tiles = (class_rows + FFN_TILE - 1) // FFN_TILE
    cum_tiles = jnp.cumsum(class_tiles)
    seg_start = (cum_tiles - class_tiles) * FFN_TILE
    run_start = seg_start[None, :] + jnp.cumsum(rows, axis=0) - rows
    n_used = cum_tiles[-1]
    tile_idx = jnp.maximum(jnp.minimum(jnp.arange(n_ffn_tiles, dtype=jnp.int32), n_used - 1), 0)
    tile_cls = jnp.sum((tile_idx[:, None] >= cum_tiles[None, :]).astype(jnp.int32), axis=1)
    tile_no = jnp.arange(n_ffn_tiles, dtype=jnp.int32)
    first_tile = (cum_tiles - class_tiles)[tile_cls]
    tile_rows = jnp.clip(class_rows[tile_cls] - (tile_no - first_tile) * FFN_TILE, 0, FFN_TILE)
    tile_rows = jnp.where(tile_no < n_used, tile_rows, 0)
    pairs = jnp.asarray(PAIRS, jnp.int32)
    i32 = lambda a: a.astype(jnp.int32).reshape(-1)

    class_ids = jnp.arange(N_CLASSES, dtype=jnp.int32)
    grp = class_ids // len(PAIRS)
    ent = pairs[class_ids % len(PAIRS)]
    has_rows = class_tiles > 0
    uses = jnp.any(ent[:, :, None] == jnp.arange(EXPERTS_PER_GROUP)[None, None, :], axis=1)
    user = jnp.where(has_rows[:, None] & uses, class_ids[:, None], -1)
    earlier = (class_ids[None, :] < class_ids[:, None])[:, :, None]
    holder = jnp.max(jnp.where(earlier, user[None, :, :], -1), axis=1)
    holder_grp = jnp.where(holder >= 0, holder // len(PAIRS), -1)
    with_rows = has_rows[None, :]
    prev_cls = jnp.max(jnp.where(with_rows & (class_ids[None, :] < class_ids[:, None]),
                                 class_ids[None, :], -1), axis=1)
    next_cls = jnp.min(jnp.where(with_rows & (class_ids[None, :] > class_ids[:, None]),
                                 class_ids[None, :], N_CLASSES), axis=1)
    need = jnp.take_along_axis(holder_grp, ent, axis=1) != grp[:, None]
    busy = jnp.take_along_axis(uses[jnp.maximum(prev_cls, 0)], ent, axis=1)
    late = need & (busy | (prev_cls[:, None] < 0))
    early = need & ~late
    expert = grp[:, None] * EXPERTS_PER_GROUP + ent
    opens = (tile_no == first_tile) & (tile_no < n_used)
    nxt = jnp.minimum(next_cls[tile_cls], N_CLASSES - 1)
    has_next = (next_cls[tile_cls] < N_CLASSES)[:, None]
    return dict(
        run_start=i32(run_start), local_start=i32(local_start), units=i32(units),
        tile_units=i32(jnp.sum(units, axis=1)),
        tail_start=i32(seg_start + class_rows),
        tail_units=i32((class_tiles * FFN_TILE - class_rows) // SUBLANES),
        tile_idx=i32(tile_idx), n_used=i32(n_used), tile_rows=i32(tile_rows),
        tile_opens=i32(opens),
        tile_entry=i32(ent[tile_cls]), tile_expert=i32(expert[tile_cls]),
        tile_wait=i32(need[tile_cls]), tile_late=i32(late[tile_cls]),
        pre_load=i32(early[nxt] & has_next), pre_entry=i32(ent[nxt]), pre_expert=i32(expert[nxt]))


def _rows(units):
    return pl.multiple_of(units * SUBLANES, SUBLANES)


def _start_runs(tile, run_start, local_start, units, make_copy):
    def per_class(c, carry):
        k = tile * N_CLASSES + c

        @pl.when(units[k] > 0)
        def _():
            make_copy(pl.multiple_of(local_start[k], SUBLANES),
                      pl.multiple_of(run_start[k], SUBLANES), _rows(units[k])).start()

        return carry

    lax.fori_loop(0, N_CLASSES, per_class, 0)


def _wait_runs(tile, tile_units, make_copy):
    make_copy(0, 0, _rows(tile_units[tile])).wait()


def _dispatch_kernel(run_start, local_start, units, tile_units, tail_start, tail_units, n_used,
                     h2_ref, info_ref, hs_ref, sorted_scr, zero_scr, sems, tail_sem, unused_sem):
    i = pl.program_id(0)
    nt = pl.num_programs(0)
    slot = i % 2
    n_ffn_tiles = hs_ref.shape[0] // FFN_TILE

    def run_copy(s):
        def make(local_row, global_row, rows):
            return pltpu.make_async_copy(sorted_scr.at[s, pl.ds(local_row, rows), :],
                                         hs_ref.at[pl.ds(global_row, rows), :], sems.at[s])
        return make

    def tail_copy(c):
        rows = _rows(tail_units[c])
        return pltpu.make_async_copy(
            zero_scr.at[pl.ds(0, rows), :],
            hs_ref.at[pl.ds(pl.multiple_of(tail_start[c], SUBLANES), rows), :], tail_sem)

    def unused_copy(t):
        return pltpu.make_async_copy(
            zero_scr, hs_ref.at[pl.ds(pl.multiple_of(t * FFN_TILE, FFN_TILE), FFN_TILE), :],
            unused_sem)

    def for_slot(fn):
        for s in range(2):
            pl.when(slot == s)(lambda s=s: fn(s))

    def for_tails(fn):
        def per_class(c, carry):
            pl.when(tail_units[c] > 0)(lambda: fn(c))
            return carry

        lax.fori_loop(0, N_CLASSES, per_class, 0)

    @pl.when(i >= 2)
    def _():
        for_slot(lambda s: _wait_runs(i - 2, tile_units, run_copy(s)))

    @pl.when(i == 0)
    def _():
        zero_scr[...] = jnp.zeros_like(zero_scr)
        for_tails(lambda c: tail_copy(c).start())

        def per_unused(t, carry):
            unused_copy(t).start()
            return carry

        lax.fori_loop(n_used[0], n_ffn_tiles, per_unused, 0)

    info = info_ref[...]
    dest_row = info.T[INFO_DEST:INFO_DEST + 1, :]
    rows = lax.broadcasted_iota(jnp.int32, (SORT_ROWS, TOKEN_TILE), 0).astype(F32)
    perm = jnp.where(rows == dest_row, 1.0, 0.0).astype(BF16)
    lane = lax.broadcasted_iota(jnp.int32, info.shape, 1)
    weights = jnp.where(lane == INFO_DEST, 0.0, info).astype(BF16)

    def sort_and_send(s):
        sorted_scr[s, :, :D_MODEL] = jnp.dot(perm, h2_ref[...], preferred_element_type=F32)
        sorted_scr[s, :, D_MODEL:] = jnp.dot(perm, weights, preferred_element_type=F32)
        _start_runs(i, run_start, local_start, units, run_copy(s))

    for_slot(sort_and_send)

    @pl.when(i == nt - 1)
    def _():
        for_slot(lambda s: _wait_runs(i, tile_units, run_copy(s)))

        @pl.when(nt >= 2)
        def _():
            for_slot(lambda s: _wait_runs(i - 1, tile_units, run_copy(1 - s)))

        for_tails(lambda c: tail_copy(c).wait())

        def per_unused(t, carry):
            unused_copy(0).wait()
            return carry

        lax.fori_loop(n_used[0], n_ffn_tiles, per_unused, 0)


def _dispatch_call(plan, h2, info, n_ffn_tiles):
    T, D = h2.shape
    tm = TOKEN_TILE
    grid_spec = pltpu.PrefetchScalarGridSpec(
        num_scalar_prefetch=7,
        grid=(T // tm,),
        in_specs=[pl.BlockSpec((tm, D), lambda i, *_: (i, 0)),
                  pl.BlockSpec((tm, LANES), lambda i, *_: (i, 0))],
        out_specs=pl.BlockSpec(memory_space=pl.ANY),
        scratch_shapes=[pltpu.VMEM((2, SORT_ROWS, ROW_WIDTH), F32),
                        pltpu.VMEM((FFN_TILE, ROW_WIDTH), F32),
                        pltpu.SemaphoreType.DMA((2,)),
                        pltpu.SemaphoreType.DMA(()),
                        pltpu.SemaphoreType.DMA(())])
    return pl.pallas_call(
        _dispatch_kernel,
        grid_spec=grid_spec,
        out_shape=jax.ShapeDtypeStruct((n_ffn_tiles * FFN_TILE, ROW_WIDTH), F32),
        compiler_params=pltpu.CompilerParams(dimension_semantics=("arbitrary",),
                                             vmem_limit_bytes=VMEM_LIMIT),
        name="dispatch",
    )(plan["run_start"], plan["local_start"], plan["units"], plan["tile_units"],
      plan["tail_start"], plan["tail_units"], plan["n_used"], h2, info)


def _ffn_kernel(tile_idx, tile_rows, tile_opens, tile_entry, tile_expert, tile_wait, tile_late,
                pre_load, pre_entry, pre_expert,
                hs_ref, wg_hbm, wu_hbm, wd_hbm, ys_ref, wg_buf, wu_buf, wd_buf, sems):
    t = pl.program_id(0)
    rows = tile_rows[t]
    entries = (tile_entry[2 * t], tile_entry[2 * t + 1])

    def load(j, e):
        return [pltpu.make_async_copy(wg_hbm.at[e], wg_buf.at[j], sems.at[j]),
                pltpu.make_async_copy(wu_hbm.at[e], wu_buf.at[j], sems.at[j]),
                pltpu.make_async_copy(wd_hbm.at[e], wd_buf.at[j], sems.at[j])]

    def start(j, e):
        for cp in load(j, e):
            cp.start()

    def wait(j, e):
        for cp in load(j, e):
            cp.wait()

    @pl.when(tile_opens[t] > 0)
    def _():
        for side in range(2):
            k = 2 * t + side
            pl.when(pre_load[k] > 0)(lambda k=k: start(pre_entry[k], pre_expert[k]))
        for side in range(2):
            k = 2 * t + side
            pl.when(tile_late[k] > 0)(lambda k=k: start(tile_entry[k], tile_expert[k]))
        for side in range(2):
            k = 2 * t + side
            pl.when(tile_wait[k] > 0)(lambda k=k: wait(tile_entry[k], tile_expert[k]))

    def compute(m):
        h = hs_ref[:m, :D_MODEL].astype(BF16)
        pv = hs_ref[:m, D_MODEL:]
        p_lo = pv[:, INFO_PLO_H:INFO_PLO_H + 1] + pv[:, INFO_PLO_L:INFO_PLO_L + 1]
        p_hi = pv[:, INFO_PHI_H:INFO_PHI_H + 1] + pv[:, INFO_PHI_L:INFO_PHI_L + 1]

        def hidden(j, p):
            a = jnp.dot(h, wg_buf[j].astype(BF16), preferred_element_type=F32)
            b = jnp.dot(h, wu_buf[j].astype(BF16), preferred_element_type=F32)
            return (a * jax.nn.sigmoid(a) * b * p).astype(BF16)

        he_a = hidden(entries[0], p_lo)
        he_b = hidden(entries[1], p_hi)
        ys_ref[:m, :] = (
            jnp.dot(he_a, wd_buf[entries[0]].astype(BF16), preferred_element_type=F32)
            + jnp.dot(he_b, wd_buf[entries[1]].astype(BF16), preferred_element_type=F32))
        if m < FFN_TILE:
            ys_ref[m:, :] = jnp.zeros((FFN_TILE - m, D_MODEL), F32)

    @pl.when(rows == 0)
    def _():
        ys_ref[...] = jnp.zeros_like(ys_ref)

    for m in range(FFN_ROW_STEP, FFN_TILE + 1, FFN_ROW_STEP):
        pl.when((rows > m - FFN_ROW_STEP) & (rows <= m))(lambda m=m: compute(m))


def _ffn_call(plan, hs, wg, wu, wd, n_ffn_tiles):
    D = D_MODEL
    hbm = pl.BlockSpec(memory_space=pl.ANY)
    grid_spec = pltpu.PrefetchScalarGridSpec(
        num_scalar_prefetch=10,
        grid=(n_ffn_tiles,),
        in_specs=[pl.BlockSpec((FFN_TILE, ROW_WIDTH), lambda t, idx, *_: (idx[t], 0)),
                  hbm, hbm, hbm],
        out_specs=pl.BlockSpec((FFN_TILE, D), lambda t, *_: (t, 0)),
        scratch_shapes=[pltpu.VMEM((EXPERTS_PER_GROUP, D, D_EXPERT), F32),
                        pltpu.VMEM((EXPERTS_PER_GROUP, D, D_EXPERT), F32),
                        pltpu.VMEM((EXPERTS_PER_GROUP, D_EXPERT, D), F32),
                        pltpu.SemaphoreType.DMA((EXPERTS_PER_GROUP,))])
    return pl.pallas_call(
        _ffn_kernel,
        grid_spec=grid_spec,
        out_shape=jax.ShapeDtypeStruct((n_ffn_tiles * FFN_TILE, D), F32),
        compiler_params=pltpu.CompilerParams(dimension_semantics=("arbitrary",),
                                             vmem_limit_bytes=VMEM_LIMIT),
        name="ffn",
    )(plan["tile_idx"], plan["tile_rows"], plan["tile_opens"], plan["tile_entry"],
      plan["tile_expert"], plan["tile_wait"], plan["tile_late"],
      plan["pre_load"], plan["pre_entry"], plan["pre_expert"],
      hs, wg, wu, wd)


def _combine_kernel(run_start, local_start, units, tile_units,
                    x1_ref, info_ref, gate2_ref, ys_ref, o_ref, ybuf, sems):
    i = pl.program_id(0)
    nt = pl.num_programs(0)
    slot = i % 2

    def run_copy(s):
        def make(local_row, global_row, rows):
            return pltpu.make_async_copy(ys_ref.at[pl.ds(global_row, rows), :],
                                         ybuf.at[s, pl.ds(local_row, rows), :], sems.at[s])
        return make

    def for_slot(fn):
        for s in range(2):
            pl.when(slot == s)(lambda s=s: fn(s))

    def fetch(tile, s):
        _start_runs(tile, run_start, local_start, units, run_copy(s))

    @pl.when(i == 0)
    def _():
        ybuf[...] = jnp.zeros_like(ybuf)
        fetch(0, 0)

    @pl.when(i + 1 < nt)
    def _():
        for_slot(lambda s: fetch(i + 1, 1 - s))

    dest_col = info_ref[:, INFO_DEST:INFO_DEST + 1]
    cols = lax.broadcasted_iota(jnp.int32, (TOKEN_TILE, SORT_ROWS), 1).astype(F32)
    unperm = jnp.where(cols == dest_col, 1.0, 0.0).astype(BF16)

    def finish(s):
        _wait_runs(i, tile_units, run_copy(s))
        y = jnp.dot(unperm, ybuf[s].astype(BF16), preferred_element_type=F32)
        o_ref[...] = x1_ref[...] + gate2_ref[0] * y

    for_slot(finish)


def _combine_call(plan, x1, info, gate2, ys, B, S):
    T, D = x1.shape
    tm = TOKEN_TILE
    tpb = S // tm
    grid_spec = pltpu.PrefetchScalarGridSpec(
        num_scalar_prefetch=4,
        grid=(T // tm,),
        in_specs=[pl.BlockSpec((tm, D), lambda i, *_: (i, 0)),
                  pl.BlockSpec((tm, LANES), lambda i, *_: (i, 0)),
                  pl.BlockSpec((1, 1, D), lambda i, *_: (i // tpb, 0, 0)),
                  pl.BlockSpec(memory_space=pl.ANY)],
        out_specs=pl.BlockSpec((tm, D), lambda i, *_: (i, 0)),
        scratch_shapes=[pltpu.VMEM((2, SORT_ROWS, D), F32),
                        pltpu.SemaphoreType.DMA((2,))])
    return pl.pallas_call(
        _combine_kernel,
        grid_spec=grid_spec,
        out_shape=jax.ShapeDtypeStruct((T, D), F32),
        compiler_params=pltpu.CompilerParams(dimension_semantics=("arbitrary",),
                                             vmem_limit_bytes=VMEM_LIMIT),
        name="combine",
    )(plan["run_start"], plan["local_start"], plan["units"], plan["tile_units"],
      x1, info, gate2, ys)


def kernel(x, c, w_ada, b_ada, norm1_g, w_in, sgu_norm_g, sgu_w, sgu_b, q_norm_g, k_norm_g,
           w_proj_a, w_proj_b, w_out, norm2_g, w_router_group, b_router_group,
           w_router_expert, b_router_expert, w_gate, w_up, w_down):
    B, S, D = x.shape
    T = B * S
    depth = w_ada.shape[0]
    n_token_tiles = T // TOKEN_TILE
    max_rows = T + n_token_tiles * N_CLASSES * (SUBLANES - 1) + N_CLASSES * (FFN_TILE - SUBLANES)
    n_ffn_tiles = -(-max_rows // FFN_TILE)
    x2 = x.reshape(T, D)
    for l in range(depth):
        mod = _ada_call(c, w_ada[l], b_ada[l])
        shift1, scale1, gate1, shift2, scale2, gate2 = [
            m.reshape(B, 1, D) for m in jnp.split(mod, 6, axis=-1)]

        u, v, q, k, va, ga, gb = _inproj_call(
            x2, shift1, scale1, norm1_g[l].reshape(1, D), w_in[l].astype(BF16),
            sgu_norm_g[l].reshape(1, D), q_norm_g[l].reshape(1, ATT_HEAD_DIM),
            k_norm_g[l].reshape(1, ATT_HEAD_DIM), B, S, tm=512)

        o = _attention_call(q, k, va)

        w_router = jnp.concatenate(
            [w_router_group[l],
             jnp.transpose(w_router_expert[l], (1, 0, 2)).reshape(D, N_EXPERTS)], axis=1)
        w_router = jnp.pad(w_router, ((0, 0), (0, LANES - w_router.shape[1])))
        w_router_hi = w_router.astype(BF16)
        w_router_lo = (w_router - w_router_hi.astype(F32)).astype(BF16)
        w_router = jnp.concatenate([w_router_hi, w_router_lo], axis=1)
        b_router = jnp.concatenate([b_router_group[l], b_router_expert[l].reshape(N_EXPERTS)])
        b_router = jnp.pad(b_router, (0, LANES - b_router.shape[0])).reshape(1, LANES)
        sgu_bias = jnp.repeat(sgu_b[l].T, D // SGU_GROUPS, axis=1)

        x1, h2, info, counts = _merge_call(
            x2, u, v, o, ga, gb, gate1, shift2, scale2, norm2_g[l].reshape(1, D), sgu_w[l],
            sgu_bias, w_proj_a[l].astype(BF16), w_proj_b[l].astype(BF16), w_out[l].astype(BF16),
            w_router, b_router, B, S)

        plan = _dispatch_plan(counts[:, 0, :N_CLASSES].astype(jnp.int32), n_ffn_tiles)
        hs = _dispatch_call(plan, h2, info, n_ffn_tiles)
        ys = _ffn_call(plan, hs, w_gate[l], w_up[l], w_down[l], n_ffn_tiles)
        x2 = _combine_call(plan, x1, info, gate2, ys, B, S)
    return x2.reshape(B, S, D)
```

```python
import math

import jax
import jax.numpy as jnp
import numpy as np
from jax import lax
from jax.experimental import pallas as pl
from jax.experimental.pallas import tpu as pltpu

D_MODEL = 1024
CHUNK = 128
SGU_GROUPS = 8
ATT_HEADS = 8
ATT_HEAD_DIM = 128
DILATED_PATTERNS = ((128, 1), (512, 4), (2048, 16))
N_GROUPS = 4
EXPERTS_PER_GROUP = 4
N_EXPERTS = N_GROUPS * EXPERTS_PER_GROUP
D_EXPERT = 512
EPS = 1e-6
NEG_INF = -1e30

N_IN_SPLITS = 7
LANES = 128
SUBLANES = 8
EXPERT_LANE0 = N_GROUPS

PAIRS = ((0, 1), (0, 2), (0, 3), (1, 2), (1, 3), (2, 3))
N_CLASSES = N_GROUPS * len(PAIRS)
TOKEN_TILE = 512
SORT_ROWS = TOKEN_TILE + 3 * 64
ROW_WIDTH = D_MODEL + LANES
FFN_TILE = 512
FFN_ROW_STEP = 128
INFO_DEST, INFO_PLO_H, INFO_PHI_H, INFO_PLO_L, INFO_PHI_L = 0, 1, 2, 3, 4

VMEM_LIMIT = 56 * 1024 * 1024
BF16 = jnp.bfloat16
F32 = jnp.float32

assert SORT_ROWS >= TOKEN_TILE + N_CLASSES * (SUBLANES - 1)


def _rms(x, g):
    return x * lax.rsqrt(jnp.mean(x * x, axis=-1, keepdims=True) + EPS) * g


def _gelu_tanh(x):
    k = -2.0 * math.sqrt(2.0 / math.pi) * math.log2(math.e)
    return x / (1.0 + jnp.exp2(x * (k + (k * 0.044715) * (x * x))))


def _ada_kernel(c_ref, w_ref, b_ref, o_ref):
    c = c_ref[...]
    cond = c * jax.nn.sigmoid(c)
    o_ref[...] = jnp.dot(cond, w_ref[...], preferred_element_type=F32,
                         precision=lax.Precision.HIGHEST) + b_ref[...]


def _ada_call(c, w_ada, b_ada):
    B, D = c.shape
    N = w_ada.shape[1]
    tn = 1024
    return pl.pallas_call(
        _ada_kernel,
        grid=(N // tn,),
        in_specs=[pl.BlockSpec((B, D), lambda j: (0, 0)),
                  pl.BlockSpec((D, tn), lambda j: (0, j)),
                  pl.BlockSpec((1, tn), lambda j: (0, j))],
        out_specs=pl.BlockSpec((B, tn), lambda j: (0, j)),
        out_shape=jax.ShapeDtypeStruct((B, N), F32),
        compiler_params=pltpu.CompilerParams(dimension_semantics=("arbitrary",),
                                             vmem_limit_bytes=VMEM_LIMIT),
        name="adaln",
    )(c, w_ada, b_ada.reshape(1, N))


def _inproj_kernel(x_ref, shift_ref, scale_ref, n1g_ref, w_ref, sgug_ref, qg_ref, kg_ref,
                   u_ref, v_ref, q_ref, k_ref, va_ref, ga_ref, gb_ref):
    x = x_ref[...]
    h = _rms(x, n1g_ref[...]) * (1.0 + scale_ref[0]) + shift_ref[0]
    h = h.astype(BF16)

    def proj(i):
        return jnp.dot(h, w_ref[:, i * D_MODEL:(i + 1) * D_MODEL], preferred_element_type=F32)

    u_ref[...] = _gelu_tanh(proj(0)).astype(BF16)
    v_ref[...] = _rms(_gelu_tanh(proj(1)), sgug_ref[...]).astype(BF16)

    q = proj(2)
    qscale = ATT_HEAD_DIM ** -0.5
    for hd in range(ATT_HEADS):
        sl = slice(hd * ATT_HEAD_DIM, (hd + 1) * ATT_HEAD_DIM)
        q_ref[0, hd] = (_rms(q[:, sl], qg_ref[...]) * qscale).astype(BF16)
    k = proj(3)
    for hd in range(ATT_HEADS):
        sl = slice(hd * ATT_HEAD_DIM, (hd + 1) * ATT_HEAD_DIM)
        k_ref[0, hd] = _rms(k[:, sl], kg_ref[...]).astype(BF16)
    ga_ref[...] = jax.nn.sigmoid(proj(5)).astype(BF16)
    gb_ref[...] = jax.nn.sigmoid(proj(6)).astype(BF16)
    va = proj(4)
    for hd in range(ATT_HEADS):
        sl = slice(hd * ATT_HEAD_DIM, (hd + 1) * ATT_HEAD_DIM)
        va_ref[0, hd] = va[:, sl].astype(BF16)


def _inproj_call(x2, shift1, scale1, n1g, w_in, sgug, qg, kg, B, S, tm):
    T, D = x2.shape
    tpb = S // tm
    tok = pl.BlockSpec((tm, D), lambda i: (i, 0))
    per_b = pl.BlockSpec((1, 1, D), lambda i: (i // tpb, 0, 0))
    row = lambda n: pl.BlockSpec((1, n), lambda i: (0, 0))
    head = pl.BlockSpec((1, ATT_HEADS, tm, ATT_HEAD_DIM), lambda i: (i // tpb, 0, i % tpb, 0))
    tok_sds = jax.ShapeDtypeStruct((T, D), BF16)
    head_sds = jax.ShapeDtypeStruct((B, ATT_HEADS, S, ATT_HEAD_DIM), BF16)
    return pl.pallas_call(
        _inproj_kernel,
        grid=(T // tm,),
        in_specs=[tok, per_b, per_b, row(D),
                  pl.BlockSpec((D, N_IN_SPLITS * D), lambda i: (0, 0), pipeline_mode=pl.Buffered(1)),
                  row(D), row(ATT_HEAD_DIM), row(ATT_HEAD_DIM)],
        out_specs=[tok, tok, head, head, head, tok, tok],
        out_shape=[tok_sds, tok_sds, head_sds, head_sds, head_sds, tok_sds, tok_sds],
        compiler_params=pltpu.CompilerParams(dimension_semantics=("parallel",),
                                             vmem_limit_bytes=VMEM_LIMIT),
        name="inproj",
    )(x2, shift1, scale1, n1g, w_in, sgug, qg, kg)


ATT_BLK = 128
STRIDED_BLK = 256
HEADS_PER_STEP = 4
RESIDUES = 4
LOCAL_PATTERNS = tuple(p for p in DILATED_PATTERNS if p[1] % RESIDUES != 0)
STRIDED_PATTERNS = tuple(p for p in DILATED_PATTERNS if p[1] % RESIDUES == 0)
assert all(d == 1 and w <= ATT_BLK for w, d in LOCAL_PATTERNS)


def _log_count_bias(count):
    return jnp.asarray(np.where(count > 0, np.log(np.maximum(count, 1)), NEG_INF), F32)


def _local_bias():
    a = np.arange(ATT_BLK)[:, None]
    col = np.arange(2 * ATT_BLK)[None, :]
    delta = ATT_BLK + a - col
    count = np.zeros(delta.shape, np.int64)
    for window, _ in LOCAL_PATTERNS:
        count += (delta >= 0) & (delta <= window)
    return _log_count_bias(count)


def _strided_bias(n):
    nblk = n // STRIDED_BLK
    a = np.arange(STRIDED_BLK)[:, None]
    col = np.arange(n)[None, :]
    delta = (STRIDED_BLK * (nblk - 1 - col // STRIDED_BLK) + a - col % STRIDED_BLK) * RESIDUES
    count = np.zeros(delta.shape, np.int64)
    for window, dilation in STRIDED_PATTERNS:
        count += (delta >= 0) & (delta <= window) & (delta % dilation == 0)
    return _log_count_bias(count)


def _qk(q, k):
    return lax.dot_general(q, k, (((1,), (1,)), ((), ())), preferred_element_type=F32)


def _attention_kernel(q_ref, k_ref, v_ref, lbias_ref, sbias_ref, o_ref,
                      stage, q4, k4, v4, ve, acc_scr, m_scr, l_scr):
    S = q_ref.shape[2]
    n = S // RESIDUES
    nblk_s = n // STRIDED_BLK
    nblk = S // ATT_BLK
    Dh = ATT_HEAD_DIM
    ones = jnp.ones((S, LANES), BF16)

    def relayout(hd):
        ve[hd, :, Dh:] = ones
        ve[hd, :, :Dh] = v_ref[0, hd]
        for r in range(RESIDUES):
            v4[hd, r, :, Dh:] = ones[:n]
        for src, dst in ((q_ref, q4), (k_ref, k4), (v_ref, v4)):
            stage[hd] = src[0, hd].astype(F32)
            for r in range(RESIDUES):
                dst[hd, r, :, :Dh] = stage[hd, pl.ds(r, n, stride=RESIDUES), :].astype(BF16)

    def strided_block(hd, r, jb):
        nk = (jb + 1) * STRIDED_BLK

        def scores():
            s = _qk(q4[hd, r, jb * STRIDED_BLK:(jb + 1) * STRIDED_BLK, :], k4[hd, r, :nk, :])
            return s + sbias_ref[:, (nblk_s - 1 - jb) * STRIDED_BLK:]

        def rest(s):
            m = jnp.max(s, axis=-1, keepdims=True)
            p = jnp.exp(s - m)
            acc = jnp.dot(p.astype(BF16), v4[hd, r, :nk, :], preferred_element_type=F32)
            rows = pl.ds(RESIDUES * jb * STRIDED_BLK + r, STRIDED_BLK, stride=RESIDUES)
            acc_scr[hd, rows, :] = acc[:, :Dh]
            m_scr[hd, rows, :] = jnp.broadcast_to(m, (STRIDED_BLK, LANES))
            l_scr[hd, rows, :] = acc[:, Dh:]

        return scores, rest

    def local_block(hd, c):
        k0 = max(c - 1, 0) * ATT_BLK
        nk = (c + 1) * ATT_BLK - k0
        blk = slice(c * ATT_BLK, (c + 1) * ATT_BLK)

        def scores():
            s = _qk(q_ref[0, hd, blk, :], k_ref[0, hd, k0:k0 + nk, :])
            return s + lbias_ref[:, 2 * ATT_BLK - nk:]

        def rest(s):
            m_s = m_scr[hd, blk, :]
            m = jnp.maximum(jnp.max(s, axis=-1, keepdims=True), m_s)
            w = jnp.exp(m_s - m)
            p = jnp.exp(s - jnp.concatenate([m] * (nk // LANES), axis=-1))
            acc = jnp.dot(p.astype(BF16), ve[hd, k0:k0 + nk, :], preferred_element_type=F32)
            l = acc[:, Dh:] + l_scr[hd, blk, :] * w
            o_ref[0, hd, blk, :] = ((acc[:, :Dh] + acc_scr[hd, blk, :] * w) / l).astype(BF16)

        return scores, rest

    for hd in range(HEADS_PER_STEP):
        relayout(hd)
        blocks = [strided_block(hd, r, jb) for r in range(RESIDUES) for jb in range(nblk_s)]
        blocks += [local_block(hd, c) for c in range(nblk)]
        s_next = blocks[0][0]()
        for b, (_, rest) in enumerate(blocks):
            s_cur = s_next
            if b + 1 < len(blocks):
                s_next = blocks[b + 1][0]()
            rest(s_cur)


def _attention_call(q, k, v):
    B, H, S, Dh = q.shape
    n = S // RESIDUES
    P = HEADS_PER_STEP
    blk = pl.BlockSpec((1, P, S, Dh), lambda b, h: (b, h, 0, 0))
    return pl.pallas_call(
        _attention_kernel,
        grid=(B, H // P),
        in_specs=[blk, blk, blk,
                  pl.BlockSpec((ATT_BLK, 2 * ATT_BLK), lambda b, h: (0, 0)),
                  pl.BlockSpec((STRIDED_BLK, n), lambda b, h: (0, 0))],
        out_specs=blk,
        out_shape=jax.ShapeDtypeStruct((B, H, S, Dh), BF16),
        scratch_shapes=[pltpu.VMEM((P, S, Dh), F32),
                        pltpu.VMEM((P, RESIDUES, n, Dh), BF16),
                        pltpu.VMEM((P, RESIDUES, n, Dh), BF16),
                        pltpu.VMEM((P, RESIDUES, n, Dh + LANES), BF16),
                        pltpu.VMEM((P, S, Dh + LANES), BF16),
                        pltpu.VMEM((P, S, Dh), F32),
                        pltpu.VMEM((P, S, LANES), F32),
                        pltpu.VMEM((P, S, LANES), F32)],
        compiler_params=pltpu.CompilerParams(dimension_semantics=("parallel", "parallel"),
                                             vmem_limit_bytes=VMEM_LIMIT),
        name="attention",
    )(q, k, v, _local_bias(), _strided_bias(n))


def _route(logits):
    tm = logits.shape[0]
    lane = lax.broadcasted_iota(jnp.int32, logits.shape, 1)
    lanef = lane.astype(F32)

    def masked_top(mask):
        top = jnp.max(jnp.where(mask, logits, -jnp.inf), axis=-1, keepdims=True)
        idx = jnp.min(jnp.where(mask & (logits == top), lanef, float(LANES)), axis=-1, keepdims=True)
        return top, idx

    gmask = lane < N_GROUPS
    gmax, gidx = masked_top(gmask)
    p_group = 1.0 / jnp.sum(jnp.where(gmask, jnp.exp(logits - gmax), 0.0), axis=-1, keepdims=True)
    e_lo = EXPERT_LANE0 + gidx * EXPERTS_PER_GROUP
    emask = (lanef >= e_lo) & (lanef < e_lo + EXPERTS_PER_GROUP)
    v1, i1 = masked_top(emask)
    v2, i2 = masked_top(emask & (lanef != i1))
    e21 = jnp.exp(v2 - v1)
    p1 = p_group / (1.0 + e21)
    p2 = p_group * e21 / (1.0 + e21)
    first_is_lo = i1 < i2
    lo = jnp.minimum(i1, i2) - e_lo
    hi = jnp.maximum(i1, i2) - e_lo
    pair = lo * (7.0 - lo) * 0.5 + (hi - lo - 1.0)
    cls = gidx * float(len(PAIRS)) + pair
    p_lo = jnp.where(first_is_lo, p1, p2)
    p_hi = jnp.where(first_is_lo, p2, p1)

    onehot = lanef == cls
    onehot_b = jnp.where(onehot, 1.0, 0.0).astype(BF16)
    r = lax.broadcasted_iota(jnp.int32, (tm, tm), 0)
    c = lax.broadcasted_iota(jnp.int32, (tm, tm), 1)
    before = jnp.where(c < r, 1.0, 0.0).astype(BF16)
    rank = jnp.dot(before, onehot_b, preferred_element_type=F32)
    counts = jnp.sum(jnp.where(onehot, 1.0, 0.0), axis=0, keepdims=True)
    units = jnp.ceil(counts * (1.0 / SUBLANES))
    ur = lax.broadcasted_iota(jnp.int32, (LANES, LANES), 0)
    uc = lax.broadcasted_iota(jnp.int32, (LANES, LANES), 1)
    upper = jnp.where(ur < uc, 1.0, 0.0).astype(BF16)
    start = jnp.dot(jnp.broadcast_to(units, (SUBLANES, LANES)).astype(BF16), upper,
                    preferred_element_type=F32)[0:1] * float(SUBLANES)
    dest = jnp.sum(jnp.where(onehot, start + rank, 0.0), axis=-1, keepdims=True)

    def hi_part(p):
        return p.astype(BF16).astype(F32)

    info = jnp.where(lane == INFO_DEST, dest, 0.0)
    info = jnp.where(lane == INFO_PLO_H, hi_part(p_lo), info)
    info = jnp.where(lane == INFO_PHI_H, hi_part(p_hi), info)
    info = jnp.where(lane == INFO_PLO_L, p_lo - hi_part(p_lo), info)
    info = jnp.where(lane == INFO_PHI_L, p_hi - hi_part(p_hi), info)
    return info, counts


def _merge_kernel(x_ref, u_ref, v_ref, o_ref, ga_ref, gb_ref, gate1_ref, shift2_ref, scale2_ref,
                  n2g_ref, sw_ref, sb_ref, wa_ref, wb_ref, wo_ref, wr_ref, br_ref,
                  x1_ref, h2_ref, info_ref, counts_ref, s_scr, logit_scr):
    tm = x_ref.shape[0]
    step = pl.program_id(0)

    @pl.when(step == 0)
    def _():
        logit_scr[...] = jnp.zeros_like(logit_scr)

    prev_logits = logit_scr[1 - step % 2]
    row = lax.broadcasted_iota(jnp.int32, (CHUNK, CHUNK), 0)
    colm = lax.broadcasted_iota(jnp.int32, (CHUNK, CHUNK), 1)
    causal = colm <= row
    for g in range(SGU_GROUPS):
        w = jnp.where(causal, sw_ref[g], 0.0).astype(BF16)
        gs = slice(g * CHUNK, (g + 1) * CHUNK)
        chunks = [slice(c * CHUNK, (c + 1) * CHUNK) for c in range(tm // CHUNK)]
        mixed = jnp.dot(w, jnp.concatenate([v_ref[cs, gs] for cs in chunks], axis=1),
                        preferred_element_type=F32)
        for cs in chunks:
            s_scr[cs, gs] = (u_ref[cs, gs].astype(F32) * (mixed[:, cs] + sb_ref[:, gs])).astype(BF16)

    y_a = jnp.dot(s_scr[...], wa_ref[...], preferred_element_type=F32)
    info, counts = _route(prev_logits)
    info_ref[...] = info
    counts_ref[0] = counts
    o = jnp.concatenate([o_ref[0, hd] for hd in range(ATT_HEADS)], axis=-1)
    y_b = jnp.dot(o, wb_ref[...], preferred_element_type=F32)
    merged = ga_ref[...].astype(F32) * y_a + gb_ref[...].astype(F32) * y_b
    y = jnp.dot(merged.astype(BF16), wo_ref[...], preferred_element_type=F32)
    x1 = x_ref[...] + gate1_ref[0] * y
    x1_ref[...] = x1

    h2 = _rms(x1, n2g_ref[...]) * (1.0 + scale2_ref[0]) + shift2_ref[0]
    h2_ref[...] = h2.astype(BF16)
    h_hi = h2.astype(BF16)
    h_lo = (h2 - h_hi.astype(F32)).astype(BF16)
    r_hi = jnp.dot(h_hi, wr_ref[...], preferred_element_type=F32)
    r_lo = jnp.dot(h_lo, wr_ref[:, :LANES], preferred_element_type=F32)
    logit_scr[step % 2] = r_hi[:, :LANES] + r_hi[:, LANES:] + r_lo + br_ref[...]


def _merge_call(x2, u, v, o, ga, gb, gate1, shift2, scale2, n2g, sgu_w, sgu_bias, wa, wb, wo,
                w_router, b_router, B, S):
    T, D = x2.shape
    tm = TOKEN_TILE
    tpb = S // tm
    n_tiles = T // tm
    tile = lambda i: jnp.minimum(i, n_tiles - 1)
    routed = lambda i: jnp.maximum(i - 1, 0)
    tok = pl.BlockSpec((tm, D), lambda i: (tile(i), 0))
    per_b = pl.BlockSpec((1, 1, D), lambda i: (tile(i) // tpb, 0, 0))
    head = pl.BlockSpec((1, ATT_HEADS, tm, ATT_HEAD_DIM),
                        lambda i: (tile(i) // tpb, 0, tile(i) % tpb, 0))
    full = lambda *shape: pl.BlockSpec(shape, lambda i: (0,) * len(shape))
    return pl.pallas_call(
        _merge_kernel,
        grid=(n_tiles + 1,),
        in_specs=[tok, tok, tok, head, tok, tok, per_b, per_b, per_b, full(1, D),
                  full(SGU_GROUPS, CHUNK, CHUNK), full(CHUNK, D),
                  full(D, D), full(D, D), full(D, D), full(D, 2 * LANES), full(1, LANES)],
        out_specs=[tok, tok, pl.BlockSpec((tm, LANES), lambda i: (routed(i), 0)),
                   pl.BlockSpec((1, 1, LANES), lambda i: (routed(i), 0, 0))],
        out_shape=[jax.ShapeDtypeStruct((T, D), F32), jax.ShapeDtypeStruct((T, D), BF16),
                   jax.ShapeDtypeStruct((T, LANES), F32),
                   jax.ShapeDtypeStruct((n_tiles, 1, LANES), F32)],
        scratch_shapes=[pltpu.VMEM((tm, D), BF16), pltpu.VMEM((2, tm, LANES), F32)],
        compiler_params=pltpu.CompilerParams(dimension_semantics=("arbitrary",),
                                             vmem_limit_bytes=VMEM_LIMIT),
        name="merge",
    )(x2, u, v, o, ga, gb, gate1, shift2, scale2, n2g, sgu_w, sgu_bias, wa, wb, wo,
      w_router, b_router)


def _dispatch_plan(counts, n_ffn_tiles):
    units = (counts + SUBLANES - 1) // SUBLANES
    rows = units * SUBLANES
    local_start = jnp.cumsum(rows, axis=1) - rows
    class_rows = jnp.sum(rows, axis=0)
    class_tiles = (class_rows + FFN_TILE - 1) // FFN_TILE
    cum_tiles = jnp.cumsum(class_tiles)
    seg_start = (cum_tiles - class_tiles) * FFN_TILE
    run_start = seg_start[None, :] + jnp.cumsum(rows, axis=0) - rows
    n_used = cum_tiles[-1]
    tile_idx = jnp.maximum(jnp.minimum(jnp.arange(n_ffn_tiles, dtype=jnp.int32), n_used - 1), 0)
    tile_cls = jnp.sum((tile_idx[:, None] >= cum_tiles[None, :]).astype(jnp.int32), axis=1)
    i32 = lambda a: a.astype(jnp.int32).reshape(-1)

    class_ids = jnp.arange(N_CLASSES, dtype=jnp.int32)
    class_experts = jnp.asarray(
        [[c // len(PAIRS) * EXPERTS_PER_GROUP + PAIRS[c % len(PAIRS)][side] for side in range(2)]
         for c in range(N_CLASSES)], jnp.int32)
    has_rows = class_tiles > 0
    later = has_rows[None, :] & (class_ids[None, :] > class_ids[:, None])
    next_cls = jnp.min(jnp.where(later, class_ids[None, :], N_CLASSES), axis=1)
    class_slot = (jnp.cumsum(has_rows.astype(jnp.int32)) - 1) % 2
    next_experts = class_experts[jnp.minimum(next_cls, N_CLASSES - 1)]
    per_class = jnp.stack([cum_tiles - class_tiles, class_rows, class_slot,
                           (next_cls < N_CLASSES).astype(jnp.int32),
                           class_experts[:, 0], class_experts[:, 1],
                           next_experts[:, 0], next_experts[:, 1]], axis=1)
    (first_tile, rows_of_class, tile_slot, tile_has_next,
     ea, eb, next_ea, next_eb) = jnp.moveaxis(per_class[tile_cls], 1, 0)
    tile_no = jnp.arange(n_ffn_tiles, dtype=jnp.int32)
    tile_rows = jnp.clip(rows_of_class - (tile_no - first_tile) * FFN_TILE, 0, FFN_TILE)
    tile_rows = jnp.where(tile_no < n_used, tile_rows, 0)
    return dict(
        run_start=i32(run_start), local_start=i32(local_start), units=i32(units),
        tile_units=i32(jnp.sum(units, axis=1)),
        tail_start=i32(seg_start + class_rows),
        tail_units=i32((class_tiles * FFN_TILE - class_rows) // SUBLANES),
        tile_idx=i32(tile_idx), n_used=i32(n_used), tile_rows=i32(tile_rows),
        tile_opens=i32((tile_no == first_tile) & (tile_no < n_used)),
        tile_slot=i32(tile_slot), tile_has_next=i32(tile_has_next),
        tile_ea=i32(ea), tile_eb=i32(eb), next_ea=i32(next_ea), next_eb=i32(next_eb))


def _rows(units):
    return pl.multiple_of(units * SUBLANES, SUBLANES)


def _start_runs(tile, run_start, local_start, units, make_copy):
    def per_class(c, carry):
        k = tile * N_CLASSES + c

        @pl.when(units[k] > 0)
        def _():
            make_copy(pl.multiple_of(local_start[k], SUBLANES),
                      pl.multiple_of(run_start[k], SUBLANES), _rows(units[k])).start()

        return carry

    lax.fori_loop(0, N_CLASSES, per_class, 0)


def _wait_runs(tile, tile_units, make_copy):
    make_copy(0, 0, _rows(tile_units[tile])).wait()


def _dispatch_kernel(run_start, local_start, units, tile_units, tail_start, tail_units, n_used,
                     h2_ref, info_ref, hs_ref, sorted_scr, zero_scr, sems, tail_sem, unused_sem):
    i = pl.program_id(0)
    nt = pl.num_programs(0)
    slot = i % 2
    n_ffn_tiles = hs_ref.shape[0] // FFN_TILE

    def run_copy(s):
        def make(local_row, global_row, rows):
            return pltpu.make_async_copy(sorted_scr.at[s, pl.ds(local_row, rows), :],
                                         hs_ref.at[pl.ds(global_row, rows), :], sems.at[s])
        return make

    def tail_copy(c):
        rows = _rows(tail_units[c])
        return pltpu.make_async_copy(
            zero_scr.at[pl.ds(0, rows), :],
            hs_ref.at[pl.ds(pl.multiple_of(tail_start[c], SUBLANES), rows), :], tail_sem)

    def unused_copy(t):
        return pltpu.make_async_copy(
            zero_scr, hs_ref.at[pl.ds(pl.multiple_of(t * FFN_TILE, FFN_TILE), FFN_TILE), :],
            unused_sem)

    def for_slot(fn):
        for s in range(2):
            pl.when(slot == s)(lambda s=s: fn(s))

    def for_tails(fn):
        def per_class(c, carry):
            pl.when(tail_units[c] > 0)(lambda: fn(c))
            return carry

        lax.fori_loop(0, N_CLASSES, per_class, 0)

    @pl.when(i >= 2)
    def _():
        for_slot(lambda s: _wait_runs(i - 2, tile_units, run_copy(s)))

    @pl.when(i == 0)
    def _():
        zero_scr[...] = jnp.zeros_like(zero_scr)
        for_tails(lambda c: tail_copy(c).start())

        def per_unused(t, carry):
            unused_copy(t).start()
            return carry

        lax.fori_loop(n_used[0], n_ffn_tiles, per_unused, 0)

    info = info_ref[...]
    dest_row = info.T[INFO_DEST:INFO_DEST + 1, :]
    rows = lax.broadcasted_iota(jnp.int32, (SORT_ROWS, TOKEN_TILE), 0).astype(F32)
    perm = jnp.where(rows == dest_row, 1.0, 0.0).astype(BF16)
    lane = lax.broadcasted_iota(jnp.int32, info.shape, 1)
    weights = jnp.where(lane == INFO_DEST, 0.0, info).astype(BF16)

    def sort_and_send(s):
        sorted_scr[s, :, :D_MODEL] = jnp.dot(perm, h2_ref[...], preferred_element_type=F32)
        sorted_scr[s, :, D_MODEL:] = jnp.dot(perm, weights, preferred_element_type=F32)
        _start_runs(i, run_start, local_start, units, run_copy(s))

    for_slot(sort_and_send)

    @pl.when(i == nt - 1)
    def _():
        for_slot(lambda s: _wait_runs(i, tile_units, run_copy(s)))

        @pl.when(nt >= 2)
        def _():
            for_slot(lambda s: _wait_runs(i - 1, tile_units, run_copy(1 - s)))

        for_tails(lambda c: tail_copy(c).wait())

        def per_unused(t, carry):
            unused_copy(0).wait()
            return carry

        lax.fori_loop(n_used[0], n_ffn_tiles, per_unused, 0)


def _dispatch_call(plan, h2, info, n_ffn_tiles):
    T, D = h2.shape
    tm = TOKEN_TILE
    grid_spec = pltpu.PrefetchScalarGridSpec(
        num_scalar_prefetch=7,
        grid=(T // tm,),
        in_specs=[pl.BlockSpec((tm, D), lambda i, *_: (i, 0)),
                  pl.BlockSpec((tm, LANES), lambda i, *_: (i, 0))],
        out_specs=pl.BlockSpec(memory_space=pl.ANY),
        scratch_shapes=[pltpu.VMEM((2, SORT_ROWS, ROW_WIDTH), F32),
                        pltpu.VMEM((FFN_TILE, ROW_WIDTH), F32),
                        pltpu.SemaphoreType.DMA((2,)),
                        pltpu.SemaphoreType.DMA(()),
                        pltpu.SemaphoreType.DMA(())])
    return pl.pallas_call(
        _dispatch_kernel,
        grid_spec=grid_spec,
        out_shape=jax.ShapeDtypeStruct((n_ffn_tiles * FFN_TILE, ROW_WIDTH), F32),
        compiler_params=pltpu.CompilerParams(dimension_semantics=("arbitrary",),
                                             vmem_limit_bytes=VMEM_LIMIT),
        name="dispatch",
    )(plan["run_start"], plan["local_start"], plan["units"], plan["tile_units"],
      plan["tail_start"], plan["tail_units"], plan["n_used"], h2, info)


def _ffn_kernel(tile_idx, tile_rows, tile_opens, tile_slot, tile_has_next,
                tile_ea, tile_eb, next_ea, next_eb,
                hs_ref, wg_hbm, wu_hbm, wd_hbm, ys_ref, wg_buf, wu_buf, wd_buf, sems):
    t = pl.program_id(0)
    rows = tile_rows[t]
    slot = tile_slot[t]

    def weight_copies(s, e_a, e_b):
        copies = []
        for j, e in enumerate((e_a, e_b)):
            copies += [pltpu.make_async_copy(wg_hbm.at[e], wg_buf.at[s, j], sems.at[s]),
                       pltpu.make_async_copy(wu_hbm.at[e], wu_buf.at[s, j], sems.at[s]),
                       pltpu.make_async_copy(wd_hbm.at[e], wd_buf.at[s, j], sems.at[s])]
        return copies

    @pl.when(t == 0)
    def _():
        for cp in weight_copies(slot, tile_ea[0], tile_eb[0]):
            cp.start()

    @pl.when(tile_opens[t] > 0)
    def _():
        @pl.when(tile_has_next[t] > 0)
        def _():
            for cp in weight_copies(1 - slot, next_ea[t], next_eb[t]):
                cp.start()

        for cp in weight_copies(slot, tile_ea[t], tile_eb[t]):
            cp.wait()

    def compute(m):
        h = hs_ref[:m, :D_MODEL].astype(BF16)
        pv = hs_ref[:m, D_MODEL:]
        p_lo = pv[:, INFO_PLO_H:INFO_PLO_H + 1] + pv[:, INFO_PLO_L:INFO_PLO_L + 1]
        p_hi = pv[:, INFO_PHI_H:INFO_PHI_H + 1] + pv[:, INFO_PHI_L:INFO_PHI_L + 1]

        def hidden(j, p):
            a = jnp.dot(h, wg_buf[slot, j].astype(BF16), preferred_element_type=F32)
            b = jnp.dot(h, wu_buf[slot, j].astype(BF16), preferred_element_type=F32)
            return (a * jax.nn.sigmoid(a) * b * p).astype(BF16)

        he_a = hidden(0, p_lo)
        he_b = hidden(1, p_hi)
        ys_ref[:m, :] = (
            jnp.dot(he_a, wd_buf[slot, 0].astype(BF16), preferred_element_type=F32)
            + jnp.dot(he_b, wd_buf[slot, 1].astype(BF16), preferred_element_type=F32))
        if m < FFN_TILE:
            ys_ref[m:, :] = jnp.zeros((FFN_TILE - m, D_MODEL), F32)

    @pl.when(rows == 0)
    def _():
        ys_ref[...] = jnp.zeros_like(ys_ref)

    for m in range(FFN_ROW_STEP, FFN_TILE + 1, FFN_ROW_STEP):
        pl.when((rows > m - FFN_ROW_STEP) & (rows <= m))(lambda m=m: compute(m))


def _ffn_call(plan, hs, wg, wu, wd, n_ffn_tiles):
    D = D_MODEL
    hbm = pl.BlockSpec(memory_space=pl.ANY)
    grid_spec = pltpu.PrefetchScalarGridSpec(
        num_scalar_prefetch=9,
        grid=(n_ffn_tiles,),
        in_specs=[pl.BlockSpec((FFN_TILE, ROW_WIDTH), lambda t, idx, *_: (idx[t], 0)),
                  hbm, hbm, hbm],
        out_specs=pl.BlockSpec((FFN_TILE, D), lambda t, *_: (t, 0)),
        scratch_shapes=[pltpu.VMEM((2, 2, D, D_EXPERT), F32),
                        pltpu.VMEM((2, 2, D, D_EXPERT), F32),
                        pltpu.VMEM((2, 2, D_EXPERT, D), F32),
                        pltpu.SemaphoreType.DMA((2,))])
    return pl.pallas_call(
        _ffn_kernel,
        grid_spec=grid_spec,
        out_shape=jax.ShapeDtypeStruct((n_ffn_tiles * FFN_TILE, D), F32),
        compiler_params=pltpu.CompilerParams(dimension_semantics=("arbitrary",),
                                             vmem_limit_bytes=VMEM_LIMIT),
        name="ffn",
    )(plan["tile_idx"], plan["tile_rows"], plan["tile_opens"], plan["tile_slot"],
      plan["tile_has_next"], plan["tile_ea"], plan["tile_eb"], plan["next_ea"], plan["next_eb"],
      hs, wg, wu, wd)


def _combine_kernel(run_start, local_start, units, tile_units,
                    x1_ref, info_ref, gate2_ref, ys_ref, o_ref, ybuf, sems):
    i = pl.program_id(0)
    nt = pl.num_programs(0)
    slot = i % 2

    def run_copy(s):
        def make(local_row, global_row, rows):
            return pltpu.make_async_copy(ys_ref.at[pl.ds(global_row, rows), :],
                                         ybuf.at[s, pl.ds(local_row, rows), :], sems.at[s])
        return make

    def for_slot(fn):
        for s in range(2):
            pl.when(slot == s)(lambda s=s: fn(s))

    def fetch(tile, s):
        _start_runs(tile, run_start, local_start, units, run_copy(s))

    @pl.when(i == 0)
    def _():
        ybuf[...] = jnp.zeros_like(ybuf)
        fetch(0, 0)

    @pl.when(i + 1 < nt)
    def _():
        for_slot(lambda s: fetch(i + 1, 1 - s))

    dest_col = info_ref[:, INFO_DEST:INFO_DEST + 1]
    cols = lax.broadcasted_iota(jnp.int32, (TOKEN_TILE, SORT_ROWS), 1).astype(F32)
    unperm = jnp.where(cols == dest_col, 1.0, 0.0).astype(BF16)

    def finish(s):
        _wait_runs(i, tile_units, run_copy(s))
        y = jnp.dot(unperm, ybuf[s].astype(BF16), preferred_element_type=F32)
        o_ref[...] = x1_ref[...] + gate2_ref[0] * y

    for_slot(finish)


def _combine_call(plan, x1, info, gate2, ys, B, S):
    T, D = x1.shape
    tm = TOKEN_TILE
    tpb = S // tm
    grid_spec = pltpu.PrefetchScalarGridSpec(
        num_scalar_prefetch=4,
        grid=(T // tm,),
        in_specs=[pl.BlockSpec((tm, D), lambda i, *_: (i, 0)),
                  pl.BlockSpec((tm, LANES), lambda i, *_: (i, 0)),
                  pl.BlockSpec((1, 1, D), lambda i, *_: (i // tpb, 0, 0)),
                  pl.BlockSpec(memory_space=pl.ANY)],
        out_specs=pl.BlockSpec((tm, D), lambda i, *_: (i, 0)),
        scratch_shapes=[pltpu.VMEM((2, SORT_ROWS, D), F32),
                        pltpu.SemaphoreType.DMA((2,))])
    return pl.pallas_call(
        _combine_kernel,
        grid_spec=grid_spec,
        out_shape=jax.ShapeDtypeStruct((T, D), F32),
        compiler_params=pltpu.CompilerParams(dimension_semantics=("arbitrary",),
                                             vmem_limit_bytes=VMEM_LIMIT),
        name="combine",
    )(plan["run_start"], plan["local_start"], plan["units"], plan["tile_units"],
      x1, info, gate2, ys)


def kernel(x, c, w_ada, b_ada, norm1_g, w_in, sgu_norm_g, sgu_w, sgu_b, q_norm_g, k_norm_g,
           w_proj_a, w_proj_b, w_out, norm2_g, w_router_group, b_router_group,
           w_router_expert, b_router_expert, w_gate, w_up, w_down):
    B, S, D = x.shape
    T = B * S
    depth = w_ada.shape[0]
    n_token_tiles = T // TOKEN_TILE
    max_rows = T + n_token_tiles * N_CLASSES * (SUBLANES - 1) + N_CLASSES * (FFN_TILE - SUBLANES)
    n_ffn_tiles = -(-max_rows // FFN_TILE)
    x2 = x.reshape(T, D)
    for l in range(depth):
        mod = _ada_call(c, w_ada[l], b_ada[l])
        shift1, scale1, gate1, shift2, scale2, gate2 = [
            m.reshape(B, 1, D) for m in jnp.split(mod, 6, axis=-1)]

        u, v, q, k, va, ga, gb = _inproj_call(
            x2, shift1, scale1, norm1_g[l].reshape(1, D), w_in[l].astype(BF16),
            sgu_norm_g[l].reshape(1, D), q_norm_g[l].reshape(1, ATT_HEAD_DIM),
            k_norm_g[l].reshape(1, ATT_HEAD_DIM), B, S, tm=512)

        o = _attention_call(q, k, va)

        w_router = jnp.concatenate(
            [w_router_group[l],
             jnp.transpose(w_router_expert[l], (1, 0, 2)).reshape(D, N_EXPERTS)], axis=1)
        w_router = jnp.pad(w_router, ((0, 0), (0, LANES - w_router.shape[1])))
        w_router_hi = w_router.astype(BF16)
        w_router_lo = (w_router - w_router_hi.astype(F32)).astype(BF16)
        w_router = jnp.concatenate([w_router_hi, w_router_lo], axis=1)
        b_router = jnp.concatenate([b_router_group[l], b_router_expert[l].reshape(N_EXPERTS)])
        b_router = jnp.pad(b_router, (0, LANES - b_router.shape[0])).reshape(1, LANES)
        sgu_bias = jnp.repeat(sgu_b[l].T, D // SGU_GROUPS, axis=1)

        x1, h2, info, counts = _merge_call(
            x2, u, v, o, ga, gb, gate1, shift2, scale2, norm2_g[l].reshape(1, D), sgu_w[l],
            sgu_bias, w_proj_a[l].astype(BF16), w_proj_b[l].astype(BF16), w_out[l].astype(BF16),
            w_router, b_router, B, S)

        plan = _dispatch_plan(counts[:, 0, :N_CLASSES].astype(jnp.int32), n_ffn_tiles)
        hs = _dispatch_call(plan, h2, info, n_ffn_tiles)
        ys = _ffn_call(plan, hs, w_gate[l], w_up[l], w_down[l], n_ffn_tiles)
        x2 = _combine_call(plan, x1, info, gate2, ys, B, S)
    return x2.reshape(B, S, D)
```

```python
import math

import jax
import jax.numpy as jnp
import numpy as np
from jax import lax
from jax.experimental import pallas as pl
from jax.experimental.pallas import tpu as pltpu

D_MODEL = 1024
CHUNK = 128
SGU_GROUPS = 8
ATT_HEADS = 8
ATT_HEAD_DIM = 128
DILATED_PATTERNS = ((128, 1), (512, 4), (2048, 16))
N_GROUPS = 4
EXPERTS_PER_GROUP = 4
N_EXPERTS = N_GROUPS * EXPERTS_PER_GROUP
D_EXPERT = 512
EPS = 1e-6
NEG_INF = -1e30

N_IN_SPLITS = 7
LANES = 128
SUBLANES = 8
EXPERT_LANE0 = N_GROUPS

PAIRS = ((0, 1), (0, 2), (0, 3), (1, 2), (1, 3), (2, 3))
N_CLASSES = N_GROUPS * len(PAIRS)
TOKEN_TILE = 512
SORT_ROWS = TOKEN_TILE + 3 * 64
ROW_WIDTH = D_MODEL + LANES
FFN_TILE = 512
FFN_ROW_STEP = 128
INFO_DEST, INFO_PLO_H, INFO_PHI_H, INFO_PLO_L, INFO_PHI_L = 0, 1, 2, 3, 4

VMEM_LIMIT = 56 * 1024 * 1024
BF16 = jnp.bfloat16
F32 = jnp.float32

assert SORT_ROWS >= TOKEN_TILE + N_CLASSES * (SUBLANES - 1)


def _rms(x, g):
    return x * lax.rsqrt(jnp.mean(x * x, axis=-1, keepdims=True) + EPS) * g


def _gelu_tanh(x):
    k = -2.0 * math.sqrt(2.0 / math.pi) * math.log2(math.e)
    return x / (1.0 + jnp.exp2(x * (k + (k * 0.044715) * (x * x))))


def _ada_kernel(c_ref, w_ref, b_ref, o_ref):
    c = c_ref[...]
    cond = c * jax.nn.sigmoid(c)
    o_ref[...] = jnp.dot(cond, w_ref[...], preferred_element_type=F32,
                         precision=lax.Precision.HIGHEST) + b_ref[...]


def _ada_call(c, w_ada, b_ada):
    B, D = c.shape
    N = w_ada.shape[1]
    tn = 1024
    return pl.pallas_call(
        _ada_kernel,
        grid=(N // tn,),
        in_specs=[pl.BlockSpec((B, D), lambda j: (0, 0)),
                  pl.BlockSpec((D, tn), lambda j: (0, j)),
                  pl.BlockSpec((1, tn), lambda j: (0, j))],
        out_specs=pl.BlockSpec((B, tn), lambda j: (0, j)),
        out_shape=jax.ShapeDtypeStruct((B, N), F32),
        compiler_params=pltpu.CompilerParams(dimension_semantics=("arbitrary",),
                                             vmem_limit_bytes=VMEM_LIMIT),
        name="adaln",
    )(c, w_ada, b_ada.reshape(1, N))


def _inproj_kernel(x_ref, shift_ref, scale_ref, n1g_ref, w_ref, sgug_ref, qg_ref, kg_ref,
                   u_ref, v_ref, q_ref, k_ref, va_ref, ga_ref, gb_ref):
    x = x_ref[...]
    h = _rms(x, n1g_ref[...]) * (1.0 + scale_ref[0]) + shift_ref[0]
    h = h.astype(BF16)

    def proj(i):
        return jnp.dot(h, w_ref[:, i * D_MODEL:(i + 1) * D_MODEL], preferred_element_type=F32)

    u_ref[...] = _gelu_tanh(proj(0)).astype(BF16)
    v_ref[...] = _rms(_gelu_tanh(proj(1)), sgug_ref[...]).astype(BF16)

    q = proj(2)
    qscale = ATT_HEAD_DIM ** -0.5 * math.log2(math.e)
    for hd in range(ATT_HEADS):
        sl = slice(hd * ATT_HEAD_DIM, (hd + 1) * ATT_HEAD_DIM)
        q_ref[0, hd] = (_rms(q[:, sl], qg_ref[...]) * qscale).astype(BF16)
    k = proj(3)
    for hd in range(ATT_HEADS):
        sl = slice(hd * ATT_HEAD_DIM, (hd + 1) * ATT_HEAD_DIM)
        k_ref[0, hd] = _rms(k[:, sl], kg_ref[...]).astype(BF16)
    ga_ref[...] = jax.nn.sigmoid(proj(5)).astype(BF16)
    gb_ref[...] = jax.nn.sigmoid(proj(6)).astype(BF16)
    va = proj(4)
    for hd in range(ATT_HEADS):
        sl = slice(hd * ATT_HEAD_DIM, (hd + 1) * ATT_HEAD_DIM)
        va_ref[0, hd] = va[:, sl].astype(BF16)


def _inproj_call(x2, shift1, scale1, n1g, w_in, sgug, qg, kg, B, S, tm):
    T, D = x2.shape
    tpb = S // tm
    tok = pl.BlockSpec((tm, D), lambda i: (i, 0))
    per_b = pl.BlockSpec((1, 1, D), lambda i: (i // tpb, 0, 0))
    row = lambda n: pl.BlockSpec((1, n), lambda i: (0, 0))
    head = pl.BlockSpec((1, ATT_HEADS, tm, ATT_HEAD_DIM), lambda i: (i // tpb, 0, i % tpb, 0))
    tok_sds = jax.ShapeDtypeStruct((T, D), BF16)
    head_sds = jax.ShapeDtypeStruct((B, ATT_HEADS, S, ATT_HEAD_DIM), BF16)
    return pl.pallas_call(
        _inproj_kernel,
        grid=(T // tm,),
        in_specs=[tok, per_b, per_b, row(D),
                  pl.BlockSpec((D, N_IN_SPLITS * D), lambda i: (0, 0), pipeline_mode=pl.Buffered(1)),
                  row(D), row(ATT_HEAD_DIM), row(ATT_HEAD_DIM)],
        out_specs=[tok, tok, head, head, head, tok, tok],
        out_shape=[tok_sds, tok_sds, head_sds, head_sds, head_sds, tok_sds, tok_sds],
        compiler_params=pltpu.CompilerParams(dimension_semantics=("parallel",),
                                             vmem_limit_bytes=VMEM_LIMIT),
        name="inproj",
    )(x2, shift1, scale1, n1g, w_in, sgug, qg, kg)


ATT_BLK = 128
STRIDED_BLK = 256
HEADS_PER_STEP = 4
RESIDUES = 4
LOCAL_PATTERNS = tuple(p for p in DILATED_PATTERNS if p[1] % RESIDUES != 0)
STRIDED_PATTERNS = tuple(p for p in DILATED_PATTERNS if p[1] % RESIDUES == 0)
assert all(d == 1 and w <= ATT_BLK for w, d in LOCAL_PATTERNS)


def _log_count_bias(count):
    return jnp.asarray(np.where(count > 0, np.log2(np.maximum(count, 1)), NEG_INF), F32)


def _local_bias():
    a = np.arange(ATT_BLK)[:, None]
    col = np.arange(2 * ATT_BLK)[None, :]
    delta = ATT_BLK + a - col
    count = np.zeros(delta.shape, np.int64)
    for window, _ in LOCAL_PATTERNS:
        count += (delta >= 0) & (delta <= window)
    return _log_count_bias(count)


def _strided_bias(n):
    nblk = n // STRIDED_BLK
    a = np.arange(STRIDED_BLK)[:, None]
    col = np.arange(n)[None, :]
    delta = (STRIDED_BLK * (nblk - 1 - col // STRIDED_BLK) + a - col % STRIDED_BLK) * RESIDUES
    count = np.zeros(delta.shape, np.int64)
    for window, dilation in STRIDED_PATTERNS:
        count += (delta >= 0) & (delta <= window) & (delta % dilation == 0)
    return _log_count_bias(count)


def _qk(q, k):
    return lax.dot_general(q, k, (((1,), (1,)), ((), ())), preferred_element_type=F32)


def _attention_kernel(q_ref, k_ref, v_ref, lbias_ref, sbias_ref, o_ref,
                      stage, q4, k4, v4, ve, acc_scr, m_scr, l_scr):
    S = q_ref.shape[2]
    n = S // RESIDUES
    nblk_s = n // STRIDED_BLK
    nblk = S // ATT_BLK
    Dh = ATT_HEAD_DIM
    ones = jnp.ones((S, LANES), BF16)

    def relayout(hd):
        ve[hd, :, Dh:] = ones
        ve[hd, :, :Dh] = v_ref[0, hd]
        for r in range(RESIDUES):
            v4[hd, r, :, Dh:] = ones[:n]
        for src, dst in ((q_ref, q4), (k_ref, k4), (v_ref, v4)):
            stage[hd] = src[0, hd].astype(F32)
            for r in range(RESIDUES):
                dst[hd, r, :, :Dh] = stage[hd, pl.ds(r, n, stride=RESIDUES), :].astype(BF16)

    def strided_block(hd, r, jb):
        nk = (jb + 1) * STRIDED_BLK

        def scores():
            s = _qk(q4[hd, r, jb * STRIDED_BLK:(jb + 1) * STRIDED_BLK, :], k4[hd, r, :nk, :])
            return s + sbias_ref[:, (nblk_s - 1 - jb) * STRIDED_BLK:]

        def rest(s):
            m = jnp.max(s, axis=-1, keepdims=True)
            p = jnp.exp2(s - m)
            acc = jnp.dot(p.astype(BF16), v4[hd, r, :nk, :], preferred_element_type=F32)
            rows = pl.ds(RESIDUES * jb * STRIDED_BLK + r, STRIDED_BLK, stride=RESIDUES)
            acc_scr[hd, rows, :] = acc[:, :Dh]
            m_scr[hd, rows, :] = jnp.broadcast_to(m, (STRIDED_BLK, LANES))
            l_scr[hd, rows, :] = acc[:, Dh:]

        return scores, rest

    def local_block(hd, c):
        k0 = max(c - 1, 0) * ATT_BLK
        nk = (c + 1) * ATT_BLK - k0
        blk = slice(c * ATT_BLK, (c + 1) * ATT_BLK)

        def scores():
            s = _qk(q_ref[0, hd, blk, :], k_ref[0, hd, k0:k0 + nk, :])
            return s + lbias_ref[:, 2 * ATT_BLK - nk:]

        def rest(s):
            m_s = m_scr[hd, blk, :]
            m = jnp.maximum(jnp.max(s, axis=-1, keepdims=True), m_s)
            w = jnp.exp2(m_s - m)
            p = jnp.exp2(s - jnp.concatenate([m] * (nk // LANES), axis=-1))
            acc = jnp.dot(p.astype(BF16), ve[hd, k0:k0 + nk, :], preferred_element_type=F32)
            l = acc[:, Dh:] + l_scr[hd, blk, :] * w
            o_ref[0, hd, blk, :] = ((acc[:, :Dh] + acc_scr[hd, blk, :] * w) / l).astype(BF16)

        return scores, rest

    for hd in range(HEADS_PER_STEP):
        relayout(hd)
        blocks = [strided_block(hd, r, jb) for r in range(RESIDUES) for jb in range(nblk_s)]
        blocks += [local_block(hd, c) for c in range(nblk)]
        s_next = blocks[0][0]()
        for b, (_, rest) in enumerate(blocks):
            s_cur = s_next
            if b + 1 < len(blocks):
                s_next = blocks[b + 1][0]()
            rest(s_cur)


def _attention_call(q, k, v):
    B, H, S, Dh = q.shape
    n = S // RESIDUES
    P = HEADS_PER_STEP
    blk = pl.BlockSpec((1, P, S, Dh), lambda b, h: (b, h, 0, 0))
    return pl.pallas_call(
        _attention_kernel,
        grid=(B, H // P),
        in_specs=[blk, blk, blk,
                  pl.BlockSpec((ATT_BLK, 2 * ATT_BLK), lambda b, h: (0, 0)),
                  pl.BlockSpec((STRIDED_BLK, n), lambda b, h: (0, 0))],
        out_specs=blk,
        out_shape=jax.ShapeDtypeStruct((B, H, S, Dh), BF16),
        scratch_shapes=[pltpu.VMEM((P, S, Dh), F32),
                        pltpu.VMEM((P, RESIDUES, n, Dh), BF16),
                        pltpu.VMEM((P, RESIDUES, n, Dh), BF16),
                        pltpu.VMEM((P, RESIDUES, n, Dh + LANES), BF16),
                        pltpu.VMEM((P, S, Dh + LANES), BF16),
                        pltpu.VMEM((P, S, Dh), F32),
                        pltpu.VMEM((P, S, LANES), F32),
                        pltpu.VMEM((P, S, LANES), F32)],
        compiler_params=pltpu.CompilerParams(dimension_semantics=("parallel", "parallel"),
                                             vmem_limit_bytes=VMEM_LIMIT),
        name="attention",
    )(q, k, v, _local_bias(), _strided_bias(n))


def _route(logits):
    tm = logits.shape[0]
    lane = lax.broadcasted_iota(jnp.int32, logits.shape, 1)
    lanef = lane.astype(F32)

    def masked_top(mask):
        top = jnp.max(jnp.where(mask, logits, -jnp.inf), axis=-1, keepdims=True)
        idx = jnp.min(jnp.where(mask & (logits == top), lanef, float(LANES)), axis=-1, keepdims=True)
        return top, idx

    gmask = lane < N_GROUPS
    gmax, gidx = masked_top(gmask)
    p_group = 1.0 / jnp.sum(jnp.where(gmask, jnp.exp(logits - gmax), 0.0), axis=-1, keepdims=True)
    e_lo = EXPERT_LANE0 + gidx * EXPERTS_PER_GROUP
    emask = (lanef >= e_lo) & (lanef < e_lo + EXPERTS_PER_GROUP)
    v1, i1 = masked_top(emask)
    v2, i2 = masked_top(emask & (lanef != i1))
    e21 = jnp.exp(v2 - v1)
    p1 = p_group / (1.0 + e21)
    p2 = p_group * e21 / (1.0 + e21)
    first_is_lo = i1 < i2
    lo = jnp.minimum(i1, i2) - e_lo
    hi = jnp.maximum(i1, i2) - e_lo
    pair = lo * (7.0 - lo) * 0.5 + (hi - lo - 1.0)
    cls = gidx * float(len(PAIRS)) + pair
    p_lo = jnp.where(first_is_lo, p1, p2)
    p_hi = jnp.where(first_is_lo, p2, p1)

    onehot = lanef == cls
    onehot_b = jnp.where(onehot, 1.0, 0.0).astype(BF16)
    r = lax.broadcasted_iota(jnp.int32, (tm, tm), 0)
    c = lax.broadcasted_iota(jnp.int32, (tm, tm), 1)
    before = jnp.where(c < r, 1.0, 0.0).astype(BF16)
    rank = jnp.dot(before, onehot_b, preferred_element_type=F32)
    counts = jnp.sum(jnp.where(onehot, 1.0, 0.0), axis=0, keepdims=True)
    units = jnp.ceil(counts * (1.0 / SUBLANES))
    ur = lax.broadcasted_iota(jnp.int32, (LANES, LANES), 0)
    uc = lax.broadcasted_iota(jnp.int32, (LANES, LANES), 1)
    upper = jnp.where(ur < uc, 1.0, 0.0).astype(BF16)
    start = jnp.dot(jnp.broadcast_to(units, (SUBLANES, LANES)).astype(BF16), upper,
                    preferred_element_type=F32)[0:1] * float(SUBLANES)
    dest = jnp.sum(jnp.where(onehot, start + rank, 0.0), axis=-1, keepdims=True)

    def hi_part(p):
        return p.astype(BF16).astype(F32)

    info = jnp.where(lane == INFO_DEST, dest, 0.0)
    info = jnp.where(lane == INFO_PLO_H, hi_part(p_lo), info)
    info = jnp.where(lane == INFO_PHI_H, hi_part(p_hi), info)
    info = jnp.where(lane == INFO_PLO_L, p_lo - hi_part(p_lo), info)
    info = jnp.where(lane == INFO_PHI_L, p_hi - hi_part(p_hi), info)
    return info, counts


def _merge_kernel(x_ref, u_ref, v_ref, o_ref, ga_ref, gb_ref, gate1_ref, shift2_ref, scale2_ref,
                  n2g_ref, sw_ref, sb_ref, wa_ref, wb_ref, wo_ref, wr_ref, br_ref,
                  x1_ref, h2_ref, info_ref, counts_ref, s_scr, logit_scr):
    tm = x_ref.shape[0]
    step = pl.program_id(0)

    @pl.when(step == 0)
    def _():
        logit_scr[...] = jnp.zeros_like(logit_scr)

    prev_logits = logit_scr[1 - step % 2]
    row = lax.broadcasted_iota(jnp.int32, (CHUNK, CHUNK), 0)
    colm = lax.broadcasted_iota(jnp.int32, (CHUNK, CHUNK), 1)
    causal = colm <= row
    for g in range(SGU_GROUPS):
        w = jnp.where(causal, sw_ref[g], 0.0).astype(BF16)
        gs = slice(g * CHUNK, (g + 1) * CHUNK)
        chunks = [slice(c * CHUNK, (c + 1) * CHUNK) for c in range(tm // CHUNK)]
        mixed = jnp.dot(w, jnp.concatenate([v_ref[cs, gs] for cs in chunks], axis=1),
                        preferred_element_type=F32)
        for cs in chunks:
            s_scr[cs, gs] = (u_ref[cs, gs].astype(F32) * (mixed[:, cs] + sb_ref[:, gs])).astype(BF16)

    y_a = jnp.dot(s_scr[...], wa_ref[...], preferred_element_type=F32)
    info, counts = _route(prev_logits)
    info_ref[...] = info
    counts_ref[0] = counts
    o = jnp.concatenate([o_ref[0, hd] for hd in range(ATT_HEADS)], axis=-1)
    y_b = jnp.dot(o, wb_ref[...], preferred_element_type=F32)
    merged = ga_ref[...].astype(F32) * y_a + gb_ref[...].astype(F32) * y_b
    y = jnp.dot(merged.astype(BF16), wo_ref[...], preferred_element_type=F32)
    x1 = x_ref[...] + gate1_ref[0] * y
    x1_ref[...] = x1

    h2 = _rms(x1, n2g_ref[...]) * (1.0 + scale2_ref[0]) + shift2_ref[0]
    h2_ref[...] = h2.astype(BF16)
    h_hi = h2.astype(BF16)
    h_lo = (h2 - h_hi.astype(F32)).astype(BF16)
    r_hi = jnp.dot(h_hi, wr_ref[...], preferred_element_type=F32)
    r_lo = jnp.dot(h_lo, wr_ref[:, :LANES], preferred_element_type=F32)
    logit_scr[step % 2] = r_hi[:, :LANES] + r_hi[:, LANES:] + r_lo + br_ref[...]


def _merge_call(x2, u, v, o, ga, gb, gate1, shift2, scale2, n2g, sgu_w, sgu_bias, wa, wb, wo,
                w_router, b_router, B, S):
    T, D = x2.shape
    tm = TOKEN_TILE
    tpb = S // tm
    n_tiles = T // tm
    tile = lambda i: jnp.minimum(i, n_tiles - 1)
    routed = lambda i: jnp.maximum(i - 1, 0)
    tok = pl.BlockSpec((tm, D), lambda i: (tile(i), 0))
    per_b = pl.BlockSpec((1, 1, D), lambda i: (tile(i) // tpb, 0, 0))
    head = pl.BlockSpec((1, ATT_HEADS, tm, ATT_HEAD_DIM),
                        lambda i: (tile(i) // tpb, 0, tile(i) % tpb, 0))
    full = lambda *shape: pl.BlockSpec(shape, lambda i: (0,) * len(shape))
    return pl.pallas_call(
        _merge_kernel,
        grid=(n_tiles + 1,),
        in_specs=[tok, tok, tok, head, tok, tok, per_b, per_b, per_b, full(1, D),
                  full(SGU_GROUPS, CHUNK, CHUNK), full(CHUNK, D),
                  full(D, D), full(D, D), full(D, D), full(D, 2 * LANES), full(1, LANES)],
        out_specs=[tok, tok, pl.BlockSpec((tm, LANES), lambda i: (routed(i), 0)),
                   pl.BlockSpec((1, 1, LANES), lambda i: (routed(i), 0, 0))],
        out_shape=[jax.ShapeDtypeStruct((T, D), F32), jax.ShapeDtypeStruct((T, D), BF16),
                   jax.ShapeDtypeStruct((T, LANES), F32),
                   jax.ShapeDtypeStruct((n_tiles, 1, LANES), F32)],
        scratch_shapes=[pltpu.VMEM((tm, D), BF16), pltpu.VMEM((2, tm, LANES), F32)],
        compiler_params=pltpu.CompilerParams(dimension_semantics=("arbitrary",),
                                             vmem_limit_bytes=VMEM_LIMIT),
        name="merge",
    )(x2, u, v, o, ga, gb, gate1, shift2, scale2, n2g, sgu_w, sgu_bias, wa, wb, wo,
      w_router, b_router)


def _dispatch_plan(counts, n_ffn_tiles):
    units = (counts + SUBLANES - 1) // SUBLANES
    rows = units * SUBLANES
    local_start = jnp.cumsum(rows, axis=1) - rows
    class_rows = jnp.sum(rows, axis=0)
    class_tiles = (class_rows + FFN_TILE - 1) // FFN_TILE
    cum_tiles = jnp.cumsum(class_tiles)
    seg_start = (cum_tiles - class_tiles) * FFN_TILE
    run_start = seg_start[None, :] + jnp.cumsum(rows, axis=0) - rows
    n_used = cum_tiles[-1]
    tile_idx = jnp.maximum(jnp.minimum(jnp.arange(n_ffn_tiles, dtype=jnp.int32), n_used - 1), 0)
    tile_cls = jnp.sum((tile_idx[:, None] >= cum_tiles[None, :]).astype(jnp.int32), axis=1)
    i32 = lambda a: a.astype(jnp.int32).reshape(-1)

    class_ids = jnp.arange(N_CLASSES, dtype=jnp.int32)
    class_experts = jnp.asarray(
        [[c // len(PAIRS) * EXPERTS_PER_GROUP + PAIRS[c % len(PAIRS)][side] for side in range(2)]
         for c in range(N_CLASSES)], jnp.int32)
    has_rows = class_tiles > 0
    later = has_rows[None, :] & (class_ids[None, :] > class_ids[:, None])
    next_cls = jnp.min(jnp.where(later, class_ids[None, :], N_CLASSES), axis=1)
    class_slot = (jnp.cumsum(has_rows.astype(jnp.int32)) - 1) % 2
    next_experts = class_experts[jnp.minimum(next_cls, N_CLASSES - 1)]
    per_class = jnp.stack([cum_tiles - class_tiles, class_rows, class_slot,
                           (next_cls < N_CLASSES).astype(jnp.int32),
                           class_experts[:, 0], class_experts[:, 1],
                           next_experts[:, 0], next_experts[:, 1]], axis=1)
    (first_tile, rows_of_class, tile_slot, tile_has_next,
     ea, eb, next_ea, next_eb) = jnp.moveaxis(per_class[tile_cls], 1, 0)
    tile_no = jnp.arange(n_ffn_tiles, dtype=jnp.int32)
    tile_rows = jnp.clip(rows_of_class - (tile_no - first_tile) * FFN_TILE, 0, FFN_TILE)
    tile_rows = jnp.where(tile_no < n_used, tile_rows, 0)
    return dict(
        run_start=i32(run_start), local_start=i32(local_start), units=i32(units),
        tile_units=i32(jnp.sum(units, axis=1)),
        tail_start=i32(seg_start + class_rows),
        tail_units=i32((class_tiles * FFN_TILE - class_rows) // SUBLANES),
        tile_idx=i32(tile_idx), n_used=i32(n_used), tile_rows=i32(tile_rows),
        tile_opens=i32((tile_no == first_tile) & (tile_no < n_used)),
        tile_slot=i32(tile_slot), tile_has_next=i32(tile_has_next),
        tile_ea=i32(ea), tile_eb=i32(eb), next_ea=i32(next_ea), next_eb=i32(next_eb))


def _rows(units):
    return pl.multiple_of(units * SUBLANES, SUBLANES)


def _start_runs(tile, run_start, local_start, units, make_copy):
    def per_class(c, carry):
        k = tile * N_CLASSES + c

        @pl.when(units[k] > 0)
        def _():
            make_copy(pl.multiple_of(local_start[k], SUBLANES),
                      pl.multiple_of(run_start[k], SUBLANES), _rows(units[k])).start()

        return carry

    lax.fori_loop(0, N_CLASSES, per_class, 0)


def _wait_runs(tile, tile_units, make_copy):
    make_copy(0, 0, _rows(tile_units[tile])).wait()


def _dispatch_kernel(run_start, local_start, units, tile_units, tail_start, tail_units, n_used,
                     h2_ref, info_ref, hs_ref, sorted_scr, zero_scr, sems, tail_sem, unused_sem):
    i = pl.program_id(0)
    nt = pl.num_programs(0)
    slot = i % 2
    n_ffn_tiles = hs_ref.shape[0] // FFN_TILE

    def run_copy(s):
        def make(local_row, global_row, rows):
            return pltpu.make_async_copy(sorted_scr.at[s, pl.ds(local_row, rows), :],
                                         hs_ref.at[pl.ds(global_row, rows), :], sems.at[s])
        return make

    def tail_copy(c):
        rows = _rows(tail_units[c])
        return pltpu.make_async_copy(
            zero_scr.at[pl.ds(0, rows), :],
            hs_ref.at[pl.ds(pl.multiple_of(tail_start[c], SUBLANES), rows), :], tail_sem)

    def unused_copy(t):
        return pltpu.make_async_copy(
            zero_scr, hs_ref.at[pl.ds(pl.multiple_of(t * FFN_TILE, FFN_TILE), FFN_TILE), :],
            unused_sem)

    def for_slot(fn):
        for s in range(2):
            pl.when(slot == s)(lambda s=s: fn(s))

    def for_tails(fn):
        def per_class(c, carry):
            pl.when(tail_units[c] > 0)(lambda: fn(c))
            return carry

        lax.fori_loop(0, N_CLASSES, per_class, 0)

    @pl.when(i >= 2)
    def _():
        for_slot(lambda s: _wait_runs(i - 2, tile_units, run_copy(s)))

    @pl.when(i == 0)
    def _():
        zero_scr[...] = jnp.zeros_like(zero_scr)
        for_tails(lambda c: tail_copy(c).start())

        def per_unused(t, carry):
            unused_copy(t).start()
            return carry

        lax.fori_loop(n_used[0], n_ffn_tiles, per_unused, 0)

    info = info_ref[...]
    dest_row = info.T[INFO_DEST:INFO_DEST + 1, :]
    rows = lax.broadcasted_iota(jnp.int32, (SORT_ROWS, TOKEN_TILE), 0).astype(F32)
    perm = jnp.where(rows == dest_row, 1.0, 0.0).astype(BF16)
    lane = lax.broadcasted_iota(jnp.int32, info.shape, 1)
    weights = jnp.where(lane == INFO_DEST, 0.0, info).astype(BF16)

    def sort_and_send(s):
        sorted_scr[s, :, :D_MODEL] = jnp.dot(perm, h2_ref[...], preferred_element_type=F32)
        sorted_scr[s, :, D_MODEL:] = jnp.dot(perm, weights, preferred_element_type=F32)
        _start_runs(i, run_start, local_start, units, run_copy(s))

    for_slot(sort_and_send)

    @pl.when(i == nt - 1)
    def _():
        for_slot(lambda s: _wait_runs(i, tile_units, run_copy(s)))

        @pl.when(nt >= 2)
        def _():
            for_slot(lambda s: _wait_runs(i - 1, tile_units, run_copy(1 - s)))

        for_tails(lambda c: tail_copy(c).wait())

        def per_unused(t, carry):
            unused_copy(0).wait()
            return carry

        lax.fori_loop(n_used[0], n_ffn_tiles, per_unused, 0)


def _dispatch_call(plan, h2, info, n_ffn_tiles):
    T, D = h2.shape
    tm = TOKEN_TILE
    grid_spec = pltpu.PrefetchScalarGridSpec(
        num_scalar_prefetch=7,
        grid=(T // tm,),
        in_specs=[pl.BlockSpec((tm, D), lambda i, *_: (i, 0)),
                  pl.BlockSpec((tm, LANES), lambda i, *_: (i, 0))],
        out_specs=pl.BlockSpec(memory_space=pl.ANY),
        scratch_shapes=[pltpu.VMEM((2, SORT_ROWS, ROW_WIDTH), F32),
                        pltpu.VMEM((FFN_TILE, ROW_WIDTH), F32),
                        pltpu.SemaphoreType.DMA((2,)),
                        pltpu.SemaphoreType.DMA(()),
                        pltpu.SemaphoreType.DMA(())])
    return pl.pallas_call(
        _dispatch_kernel,
        grid_spec=grid_spec,
        out_shape=jax.ShapeDtypeStruct((n_ffn_tiles * FFN_TILE, ROW_WIDTH), F32),
        compiler_params=pltpu.CompilerParams(dimension_semantics=("arbitrary",),
                                             vmem_limit_bytes=VMEM_LIMIT),
        name="dispatch",
    )(plan["run_start"], plan["local_start"], plan["units"], plan["tile_units"],
      plan["tail_start"], plan["tail_units"], plan["n_used"], h2, info)


def _ffn_kernel(tile_idx, tile_rows, tile_opens, tile_slot, tile_has_next,
                tile_ea, tile_eb, next_ea, next_eb,
                hs_ref, wg_hbm, wu_hbm, wd_hbm, ys_ref, wg_buf, wu_buf, wd_buf, sems):
    t = pl.program_id(0)
    rows = tile_rows[t]
    slot = tile_slot[t]

    def weight_copies(s, e_a, e_b):
        copies = []
        for j, e in enumerate((e_a, e_b)):
            copies += [pltpu.make_async_copy(wg_hbm.at[e], wg_buf.at[s, j], sems.at[s]),
                       pltpu.make_async_copy(wu_hbm.at[e], wu_buf.at[s, j], sems.at[s]),
                       pltpu.make_async_copy(wd_hbm.at[e], wd_buf.at[s, j], sems.at[s])]
        return copies

    @pl.when(t == 0)
    def _():
        for cp in weight_copies(slot, tile_ea[0], tile_eb[0]):
            cp.start()

    @pl.when(tile_opens[t] > 0)
    def _():
        @pl.when(tile_has_next[t] > 0)
        def _():
            for cp in weight_copies(1 - slot, next_ea[t], next_eb[t]):
                cp.start()

        for cp in weight_copies(slot, tile_ea[t], tile_eb[t]):
            cp.wait()

    def compute(m):
        h = hs_ref[:m, :D_MODEL].astype(BF16)
        pv = hs_ref[:m, D_MODEL:]
        p_lo = pv[:, INFO_PLO_H:INFO_PLO_H + 1] + pv[:, INFO_PLO_L:INFO_PLO_L + 1]
        p_hi = pv[:, INFO_PHI_H:INFO_PHI_H + 1] + pv[:, INFO_PHI_L:INFO_PHI_L + 1]

        def hidden(j, p):
            a = jnp.dot(h, wg_buf[slot, j].astype(BF16), preferred_element_type=F32)
            b = jnp.dot(h, wu_buf[slot, j].astype(BF16), preferred_element_type=F32)
            return (a * jax.nn.sigmoid(a) * b * p).astype(BF16)

        he_a = hidden(0, p_lo)
        he_b = hidden(1, p_hi)
        ys_ref[:m, :] = (
            jnp.dot(he_a, wd_buf[slot, 0].astype(BF16), preferred_element_type=F32)
            + jnp.dot(he_b, wd_buf[slot, 1].astype(BF16), preferred_element_type=F32))
        if m < FFN_TILE:
            ys_ref[m:, :] = jnp.zeros((FFN_TILE - m, D_MODEL), F32)

    @pl.when(rows == 0)
    def _():
        ys_ref[...] = jnp.zeros_like(ys_ref)

    for m in range(FFN_ROW_STEP, FFN_TILE + 1, FFN_ROW_STEP):
        pl.when((rows > m - FFN_ROW_STEP) & (rows <= m))(lambda m=m: compute(m))


def _ffn_call(plan, hs, wg, wu, wd, n_ffn_tiles):
    D = D_MODEL
    hbm = pl.BlockSpec(memory_space=pl.ANY)
    grid_spec = pltpu.PrefetchScalarGridSpec(
        num_scalar_prefetch=9,
        grid=(n_ffn_tiles,),
        in_specs=[pl.BlockSpec((FFN_TILE, ROW_WIDTH), lambda t, idx, *_: (idx[t], 0)),
                  hbm, hbm, hbm],
        out_specs=pl.BlockSpec((FFN_TILE, D), lambda t, *_: (t, 0)),
        scratch_shapes=[pltpu.VMEM((2, 2, D, D_EXPERT), F32),
                        pltpu.VMEM((2, 2, D, D_EXPERT), F32),
                        pltpu.VMEM((2, 2, D_EXPERT, D), F32),
                        pltpu.SemaphoreType.DMA((2,))])
    return pl.pallas_call(
        _ffn_kernel,
        grid_spec=grid_spec,
        out_shape=jax.ShapeDtypeStruct((n_ffn_tiles * FFN_TILE, D), F32),
        compiler_params=pltpu.CompilerParams(dimension_semantics=("arbitrary",),
                                             vmem_limit_bytes=VMEM_LIMIT),
        name="ffn",
    )(plan["tile_idx"], plan["tile_rows"], plan["tile_opens"], plan["tile_slot"],
      plan["tile_has_next"], plan["tile_ea"], plan["tile_eb"], plan["next_ea"], plan["next_eb"],
      hs, wg, wu, wd)


def _combine_kernel(run_start, local_start, units, tile_units,
                    x1_ref, info_ref, gate2_ref, ys_ref, o_ref, ybuf, sems):
    i = pl.program_id(0)
    nt = pl.num_programs(0)
    slot = i % 2

    def run_copy(s):
        def make(local_row, global_row, rows):
            return pltpu.make_async_copy(ys_ref.at[pl.ds(global_row, rows), :],
                                         ybuf.at[s, pl.ds(local_row, rows), :], sems.at[s])
        return make

    def for_slot(fn):
        for s in range(2):
            pl.when(slot == s)(lambda s=s: fn(s))

    def fetch(tile, s):
        _start_runs(tile, run_start, local_start, units, run_copy(s))

    @pl.when(i == 0)
    def _():
        ybuf[...] = jnp.zeros_like(ybuf)
        fetch(0, 0)

    @pl.when(i + 1 < nt)
    def _():
        for_slot(lambda s: fetch(i + 1, 1 - s))

    dest_col = info_ref[:, INFO_DEST:INFO_DEST + 1]
    cols = lax.broadcasted_iota(jnp.int32, (TOKEN_TILE, SORT_ROWS), 1).astype(F32)
    unperm = jnp.where(cols == dest_col, 1.0, 0.0).astype(BF16)

    def finish(s):
        _wait_runs(i, tile_units, run_copy(s))
        y = jnp.dot(unperm, ybuf[s].astype(BF16), preferred_element_type=F32)
        o_ref[...] = x1_ref[...] + gate2_ref[0] * y

    for_slot(finish)


def _combine_call(plan, x1, info, gate2, ys, B, S):
    T, D = x1.shape
    tm = TOKEN_TILE
    tpb = S // tm
    grid_spec = pltpu.PrefetchScalarGridSpec(
        num_scalar_prefetch=4,
        grid=(T // tm,),
        in_specs=[pl.BlockSpec((tm, D), lambda i, *_: (i, 0)),
                  pl.BlockSpec((tm, LANES), lambda i, *_: (i, 0)),
                  pl.BlockSpec((1, 1, D), lambda i, *_: (i // tpb, 0, 0)),
                  pl.BlockSpec(memory_space=pl.ANY)],
        out_specs=pl.BlockSpec((tm, D), lambda i, *_: (i, 0)),
        scratch_shapes=[pltpu.VMEM((2, SORT_ROWS, D), F32),
                        pltpu.SemaphoreType.DMA((2,))])
    return pl.pallas_call(
        _combine_kernel,
        grid_spec=grid_spec,
        out_shape=jax.ShapeDtypeStruct((T, D), F32),
        compiler_params=pltpu.CompilerParams(dimension_semantics=("arbitrary",),
                                             vmem_limit_bytes=VMEM_LIMIT),
        name="combine",
    )(plan["run_start"], plan["local_start"], plan["units"], plan["tile_units"],
      x1, info, gate2, ys)


def kernel(x, c, w_ada, b_ada, norm1_g, w_in, sgu_norm_g, sgu_w, sgu_b, q_norm_g, k_norm_g,
           w_proj_a, w_proj_b, w_out, norm2_g, w_router_group, b_router_group,
           w_router_expert, b_router_expert, w_gate, w_up, w_down):
    B, S, D = x.shape
    T = B * S
    depth = w_ada.shape[0]
    n_token_tiles = T // TOKEN_TILE
    max_rows = T + n_token_tiles * N_CLASSES * (SUBLANES - 1) + N_CLASSES * (FFN_TILE - SUBLANES)
    n_ffn_tiles = -(-max_rows // FFN_TILE)
    x2 = x.reshape(T, D)
    for l in range(depth):
        mod = _ada_call(c, w_ada[l], b_ada[l])
        shift1, scale1, gate1, shift2, scale2, gate2 = [
            m.reshape(B, 1, D) for m in jnp.split(mod, 6, axis=-1)]

        u, v, q, k, va, ga, gb = _inproj_call(
            x2, shift1, scale1, norm1_g[l].reshape(1, D), w_in[l].astype(BF16),
            sgu_norm_g[l].reshape(1, D), q_norm_g[l].reshape(1, ATT_HEAD_DIM),
            k_norm_g[l].reshape(1, ATT_HEAD_DIM), B, S, tm=512)

        o = _attention_call(q, k, va)

        w_router = jnp.concatenate(
            [w_router_group[l],
             jnp.transpose(w_router_expert[l], (1, 0, 2)).reshape(D, N_EXPERTS)], axis=1)
        w_router = jnp.pad(w_router, ((0, 0), (0, LANES - w_router.shape[1])))
        w_router_hi = w_router.astype(BF16)
        w_router_lo = (w_router - w_router_hi.astype(F32)).astype(BF16)
        w_router = jnp.concatenate([w_router_hi, w_router_lo], axis=1)
        b_router = jnp.concatenate([b_router_group[l], b_router_expert[l].reshape(N_EXPERTS)])
        b_router = jnp.pad(b_router, (0, LANES - b_router.shape[0])).reshape(1, LANES)
        sgu_bias = jnp.repeat(sgu_b[l].T, D // SGU_GROUPS, axis=1)

        x1, h2, info, counts = _merge_call(
            x2, u, v, o, ga, gb, gate1, shift2, scale2, norm2_g[l].reshape(1, D), sgu_w[l],
            sgu_bias, w_proj_a[l].astype(BF16), w_proj_b[l].astype(BF16), w_out[l].astype(BF16),
            w_router, b_router, B, S)

        plan = _dispatch_plan(counts[:, 0, :N_CLASSES].astype(jnp.int32), n_ffn_tiles)
        hs = _dispatch_call(plan, h2, info, n_ffn_tiles)
        ys = _ffn_call(plan, hs, w_gate[l], w_up[l], w_down[l], n_ffn_tiles)
        x2 = _combine_call(plan, x1, info, gate2, ys, B, S)
    return x2.reshape(B, S, D)
```
